```python
import jax, jax.numpy as jnp
from jax import lax
import numpy as np

D_MODEL = 1024
BATCH = 16
SEQ = 256
DEPTH = 4
DEC_BATCH = 4
DEC_SEQ = 1024
PAST_LEN = 256

GRID_W = 64
HEAD_DIM = 64
WIDTH_A = D_MODEL // 2
N_HEADS_A = WIDTH_A // HEAD_DIM
WIDTH_B = D_MODEL // 4
N_GROUPS_B = 4
GROUP_B = WIDTH_B // N_GROUPS_B
WIDTH_C = D_MODEL // 4
N_GROUPS_C = 4
GROUP_C = WIDTH_C // N_GROUPS_C
CHUNK = 128
MIX_WIDTH = WIDTH_A + WIDTH_B + WIDTH_C
PROJ_WIDTH = 3 * WIDTH_A + WIDTH_B + 2 * WIDTH_C
WIN_ROWS_MAX = 8
WIN_COLS = 16
D_FF = ((8 * D_MODEL // 3 + 127) // 128) * 128
N_SUB = 3
EPS = 1e-6
NEG = -1e30

kernel_name = 'hybrid_natten_fnet_gmlp_diffusion_step'


def rmsnorm(x, g):
    xf = x.astype(jnp.float32)
    y = xf * lax.rsqrt(jnp.mean(xf * xf, axis=-1, keepdims=True) + EPS)
    return (y * g.astype(jnp.float32)).astype(x.dtype)


def modulate(x, g, shift, scale):
    return rmsnorm(x, g) * (1 + scale) + shift


def swiglu(h, w_in, w_out):
    gate, up = jnp.split(h @ w_in, 2, axis=-1)
    return (jax.nn.silu(gate) * up) @ w_out


def context_attention(q, k, v):
    B, S, H, Dh = q.shape
    s = jnp.einsum('bqhd,bkhd->bhqk', q, k).astype(jnp.float32) * (Dh ** -0.5)
    p = jax.nn.softmax(s, axis=-1).astype(v.dtype)
    return jnp.einsum('bhqk,bkhd->bqhd', p, v).reshape(B, S, H * Dh)


def neighbourhood_attention(q, k, v, ck, cv, rpb):
    B, N, H, Dh = q.shape
    rows = N // GRID_W
    wr = min(WIN_ROWS_MAX, rows)
    r = np.arange(rows)
    row_start = np.clip(r - wr // 2, 0, rows - wr)
    row_idx = row_start[:, None] + np.arange(wr)[None, :]
    j = np.arange(GRID_W)
    col_start = np.clip(j - WIN_COLS // 2, 0, GRID_W - WIN_COLS)
    col_mask = (j[None, :] >= col_start[:, None]) & (j[None, :] < col_start[:, None] + WIN_COLS)
    row_off = row_idx - r[:, None] + WIN_ROWS_MAX - 1
    col_off = np.clip(j[None, :] - j[:, None] + WIN_COLS - 1, 0, 2 * WIN_COLS - 2)
    bias = rpb[:, row_off[:, None, :, None], col_off[None, :, None, :]]

    qg = q.reshape(B, rows, GRID_W, H, Dh)
    kg = k.reshape(B, rows, GRID_W, H, Dh)[:, row_idx]
    vg = v.reshape(B, rows, GRID_W, H, Dh)[:, row_idx]
    scale = Dh ** -0.5
    s_loc = jnp.einsum('brqhd,brakhd->bhrqak', qg, kg).astype(jnp.float32) * scale + bias.astype(jnp.float32)[None]
    s_loc = jnp.where(col_mask[None, None, None, :, None, :], s_loc, NEG)
    s_ctx = jnp.einsum('brqhd,bchd->bhrqc', qg, ck).astype(jnp.float32) * scale
    L = wr * GRID_W
    s = jnp.concatenate([s_loc.reshape(B, H, rows, GRID_W, L), s_ctx], axis=-1)
    p = jax.nn.softmax(s, axis=-1).astype(v.dtype)
    p_loc = p[..., :L].reshape(B, H, rows, GRID_W, wr, GRID_W)
    p_ctx = p[..., L:]
    o = jnp.einsum('bhrqak,brakhd->brqhd', p_loc, vg) + jnp.einsum('bhrqc,bchd->brqhd', p_ctx, cv)
    return o.reshape(B, N, H * Dh)


def fourier_mix(f):
    B, N, _ = f.shape
    fg = f.reshape(B, N, N_GROUPS_B, GROUP_B).astype(jnp.float32)
    out = jnp.fft.fftn(fg, axes=(1, 3), norm='ortho').real
    return out.reshape(B, N, WIDTH_B).astype(f.dtype)


def gmlp_mix(z, gn, ws, gb):
    B, N, _ = z.shape
    u, v = jnp.split(jax.nn.gelu(z), 2, axis=-1)
    shp = (B, N // CHUNK, CHUNK, N_GROUPS_C, GROUP_C)
    u = u.reshape(shp)
    vf = v.reshape(shp).astype(jnp.float32)
    mu = jnp.mean(vf, axis=-1, keepdims=True)
    var = jnp.mean(jnp.square(vf - mu), axis=-1, keepdims=True)
    vn = ((vf - mu) * lax.rsqrt(var + EPS) * gn.astype(jnp.float32)).astype(z.dtype)
    sp = jnp.einsum('gpq,bnqgc->bnpgc', ws, vn) + gb.T[:, :, None]
    return (u * sp).reshape(B, N, WIDTH_C)


def trunk_layer(x, mod, npre, npost, fw_in, fw_out, w_in, w_out, gn, ws, gb, attn_fn):
    B, N, _ = x.shape
    h = modulate(x, npre[0], mod[:, :, 0], mod[:, :, 1])
    x = x + 0.5 * mod[:, :, 2] * rmsnorm(swiglu(h, fw_in[0], fw_out[0]), npost[0])

    h = modulate(x, npre[1], mod[:, :, 3], mod[:, :, 4])
    z = h @ w_in
    q, k, v, f, g = jnp.split(z, [WIDTH_A, 2 * WIDTH_A, 3 * WIDTH_A, 3 * WIDTH_A + WIDTH_B], axis=-1)
    q = q.reshape(B, N, N_HEADS_A, HEAD_DIM)
    k = k.reshape(B, N, N_HEADS_A, HEAD_DIM)
    v = v.reshape(B, N, N_HEADS_A, HEAD_DIM)
    o = jnp.concatenate([attn_fn(q, k, v), fourier_mix(f), gmlp_mix(g, gn, ws, gb)], axis=-1)
    x = x + mod[:, :, 5] * rmsnorm(o @ w_out, npost[1])

    h = modulate(x, npre[2], mod[:, :, 6], mod[:, :, 7])
    x = x + 0.5 * mod[:, :, 8] * rmsnorm(swiglu(h, fw_in[1], fw_out[1]), npost[2])
    return x, k, v


def setup_inputs(seed: int = 0) -> dict:
    key = jax.random.key(seed)
    ks = jax.random.split(key, 20)
    nrm = jax.random.normal
    f32 = jnp.float32
    return {
        'x_prompt': nrm(ks[0], (BATCH, SEQ, D_MODEL), f32),
        'x_sample': nrm(ks[1], (DEC_BATCH, DEC_SEQ, D_MODEL), f32),
        'cache_k': nrm(ks[2], (DEC_BATCH, DEPTH, PAST_LEN, N_HEADS_A, HEAD_DIM), f32),
        'cache_v': nrm(ks[3], (DEC_BATCH, DEPTH, PAST_LEN, N_HEADS_A, HEAD_DIM), f32),
        'c': nrm(ks[4], (DEC_BATCH, D_MODEL), f32),
        'c_ctx': nrm(ks[5], (D_MODEL,), f32),
        'ada_w': nrm(ks[6], (DEPTH, D_MODEL, 3 * N_SUB * D_MODEL), f32) * (0.5 * D_MODEL ** -0.5),
        'ada_b': nrm(ks[7], (DEPTH, 3 * N_SUB * D_MODEL), f32) * 0.02,
        'norm_pre': 1.0 + 0.01 * nrm(ks[8], (DEPTH, N_SUB, D_MODEL), f32),
        'norm_post': 1.0 + 0.01 * nrm(ks[9], (DEPTH, N_SUB, D_MODEL), f32),
        'ffn_w_in': nrm(ks[10], (DEPTH, 2, D_MODEL, 2 * D_FF), f32) * D_MODEL ** -0.5,
        'ffn_w_out': nrm(ks[11], (DEPTH, 2, D_FF, D_MODEL), f32) * D_FF ** -0.5,
        'w_in': nrm(ks[12], (DEPTH, D_MODEL, PROJ_WIDTH), f32) * D_MODEL ** -0.5,
        'w_out': nrm(ks[13], (DEPTH, MIX_WIDTH, D_MODEL), f32) * MIX_WIDTH ** -0.5,
        'rpb': nrm(ks[14], (DEPTH, N_HEADS_A, 2 * WIN_ROWS_MAX - 1, 2 * WIN_COLS - 1), f32) * 0.1,
        'gmlp_norm': 1.0 + 0.01 * nrm(ks[15], (DEPTH, N_GROUPS_C, GROUP_C), f32),
        'gmlp_w': nrm(ks[16], (DEPTH, N_GROUPS_C, CHUNK, CHUNK), f32) * CHUNK ** -0.5,
        'gmlp_b': 1.0 + 0.01 * nrm(ks[17], (DEPTH, N_GROUPS_C, CHUNK), f32),
    }


def reference(x_prompt, x_sample, cache_k, cache_v, c, c_ctx, ada_w, ada_b, norm_pre, norm_post,
              ffn_w_in, ffn_w_out, w_in, w_out, rpb, gmlp_norm, gmlp_w, gmlp_b):
    xp = x_prompt
    xs = x_sample
    new_ks = []
    new_vs = []
    for l in range(DEPTH):
        mod_ctx = (jax.nn.silu(c_ctx) @ ada_w[l] + ada_b[l]).reshape(1, 1, 3 * N_SUB, D_MODEL)
        mod_lat = (jax.nn.silu(c) @ ada_w[l] + ada_b[l]).reshape(-1, 1, 3 * N_SUB, D_MODEL)
        xp, k_l, v_l = trunk_layer(xp, mod_ctx, norm_pre[l], norm_post[l], ffn_w_in[l], ffn_w_out[l],
                                   w_in[l], w_out[l], gmlp_norm[l], gmlp_w[l], gmlp_b[l],
                                   context_attention)
        new_ks.append(k_l)
        new_vs.append(v_l)
        ck = cache_k[:, l]
        cv = cache_v[:, l]
        rp = rpb[l]
        xs, _, _ = trunk_layer(xs, mod_lat, norm_pre[l], norm_post[l], ffn_w_in[l], ffn_w_out[l],
                               w_in[l], w_out[l], gmlp_norm[l], gmlp_w[l], gmlp_b[l],
                               lambda q, k, v: neighbourhood_attention(q, k, v, ck, cv, rp))
    new_k = jnp.stack(new_ks, axis=1)
    new_v = jnp.stack(new_vs, axis=1)
    return (xp, xs, new_k, new_v)
```

```python
import functools

import numpy as np
import jax
import jax.numpy as jnp
from jax import lax
from jax.experimental import pallas as pl
from jax.experimental.pallas import tpu as pltpu

D = 1024
DEPTH = 4
N_CTX_B = 16
CTX_SEQ = 256
N_LAT_B = 4
LAT_SEQ = 1024
N_LAT = N_LAT_B * LAT_SEQ
N_CTX = N_CTX_B * CTX_SEQ
N_TOK = N_LAT + N_CTX
PAST = 256
GRID_W = 64
GRID_ROWS = LAT_SEQ // GRID_W
HD = 64
NH = 8
WA = 512
WB = 256
WC = 256
NGRP = 4
GC = 64
CHUNK = 128
PROJ_W = 3 * WA + WB + 2 * WC
D_FF = 2816
FF_CHUNK = 256
N_FF_CHUNKS = D_FF // FF_CHUNK
WIN_ROWS = 8
WIN_COLS = 16
N_ROW_OFF = 2 * WIN_ROWS - 1
N_COL_OFF = 2 * WIN_COLS - 1
N_DELTA = 8
EPS = 1e-6
NEG = -1e30
MOD_ROWS = 8
CTX_MOD_ROW = 4
HEADS_PER_GROUP = 4
N_HEAD_GROUPS = NH // HEADS_PER_GROUP

BF = jnp.bfloat16
F32 = jnp.float32

VMEM_LIMIT = 56 * 1024 * 1024


def _dot(a, b):
    return jnp.dot(a, b, preferred_element_type=F32)


def _dot_nt(a, b):
    return lax.dot_general(a, b, (((1,), (1,)), ((), ())), preferred_element_type=F32)


def _split3_dot(a, b):
    a0 = a.astype(BF)
    r1 = a - a0.astype(F32)
    a1 = r1.astype(BF)
    a2 = (r1 - a1.astype(F32)).astype(BF)
    return _dot(a0, b) + _dot(a1, b) + _dot(a2, b)


def _rms(x, g):
    return x * lax.rsqrt(jnp.mean(x * x, axis=-1, keepdims=True) + EPS) * g


def _mod_row(tile, tm):
    return jnp.minimum((tile * tm) // LAT_SEQ, CTX_MOD_ROW)


def _ada_kernel(cs_ref, w_ref, b_ref, o_ref):
    cs = cs_ref[...]
    s = cs * jax.nn.sigmoid(cs)
    o_ref[...] = _dot(s.astype(BF), w_ref[...].astype(BF)) + b_ref[...]


def _ada_mods(cs, ada_w, ada_b):
    n_out = 9 * D
    tn = D
    out = pl.pallas_call(
        _ada_kernel,
        grid=(DEPTH, n_out // tn),
        in_specs=[
            pl.BlockSpec((MOD_ROWS, D), lambda l, n: (0, 0)),
            pl.BlockSpec((None, D, tn), lambda l, n: (l, 0, n)),
            pl.BlockSpec((None, 1, tn), lambda l, n: (l, 0, n)),
        ],
        out_specs=pl.BlockSpec((None, MOD_ROWS, tn), lambda l, n: (l, 0, n)),
        out_shape=jax.ShapeDtypeStruct((DEPTH, MOD_ROWS, n_out), F32),
        compiler_params=pltpu.CompilerParams(
            dimension_semantics=("arbitrary", "arbitrary"), vmem_limit_bytes=VMEM_LIMIT),
        name="ada_mods",
    )(cs, ada_w, ada_b.reshape(DEPTH, 1, n_out))
    return out.reshape(DEPTH, MOD_ROWS, 9, D)


def _bias_kernel(rpb_ref, o_ref):
    l = pl.program_id(0)
    hg = pl.program_id(1)
    qq = lax.broadcasted_iota(jnp.int32, (GRID_W, GRID_W), 0)
    kk = lax.broadcasted_iota(jnp.int32, (GRID_W, GRID_W), 1)
    col_off = jnp.clip(kk - qq + (WIN_COLS - 1), 0, N_COL_OFF - 1)
    col_start = jnp.clip(qq - WIN_COLS // 2, 0, GRID_W - WIN_COLS)
    visible = (kk >= col_start) & (kk < col_start + WIN_COLS)
    for i in range(HEADS_PER_GROUP):
        h = hg * HEADS_PER_GROUP + i
        for ro in range(N_ROW_OFF):
            base = ((l * NH + h) * N_ROW_OFF + ro) * N_COL_OFF
            t = jnp.zeros((GRID_W, GRID_W), F32)
            for d in range(N_COL_OFF):
                t = jnp.where(col_off == d, rpb_ref[base + d], t)
            t = jnp.where(visible, t, NEG).astype(o_ref.dtype)
            for a in range(WIN_ROWS):
                delta_idx = ro - a
                if 0 <= delta_idx < N_DELTA:
                    o_ref[delta_idx, i * GRID_W:(i + 1) * GRID_W, a * GRID_W:(a + 1) * GRID_W] = t


def _bias_table(rpb):
    return pl.pallas_call(
        _bias_kernel,
        grid=(DEPTH, N_HEAD_GROUPS),
        in_specs=[pl.BlockSpec(memory_space=pltpu.SMEM)],
        out_specs=pl.BlockSpec((None, None, N_DELTA, HEADS_PER_GROUP * GRID_W, WIN_ROWS * GRID_W),
                               lambda l, g: (l, g, 0, 0, 0)),
        out_shape=jax.ShapeDtypeStruct(
            (DEPTH, N_HEAD_GROUPS, N_DELTA, HEADS_PER_GROUP * GRID_W, WIN_ROWS * GRID_W), BF),
        compiler_params=pltpu.CompilerParams(
            dimension_semantics=("arbitrary", "arbitrary"), vmem_limit_bytes=VMEM_LIMIT),
        name="rpb_table",
    )(rpb.reshape(-1))


def _ffn_kernel(x_ref, mod_ref, npre_ref, npost_ref, wi_ref, wo_ref, o_ref, *, sub):
    mo = 0 if sub == 0 else 6
    ni = 0 if sub == 0 else 2
    x = x_ref[...]
    shift = mod_ref[mo:mo + 1, :]
    scale = mod_ref[mo + 1:mo + 2, :]
    gate = mod_ref[mo + 2:mo + 3, :]
    h = (_rms(x, npre_ref[ni:ni + 1, :]) * (1.0 + scale) + shift).astype(BF)
    acc = jnp.zeros(x.shape, F32)
    for j in range(N_FF_CHUNKS):
        g = _dot(h, wi_ref[:, j * FF_CHUNK:(j + 1) * FF_CHUNK])
        u = _dot(h, wi_ref[:, D_FF + j * FF_CHUNK:D_FF + (j + 1) * FF_CHUNK])
        a = (g * jax.nn.sigmoid(g) * u).astype(BF)
        acc = acc + _dot(a, wo_ref[j * FF_CHUNK:(j + 1) * FF_CHUNK, :])
    o_ref[...] = x + 0.5 * gate * _rms(acc, npost_ref[ni:ni + 1, :])


def _ffn(x, mods, npre, npost, wi, wo, l, sub, tm=512):
    n_tiles = N_TOK // tm
    return pl.pallas_call(
        functools.partial(_ffn_kernel, sub=sub),
        grid=(n_tiles,),
        in_specs=[
            pl.BlockSpec((tm, D), lambda t: (t, 0)),
            pl.BlockSpec((None, None, 9, D), lambda t: (l, _mod_row(t, tm), 0, 0)),
            pl.BlockSpec((None, 3, D), lambda t: (l, 0, 0)),
            pl.BlockSpec((None, 3, D), lambda t: (l, 0, 0)),
            pl.BlockSpec((None, None, D, 2 * D_FF), lambda t: (l, sub, 0, 0),
                         pipeline_mode=pl.Buffered(1)),
            pl.BlockSpec((None, None, D_FF, D), lambda t: (l, sub, 0, 0),
                         pipeline_mode=pl.Buffered(1)),
        ],
        out_specs=pl.BlockSpec((tm, D), lambda t: (t, 0)),
        out_shape=jax.ShapeDtypeStruct((N_TOK, D), F32),
        compiler_params=pltpu.CompilerParams(
            dimension_semantics=("arbitrary",), vmem_limit_bytes=VMEM_LIMIT),
        name=f"ffn_l{l}_s{sub}",
    )(x, mods, npre, npost, wi, wo)


def _proj_kernel(x_ref, mod_ref, npre_ref, w_ref, qkv_ref, f_ref, g_ref, kv_ref):
    x = x_ref[...]
    shift = mod_ref[3:4, :]
    scale = mod_ref[4:5, :]
    h = (_rms(x, npre_ref[1:2, :]) * (1.0 + scale) + shift).astype(BF)
    z = _dot(h, w_ref[...])
    qkv_ref[...] = z[:, :3 * WA].astype(BF)
    f_ref[...] = z[:, 3 * WA:3 * WA + WB].astype(BF)
    g_ref[...] = z[:, 3 * WA + WB:].astype(BF)
    kv_ref[...] = z[:, WA:3 * WA]


def _proj(x, mods, npre, w_in, l, tm=512):
    n_tiles = N_TOK // tm
    n_lat_tiles = N_LAT // tm
    return pl.pallas_call(
        _proj_kernel,
        grid=(n_tiles,),
        in_specs=[
            pl.BlockSpec((tm, D), lambda t: (t, 0)),
            pl.BlockSpec((None, None, 9, D), lambda t: (l, _mod_row(t, tm), 0, 0)),
            pl.BlockSpec((None, 3, D), lambda t: (l, 0, 0)),
            pl.BlockSpec((None, D, PROJ_W), lambda t: (l, 0, 0), pipeline_mode=pl.Buffered(1)),
        ],
        out_specs=[
            pl.BlockSpec((tm, 3 * WA), lambda t: (t, 0)),
            pl.BlockSpec((tm, WB), lambda t: (t, 0)),
            pl.BlockSpec((tm, 2 * WC), lambda t: (t, 0)),
            pl.BlockSpec((tm, 2 * WA), lambda t: (jnp.maximum(t - n_lat_tiles, 0), 0)),
        ],
        out_shape=[
            jax.ShapeDtypeStruct((N_TOK, 3 * WA), BF),
            jax.ShapeDtypeStruct((N_TOK, WB), BF),
            jax.ShapeDtypeStruct((N_TOK, 2 * WC), BF),
            jax.ShapeDtypeStruct((N_CTX, 2 * WA), F32),
        ],
        compiler_params=pltpu.CompilerParams(
            dimension_semantics=("arbitrary",), vmem_limit_bytes=VMEM_LIMIT),
        name=f"proj_l{l}",
    )(x, mods, npre, w_in)


def _fourier(f, fc_ref, fn_ref):
    xcs = _dot(f, fc_ref[...])
    stacked = jnp.concatenate([xcs[:, :WB], xcs[:, WB:]], axis=0)
    return _dot(fn_ref[...], stacked.astype(BF))


def _gmlp(g, gn_ref, ws_ref, gb_ref, avg_ref):
    gl = jax.nn.gelu(g.astype(F32), approximate=True)
    u = gl[:, :WC]
    v = gl[:, WC:]
    avg = avg_ref[...]
    mu = _split3_dot(v, avg)
    d = v - mu
    var = _split3_dot(d * d, avg)
    vn = (d * lax.rsqrt(var + EPS) * gn_ref[...]).astype(BF)
    lane_group = lax.broadcasted_iota(jnp.int32, (CHUNK, WC), 1) // GC
    outs = []
    for c in range(g.shape[0] // CHUNK):
        vc = vn[c * CHUNK:(c + 1) * CHUNK, :]
        sp = _dot(ws_ref[0], vc)
        for grp in range(1, NGRP):
            sp = jnp.where(lane_group == grp, _dot(ws_ref[grp], vc), sp)
        outs.append(u[c * CHUNK:(c + 1) * CHUNK, :] * (sp + gb_ref[...]))
    return jnp.concatenate(outs, axis=0)


def _mix_out(x, o, gate, npost, wo_ref):
    y = _dot(o, wo_ref[...])
    return x + gate * _rms(y, npost)


def _ctx_mixer_kernel(x_ref, qkv_ref, f_ref, g_ref, mod_ref, npost_ref, wo_ref, fc_ref, fn_ref,
                      gn_ref, ws_ref, gb_ref, avg_ref, o_ref, mix_ref):
    for hp in range(NH // 2):
        pair = []
        for h in (2 * hp, 2 * hp + 1):
            q = qkv_ref[:, h * HD:(h + 1) * HD]
            k = qkv_ref[:, WA + h * HD:WA + (h + 1) * HD]
            v = qkv_ref[:, 2 * WA + h * HD:2 * WA + (h + 1) * HD]
            s = _dot_nt(q, k) * (HD ** -0.5)
            m = jnp.max(s, axis=-1, keepdims=True)
            p = jnp.exp(s - m)
            den = jnp.sum(p, axis=-1, keepdims=True)
            pair.append(_dot(p.astype(BF), v) / den)
        mix_ref[:, 2 * hp * HD:(2 * hp + 2) * HD] = jnp.concatenate(pair, axis=-1).astype(BF)
    mix_ref[:, WA:WA + WB] = _fourier(f_ref[...], fc_ref, fn_ref).astype(BF)
    mix_ref[:, WA + WB:] = _gmlp(g_ref[...], gn_ref, ws_ref, gb_ref, avg_ref).astype(BF)
    o_ref[...] = _mix_out(x_ref[...], mix_ref[...], mod_ref[5:6, :], npost_ref[1:2, :], wo_ref)


def _const_spec(shape):
    nd = len(shape)
    return pl.BlockSpec(shape, lambda *_: (0,) * nd)


def _ctx_mixer(x, qkv, f, g, mods, npost, w_out, fc, fn, gn, ws, gb, avg, l):
    off = N_LAT // CTX_SEQ
    row = lambda b: (b + off, 0)
    return pl.pallas_call(
        _ctx_mixer_kernel,
        grid=(N_CTX_B,),
        in_specs=[
            pl.BlockSpec((CTX_SEQ, D), row),
            pl.BlockSpec((CTX_SEQ, 3 * WA), row),
            pl.BlockSpec((CTX_SEQ, WB), row),
            pl.BlockSpec((CTX_SEQ, 2 * WC), row),
            pl.BlockSpec((None, None, 9, D), lambda b: (l, CTX_MOD_ROW, 0, 0)),
            pl.BlockSpec((None, 3, D), lambda b: (l, 0, 0)),
            pl.BlockSpec((None, D, D), lambda b: (l, 0, 0)),
            _const_spec(fc.shape),
            _const_spec(fn.shape),
            pl.BlockSpec((None, 1, WC), lambda b: (l, 0, 0)),
            pl.BlockSpec((None, NGRP, CHUNK, CHUNK), lambda b: (l, 0, 0, 0)),
            pl.BlockSpec((None, CHUNK, WC), lambda b: (l, 0, 0)),
            _const_spec(avg.shape),
        ],
        out_specs=pl.BlockSpec((CTX_SEQ, D), row),
        out_shape=jax.ShapeDtypeStruct((N_TOK, D), F32),
        scratch_shapes=[pltpu.VMEM((CTX_SEQ, D), BF)],
        input_output_aliases={0: 0},
        compiler_params=pltpu.CompilerParams(
            dimension_semantics=("arbitrary",), vmem_limit_bytes=VMEM_LIMIT),
        name=f"ctx_mixer_l{l}",
    )(x, qkv, f, g, mods, npost, w_out, fc, fn, gn, ws, gb, avg)


def _window_start(r):
    return min(max(r - WIN_ROWS // 2, 0), GRID_ROWS - WIN_ROWS)


def _nbr_attn_kernel(qkv_ref, ck_ref, cv_ref, bias_ref, o_ref):
    gw = HEADS_PER_GROUP * HD
    row_head = lax.broadcasted_iota(jnp.int32, (HEADS_PER_GROUP * GRID_W, gw), 0) // GRID_W
    lane_head = lax.broadcasted_iota(jnp.int32, (HEADS_PER_GROUP * GRID_W, gw), 1) // HD
    own = row_head == lane_head
    for hg in range(N_HEAD_GROUPS):
        ck = ck_ref[:, hg * gw:(hg + 1) * gw].astype(BF)
        cv = cv_ref[:, hg * gw:(hg + 1) * gw].astype(BF)
        for r in range(GRID_ROWS):
            ws = _window_start(r)
            q = qkv_ref[r * GRID_W:(r + 1) * GRID_W, hg * gw:(hg + 1) * gw]
            qs = jnp.where(own, jnp.concatenate([q] * HEADS_PER_GROUP, axis=0), jnp.zeros((), BF))
            kw = qkv_ref[ws * GRID_W:(ws + WIN_ROWS) * GRID_W, WA + hg * gw:WA + (hg + 1) * gw]
            vw = qkv_ref[ws * GRID_W:(ws + WIN_ROWS) * GRID_W, 2 * WA + hg * gw:2 * WA + (hg + 1) * gw]
            s_loc = _dot_nt(qs, kw) * (HD ** -0.5) + bias_ref[hg, ws - r + N_DELTA - 1].astype(F32)
            s_ctx = _dot_nt(qs, ck) * (HD ** -0.5)
            m = jnp.maximum(jnp.max(s_loc, axis=-1, keepdims=True),
                            jnp.max(s_ctx, axis=-1, keepdims=True))
            p_loc = jnp.exp(s_loc - m)
            p_ctx = jnp.exp(s_ctx - m)
            den = jnp.sum(p_loc, axis=-1, keepdims=True) + jnp.sum(p_ctx, axis=-1, keepdims=True)
            o = (_dot(p_loc.astype(BF), vw) + _dot(p_ctx.astype(BF), cv)) / den
            o = jnp.where(own, o, 0.0)
            o = (o[0:GRID_W] + o[GRID_W:2 * GRID_W]) + (o[2 * GRID_W:3 * GRID_W] + o[3 * GRID_W:])
            o_ref[r * GRID_W:(r + 1) * GRID_W, hg * gw:(hg + 1) * gw] = o.astype(BF)


def _nbr_attn(qkv, cache_k, cache_v, bias, l):
    return pl.pallas_call(
        _nbr_attn_kernel,
        grid=(N_LAT_B,),
        in_specs=[
            pl.BlockSpec((LAT_SEQ, 3 * WA), lambda b: (b, 0)),
            pl.BlockSpec((None, None, PAST, WA), lambda b: (b, l, 0, 0)),
            pl.BlockSpec((None, None, PAST, WA), lambda b: (b, l, 0, 0)),
            pl.BlockSpec((None,) + bias.shape[1:], lambda b: (l, 0, 0, 0, 0)),
        ],
        out_specs=pl.BlockSpec((LAT_SEQ, WA), lambda b: (b, 0)),
        out_shape=jax.ShapeDtypeStruct((N_LAT, WA), BF),
        compiler_params=pltpu.CompilerParams(
            dimension_semantics=("arbitrary",), vmem_limit_bytes=VMEM_LIMIT),
        name=f"nbr_attn_l{l}",
    )(qkv, cache_k, cache_v, bias)


def _lat_mixer_kernel(x_ref, oa_ref, f_ref, g_ref, mod_ref, npost_ref, wo_ref, fc_ref, fn_ref,
                      gn_ref, ws_ref, gb_ref, avg_ref, o_ref, mix_ref):
    mix_ref[:, :WA] = oa_ref[...]
    mix_ref[:, WA:WA + WB] = _fourier(f_ref[...], fc_ref, fn_ref).astype(BF)
    mix_ref[:, WA + WB:] = _gmlp(g_ref[...], gn_ref, ws_ref, gb_ref, avg_ref).astype(BF)
    o_ref[...] = _mix_out(x_ref[...], mix_ref[...], mod_ref[5:6, :], npost_ref[1:2, :], wo_ref)


def _lat_mixer(x, oa, f, g, mods, npost, w_out, fc, fn, gn, ws, gb, avg, l):
    row = lambda b: (b, 0)
    return pl.pallas_call(
        _lat_mixer_kernel,
        grid=(N_LAT_B,),
        in_specs=[
            pl.BlockSpec((LAT_SEQ, D), row),
            pl.BlockSpec((LAT_SEQ, WA), row),
            pl.BlockSpec((LAT_SEQ, WB), row),
            pl.BlockSpec((LAT_SEQ, 2 * WC), row),
            pl.BlockSpec((None, None, 9, D), lambda b: (l, b, 0, 0)),
            pl.BlockSpec((None, 3, D), lambda b: (l, 0, 0)),
            pl.BlockSpec((None, D, D), lambda b: (l, 0, 0), pipeline_mode=pl.Buffered(1)),
            _const_spec(fc.shape),
            pl.BlockSpec(fn.shape, lambda b: (0, 0), pipeline_mode=pl.Buffered(1)),
            pl.BlockSpec((None, 1, WC), lambda b: (l, 0, 0)),
            pl.BlockSpec((None, NGRP, CHUNK, CHUNK), lambda b: (l, 0, 0, 0)),
            pl.BlockSpec((None, CHUNK, WC), lambda b: (l, 0, 0)),
            _const_spec(avg.shape),
        ],
        out_specs=pl.BlockSpec((LAT_SEQ, D), row),
        out_shape=jax.ShapeDtypeStruct((N_TOK, D), F32),
        scratch_shapes=[pltpu.VMEM((LAT_SEQ, D), BF)],
        input_output_aliases={0: 0},
        compiler_params=pltpu.CompilerParams(
            dimension_semantics=("arbitrary",), vmem_limit_bytes=VMEM_LIMIT),
        name=f"lat_mixer_l{l}",
    )(x, oa, f, g, mods, npost, w_out, fc, fn, gn, ws, gb, avg)


def _dft_tables(n):
    idx = np.arange(n)
    ang = 2.0 * np.pi * ((idx[:, None] * idx[None, :]) % n) / n
    return (np.concatenate([np.cos(ang), -np.sin(ang)], axis=1) / np.sqrt(n)).astype(np.float32)


def _channel_dft_table():
    idx = np.arange(GC)
    ang = 2.0 * np.pi * ((idx[:, None] * idx[None, :]) % GC) / GC
    eye = np.eye(NGRP)
    return (np.concatenate([np.kron(eye, np.cos(ang)), np.kron(eye, np.sin(ang))], axis=1)
            / np.sqrt(GC)).astype(np.float32)


def _group_average_table():
    return np.kron(np.eye(NGRP), np.full((GC, GC), 1.0 / GC)).astype(np.float32)


def kernel(x_prompt, x_sample, cache_k, cache_v, c, c_ctx, ada_w, ada_b, norm_pre, norm_post,
           ffn_w_in, ffn_w_out, w_in, w_out, rpb, gmlp_norm, gmlp_w, gmlp_b):
    x = jnp.concatenate([x_sample.reshape(N_LAT, D), x_prompt.reshape(N_CTX, D)], axis=0)
    cs = jnp.concatenate([c, c_ctx[None, :], jnp.zeros((MOD_ROWS - N_LAT_B - 1, D), F32)], axis=0)
    mods = _ada_mods(cs, ada_w, ada_b)
    bias = _bias_table(rpb)

    ffn_wi = ffn_w_in.astype(BF)
    ffn_wo = ffn_w_out.astype(BF)
    w_in_b = w_in.astype(BF)
    w_out_b = w_out.astype(BF)
    ws_b = gmlp_w.astype(BF)
    gn = gmlp_norm.reshape(DEPTH, 1, WC)
    gb = jnp.repeat(jnp.transpose(gmlp_b, (0, 2, 1)), GC, axis=2)
    ck = cache_k.reshape(N_LAT_B, DEPTH, PAST, WA)
    cv = cache_v.reshape(N_LAT_B, DEPTH, PAST, WA)

    fc = jnp.asarray(_channel_dft_table()).astype(BF)
    fn_ctx = jnp.asarray(_dft_tables(CTX_SEQ)).astype(BF)
    fn_lat = jnp.asarray(_dft_tables(LAT_SEQ)).astype(BF)
    avg = jnp.asarray(_group_average_table()).astype(BF)

    kvs = []
    for l in range(DEPTH):
        x = _ffn(x, mods, norm_pre, norm_post, ffn_wi, ffn_wo, l, 0)
        qkv, f, g, kv = _proj(x, mods, norm_pre, w_in_b, l)
        kvs.append(kv)
        oa = _nbr_attn(qkv, ck, cv, bias, l)
        x = _ctx_mixer(x, qkv, f, g, mods, norm_post, w_out_b, fc, fn_ctx, gn, ws_b, gb, avg, l)
        x = _lat_mixer(x, oa, f, g, mods, norm_post, w_out_b, fc, fn_lat, gn, ws_b, gb, avg, l)
        x = _ffn(x, mods, norm_pre, norm_post, ffn_wi, ffn_wo, l, 1)

    kv = jnp.stack(kvs, axis=0).reshape(DEPTH, N_CTX_B, CTX_SEQ, 2, NH, HD)
    new_k = jnp.transpose(kv[:, :, :, 0], (1, 0, 2, 3, 4))
    new_v = jnp.transpose(kv[:, :, :, 1], (1, 0, 2, 3, 4))
    y_sample = x[:N_LAT].reshape(N_LAT_B, LAT_SEQ, D)
    y_prompt = x[N_LAT:].reshape(N_CTX_B, CTX_SEQ, D)
    return (y_prompt, y_sample, new_k, new_v)
```

```python
import functools

import numpy as np
import jax
import jax.numpy as jnp
from jax import lax
from jax.experimental import pallas as pl
from jax.experimental.pallas import tpu as pltpu

D = 1024
DEPTH = 4
N_CTX_B = 16
CTX_SEQ = 256
N_LAT_B = 4
LAT_SEQ = 1024
N_LAT = N_LAT_B * LAT_SEQ
N_CTX = N_CTX_B * CTX_SEQ
N_TOK = N_LAT + N_CTX
PAST = 256
GRID_W = 64
GRID_ROWS = LAT_SEQ // GRID_W
HD = 64
NH = 8
WA = 512
WB = 256
WC = 256
NGRP = 4
GC = 64
CHUNK = 128
PROJ_W = 3 * WA + WB + 2 * WC
D_FF = 2816
FF_CHUNK = 256
N_FF_CHUNKS = D_FF // FF_CHUNK
WIN_ROWS = 8
WIN_COLS = 16
N_ROW_OFF = 2 * WIN_ROWS - 1
N_COL_OFF = 2 * WIN_COLS - 1
N_DELTA = 8
EPS = 1e-6
NEG = -1e30
MOD_ROWS = 8
CTX_MOD_ROW = 4
HEADS_PER_GROUP = 4
N_HEAD_GROUPS = NH // HEADS_PER_GROUP

BF = jnp.bfloat16
F32 = jnp.float32

VMEM_LIMIT = 56 * 1024 * 1024


def _dot(a, b):
    return jnp.dot(a, b, preferred_element_type=F32)


def _dot_nt(a, b):
    return lax.dot_general(a, b, (((1,), (1,)), ((), ())), preferred_element_type=F32)


def _split3_dot(a, b):
    a0 = a.astype(BF)
    r1 = a - a0.astype(F32)
    a1 = r1.astype(BF)
    a2 = (r1 - a1.astype(F32)).astype(BF)
    return _dot(a0, b) + _dot(a1, b) + _dot(a2, b)


def _rms(x, g):
    return x * lax.rsqrt(jnp.mean(x * x, axis=-1, keepdims=True) + EPS) * g


def _mod_row(tile, tm):
    return jnp.minimum((tile * tm) // LAT_SEQ, CTX_MOD_ROW)


def _ada_kernel(cs_ref, w_ref, b_ref, o_ref):
    cs = cs_ref[...]
    s = cs * jax.nn.sigmoid(cs)
    o_ref[...] = _dot(s.astype(BF), w_ref[...].astype(BF)) + b_ref[...]


def _ada_mods(cs, ada_w, ada_b):
    n_out = 9 * D
    tn = D
    out = pl.pallas_call(
        _ada_kernel,
        grid=(DEPTH, n_out // tn),
        in_specs=[
            pl.BlockSpec((MOD_ROWS, D), lambda l, n: (0, 0)),
            pl.BlockSpec((None, D, tn), lambda l, n: (l, 0, n)),
            pl.BlockSpec((None, 1, tn), lambda l, n: (l, 0, n)),
        ],
        out_specs=pl.BlockSpec((None, MOD_ROWS, tn), lambda l, n: (l, 0, n)),
        out_shape=jax.ShapeDtypeStruct((DEPTH, MOD_ROWS, n_out), F32),
        compiler_params=pltpu.CompilerParams(
            dimension_semantics=("arbitrary", "arbitrary"), vmem_limit_bytes=VMEM_LIMIT),
        name="ada_mods",
    )(cs, ada_w, ada_b.reshape(DEPTH, 1, n_out))
    return out.reshape(DEPTH, MOD_ROWS, 9, D)


RPB_LANES = 128
RPB_HALF = RPB_LANES // 2


def _bias_kernel(rpb_ref, o_ref):
    qq = lax.broadcasted_iota(jnp.int32, (GRID_W, RPB_LANES), 0)
    lane = lax.broadcasted_iota(jnp.int32, (GRID_W, RPB_LANES), 1)
    second = lane >= GRID_W
    kk = jnp.where(second, lane - GRID_W, lane)
    col_off = jnp.clip(kk - qq + (WIN_COLS - 1), 0, N_COL_OFF - 1)
    gather_idx = jnp.where(second, col_off + RPB_HALF, col_off)
    col_start = jnp.clip(qq - WIN_COLS // 2, 0, GRID_W - WIN_COLS)
    visible = (kk >= col_start) & (kk < col_start + WIN_COLS)
    low_half = lax.broadcasted_iota(jnp.int32, (1, RPB_LANES), 1) < RPB_HALF
    for i in range(HEADS_PER_GROUP):
        for ro in range(N_ROW_OFF - 1):
            row0 = rpb_ref[i, ro:ro + 1, :]
            row1 = pltpu.roll(rpb_ref[i, ro + 1:ro + 2, :], RPB_HALF, 1)
            both = jnp.broadcast_to(jnp.where(low_half, row0, row1), (GRID_W, RPB_LANES))
            t = jnp.take_along_axis(both, gather_idx, axis=1)
            t = jnp.where(visible, t, NEG).astype(o_ref.dtype)
            for a in range(0, WIN_ROWS, 2):
                delta_idx = ro - a
                if 0 <= delta_idx < N_DELTA:
                    o_ref[delta_idx, i * GRID_W:(i + 1) * GRID_W, a * GRID_W:(a + 2) * GRID_W] = t


def _bias_table(rpb):
    rpb_padded = jnp.pad(rpb, ((0, 0), (0, 0), (0, 0), (0, RPB_LANES - N_COL_OFF)))
    return pl.pallas_call(
        _bias_kernel,
        grid=(DEPTH, N_HEAD_GROUPS),
        in_specs=[pl.BlockSpec((None, HEADS_PER_GROUP, N_ROW_OFF, RPB_LANES),
                               lambda l, g: (l, g, 0, 0))],
        out_specs=pl.BlockSpec((None, None, N_DELTA, HEADS_PER_GROUP * GRID_W, WIN_ROWS * GRID_W),
                               lambda l, g: (l, g, 0, 0, 0)),
        out_shape=jax.ShapeDtypeStruct(
            (DEPTH, N_HEAD_GROUPS, N_DELTA, HEADS_PER_GROUP * GRID_W, WIN_ROWS * GRID_W), BF),
        compiler_params=pltpu.CompilerParams(
            dimension_semantics=("arbitrary", "arbitrary"), vmem_limit_bytes=VMEM_LIMIT),
        name="rpb_table",
    )(rpb_padded)


def _lat_tile(t, n_lat_tiles):
    return jnp.minimum(t, n_lat_tiles - 1)


def _ctx_tile(t, n_lat_tiles):
    return jnp.maximum(t - n_lat_tiles, 0)


def _ffn_kernel(xl_ref, xc_ref, mod_ref, npre_ref, npost_ref, wi_ref, wo_ref, ol_ref, oc_ref, *,
                sub, n_lat_tiles):
    mo = 0 if sub == 0 else 6
    ni = 0 if sub == 0 else 2
    is_lat = pl.program_id(0) < n_lat_tiles
    x = jnp.where(is_lat, xl_ref[...], xc_ref[...])
    shift = mod_ref[mo:mo + 1, :]
    scale = mod_ref[mo + 1:mo + 2, :]
    gate = mod_ref[mo + 2:mo + 3, :]
    h = (_rms(x, npre_ref[ni:ni + 1, :]) * (1.0 + scale) + shift).astype(BF)
    acc = jnp.zeros(x.shape, F32)
    for j in range(N_FF_CHUNKS):
        g = _dot(h, wi_ref[:, j * FF_CHUNK:(j + 1) * FF_CHUNK])
        u = _dot(h, wi_ref[:, D_FF + j * FF_CHUNK:D_FF + (j + 1) * FF_CHUNK])
        a = (g * jax.nn.sigmoid(g) * u).astype(BF)
        acc = acc + _dot(a, wo_ref[j * FF_CHUNK:(j + 1) * FF_CHUNK, :])
    out = x + 0.5 * gate * _rms(acc, npost_ref[ni:ni + 1, :])

    @pl.when(is_lat)
    def _():
        ol_ref[...] = out

    @pl.when(jnp.logical_not(is_lat))
    def _():
        oc_ref[...] = out


def _ffn(xl, xc, mods, npre, npost, wi, wo, l, sub, tm=512):
    n_tiles = N_TOK // tm
    nl = N_LAT // tm
    lat_idx = lambda t: (_lat_tile(t, nl), 0)
    ctx_idx = lambda t: (_ctx_tile(t, nl), 0)
    return pl.pallas_call(
        functools.partial(_ffn_kernel, sub=sub, n_lat_tiles=nl),
        grid=(n_tiles,),
        in_specs=[
            pl.BlockSpec((tm, D), lat_idx),
            pl.BlockSpec((tm, D), ctx_idx),
            pl.BlockSpec((None, None, 9, D), lambda t: (l, _mod_row(t, tm), 0, 0)),
            pl.BlockSpec((None, 3, D), lambda t: (l, 0, 0)),
            pl.BlockSpec((None, 3, D), lambda t: (l, 0, 0)),
            pl.BlockSpec((None, None, D, 2 * D_FF), lambda t: (l, sub, 0, 0),
                         pipeline_mode=pl.Buffered(1)),
            pl.BlockSpec((None, None, D_FF, D), lambda t: (l, sub, 0, 0),
                         pipeline_mode=pl.Buffered(1)),
        ],
        out_specs=[pl.BlockSpec((tm, D), lat_idx), pl.BlockSpec((tm, D), ctx_idx)],
        out_shape=[jax.ShapeDtypeStruct((N_LAT, D), F32), jax.ShapeDtypeStruct((N_CTX, D), F32)],
        compiler_params=pltpu.CompilerParams(
            dimension_semantics=("arbitrary",), vmem_limit_bytes=VMEM_LIMIT),
        name=f"ffn_l{l}_s{sub}",
    )(xl, xc, mods, npre, npost, wi, wo)


def _proj_kernel(xl_ref, xc_ref, mod_ref, npre_ref, w_ref, *rest, n_lat_tiles, n_carried):
    qkv_ref, f_ref, g_ref, k_ref, v_ref, wb_ref = rest[n_carried:]
    t = pl.program_id(0)

    @pl.when(t == 0)
    def _():
        wb_ref[...] = w_ref[...].astype(BF)

    x = jnp.where(t < n_lat_tiles, xl_ref[...], xc_ref[...])
    shift = mod_ref[3:4, :]
    scale = mod_ref[4:5, :]
    h = (_rms(x, npre_ref[1:2, :]) * (1.0 + scale) + shift).astype(BF)
    z = _dot(h, wb_ref[...])
    qkv_ref[...] = z[:, :3 * WA].astype(BF)
    f_ref[...] = z[:, 3 * WA:3 * WA + WB].astype(BF)
    g_ref[...] = z[:, 3 * WA + WB:].astype(BF)
    k_ref[...] = z[:, WA:2 * WA].reshape(k_ref.shape)
    v_ref[...] = z[:, 2 * WA:3 * WA].reshape(v_ref.shape)


def _proj(xl, xc, mods, npre, w_in, carried, l, tm=512):
    n_tiles = N_TOK // tm
    nl = N_LAT // tm
    bpt = tm // CTX_SEQ
    kv_shape = jax.ShapeDtypeStruct((N_CTX_B, DEPTH, CTX_SEQ, WA), F32)
    kv_spec = pl.BlockSpec((bpt, None, CTX_SEQ, WA), lambda t: (_ctx_tile(t, nl), l, 0, 0))
    n_fixed = 5
    return pl.pallas_call(
        functools.partial(_proj_kernel, n_lat_tiles=nl, n_carried=len(carried)),
        grid=(n_tiles,),
        in_specs=[
            pl.BlockSpec((tm, D), lambda t: (_lat_tile(t, nl), 0)),
            pl.BlockSpec((tm, D), lambda t: (_ctx_tile(t, nl), 0)),
            pl.BlockSpec((None, None, 9, D), lambda t: (l, _mod_row(t, tm), 0, 0)),
            pl.BlockSpec((None, 3, D), lambda t: (l, 0, 0)),
            pl.BlockSpec((None, D, PROJ_W), lambda t: (l, 0, 0), pipeline_mode=pl.Buffered(1)),
        ] + [pl.BlockSpec(memory_space=pl.ANY)] * len(carried),
        out_specs=[
            pl.BlockSpec((tm, 3 * WA), lambda t: (t, 0)),
            pl.BlockSpec((tm, WB), lambda t: (t, 0)),
            pl.BlockSpec((tm, 2 * WC), lambda t: (t, 0)),
            kv_spec,
            kv_spec,
        ],
        out_shape=[
            jax.ShapeDtypeStruct((N_TOK, 3 * WA), BF),
            jax.ShapeDtypeStruct((N_TOK, WB), BF),
            jax.ShapeDtypeStruct((N_TOK, 2 * WC), BF),
            kv_shape,
            kv_shape,
        ],
        scratch_shapes=[pltpu.VMEM((D, PROJ_W), BF)],
        input_output_aliases={n_fixed + i: 3 + i for i in range(len(carried))},
        compiler_params=pltpu.CompilerParams(
            dimension_semantics=("arbitrary",), vmem_limit_bytes=VMEM_LIMIT),
        name=f"proj_l{l}",
    )(xl, xc, mods, npre, w_in, *carried)


def _fourier(f, fc_ref, fn_ref):
    xcs = _dot(f, fc_ref[...])
    stacked = jnp.concatenate([xcs[:, :WB], xcs[:, WB:]], axis=0)
    return _dot(fn_ref[...], stacked.astype(BF))


def _gmlp(g, gn_ref, ws_ref, gb_ref, avg_ref):
    gl = jax.nn.gelu(g.astype(F32), approximate=True)
    u = gl[:, :WC]
    v = gl[:, WC:]
    avg = avg_ref[...]
    mu = _split3_dot(v, avg)
    d = v - mu
    var = _split3_dot(d * d, avg)
    vn = (d * lax.rsqrt(var + EPS) * gn_ref[...]).astype(BF)
    lane_group = lax.broadcasted_iota(jnp.int32, (CHUNK, WC), 1) // GC
    outs = []
    for c in range(g.shape[0] // CHUNK):
        vc = vn[c * CHUNK:(c + 1) * CHUNK, :]
        sp = _dot(ws_ref[0].astype(BF), vc)
        for grp in range(1, NGRP):
            sp = jnp.where(lane_group == grp, _dot(ws_ref[grp].astype(BF), vc), sp)
        outs.append(u[c * CHUNK:(c + 1) * CHUNK, :] * (sp + gb_ref[...]))
    return jnp.concatenate(outs, axis=0)


def _cast_weight_once(w_ref, wb_ref):
    @pl.when(pl.program_id(0) == 0)
    def _():
        wb_ref[...] = w_ref[...].astype(BF)


def _mix_out(x, o, gate, npost, wb_ref):
    y = _dot(o, wb_ref[...])
    return x + gate * _rms(y, npost)


def _ctx_mixer_kernel(x_ref, qkv_ref, f_ref, g_ref, mod_ref, npost_ref, wo_ref, fc_ref, fn_ref,
                      gn_ref, ws_ref, gb_ref, avg_ref, o_ref, mix_ref, wb_ref):
    _cast_weight_once(wo_ref, wb_ref)
    for hp in range(NH // 2):
        pair = []
        for h in (2 * hp, 2 * hp + 1):
            q = qkv_ref[:, h * HD:(h + 1) * HD]
            k = qkv_ref[:, WA + h * HD:WA + (h + 1) * HD]
            v = qkv_ref[:, 2 * WA + h * HD:2 * WA + (h + 1) * HD]
            s = _dot_nt(q, k) * (HD ** -0.5)
            m = jnp.max(s, axis=-1, keepdims=True)
            p = jnp.exp(s - m)
            den = jnp.sum(p, axis=-1, keepdims=True)
            pair.append(_dot(p.astype(BF), v) / den)
        mix_ref[:, 2 * hp * HD:(2 * hp + 2) * HD] = jnp.concatenate(pair, axis=-1).astype(BF)
    mix_ref[:, WA:WA + WB] = _fourier(f_ref[...], fc_ref, fn_ref).astype(BF)
    mix_ref[:, WA + WB:] = _gmlp(g_ref[...], gn_ref, ws_ref, gb_ref, avg_ref).astype(BF)
    o_ref[...] = _mix_out(x_ref[...], mix_ref[...], mod_ref[5:6, :], npost_ref[1:2, :], wb_ref)


def _const_spec(shape):
    nd = len(shape)
    return pl.BlockSpec(shape, lambda *_: (0,) * nd)


def _ctx_mixer(x, qkv, f, g, mods, npost, w_out, fc, fn, gn, ws, gb, avg, l):
    off = N_LAT // CTX_SEQ
    row = lambda b: (b + off, 0)
    return pl.pallas_call(
        _ctx_mixer_kernel,
        grid=(N_CTX_B,),
        in_specs=[
            pl.BlockSpec((CTX_SEQ, D), lambda b: (b, 0)),
            pl.BlockSpec((CTX_SEQ, 3 * WA), row),
            pl.BlockSpec((CTX_SEQ, WB), row),
            pl.BlockSpec((CTX_SEQ, 2 * WC), row),
            pl.BlockSpec((None, None, 9, D), lambda b: (l, CTX_MOD_ROW, 0, 0)),
            pl.BlockSpec((None, 3, D), lambda b: (l, 0, 0)),
            pl.BlockSpec((None, D, D), lambda b: (l, 0, 0), pipeline_mode=pl.Buffered(1)),
            _const_spec(fc.shape),
            _const_spec(fn.shape),
            pl.BlockSpec((None, 1, WC), lambda b: (l, 0, 0)),
            pl.BlockSpec((None, NGRP, CHUNK, CHUNK), lambda b: (l, 0, 0, 0)),
            pl.BlockSpec((None, CHUNK, WC), lambda b: (l, 0, 0)),
            _const_spec(avg.shape),
        ],
        out_specs=pl.BlockSpec((CTX_SEQ, D), lambda b: (b, 0)),
        out_shape=jax.ShapeDtypeStruct((N_CTX, D), F32),
        scratch_shapes=[pltpu.VMEM((CTX_SEQ, D), BF), pltpu.VMEM((D, D), BF)],
        compiler_params=pltpu.CompilerParams(
            dimension_semantics=("arbitrary",), vmem_limit_bytes=VMEM_LIMIT),
        name=f"ctx_mixer_l{l}",
    )(x, qkv, f, g, mods, npost, w_out, fc, fn, gn, ws, gb, avg)


def _window_start(r):
    return min(max(r - WIN_ROWS // 2, 0), GRID_ROWS - WIN_ROWS)


def _nbr_attn_kernel(qkv_ref, ck_ref, cv_ref, bias_ref, o_ref):
    gw = HEADS_PER_GROUP * HD
    row_head = lax.broadcasted_iota(jnp.int32, (HEADS_PER_GROUP * GRID_W, gw), 0) // GRID_W
    lane_head = lax.broadcasted_iota(jnp.int32, (HEADS_PER_GROUP * GRID_W, gw), 1) // HD
    own = row_head == lane_head
    for hg in range(N_HEAD_GROUPS):
        ck = ck_ref[:, hg * gw:(hg + 1) * gw].astype(BF)
        cv = cv_ref[:, hg * gw:(hg + 1) * gw].astype(BF)
        for r in range(GRID_ROWS):
            ws = _window_start(r)
            q = qkv_ref[r * GRID_W:(r + 1) * GRID_W, hg * gw:(hg + 1) * gw]
            qs = jnp.where(own, jnp.concatenate([q] * HEADS_PER_GROUP, axis=0), jnp.zeros((), BF))
            kw = qkv_ref[ws * GRID_W:(ws + WIN_ROWS) * GRID_W, WA + hg * gw:WA + (hg + 1) * gw]
            vw = qkv_ref[ws * GRID_W:(ws + WIN_ROWS) * GRID_W, 2 * WA + hg * gw:2 * WA + (hg + 1) * gw]
            s_loc = _dot_nt(qs, kw) * (HD ** -0.5) + bias_ref[hg, ws - r + N_DELTA - 1].astype(F32)
            s_ctx = _dot_nt(qs, ck) * (HD ** -0.5)
            m = jnp.maximum(jnp.max(s_loc, axis=-1, keepdims=True),
                            jnp.max(s_ctx, axis=-1, keepdims=True))
            p_loc = jnp.exp(s_loc - m)
            p_ctx = jnp.exp(s_ctx - m)
            den = jnp.sum(p_loc, axis=-1, keepdims=True) + jnp.sum(p_ctx, axis=-1, keepdims=True)
            o = (_dot(p_loc.astype(BF), vw) + _dot(p_ctx.astype(BF), cv)) / den
            o = jnp.where(own, o, 0.0)
            o = (o[0:GRID_W] + o[GRID_W:2 * GRID_W]) + (o[2 * GRID_W:3 * GRID_W] + o[3 * GRID_W:])
            o_ref[r * GRID_W:(r + 1) * GRID_W, hg * gw:(hg + 1) * gw] = o.astype(BF)


def _nbr_attn(qkv, cache_k, cache_v, bias, l):
    return pl.pallas_call(
        _nbr_attn_kernel,
        grid=(N_LAT_B,),
        in_specs=[
            pl.BlockSpec((LAT_SEQ, 3 * WA), lambda b: (b, 0)),
            pl.BlockSpec((None, None, PAST, WA), lambda b: (b, l, 0, 0)),
            pl.BlockSpec((None, None, PAST, WA), lambda b: (b, l, 0, 0)),
            pl.BlockSpec((None,) + bias.shape[1:], lambda b: (l, 0, 0, 0, 0)),
        ],
        out_specs=pl.BlockSpec((LAT_SEQ, WA), lambda b: (b, 0)),
        out_shape=jax.ShapeDtypeStruct((N_LAT, WA), BF),
        compiler_params=pltpu.CompilerParams(
            dimension_semantics=("arbitrary",), vmem_limit_bytes=VMEM_LIMIT),
        name=f"nbr_attn_l{l}",
    )(qkv, cache_k, cache_v, bias)


def _lat_mixer_kernel(x_ref, oa_ref, f_ref, g_ref, mod_ref, npost_ref, wo_ref, fc_ref, fn_ref,
                      gn_ref, ws_ref, gb_ref, avg_ref, o_ref, mix_ref, wb_ref):
    _cast_weight_once(wo_ref, wb_ref)
    mix_ref[:, :WA] = oa_ref[...]
    mix_ref[:, WA:WA + WB] = _fourier(f_ref[...], fc_ref, fn_ref).astype(BF)
    mix_ref[:, WA + WB:] = _gmlp(g_ref[...], gn_ref, ws_ref, gb_ref, avg_ref).astype(BF)
    o_ref[...] = _mix_out(x_ref[...], mix_ref[...], mod_ref[5:6, :], npost_ref[1:2, :], wb_ref)


def _lat_mixer(x, oa, f, g, mods, npost, w_out, fc, fn, gn, ws, gb, avg, l):
    row = lambda b: (b, 0)
    return pl.pallas_call(
        _lat_mixer_kernel,
        grid=(N_LAT_B,),
        in_specs=[
            pl.BlockSpec((LAT_SEQ, D), row),
            pl.BlockSpec((LAT_SEQ, WA), row),
            pl.BlockSpec((LAT_SEQ, WB), row),
            pl.BlockSpec((LAT_SEQ, 2 * WC), row),
            pl.BlockSpec((None, None, 9, D), lambda b: (l, b, 0, 0)),
            pl.BlockSpec((None, 3, D), lambda b: (l, 0, 0)),
            pl.BlockSpec((None, D, D), lambda b: (l, 0, 0), pipeline_mode=pl.Buffered(1)),
            _const_spec(fc.shape),
            pl.BlockSpec(fn.shape, lambda b: (0, 0), pipeline_mode=pl.Buffered(1)),
            pl.BlockSpec((None, 1, WC), lambda b: (l, 0, 0)),
            pl.BlockSpec((None, NGRP, CHUNK, CHUNK), lambda b: (l, 0, 0, 0)),
            pl.BlockSpec((None, CHUNK, WC), lambda b: (l, 0, 0)),
            _const_spec(avg.shape),
        ],
        out_specs=pl.BlockSpec((LAT_SEQ, D), row),
        out_shape=jax.ShapeDtypeStruct((N_LAT, D), F32),
        scratch_shapes=[pltpu.VMEM((LAT_SEQ, D), BF), pltpu.VMEM((D, D), BF)],
        compiler_params=pltpu.CompilerParams(
            dimension_semantics=("arbitrary",), vmem_limit_bytes=VMEM_LIMIT),
        name=f"lat_mixer_l{l}",
    )(x, oa, f, g, mods, npost, w_out, fc, fn, gn, ws, gb, avg)


def _dft_tables(n):
    idx = np.arange(n)
    ang = 2.0 * np.pi * ((idx[:, None] * idx[None, :]) % n) / n
    return (np.concatenate([np.cos(ang), -np.sin(ang)], axis=1) / np.sqrt(n)).astype(np.float32)


def _channel_dft_table():
    idx = np.arange(GC)
    ang = 2.0 * np.pi * ((idx[:, None] * idx[None, :]) % GC) / GC
    eye = np.eye(NGRP)
    return (np.concatenate([np.kron(eye, np.cos(ang)), np.kron(eye, np.sin(ang))], axis=1)
            / np.sqrt(GC)).astype(np.float32)


def _group_average_table():
    return np.kron(np.eye(NGRP), np.full((GC, GC), 1.0 / GC)).astype(np.float32)


def kernel(x_prompt, x_sample, cache_k, cache_v, c, c_ctx, ada_w, ada_b, norm_pre, norm_post,
           ffn_w_in, ffn_w_out, w_in, w_out, rpb, gmlp_norm, gmlp_w, gmlp_b):
    xl = x_sample.reshape(N_LAT, D)
    xc = x_prompt.reshape(N_CTX, D)
    cs = jnp.concatenate([c, c_ctx[None, :], jnp.zeros((MOD_ROWS - N_LAT_B - 1, D), F32)], axis=0)
    mods = _ada_mods(cs, ada_w, ada_b)
    bias = _bias_table(rpb)

    ffn_wi = ffn_w_in.astype(BF)
    ffn_wo = ffn_w_out.astype(BF)
    gn = gmlp_norm.reshape(DEPTH, 1, WC)
    gb = jnp.repeat(jnp.transpose(gmlp_b, (0, 2, 1)), GC, axis=2)
    ck = cache_k.reshape(N_LAT_B, DEPTH, PAST, WA)
    cv = cache_v.reshape(N_LAT_B, DEPTH, PAST, WA)

    fc = jnp.asarray(_channel_dft_table()).astype(BF)
    fn_ctx = jnp.asarray(_dft_tables(CTX_SEQ)).astype(BF)
    fn_lat = jnp.asarray(_dft_tables(LAT_SEQ)).astype(BF)
    avg = jnp.asarray(_group_average_table()).astype(BF)

    carried = ()
    for l in range(DEPTH):
        xl, xc = _ffn(xl, xc, mods, norm_pre, norm_post, ffn_wi, ffn_wo, l, 0)
        qkv, f, g, new_k, new_v = _proj(xl, xc, mods, norm_pre, w_in, carried, l)
        carried = (new_k, new_v)
        oa = _nbr_attn(qkv, ck, cv, bias, l)
        xc = _ctx_mixer(xc, qkv, f, g, mods, norm_post, w_out, fc, fn_ctx, gn, gmlp_w, gb, avg, l)
        xl = _lat_mixer(xl, oa, f, g, mods, norm_post, w_out, fc, fn_lat, gn, gmlp_w, gb, avg, l)
        xl, xc = _ffn(xl, xc, mods, norm_pre, norm_post, ffn_wi, ffn_wo, l, 1)

    kv_out = (N_CTX_B, DEPTH, CTX_SEQ, NH, HD)
    return (xc.reshape(N_CTX_B, CTX_SEQ, D), xl.reshape(N_LAT_B, LAT_SEQ, D),
            new_k.reshape(kv_out), new_v.reshape(kv_out))
```

```python
import functools

import numpy as np
import jax
import jax.numpy as jnp
from jax import lax
from jax.experimental import pallas as pl
from jax.experimental.pallas import tpu as pltpu

D = 1024
DEPTH = 4
N_CTX_B = 16
CTX_SEQ = 256
N_LAT_B = 4
LAT_SEQ = 1024
N_LAT = N_LAT_B * LAT_SEQ
N_CTX = N_CTX_B * CTX_SEQ
N_TOK = N_LAT + N_CTX
PAST = 256
GRID_W = 64
GRID_ROWS = LAT_SEQ // GRID_W
HD = 64
NH = 8
WA = 512
WB = 256
WC = 256
NGRP = 4
GC = 64
CHUNK = 128
PROJ_W = 3 * WA + WB + 2 * WC
D_FF = 2816
FF_CHUNK = 256
N_FF_CHUNKS = D_FF // FF_CHUNK
WIN_ROWS = 8
WIN_COLS = 16
N_ROW_OFF = 2 * WIN_ROWS - 1
N_COL_OFF = 2 * WIN_COLS - 1
N_DELTA = 8
EPS = 1e-6
NEG = -1e30
MOD_ROWS = 8
CTX_MOD_ROW = 4
HEADS_PER_GROUP = 4
N_HEAD_GROUPS = NH // HEADS_PER_GROUP

BF = jnp.bfloat16
F32 = jnp.float32

VMEM_LIMIT = 56 * 1024 * 1024


def _dot(a, b):
    return jnp.dot(a, b, preferred_element_type=F32)


def _dot_nt(a, b):
    return lax.dot_general(a, b, (((1,), (1,)), ((), ())), preferred_element_type=F32)


def _split3_dot(a, b):
    a0 = a.astype(BF)
    r1 = a - a0.astype(F32)
    a1 = r1.astype(BF)
    a2 = (r1 - a1.astype(F32)).astype(BF)
    return _dot(a0, b) + _dot(a1, b) + _dot(a2, b)


def _rms(x, g):
    return x * lax.rsqrt(jnp.mean(x * x, axis=-1, keepdims=True) + EPS) * g


def _mod_row(tile, tm):
    return jnp.minimum((tile * tm) // LAT_SEQ, CTX_MOD_ROW)


def _ada_kernel(cs_ref, w_ref, b_ref, o_ref):
    cs = cs_ref[...]
    s = cs * jax.nn.sigmoid(cs)
    o_ref[...] = _dot(s.astype(BF), w_ref[...].astype(BF)) + b_ref[...]


def _ada_mods(cs, ada_w, ada_b):
    n_out = 9 * D
    tn = D
    out = pl.pallas_call(
        _ada_kernel,
        grid=(DEPTH, n_out // tn),
        in_specs=[
            pl.BlockSpec((MOD_ROWS, D), lambda l, n: (0, 0)),
            pl.BlockSpec((None, D, tn), lambda l, n: (l, 0, n)),
            pl.BlockSpec((None, 1, tn), lambda l, n: (l, 0, n)),
        ],
        out_specs=pl.BlockSpec((None, MOD_ROWS, tn), lambda l, n: (l, 0, n)),
        out_shape=jax.ShapeDtypeStruct((DEPTH, MOD_ROWS, n_out), F32),
        compiler_params=pltpu.CompilerParams(
            dimension_semantics=("arbitrary", "arbitrary"), vmem_limit_bytes=VMEM_LIMIT),
        name="ada_mods",
    )(cs, ada_w, ada_b.reshape(DEPTH, 1, n_out))
    return out.reshape(DEPTH, MOD_ROWS, 9, D)


RPB_LANES = 128
RPB_HALF = RPB_LANES // 2


def _bias_kernel(rpb_ref, o_ref):
    qq = lax.broadcasted_iota(jnp.int32, (GRID_W, RPB_LANES), 0)
    lane = lax.broadcasted_iota(jnp.int32, (GRID_W, RPB_LANES), 1)
    second = lane >= GRID_W
    kk = jnp.where(second, lane - GRID_W, lane)
    col_off = jnp.clip(kk - qq + (WIN_COLS - 1), 0, N_COL_OFF - 1)
    gather_idx = jnp.where(second, col_off + RPB_HALF, col_off)
    col_start = jnp.clip(qq - WIN_COLS // 2, 0, GRID_W - WIN_COLS)
    visible = (kk >= col_start) & (kk < col_start + WIN_COLS)
    low_half = lax.broadcasted_iota(jnp.int32, (1, RPB_LANES), 1) < RPB_HALF
    for i in range(HEADS_PER_GROUP):
        for ro in range(N_ROW_OFF - 1):
            row0 = rpb_ref[i, ro:ro + 1, :]
            row1 = pltpu.roll(rpb_ref[i, ro + 1:ro + 2, :], RPB_HALF, 1)
            both = jnp.broadcast_to(jnp.where(low_half, row0, row1), (GRID_W, RPB_LANES))
            t = jnp.take_along_axis(both, gather_idx, axis=1)
            t = jnp.where(visible, t, NEG).astype(o_ref.dtype)
            for a in range(0, WIN_ROWS, 2):
                delta_idx = ro - a
                if 0 <= delta_idx < N_DELTA:
                    o_ref[delta_idx, i * GRID_W:(i + 1) * GRID_W, a * GRID_W:(a + 2) * GRID_W] = t


def _bias_table(rpb):
    rpb_padded = jnp.pad(rpb, ((0, 0), (0, 0), (0, 0), (0, RPB_LANES - N_COL_OFF)))
    return pl.pallas_call(
        _bias_kernel,
        grid=(DEPTH, N_HEAD_GROUPS),
        in_specs=[pl.BlockSpec((None, HEADS_PER_GROUP, N_ROW_OFF, RPB_LANES),
                               lambda l, g: (l, g, 0, 0))],
        out_specs=pl.BlockSpec((None, None, N_DELTA, HEADS_PER_GROUP * GRID_W, WIN_ROWS * GRID_W),
                               lambda l, g: (l, g, 0, 0, 0)),
        out_shape=jax.ShapeDtypeStruct(
            (DEPTH, N_HEAD_GROUPS, N_DELTA, HEADS_PER_GROUP * GRID_W, WIN_ROWS * GRID_W), BF),
        compiler_params=pltpu.CompilerParams(
            dimension_semantics=("arbitrary", "arbitrary"), vmem_limit_bytes=VMEM_LIMIT),
        name="rpb_table",
    )(rpb_padded)


def _lat_tile(t, n_lat_tiles):
    return jnp.minimum(t, n_lat_tiles - 1)


def _ctx_tile(t, n_lat_tiles):
    return jnp.maximum(t - n_lat_tiles, 0)


def _ffn_kernel(xl_ref, xc_ref, mod_ref, npre_ref, npost_ref, wi_hbm, wo_hbm, ol_ref, oc_ref,
                wib_ref, wob_ref, sg_ref, su_ref, so_ref, sem, *, l, sub, n_lat_tiles):
    mo = 0 if sub == 0 else 6
    ni = 0 if sub == 0 else 2
    t = pl.program_id(0)

    def gate_cols(j):
        return slice(j * FF_CHUNK, (j + 1) * FF_CHUNK)

    def up_cols(j):
        return slice(D_FF + j * FF_CHUNK, D_FF + (j + 1) * FF_CHUNK)

    def chunk_copies(j):
        slot = j % 2
        return (
            pltpu.make_async_copy(wi_hbm.at[l, sub, :, gate_cols(j)], sg_ref.at[slot], sem.at[0, slot]),
            pltpu.make_async_copy(wi_hbm.at[l, sub, :, up_cols(j)], su_ref.at[slot], sem.at[1, slot]),
            pltpu.make_async_copy(wo_hbm.at[l, sub, gate_cols(j), :], so_ref.at[slot], sem.at[2, slot]),
        )

    def streamed_weights(j):
        if j + 1 < N_FF_CHUNKS:
            for cp in chunk_copies(j + 1):
                cp.start()
        for cp in chunk_copies(j):
            cp.wait()
        slot = j % 2
        wg = sg_ref[slot].astype(BF)
        wu = su_ref[slot].astype(BF)
        wo = so_ref[slot].astype(BF)
        wib_ref[:, gate_cols(j)] = wg
        wib_ref[:, up_cols(j)] = wu
        wob_ref[gate_cols(j), :] = wo
        return wg, wu, wo

    def resident_weights(j):
        return wib_ref[:, gate_cols(j)], wib_ref[:, up_cols(j)], wob_ref[gate_cols(j), :]

    def half_step(x, weights):
        shift = mod_ref[mo:mo + 1, :]
        scale = mod_ref[mo + 1:mo + 2, :]
        gate = mod_ref[mo + 2:mo + 3, :]
        h = (_rms(x, npre_ref[ni:ni + 1, :]) * (1.0 + scale) + shift).astype(BF)
        acc = jnp.zeros(x.shape, F32)
        for j in range(N_FF_CHUNKS):
            wg, wu, wo = weights(j)
            g = _dot(h, wg)
            u = _dot(h, wu)
            a = (g * jax.nn.sigmoid(g) * u).astype(BF)
            acc = acc + _dot(a, wo)
        return x + 0.5 * gate * _rms(acc, npost_ref[ni:ni + 1, :])

    @pl.when(t == 0)
    def _():
        for cp in chunk_copies(0):
            cp.start()
        ol_ref[...] = half_step(xl_ref[...], streamed_weights)

    @pl.when((t > 0) & (t < n_lat_tiles))
    def _():
        ol_ref[...] = half_step(xl_ref[...], resident_weights)

    @pl.when(t >= n_lat_tiles)
    def _():
        oc_ref[...] = half_step(xc_ref[...], resident_weights)


def _ffn(xl, xc, mods, npre, npost, wi, wo, l, sub, tm=512):
    n_tiles = N_TOK // tm
    nl = N_LAT // tm
    lat_idx = lambda t: (_lat_tile(t, nl), 0)
    ctx_idx = lambda t: (_ctx_tile(t, nl), 0)
    return pl.pallas_call(
        functools.partial(_ffn_kernel, l=l, sub=sub, n_lat_tiles=nl),
        grid=(n_tiles,),
        in_specs=[
            pl.BlockSpec((tm, D), lat_idx),
            pl.BlockSpec((tm, D), ctx_idx),
            pl.BlockSpec((None, None, 9, D), lambda t: (l, _mod_row(t, tm), 0, 0)),
            pl.BlockSpec((None, 3, D), lambda t: (l, 0, 0)),
            pl.BlockSpec((None, 3, D), lambda t: (l, 0, 0)),
            pl.BlockSpec(memory_space=pl.ANY),
            pl.BlockSpec(memory_space=pl.ANY),
        ],
        out_specs=[pl.BlockSpec((tm, D), lat_idx), pl.BlockSpec((tm, D), ctx_idx)],
        out_shape=[jax.ShapeDtypeStruct((N_LAT, D), F32), jax.ShapeDtypeStruct((N_CTX, D), F32)],
        scratch_shapes=[
            pltpu.VMEM((D, 2 * D_FF), BF),
            pltpu.VMEM((D_FF, D), BF),
            pltpu.VMEM((2, D, FF_CHUNK), F32),
            pltpu.VMEM((2, D, FF_CHUNK), F32),
            pltpu.VMEM((2, FF_CHUNK, D), F32),
            pltpu.SemaphoreType.DMA((3, 2)),
        ],
        compiler_params=pltpu.CompilerParams(
            dimension_semantics=("arbitrary",), vmem_limit_bytes=VMEM_LIMIT),
        name=f"ffn_l{l}_s{sub}",
    )(xl, xc, mods, npre, npost, wi, wo)


def _proj_kernel(xl_ref, xc_ref, mod_ref, npre_ref, w_ref, *rest, n_lat_tiles, n_carried):
    qkv_ref, f_ref, g_ref, k_ref, v_ref, wb_ref = rest[n_carried:]
    t = pl.program_id(0)

    @pl.when(t == 0)
    def _():
        wb_ref[...] = w_ref[...].astype(BF)

    x = jnp.where(t < n_lat_tiles, xl_ref[...], xc_ref[...])
    shift = mod_ref[3:4, :]
    scale = mod_ref[4:5, :]
    h = (_rms(x, npre_ref[1:2, :]) * (1.0 + scale) + shift).astype(BF)
    z = _dot(h, wb_ref[...])
    qkv_ref[...] = z[:, :3 * WA].astype(BF)
    f_ref[...] = z[:, 3 * WA:3 * WA + WB].astype(BF)
    g_ref[...] = z[:, 3 * WA + WB:].astype(BF)
    k_ref[...] = z[:, WA:2 * WA].reshape(k_ref.shape)
    v_ref[...] = z[:, 2 * WA:3 * WA].reshape(v_ref.shape)


def _proj(xl, xc, mods, npre, w_in, carried, l, tm=512):
    n_tiles = N_TOK // tm
    nl = N_LAT // tm
    bpt = tm // CTX_SEQ
    kv_shape = jax.ShapeDtypeStruct((N_CTX_B, DEPTH, CTX_SEQ, WA), F32)
    kv_spec = pl.BlockSpec((bpt, None, CTX_SEQ, WA), lambda t: (_ctx_tile(t, nl), l, 0, 0))
    n_fixed = 5
    return pl.pallas_call(
        functools.partial(_proj_kernel, n_lat_tiles=nl, n_carried=len(carried)),
        grid=(n_tiles,),
        in_specs=[
            pl.BlockSpec((tm, D), lambda t: (_lat_tile(t, nl), 0)),
            pl.BlockSpec((tm, D), lambda t: (_ctx_tile(t, nl), 0)),
            pl.BlockSpec((None, None, 9, D), lambda t: (l, _mod_row(t, tm), 0, 0)),
            pl.BlockSpec((None, 3, D), lambda t: (l, 0, 0)),
            pl.BlockSpec((None, D, PROJ_W), lambda t: (l, 0, 0), pipeline_mode=pl.Buffered(1)),
        ] + [pl.BlockSpec(memory_space=pl.ANY)] * len(carried),
        out_specs=[
            pl.BlockSpec((tm, 3 * WA), lambda t: (t, 0)),
            pl.BlockSpec((tm, WB), lambda t: (t, 0)),
            pl.BlockSpec((tm, 2 * WC), lambda t: (t, 0)),
            kv_spec,
            kv_spec,
        ],
        out_shape=[
            jax.ShapeDtypeStruct((N_TOK, 3 * WA), BF),
            jax.ShapeDtypeStruct((N_TOK, WB), BF),
            jax.ShapeDtypeStruct((N_TOK, 2 * WC), BF),
            kv_shape,
            kv_shape,
        ],
        scratch_shapes=[pltpu.VMEM((D, PROJ_W), BF)],
        input_output_aliases={n_fixed + i: 3 + i for i in range(len(carried))},
        compiler_params=pltpu.CompilerParams(
            dimension_semantics=("arbitrary",), vmem_limit_bytes=VMEM_LIMIT),
        name=f"proj_l{l}",
    )(xl, xc, mods, npre, w_in, *carried)


def _fourier(f, fc_ref, fn_ref):
    xcs = _dot(f, fc_ref[...])
    stacked = jnp.concatenate([xcs[:, :WB], xcs[:, WB:]], axis=0)
    return _dot(fn_ref[...], stacked.astype(BF))


def _gmlp(g, gn_ref, ws_ref, gb_ref, avg_ref):
    gl = jax.nn.gelu(g.astype(F32), approximate=True)
    u = gl[:, :WC]
    v = gl[:, WC:]
    avg = avg_ref[...]
    mu = _split3_dot(v, avg)
    d = v - mu
    var = _split3_dot(d * d, avg)
    vn = (d * lax.rsqrt(var + EPS) * gn_ref[...]).astype(BF)
    lane_group = lax.broadcasted_iota(jnp.int32, (CHUNK, WC), 1) // GC
    outs = []
    for c in range(g.shape[0] // CHUNK):
        vc = vn[c * CHUNK:(c + 1) * CHUNK, :]
        sp = _dot(ws_ref[0].astype(BF), vc)
        for grp in range(1, NGRP):
            sp = jnp.where(lane_group == grp, _dot(ws_ref[grp].astype(BF), vc), sp)
        outs.append(u[c * CHUNK:(c + 1) * CHUNK, :] * (sp + gb_ref[...]))
    return jnp.concatenate(outs, axis=0)


def _cast_weight_once(w_ref, wb_ref):
    @pl.when(pl.program_id(0) == 0)
    def _():
        wb_ref[...] = w_ref[...].astype(BF)


def _mix_out(x, o, gate, npost, wb_ref):
    y = _dot(o, wb_ref[...])
    return x + gate * _rms(y, npost)


def _ctx_mixer_kernel(x_ref, qkv_ref, f_ref, g_ref, mod_ref, npost_ref, wo_ref, fc_ref, fn_ref,
                      gn_ref, ws_ref, gb_ref, avg_ref, o_ref, mix_ref, wb_ref):
    _cast_weight_once(wo_ref, wb_ref)
    for hp in range(NH // 2):
        pair = []
        for h in (2 * hp, 2 * hp + 1):
            q = qkv_ref[:, h * HD:(h + 1) * HD]
            k = qkv_ref[:, WA + h * HD:WA + (h + 1) * HD]
            v = qkv_ref[:, 2 * WA + h * HD:2 * WA + (h + 1) * HD]
            s = _dot_nt(q, k) * (HD ** -0.5)
            m = jnp.max(s, axis=-1, keepdims=True)
            p = jnp.exp(s - m)
            den = jnp.sum(p, axis=-1, keepdims=True)
            pair.append(_dot(p.astype(BF), v) / den)
        mix_ref[:, 2 * hp * HD:(2 * hp + 2) * HD] = jnp.concatenate(pair, axis=-1).astype(BF)
    mix_ref[:, WA:WA + WB] = _fourier(f_ref[...], fc_ref, fn_ref).astype(BF)
    mix_ref[:, WA + WB:] = _gmlp(g_ref[...], gn_ref, ws_ref, gb_ref, avg_ref).astype(BF)
    o_ref[...] = _mix_out(x_ref[...], mix_ref[...], mod_ref[5:6, :], npost_ref[1:2, :], wb_ref)


def _const_spec(shape):
    nd = len(shape)
    return pl.BlockSpec(shape, lambda *_: (0,) * nd)


def _ctx_mixer(x, qkv, f, g, mods, npost, w_out, fc, fn, gn, ws, gb, avg, l):
    off = N_LAT // CTX_SEQ
    row = lambda b: (b + off, 0)
    return pl.pallas_call(
        _ctx_mixer_kernel,
        grid=(N_CTX_B,),
        in_specs=[
            pl.BlockSpec((CTX_SEQ, D), lambda b: (b, 0)),
            pl.BlockSpec((CTX_SEQ, 3 * WA), row),
            pl.BlockSpec((CTX_SEQ, WB), row),
            pl.BlockSpec((CTX_SEQ, 2 * WC), row),
            pl.BlockSpec((None, None, 9, D), lambda b: (l, CTX_MOD_ROW, 0, 0)),
            pl.BlockSpec((None, 3, D), lambda b: (l, 0, 0)),
            pl.BlockSpec((None, D, D), lambda b: (l, 0, 0), pipeline_mode=pl.Buffered(1)),
            _const_spec(fc.shape),
            _const_spec(fn.shape),
            pl.BlockSpec((None, 1, WC), lambda b: (l, 0, 0)),
            pl.BlockSpec((None, NGRP, CHUNK, CHUNK), lambda b: (l, 0, 0, 0)),
            pl.BlockSpec((None, CHUNK, WC), lambda b: (l, 0, 0)),
            _const_spec(avg.shape),
        ],
        out_specs=pl.BlockSpec((CTX_SEQ, D), lambda b: (b, 0)),
        out_shape=jax.ShapeDtypeStruct((N_CTX, D), F32),
        scratch_shapes=[pltpu.VMEM((CTX_SEQ, D), BF), pltpu.VMEM((D, D), BF)],
        compiler_params=pltpu.CompilerParams(
            dimension_semantics=("arbitrary",), vmem_limit_bytes=VMEM_LIMIT),
        name=f"ctx_mixer_l{l}",
    )(x, qkv, f, g, mods, npost, w_out, fc, fn, gn, ws, gb, avg)


def _window_start(r):
    return min(max(r - WIN_ROWS // 2, 0), GRID_ROWS - WIN_ROWS)


def _nbr_attn_kernel(qkv_ref, ck_ref, cv_ref, bias_ref, o_ref):
    gw = HEADS_PER_GROUP * HD
    row_head = lax.broadcasted_iota(jnp.int32, (HEADS_PER_GROUP * GRID_W, gw), 0) // GRID_W
    lane_head = lax.broadcasted_iota(jnp.int32, (HEADS_PER_GROUP * GRID_W, gw), 1) // HD
    own = row_head == lane_head
    for hg in range(N_HEAD_GROUPS):
        ck = ck_ref[:, hg * gw:(hg + 1) * gw].astype(BF)
        cv = cv_ref[:, hg * gw:(hg + 1) * gw].astype(BF)
        for r in range(GRID_ROWS):
            ws = _window_start(r)
            q = qkv_ref[r * GRID_W:(r + 1) * GRID_W, hg * gw:(hg + 1) * gw]
            qs = jnp.where(own, jnp.concatenate([q] * HEADS_PER_GROUP, axis=0), jnp.zeros((), BF))
            kw = qkv_ref[ws * GRID_W:(ws + WIN_ROWS) * GRID_W, WA + hg * gw:WA + (hg + 1) * gw]
            vw = qkv_ref[ws * GRID_W:(ws + WIN_ROWS) * GRID_W, 2 * WA + hg * gw:2 * WA + (hg + 1) * gw]
            s_loc = _dot_nt(qs, kw) * (HD ** -0.5) + bias_ref[hg, ws - r + N_DELTA - 1].astype(F32)
            s_ctx = _dot_nt(qs, ck) * (HD ** -0.5)
            m = jnp.maximum(jnp.max(s_loc, axis=-1, keepdims=True),
                            jnp.max(s_ctx, axis=-1, keepdims=True))
            p_loc = jnp.exp(s_loc - m)
            p_ctx = jnp.exp(s_ctx - m)
            den = jnp.sum(p_loc, axis=-1, keepdims=True) + jnp.sum(p_ctx, axis=-1, keepdims=True)
            o = (_dot(p_loc.astype(BF), vw) + _dot(p_ctx.astype(BF), cv)) / den
            o = jnp.where(own, o, 0.0)
            o = (o[0:GRID_W] + o[GRID_W:2 * GRID_W]) + (o[2 * GRID_W:3 * GRID_W] + o[3 * GRID_W:])
            o_ref[r * GRID_W:(r + 1) * GRID_W, hg * gw:(hg + 1) * gw] = o.astype(BF)


def _nbr_attn(qkv, cache_k, cache_v, bias, l):
    return pl.pallas_call(
        _nbr_attn_kernel,
        grid=(N_LAT_B,),
        in_specs=[
            pl.BlockSpec((LAT_SEQ, 3 * WA), lambda b: (b, 0)),
            pl.BlockSpec((None, None, PAST, WA), lambda b: (b, l, 0, 0)),
            pl.BlockSpec((None, None, PAST, WA), lambda b: (b, l, 0, 0)),
            pl.BlockSpec((None,) + bias.shape[1:], lambda b: (l, 0, 0, 0, 0)),
        ],
        out_specs=pl.BlockSpec((LAT_SEQ, WA), lambda b: (b, 0)),
        out_shape=jax.ShapeDtypeStruct((N_LAT, WA), BF),
        compiler_params=pltpu.CompilerParams(
            dimension_semantics=("arbitrary",), vmem_limit_bytes=VMEM_LIMIT),
        name=f"nbr_attn_l{l}",
    )(qkv, cache_k, cache_v, bias)


def _lat_mixer_kernel(x_ref, oa_ref, f_ref, g_ref, mod_ref, npost_ref, wo_ref, fc_ref, fn_ref,
                      gn_ref, ws_ref, gb_ref, avg_ref, o_ref, mix_ref, wb_ref):
    _cast_weight_once(wo_ref, wb_ref)
    mix_ref[:, :WA] = oa_ref[...]
    mix_ref[:, WA:WA + WB] = _fourier(f_ref[...], fc_ref, fn_ref).astype(BF)
    mix_ref[:, WA + WB:] = _gmlp(g_ref[...], gn_ref, ws_ref, gb_ref, avg_ref).astype(BF)
    o_ref[...] = _mix_out(x_ref[...], mix_ref[...], mod_ref[5:6, :], npost_ref[1:2, :], wb_ref)


def _lat_mixer(x, oa, f, g, mods, npost, w_out, fc, fn, gn, ws, gb, avg, l):
    row = lambda b: (b, 0)
    return pl.pallas_call(
        _lat_mixer_kernel,
        grid=(N_LAT_B,),
        in_specs=[
            pl.BlockSpec((LAT_SEQ, D), row),
            pl.BlockSpec((LAT_SEQ, WA), row),
            pl.BlockSpec((LAT_SEQ, WB), row),
            pl.BlockSpec((LAT_SEQ, 2 * WC), row),
            pl.BlockSpec((None, None, 9, D), lambda b: (l, b, 0, 0)),
            pl.BlockSpec((None, 3, D), lambda b: (l, 0, 0)),
            pl.BlockSpec((None, D, D), lambda b: (l, 0, 0), pipeline_mode=pl.Buffered(1)),
            _const_spec(fc.shape),
            pl.BlockSpec(fn.shape, lambda b: (0, 0), pipeline_mode=pl.Buffered(1)),
            pl.BlockSpec((None, 1, WC), lambda b: (l, 0, 0)),
            pl.BlockSpec((None, NGRP, CHUNK, CHUNK), lambda b: (l, 0, 0, 0)),
            pl.BlockSpec((None, CHUNK, WC), lambda b: (l, 0, 0)),
            _const_spec(avg.shape),
        ],
        out_specs=pl.BlockSpec((LAT_SEQ, D), row),
        out_shape=jax.ShapeDtypeStruct((N_LAT, D), F32),
        scratch_shapes=[pltpu.VMEM((LAT_SEQ, D), BF), pltpu.VMEM((D, D), BF)],
        compiler_params=pltpu.CompilerParams(
            dimension_semantics=("arbitrary",), vmem_limit_bytes=VMEM_LIMIT),
        name=f"lat_mixer_l{l}",
    )(x, oa, f, g, mods, npost, w_out, fc, fn, gn, ws, gb, avg)


def _dft_tables(n):
    idx = np.arange(n)
    ang = 2.0 * np.pi * ((idx[:, None] * idx[None, :]) % n) / n
    return (np.concatenate([np.cos(ang), -np.sin(ang)], axis=1) / np.sqrt(n)).astype(np.float32)


def _channel_dft_table():
    idx = np.arange(GC)
    ang = 2.0 * np.pi * ((idx[:, None] * idx[None, :]) % GC) / GC
    eye = np.eye(NGRP)
    return (np.concatenate([np.kron(eye, np.cos(ang)), np.kron(eye, np.sin(ang))], axis=1)
            / np.sqrt(GC)).astype(np.float32)


def _group_average_table():
    return np.kron(np.eye(NGRP), np.full((GC, GC), 1.0 / GC)).astype(np.float32)


def kernel(x_prompt, x_sample, cache_k, cache_v, c, c_ctx, ada_w, ada_b, norm_pre, norm_post,
           ffn_w_in, ffn_w_out, w_in, w_out, rpb, gmlp_norm, gmlp_w, gmlp_b):
    xl = x_sample.reshape(N_LAT, D)
    xc = x_prompt.reshape(N_CTX, D)
    cs = jnp.concatenate([c, c_ctx[None, :], jnp.zeros((MOD_ROWS - N_LAT_B - 1, D), F32)], axis=0)
    mods = _ada_mods(cs, ada_w, ada_b)
    bias = _bias_table(rpb)

    gn = gmlp_norm.reshape(DEPTH, 1, WC)
    gb = jnp.repeat(jnp.transpose(gmlp_b, (0, 2, 1)), GC, axis=2)
    ck = cache_k.reshape(N_LAT_B, DEPTH, PAST, WA)
    cv = cache_v.reshape(N_LAT_B, DEPTH, PAST, WA)

    fc = jnp.asarray(_channel_dft_table()).astype(BF)
    fn_ctx = jnp.asarray(_dft_tables(CTX_SEQ)).astype(BF)
    fn_lat = jnp.asarray(_dft_tables(LAT_SEQ)).astype(BF)
    avg = jnp.asarray(_group_average_table()).astype(BF)

    carried = (jnp.zeros((N_CTX_B, DEPTH, CTX_SEQ, WA), F32),) * 2
    for l in range(DEPTH):
        xl, xc = _ffn(xl, xc, mods, norm_pre, norm_post, ffn_w_in, ffn_w_out, l, 0)
        qkv, f, g, new_k, new_v = _proj(xl, xc, mods, norm_pre, w_in, carried, l)
        carried = (new_k, new_v)
        oa = _nbr_attn(qkv, ck, cv, bias, l)
        xc = _ctx_mixer(xc, qkv, f, g, mods, norm_post, w_out, fc, fn_ctx, gn, gmlp_w, gb, avg, l)
        xl = _lat_mixer(xl, oa, f, g, mods, norm_post, w_out, fc, fn_lat, gn, gmlp_w, gb, avg, l)
        xl, xc = _ffn(xl, xc, mods, norm_pre, norm_post, ffn_w_in, ffn_w_out, l, 1)

    kv_out = (N_CTX_B, DEPTH, CTX_SEQ, NH, HD)
    return (xc.reshape(N_CTX_B, CTX_SEQ, D), xl.reshape(N_LAT_B, LAT_SEQ, D),
            new_k.reshape(kv_out), new_v.reshape(kv_out))
```

```python
import functools

import numpy as np
import jax
import jax.numpy as jnp
from jax import lax
from jax.experimental import pallas as pl
from jax.experimental.pallas import tpu as pltpu

D = 1024
DEPTH = 4
N_CTX_B = 16
CTX_SEQ = 256
N_LAT_B = 4
LAT_SEQ = 1024
N_LAT = N_LAT_B * LAT_SEQ
N_CTX = N_CTX_B * CTX_SEQ
N_TOK = N_LAT + N_CTX
PAST = 256
GRID_W = 64
GRID_ROWS = LAT_SEQ // GRID_W
HD = 64
NH = 8
WA = 512
WB = 256
WC = 256
NGRP = 4
GC = 64
CHUNK = 128
PROJ_W = 3 * WA + WB + 2 * WC
D_FF = 2816
FF_CHUNK = 256
N_FF_CHUNKS = D_FF // FF_CHUNK
WIN_ROWS = 8
WIN_COLS = 16
N_ROW_OFF = 2 * WIN_ROWS - 1
N_COL_OFF = 2 * WIN_COLS - 1
N_DELTA = 8
EPS = 1e-6
NEG = -1e30
MOD_ROWS = 8
CTX_MOD_ROW = 4
HEADS_PER_GROUP = 4
N_HEAD_GROUPS = NH // HEADS_PER_GROUP

BF = jnp.bfloat16
F32 = jnp.float32

VMEM_LIMIT = 56 * 1024 * 1024


def _dot(a, b):
    return jnp.dot(a, b, preferred_element_type=F32)


def _dot_nt(a, b):
    return lax.dot_general(a, b, (((1,), (1,)), ((), ())), preferred_element_type=F32)


def _split3_dot(a, b):
    a0 = a.astype(BF)
    r1 = a - a0.astype(F32)
    a1 = r1.astype(BF)
    a2 = (r1 - a1.astype(F32)).astype(BF)
    return _dot(a0, b) + _dot(a1, b) + _dot(a2, b)


def _rms(x, g):
    return x * lax.rsqrt(jnp.mean(x * x, axis=-1, keepdims=True) + EPS) * g


def _mod_row(tile, tm):
    return jnp.minimum((tile * tm) // LAT_SEQ, CTX_MOD_ROW)


def _ada_kernel(cs_ref, w_ref, b_ref, o_ref):
    cs = cs_ref[...]
    s = cs * jax.nn.sigmoid(cs)
    o_ref[...] = _dot(s.astype(BF), w_ref[...].astype(BF)) + b_ref[...]


def _ada_mods(cs, ada_w, ada_b):
    n_out = 9 * D
    tn = D
    out = pl.pallas_call(
        _ada_kernel,
        grid=(DEPTH, n_out // tn),
        in_specs=[
            pl.BlockSpec((MOD_ROWS, D), lambda l, n: (0, 0)),
            pl.BlockSpec((None, D, tn), lambda l, n: (l, 0, n)),
            pl.BlockSpec((None, 1, tn), lambda l, n: (l, 0, n)),
        ],
        out_specs=pl.BlockSpec((None, MOD_ROWS, tn), lambda l, n: (l, 0, n)),
        out_shape=jax.ShapeDtypeStruct((DEPTH, MOD_ROWS, n_out), F32),
        compiler_params=pltpu.CompilerParams(
            dimension_semantics=("arbitrary", "arbitrary"), vmem_limit_bytes=VMEM_LIMIT),
        name="ada_mods",
    )(cs, ada_w, ada_b.reshape(DEPTH, 1, n_out))
    return out.reshape(DEPTH, MOD_ROWS, 9, D)


RPB_LANES = 128
RPB_HALF = RPB_LANES // 2


def _bias_kernel(rpb_ref, o_ref):
    qq = lax.broadcasted_iota(jnp.int32, (GRID_W, RPB_LANES), 0)
    lane = lax.broadcasted_iota(jnp.int32, (GRID_W, RPB_LANES), 1)
    second = lane >= GRID_W
    kk = jnp.where(second, lane - GRID_W, lane)
    col_off = jnp.clip(kk - qq + (WIN_COLS - 1), 0, N_COL_OFF - 1)
    gather_idx = jnp.where(second, col_off + RPB_HALF, col_off)
    col_start = jnp.clip(qq - WIN_COLS // 2, 0, GRID_W - WIN_COLS)
    visible = (kk >= col_start) & (kk < col_start + WIN_COLS)
    low_half = lax.broadcasted_iota(jnp.int32, (1, RPB_LANES), 1) < RPB_HALF
    for i in range(HEADS_PER_GROUP):
        for ro in range(N_ROW_OFF - 1):
            row0 = rpb_ref[i, ro:ro + 1, :]
            row1 = pltpu.roll(rpb_ref[i, ro + 1:ro + 2, :], RPB_HALF, 1)
            both = jnp.broadcast_to(jnp.where(low_half, row0, row1), (GRID_W, RPB_LANES))
            t = jnp.take_along_axis(both, gather_idx, axis=1)
            t = jnp.where(visible, t, NEG).astype(o_ref.dtype)
            for a in range(0, WIN_ROWS, 2):
                delta_idx = ro - a
                if 0 <= delta_idx < N_DELTA:
                    o_ref[delta_idx, i * GRID_W:(i + 1) * GRID_W, a * GRID_W:(a + 2) * GRID_W] = t


def _bias_table(rpb):
    rpb_padded = jnp.pad(rpb, ((0, 0), (0, 0), (0, 0), (0, RPB_LANES - N_COL_OFF)))
    return pl.pallas_call(
        _bias_kernel,
        grid=(DEPTH, N_HEAD_GROUPS),
        in_specs=[pl.BlockSpec((None, HEADS_PER_GROUP, N_ROW_OFF, RPB_LANES),
                               lambda l, g: (l, g, 0, 0))],
        out_specs=pl.BlockSpec((None, None, N_DELTA, HEADS_PER_GROUP * GRID_W, WIN_ROWS * GRID_W),
                               lambda l, g: (l, g, 0, 0, 0)),
        out_shape=jax.ShapeDtypeStruct(
            (DEPTH, N_HEAD_GROUPS, N_DELTA, HEADS_PER_GROUP * GRID_W, WIN_ROWS * GRID_W), BF),
        compiler_params=pltpu.CompilerParams(
            dimension_semantics=("arbitrary", "arbitrary"), vmem_limit_bytes=VMEM_LIMIT),
        name="rpb_table",
    )(rpb_padded)


def _lat_tile(t, n_lat_tiles):
    return jnp.minimum(t, n_lat_tiles - 1)


def _ctx_tile(t, n_lat_tiles):
    return jnp.maximum(t - n_lat_tiles, 0)


def _ffn_kernel(*refs, l, sub, split_in, split_out, n_lat_tiles):
    refs = list(refs)
    x_refs = [refs.pop(0) for _ in range(2 if split_in else 1)]
    mod_ref, npre_ref, npost_ref, wi_hbm, wo_hbm = [refs.pop(0) for _ in range(5)]
    o_refs = [refs.pop(0) for _ in range(2 if split_out else 1)]
    wg_ref, wu_ref, wo_ref, sg_ref, su_ref, so_ref, h_ref, acc_ref, sem = refs

    mo = 0 if sub == 0 else 6
    ni = 0 if sub == 0 else 2
    t = pl.program_id(0)
    is_lat = t < n_lat_tiles

    def load_x():
        if split_in:
            return jnp.where(is_lat, x_refs[0][...], x_refs[1][...])
        return x_refs[0][...]

    def store(out):
        if not split_out:
            o_refs[0][...] = out
            return

        @pl.when(is_lat)
        def _():
            o_refs[0][...] = out

        @pl.when(jnp.logical_not(is_lat))
        def _():
            o_refs[1][...] = out

    def modulated(x):
        shift = mod_ref[mo:mo + 1, :]
        scale = mod_ref[mo + 1:mo + 2, :]
        return (_rms(x, npre_ref[ni:ni + 1, :]) * (1.0 + scale) + shift).astype(BF)

    def residual(x, acc):
        gate = mod_ref[mo + 2:mo + 3, :]
        return x + 0.5 * gate * _rms(acc, npost_ref[ni:ni + 1, :])

    def chunk_update(h, wg, wu, wo):
        g = _dot(h, wg)
        u = _dot(h, wu)
        a = (g * jax.nn.sigmoid(g) * u).astype(BF)
        return _dot(a, wo)

    def chunk_copies(j, slot):
        cols = pl.ds(pl.multiple_of(j * FF_CHUNK, FF_CHUNK), FF_CHUNK)
        up = pl.ds(pl.multiple_of(D_FF + j * FF_CHUNK, FF_CHUNK), FF_CHUNK)
        return (
            pltpu.make_async_copy(wi_hbm.at[l, sub, :, cols], sg_ref.at[slot], sem.at[0, slot]),
            pltpu.make_async_copy(wi_hbm.at[l, sub, :, up], su_ref.at[slot], sem.at[1, slot]),
            pltpu.make_async_copy(wo_hbm.at[l, sub, cols, :], so_ref.at[slot], sem.at[2, slot]),
        )

    @pl.when(t == 0)
    def _():
        for cp in chunk_copies(0, 0):
            cp.start()
        h_ref[...] = modulated(load_x())
        acc_ref[...] = jnp.zeros(acc_ref.shape, F32)

        def body(j, carry):
            slot = lax.rem(j, 2)

            @pl.when(j + 1 < N_FF_CHUNKS)
            def _():
                for cp in chunk_copies(j + 1, 1 - slot):
                    cp.start()

            for cp in chunk_copies(j, slot):
                cp.wait()
            wg = sg_ref[slot].astype(BF)
            wu = su_ref[slot].astype(BF)
            wo = so_ref[slot].astype(BF)
            wg_ref[j] = wg
            wu_ref[j] = wu
            wo_ref[j] = wo
            acc_ref[...] += chunk_update(h_ref[...], wg, wu, wo)
            return carry

        lax.fori_loop(0, N_FF_CHUNKS, body, 0)
        store(residual(load_x(), acc_ref[...]))

    @pl.when(t > 0)
    def _():
        x = load_x()
        h = modulated(x)
        acc = jnp.zeros(x.shape, F32)
        for j in range(N_FF_CHUNKS):
            acc = acc + chunk_update(h, wg_ref[j], wu_ref[j], wo_ref[j])
        store(residual(x, acc))


def _ffn(xs, mods, npre, npost, wi, wo, l, sub, split_out=False, tm=512):
    n_tiles = N_TOK // tm
    nl = N_LAT // tm
    split_in = len(xs) == 2
    lat_idx = lambda t: (_lat_tile(t, nl), 0)
    ctx_idx = lambda t: (_ctx_tile(t, nl), 0)
    split_specs = [pl.BlockSpec((tm, D), lat_idx), pl.BlockSpec((tm, D), ctx_idx)]
    joint_specs = [pl.BlockSpec((tm, D), lambda t: (t, 0))]
    split_shapes = [jax.ShapeDtypeStruct((N_LAT, D), F32), jax.ShapeDtypeStruct((N_CTX, D), F32)]
    joint_shapes = [jax.ShapeDtypeStruct((N_TOK, D), F32)]
    out = pl.pallas_call(
        functools.partial(_ffn_kernel, l=l, sub=sub, split_in=split_in, split_out=split_out,
                          n_lat_tiles=nl),
        grid=(n_tiles,),
        in_specs=(split_specs if split_in else joint_specs) + [
            pl.BlockSpec((None, None, 9, D), lambda t: (l, _mod_row(t, tm), 0, 0)),
            pl.BlockSpec((None, 3, D), lambda t: (l, 0, 0)),
            pl.BlockSpec((None, 3, D), lambda t: (l, 0, 0)),
            pl.BlockSpec(memory_space=pl.ANY),
            pl.BlockSpec(memory_space=pl.ANY),
        ],
        out_specs=split_specs if split_out else joint_specs,
        out_shape=split_shapes if split_out else joint_shapes,
        scratch_shapes=[
            pltpu.VMEM((N_FF_CHUNKS, D, FF_CHUNK), BF),
            pltpu.VMEM((N_FF_CHUNKS, D, FF_CHUNK), BF),
            pltpu.VMEM((N_FF_CHUNKS, FF_CHUNK, D), BF),
            pltpu.VMEM((2, D, FF_CHUNK), F32),
            pltpu.VMEM((2, D, FF_CHUNK), F32),
            pltpu.VMEM((2, FF_CHUNK, D), F32),
            pltpu.VMEM((tm, D), BF),
            pltpu.VMEM((tm, D), F32),
            pltpu.SemaphoreType.DMA((3, 2)),
        ],
        compiler_params=pltpu.CompilerParams(
            dimension_semantics=("arbitrary",), vmem_limit_bytes=VMEM_LIMIT),
        name=f"ffn_l{l}_s{sub}",
    )(*xs, mods, npre, npost, wi, wo)
    return tuple(out) if split_out else out[0]


def _proj_kernel(x_ref, mod_ref, npre_ref, w_ref, *rest, n_carried):
    qkv_ref, f_ref, g_ref, k_ref, v_ref, wb_ref = rest[n_carried:]

    @pl.when(pl.program_id(0) == 0)
    def _():
        wb_ref[...] = w_ref[...].astype(BF)

    x = x_ref[...]
    shift = mod_ref[3:4, :]
    scale = mod_ref[4:5, :]
    h = (_rms(x, npre_ref[1:2, :]) * (1.0 + scale) + shift).astype(BF)
    z = _dot(h, wb_ref[...])
    qkv_ref[...] = z[:, :3 * WA].astype(BF)
    f_ref[...] = z[:, 3 * WA:3 * WA + WB].astype(BF)
    g_ref[...] = z[:, 3 * WA + WB:].astype(BF)
    k_ref[...] = z[:, WA:2 * WA].reshape(k_ref.shape)
    v_ref[...] = z[:, 2 * WA:3 * WA].reshape(v_ref.shape)


def _proj(x, mods, npre, w_in, carried, l, tm=512):
    n_tiles = N_TOK // tm
    nl = N_LAT // tm
    bpt = tm // CTX_SEQ
    kv_shape = jax.ShapeDtypeStruct((N_CTX_B, DEPTH, CTX_SEQ, WA), F32)
    kv_spec = pl.BlockSpec((bpt, None, CTX_SEQ, WA), lambda t: (_ctx_tile(t, nl), l, 0, 0))
    n_fixed = 4
    return pl.pallas_call(
        functools.partial(_proj_kernel, n_carried=len(carried)),
        grid=(n_tiles,),
        in_specs=[
            pl.BlockSpec((tm, D), lambda t: (t, 0)),
            pl.BlockSpec((None, None, 9, D), lambda t: (l, _mod_row(t, tm), 0, 0)),
            pl.BlockSpec((None, 3, D), lambda t: (l, 0, 0)),
            pl.BlockSpec((None, D, PROJ_W), lambda t: (l, 0, 0), pipeline_mode=pl.Buffered(1)),
        ] + [pl.BlockSpec(memory_space=pl.ANY)] * len(carried),
        out_specs=[
            pl.BlockSpec((tm, 3 * WA), lambda t: (t, 0)),
            pl.BlockSpec((tm, WB), lambda t: (t, 0)),
            pl.BlockSpec((tm, 2 * WC), lambda t: (t, 0)),
            kv_spec,
            kv_spec,
        ],
        out_shape=[
            jax.ShapeDtypeStruct((N_TOK, 3 * WA), BF),
            jax.ShapeDtypeStruct((N_TOK, WB), BF),
            jax.ShapeDtypeStruct((N_TOK, 2 * WC), BF),
            kv_shape,
            kv_shape,
        ],
        scratch_shapes=[pltpu.VMEM((D, PROJ_W), BF)],
        input_output_aliases={n_fixed + i: 3 + i for i in range(len(carried))},
        compiler_params=pltpu.CompilerParams(
            dimension_semantics=("arbitrary",), vmem_limit_bytes=VMEM_LIMIT),
        name=f"proj_l{l}",
    )(x, mods, npre, w_in, *carried)


def _fourier(f, fc_ref, fn_ref):
    xcs = _dot(f, fc_ref[...])
    stacked = jnp.concatenate([xcs[:, :WB], xcs[:, WB:]], axis=0)
    return _dot(fn_ref[...], stacked.astype(BF))


def _gmlp(g, gn_ref, ws_ref, gb_ref, avg_ref):
    gl = jax.nn.gelu(g.astype(F32), approximate=True)
    u = gl[:, :WC]
    v = gl[:, WC:]
    avg = avg_ref[...]
    mu = _split3_dot(v, avg)
    d = v - mu
    var = _split3_dot(d * d, avg)
    vn = (d * lax.rsqrt(var + EPS) * gn_ref[...]).astype(BF)
    lane_group = lax.broadcasted_iota(jnp.int32, (CHUNK, WC), 1) // GC
    outs = []
    for c in range(g.shape[0] // CHUNK):
        vc = vn[c * CHUNK:(c + 1) * CHUNK, :]
        sp = _dot(ws_ref[0].astype(BF), vc)
        for grp in range(1, NGRP):
            sp = jnp.where(lane_group == grp, _dot(ws_ref[grp].astype(BF), vc), sp)
        outs.append(u[c * CHUNK:(c + 1) * CHUNK, :] * (sp + gb_ref[...]))
    return jnp.concatenate(outs, axis=0)


def _cast_weight_once(w_ref, wb_ref):
    @pl.when(pl.program_id(0) == 0)
    def _():
        wb_ref[...] = w_ref[...].astype(BF)


def _mix_out(x, o, gate, npost, wb_ref):
    y = _dot(o, wb_ref[...])
    return x + gate * _rms(y, npost)


def _ctx_mixer_kernel(x_ref, qkv_ref, f_ref, g_ref, mod_ref, npost_ref, wo_ref, fc_ref, fn_ref,
                      gn_ref, ws_ref, gb_ref, avg_ref, o_ref, mix_ref, wb_ref):
    _cast_weight_once(wo_ref, wb_ref)
    for hp in range(NH // 2):
        pair = []
        for h in (2 * hp, 2 * hp + 1):
            q = qkv_ref[:, h * HD:(h + 1) * HD]
            k = qkv_ref[:, WA + h * HD:WA + (h + 1) * HD]
            v = qkv_ref[:, 2 * WA + h * HD:2 * WA + (h + 1) * HD]
            s = _dot_nt(q, k) * (HD ** -0.5)
            m = jnp.max(s, axis=-1, keepdims=True)
            p = jnp.exp(s - m)
            den = jnp.sum(p, axis=-1, keepdims=True)
            pair.append(_dot(p.astype(BF), v) / den)
        mix_ref[:, 2 * hp * HD:(2 * hp + 2) * HD] = jnp.concatenate(pair, axis=-1).astype(BF)
    mix_ref[:, WA:WA + WB] = _fourier(f_ref[...], fc_ref, fn_ref).astype(BF)
    mix_ref[:, WA + WB:] = _gmlp(g_ref[...], gn_ref, ws_ref, gb_ref, avg_ref).astype(BF)
    o_ref[...] = _mix_out(x_ref[...], mix_ref[...], mod_ref[5:6, :], npost_ref[1:2, :], wb_ref)


def _const_spec(shape):
    nd = len(shape)
    return pl.BlockSpec(shape, lambda *_: (0,) * nd)


def _ctx_mixer(x, qkv, f, g, mods, npost, w_out, fc, fn, gn, ws, gb, avg, l):
    off = N_LAT // CTX_SEQ
    row = lambda b: (b + off, 0)
    return pl.pallas_call(
        _ctx_mixer_kernel,
        grid=(N_CTX_B,),
        in_specs=[
            pl.BlockSpec((CTX_SEQ, D), row),
            pl.BlockSpec((CTX_SEQ, 3 * WA), row),
            pl.BlockSpec((CTX_SEQ, WB), row),
            pl.BlockSpec((CTX_SEQ, 2 * WC), row),
            pl.BlockSpec((None, None, 9, D), lambda b: (l, CTX_MOD_ROW, 0, 0)),
            pl.BlockSpec((None, 3, D), lambda b: (l, 0, 0)),
            pl.BlockSpec((None, D, D), lambda b: (l, 0, 0), pipeline_mode=pl.Buffered(1)),
            _const_spec(fc.shape),
            _const_spec(fn.shape),
            pl.BlockSpec((None, 1, WC), lambda b: (l, 0, 0)),
            pl.BlockSpec((None, NGRP, CHUNK, CHUNK), lambda b: (l, 0, 0, 0)),
            pl.BlockSpec((None, CHUNK, WC), lambda b: (l, 0, 0)),
            _const_spec(avg.shape),
        ],
        out_specs=pl.BlockSpec((CTX_SEQ, D), row),
        out_shape=jax.ShapeDtypeStruct((N_TOK, D), F32),
        input_output_aliases={0: 0},
        scratch_shapes=[pltpu.VMEM((CTX_SEQ, D), BF), pltpu.VMEM((D, D), BF)],
        compiler_params=pltpu.CompilerParams(
            dimension_semantics=("arbitrary",), vmem_limit_bytes=VMEM_LIMIT),
        name=f"ctx_mixer_l{l}",
    )(x, qkv, f, g, mods, npost, w_out, fc, fn, gn, ws, gb, avg)


def _window_start(r):
    return min(max(r - WIN_ROWS // 2, 0), GRID_ROWS - WIN_ROWS)


def _nbr_attn_kernel(qkv_ref, ck_ref, cv_ref, bias_ref, o_ref):
    gw = HEADS_PER_GROUP * HD
    row_head = lax.broadcasted_iota(jnp.int32, (HEADS_PER_GROUP * GRID_W, gw), 0) // GRID_W
    lane_head = lax.broadcasted_iota(jnp.int32, (HEADS_PER_GROUP * GRID_W, gw), 1) // HD
    own = row_head == lane_head
    for hg in range(N_HEAD_GROUPS):
        ck = ck_ref[:, hg * gw:(hg + 1) * gw].astype(BF)
        cv = cv_ref[:, hg * gw:(hg + 1) * gw].astype(BF)
        for r in range(GRID_ROWS):
            ws = _window_start(r)
            q = qkv_ref[r * GRID_W:(r + 1) * GRID_W, hg * gw:(hg + 1) * gw]
            qs = jnp.where(own, jnp.concatenate([q] * HEADS_PER_GROUP, axis=0), jnp.zeros((), BF))
            kw = qkv_ref[ws * GRID_W:(ws + WIN_ROWS) * GRID_W, WA + hg * gw:WA + (hg + 1) * gw]
            vw = qkv_ref[ws * GRID_W:(ws + WIN_ROWS) * GRID_W, 2 * WA + hg * gw:2 * WA + (hg + 1) * gw]
            s_loc = _dot_nt(qs, kw) * (HD ** -0.5) + bias_ref[hg, ws - r + N_DELTA - 1].astype(F32)
            s_ctx = _dot_nt(qs, ck) * (HD ** -0.5)
            m = jnp.maximum(jnp.max(s_loc, axis=-1, keepdims=True),
                            jnp.max(s_ctx, axis=-1, keepdims=True))
            p_loc = jnp.exp(s_loc - m)
            p_ctx = jnp.exp(s_ctx - m)
            den = jnp.sum(p_loc, axis=-1, keepdims=True) + jnp.sum(p_ctx, axis=-1, keepdims=True)
            o = (_dot(p_loc.astype(BF), vw) + _dot(p_ctx.astype(BF), cv)) / den
            o = jnp.where(own, o, 0.0)
            o = (o[0:GRID_W] + o[GRID_W:2 * GRID_W]) + (o[2 * GRID_W:3 * GRID_W] + o[3 * GRID_W:])
            o_ref[r * GRID_W:(r + 1) * GRID_W, hg * gw:(hg + 1) * gw] = o.astype(BF)


def _nbr_attn(qkv, cache_k, cache_v, bias, l):
    return pl.pallas_call(
        _nbr_attn_kernel,
        grid=(N_LAT_B,),
        in_specs=[
            pl.BlockSpec((LAT_SEQ, 3 * WA), lambda b: (b, 0)),
            pl.BlockSpec((None, None, PAST, WA), lambda b: (b, l, 0, 0)),
            pl.BlockSpec((None, None, PAST, WA), lambda b: (b, l, 0, 0)),
            pl.BlockSpec((None,) + bias.shape[1:], lambda b: (l, 0, 0, 0, 0)),
        ],
        out_specs=pl.BlockSpec((LAT_SEQ, WA), lambda b: (b, 0)),
        out_shape=jax.ShapeDtypeStruct((N_LAT, WA), BF),
        compiler_params=pltpu.CompilerParams(
            dimension_semantics=("arbitrary",), vmem_limit_bytes=VMEM_LIMIT),
        name=f"nbr_attn_l{l}",
    )(qkv, cache_k, cache_v, bias)


def _lat_mixer_kernel(x_ref, oa_ref, f_ref, g_ref, mod_ref, npost_ref, wo_ref, fc_ref, fn_ref,
                      gn_ref, ws_ref, gb_ref, avg_ref, o_ref, mix_ref, wb_ref):
    _cast_weight_once(wo_ref, wb_ref)
    mix_ref[:, :WA] = oa_ref[...]
    mix_ref[:, WA:WA + WB] = _fourier(f_ref[...], fc_ref, fn_ref).astype(BF)
    mix_ref[:, WA + WB:] = _gmlp(g_ref[...], gn_ref, ws_ref, gb_ref, avg_ref).astype(BF)
    o_ref[...] = _mix_out(x_ref[...], mix_ref[...], mod_ref[5:6, :], npost_ref[1:2, :], wb_ref)


def _lat_mixer(x, oa, f, g, mods, npost, w_out, fc, fn, gn, ws, gb, avg, l):
    row = lambda b: (b, 0)
    return pl.pallas_call(
        _lat_mixer_kernel,
        grid=(N_LAT_B,),
        in_specs=[
            pl.BlockSpec((LAT_SEQ, D), row),
            pl.BlockSpec((LAT_SEQ, WA), row),
            pl.BlockSpec((LAT_SEQ, WB), row),
            pl.BlockSpec((LAT_SEQ, 2 * WC), row),
            pl.BlockSpec((None, None, 9, D), lambda b: (l, b, 0, 0)),
            pl.BlockSpec((None, 3, D), lambda b: (l, 0, 0)),
            pl.BlockSpec((None, D, D), lambda b: (l, 0, 0), pipeline_mode=pl.Buffered(1)),
            _const_spec(fc.shape),
            pl.BlockSpec(fn.shape, lambda b: (0, 0), pipeline_mode=pl.Buffered(1)),
            pl.BlockSpec((None, 1, WC), lambda b: (l, 0, 0)),
            pl.BlockSpec((None, NGRP, CHUNK, CHUNK), lambda b: (l, 0, 0, 0)),
            pl.BlockSpec((None, CHUNK, WC), lambda b: (l, 0, 0)),
            _const_spec(avg.shape),
        ],
        out_specs=pl.BlockSpec((LAT_SEQ, D), row),
        out_shape=jax.ShapeDtypeStruct((N_TOK, D), F32),
        input_output_aliases={0: 0},
        scratch_shapes=[pltpu.VMEM((LAT_SEQ, D), BF), pltpu.VMEM((D, D), BF)],
        compiler_params=pltpu.CompilerParams(
            dimension_semantics=("arbitrary",), vmem_limit_bytes=VMEM_LIMIT),
        name=f"lat_mixer_l{l}",
    )(x, oa, f, g, mods, npost, w_out, fc, fn, gn, ws, gb, avg)


def _dft_tables(n):
    idx = np.arange(n)
    ang = 2.0 * np.pi * ((idx[:, None] * idx[None, :]) % n) / n
    return (np.concatenate([np.cos(ang), -np.sin(ang)], axis=1) / np.sqrt(n)).astype(np.float32)


def _channel_dft_table():
    idx = np.arange(GC)
    ang = 2.0 * np.pi * ((idx[:, None] * idx[None, :]) % GC) / GC
    eye = np.eye(NGRP)
    return (np.concatenate([np.kron(eye, np.cos(ang)), np.kron(eye, np.sin(ang))], axis=1)
            / np.sqrt(GC)).astype(np.float32)


def _group_average_table():
    return np.kron(np.eye(NGRP), np.full((GC, GC), 1.0 / GC)).astype(np.float32)


def kernel(x_prompt, x_sample, cache_k, cache_v, c, c_ctx, ada_w, ada_b, norm_pre, norm_post,
           ffn_w_in, ffn_w_out, w_in, w_out, rpb, gmlp_norm, gmlp_w, gmlp_b):
    xs = (x_sample.reshape(N_LAT, D), x_prompt.reshape(N_CTX, D))
    cs =jnp.concatenate([c, c_ctx[None, :], jnp.zeros((MOD_ROWS - N_LAT_B - 1, D), F32)], axis=0)
    mods = _ada_mods(cs, ada_w, ada_b)
    bias = _bias_table(rpb)

    gn = gmlp_norm.reshape(DEPTH, 1, WC)
    gb = jnp.repeat(jnp.transpose(gmlp_b, (0, 2, 1)), GC, axis=2)
    ck = cache_k.reshape(N_LAT_B, DEPTH, PAST, WA)
    cv = cache_v.reshape(N_LAT_B, DEPTH, PAST, WA)

    fc = jnp.asarray(_channel_dft_table()).astype(BF)
    fn_ctx = jnp.asarray(_dft_tables(CTX_SEQ)).astype(BF)
    fn_lat = jnp.asarray(_dft_tables(LAT_SEQ)).astype(BF)
    avg = jnp.asarray(_group_average_table()).astype(BF)

    carried = (jnp.zeros((N_CTX_B, DEPTH, CTX_SEQ, WA), F32),) * 2
    for l in range(DEPTH):
        x = _ffn(xs, mods, norm_pre, norm_post, ffn_w_in, ffn_w_out, l, 0)
        qkv, f, g, new_k, new_v = _proj(x, mods, norm_pre, w_in, carried, l)
        carried = (new_k, new_v)
        oa = _nbr_attn(qkv, ck, cv, bias, l)
        x = _ctx_mixer(x, qkv, f, g, mods, norm_post, w_out, fc, fn_ctx, gn, gmlp_w, gb, avg, l)
        x = _lat_mixer(x, oa, f, g, mods, norm_post, w_out, fc, fn_lat, gn, gmlp_w, gb, avg, l)
        xs = _ffn((x,), mods, norm_pre, norm_post, ffn_w_in, ffn_w_out, l, 1,
                  split_out=(l == DEPTH - 1))
        if l < DEPTH - 1:
            xs = (xs,)
    xl, xc = xs

    kv_out = (N_CTX_B, DEPTH, CTX_SEQ, NH, HD)
    return (xc.reshape(N_CTX_B, CTX_SEQ, D), xl.reshape(N_LAT_B, LAT_SEQ, D),
            new_k.reshape(kv_out), new_v.reshape(kv_out))
```

```python
import functools

import numpy as np
import jax
import jax.numpy as jnp
from jax import lax
from jax.experimental import pallas as pl
from jax.experimental.pallas import tpu as pltpu

D = 1024
DEPTH = 4
N_CTX_B = 16
CTX_SEQ = 256
N_LAT_B = 4
LAT_SEQ = 1024
N_LAT = N_LAT_B * LAT_SEQ
N_CTX = N_CTX_B * CTX_SEQ
N_TOK = N_LAT + N_CTX
PAST = 256
GRID_W = 64
GRID_ROWS = LAT_SEQ // GRID_W
HD = 64
NH = 8
WA = 512
WB = 256
WC = 256
NGRP = 4
GC = 64
CHUNK = 128
PROJ_W = 3 * WA + WB + 2 * WC
D_FF = 2816
FF_CHUNK = 256
N_FF_CHUNKS = D_FF // FF_CHUNK
WIN_ROWS = 8
WIN_COLS = 16
N_ROW_OFF = 2 * WIN_ROWS - 1
N_COL_OFF = 2 * WIN_COLS - 1
N_DELTA = 8
EPS = 1e-6
NEG = -1e30
MOD_ROWS = 8
CTX_MOD_ROW = 4
CTX_PER_STEP = 2
HEADS_PER_GROUP = 4
N_HEAD_GROUPS = NH // HEADS_PER_GROUP

BF = jnp.bfloat16
F32 = jnp.float32

VMEM_LIMIT = 56 * 1024 * 1024


def _dot(a, b):
    return jnp.dot(a, b, preferred_element_type=F32)


def _dot_nt(a, b):
    return lax.dot_general(a, b, (((1,), (1,)), ((), ())), preferred_element_type=F32)


def _split3_dot(a, b):
    a0 = a.astype(BF)
    r1 = a - a0.astype(F32)
    a1 = r1.astype(BF)
    a2 = (r1 - a1.astype(F32)).astype(BF)
    return _dot(a0, b) + _dot(a1, b) + _dot(a2, b)


def _rms(x, g):
    return x * lax.rsqrt(jnp.mean(x * x, axis=-1, keepdims=True) + EPS) * g


def _mod_row(tile, tm):
    return jnp.minimum((tile * tm) // LAT_SEQ, CTX_MOD_ROW)


def _ada_kernel(cs_ref, w_ref, b_ref, o_ref):
    cs = cs_ref[...]
    s = cs * jax.nn.sigmoid(cs)
    o_ref[...] = _dot(s.astype(BF), w_ref[...].astype(BF)) + b_ref[...]


def _ada_mods(cs, ada_w, ada_b):
    n_out = 9 * D
    tn = D
    out = pl.pallas_call(
        _ada_kernel,
        grid=(DEPTH, n_out // tn),
        in_specs=[
            pl.BlockSpec((MOD_ROWS, D), lambda l, n: (0, 0)),
            pl.BlockSpec((None, D, tn), lambda l, n: (l, 0, n)),
            pl.BlockSpec((None, 1, tn), lambda l, n: (l, 0, n)),
        ],
        out_specs=pl.BlockSpec((None, MOD_ROWS, tn), lambda l, n: (l, 0, n)),
        out_shape=jax.ShapeDtypeStruct((DEPTH, MOD_ROWS, n_out), F32),
        compiler_params=pltpu.CompilerParams(
            dimension_semantics=("arbitrary", "arbitrary"), vmem_limit_bytes=VMEM_LIMIT),
        name="ada_mods",
    )(cs, ada_w, ada_b.reshape(DEPTH, 1, n_out))
    return out.reshape(DEPTH, MOD_ROWS, 9, D)


RPB_LANES = 128
RPB_HALF = RPB_LANES // 2


def _bias_kernel(rpb_ref, o_ref):
    qq = lax.broadcasted_iota(jnp.int32, (GRID_W, RPB_LANES), 0)
    lane = lax.broadcasted_iota(jnp.int32, (GRID_W, RPB_LANES), 1)
    second = lane >= GRID_W
    kk = jnp.where(second, lane - GRID_W, lane)
    col_off = jnp.clip(kk - qq + (WIN_COLS - 1), 0, N_COL_OFF - 1)
    gather_idx = jnp.where(second, col_off + RPB_HALF, col_off)
    col_start = jnp.clip(qq - WIN_COLS // 2, 0, GRID_W - WIN_COLS)
    visible = (kk >= col_start) & (kk < col_start + WIN_COLS)
    low_half = lax.broadcasted_iota(jnp.int32, (1, RPB_LANES), 1) < RPB_HALF
    for i in range(HEADS_PER_GROUP):
        for ro in range(N_ROW_OFF - 1):
            row0 = rpb_ref[i, ro:ro + 1, :]
            row1 = pltpu.roll(rpb_ref[i, ro + 1:ro + 2, :], RPB_HALF, 1)
            both = jnp.broadcast_to(jnp.where(low_half, row0, row1), (GRID_W, RPB_LANES))
            t = jnp.take_along_axis(both, gather_idx, axis=1)
            t = jnp.where(visible, t, NEG).astype(o_ref.dtype)
            for a in range(0, WIN_ROWS, 2):
                delta_idx = ro - a
                if 0 <= delta_idx < N_DELTA:
                    o_ref[delta_idx, i * GRID_W:(i + 1) * GRID_W, a * GRID_W:(a + 2) * GRID_W] = t


def _bias_table(rpb):
    rpb_padded = jnp.pad(rpb, ((0, 0), (0, 0), (0, 0), (0, RPB_LANES - N_COL_OFF)))
    return pl.pallas_call(
        _bias_kernel,
        grid=(DEPTH, N_HEAD_GROUPS),
        in_specs=[pl.BlockSpec((None, HEADS_PER_GROUP, N_ROW_OFF, RPB_LANES),
                               lambda l, g: (l, g, 0, 0))],
        out_specs=pl.BlockSpec((None, None, N_DELTA, HEADS_PER_GROUP * GRID_W, WIN_ROWS * GRID_W),
                               lambda l, g: (l, g, 0, 0, 0)),
        out_shape=jax.ShapeDtypeStruct(
            (DEPTH, N_HEAD_GROUPS, N_DELTA, HEADS_PER_GROUP * GRID_W, WIN_ROWS * GRID_W), BF),
        compiler_params=pltpu.CompilerParams(
            dimension_semantics=("arbitrary", "arbitrary"), vmem_limit_bytes=VMEM_LIMIT),
        name="rpb_table",
    )(rpb_padded)


def _lat_tile(t, n_lat_tiles):
    return jnp.minimum(t, n_lat_tiles - 1)


def _ctx_tile(t, n_lat_tiles):
    return jnp.maximum(t - n_lat_tiles, 0)


def _ffn_kernel(*refs, l, sub, split_in, split_out, n_lat_tiles):
    refs = list(refs)
    x_refs = [refs.pop(0) for _ in range(2 if split_in else 1)]
    mod_ref, npre_ref, npost_ref, wi_hbm, wo_hbm = [refs.pop(0) for _ in range(5)]
    o_refs = [refs.pop(0) for _ in range(2 if split_out else 1)]
    wg_ref, wu_ref, wo_ref, sg_ref, su_ref, so_ref, h_ref, acc_ref, sem = refs

    mo = 0 if sub == 0 else 6
    ni = 0 if sub == 0 else 2
    t = pl.program_id(0)
    is_lat = t < n_lat_tiles

    def load_x():
        if split_in:
            return jnp.where(is_lat, x_refs[0][...], x_refs[1][...])
        return x_refs[0][...]

    def store(out):
        if not split_out:
            o_refs[0][...] = out
            return

        @pl.when(is_lat)
        def _():
            o_refs[0][...] = out

        @pl.when(jnp.logical_not(is_lat))
        def _():
            o_refs[1][...] = out

    def modulated(x):
        shift = mod_ref[mo:mo + 1, :]
        scale = mod_ref[mo + 1:mo + 2, :]
        return (_rms(x, npre_ref[ni:ni + 1, :]) * (1.0 + scale) + shift).astype(BF)

    def residual(x, acc):
        gate = mod_ref[mo + 2:mo + 3, :]
        return x + 0.5 * gate * _rms(acc, npost_ref[ni:ni + 1, :])

    def chunk_update(h, wg, wu, wo):
        g = _dot(h, wg)
        u = _dot(h, wu)
        a = (g * jax.nn.sigmoid(g) * u).astype(BF)
        return _dot(a, wo)

    def chunk_copies(j, slot):
        cols = pl.ds(pl.multiple_of(j * FF_CHUNK, FF_CHUNK), FF_CHUNK)
        up = pl.ds(pl.multiple_of(D_FF + j * FF_CHUNK, FF_CHUNK), FF_CHUNK)
        return (
            pltpu.make_async_copy(wi_hbm.at[l, sub, :, cols], sg_ref.at[slot], sem.at[0, slot]),
            pltpu.make_async_copy(wi_hbm.at[l, sub, :, up], su_ref.at[slot], sem.at[1, slot]),
            pltpu.make_async_copy(wo_hbm.at[l, sub, cols, :], so_ref.at[slot], sem.at[2, slot]),
        )

    @pl.when(t == 0)
    def _():
        for cp in chunk_copies(0, 0):
            cp.start()
        h_ref[...] = modulated(load_x())
        acc_ref[...] = jnp.zeros(acc_ref.shape, F32)

        def body(j, carry):
            slot = lax.rem(j, 2)

            @pl.when(j + 1 < N_FF_CHUNKS)
            def _():
                for cp in chunk_copies(j + 1, 1 - slot):
                    cp.start()

            for cp in chunk_copies(j, slot):
                cp.wait()
            wg = sg_ref[slot].astype(BF)
            wu = su_ref[slot].astype(BF)
            wo = so_ref[slot].astype(BF)
            wg_ref[j] = wg
            wu_ref[j] = wu
            wo_ref[j] = wo
            acc_ref[...] += chunk_update(h_ref[...], wg, wu, wo)
            return carry

        lax.fori_loop(0, N_FF_CHUNKS, body, 0)
        store(residual(load_x(), acc_ref[...]))

    @pl.when(t > 0)
    def _():
        x = load_x()
        h = modulated(x)
        acc = jnp.zeros(x.shape, F32)
        for j in range(N_FF_CHUNKS):
            acc = acc + chunk_update(h, wg_ref[j], wu_ref[j], wo_ref[j])
        store(residual(x, acc))


def _ffn(xs, mods, npre, npost, wi, wo, l, sub, split_out=False, tm=512):
    n_tiles = N_TOK // tm
    nl = N_LAT // tm
    split_in = len(xs) == 2
    lat_idx = lambda t: (_lat_tile(t, nl), 0)
    ctx_idx = lambda t: (_ctx_tile(t, nl), 0)
    split_specs = [pl.BlockSpec((tm, D), lat_idx), pl.BlockSpec((tm, D), ctx_idx)]
    joint_specs = [pl.BlockSpec((tm, D), lambda t: (t, 0))]
    split_shapes = [jax.ShapeDtypeStruct((N_LAT, D), F32), jax.ShapeDtypeStruct((N_CTX, D), F32)]
    joint_shapes = [jax.ShapeDtypeStruct((N_TOK, D), F32)]
    out = pl.pallas_call(
        functools.partial(_ffn_kernel, l=l, sub=sub, split_in=split_in, split_out=split_out,
                          n_lat_tiles=nl),
        grid=(n_tiles,),
        in_specs=(split_specs if split_in else joint_specs) + [
            pl.BlockSpec((None, None, 9, D), lambda t: (l, _mod_row(t, tm), 0, 0)),
            pl.BlockSpec((None, 3, D), lambda t: (l, 0, 0)),
            pl.BlockSpec((None, 3, D), lambda t: (l, 0, 0)),
            pl.BlockSpec(memory_space=pl.ANY),
            pl.BlockSpec(memory_space=pl.ANY),
        ],
        out_specs=split_specs if split_out else joint_specs,
        out_shape=split_shapes if split_out else joint_shapes,
        scratch_shapes=[
            pltpu.VMEM((N_FF_CHUNKS, D, FF_CHUNK), BF),
            pltpu.VMEM((N_FF_CHUNKS, D, FF_CHUNK), BF),
            pltpu.VMEM((N_FF_CHUNKS, FF_CHUNK, D), BF),
            pltpu.VMEM((2, D, FF_CHUNK), F32),
            pltpu.VMEM((2, D, FF_CHUNK), F32),
            pltpu.VMEM((2, FF_CHUNK, D), F32),
            pltpu.VMEM((tm, D), BF),
            pltpu.VMEM((tm, D), F32),
            pltpu.SemaphoreType.DMA((3, 2)),
        ],
        compiler_params=pltpu.CompilerParams(
            dimension_semantics=("arbitrary",), vmem_limit_bytes=VMEM_LIMIT),
        name=f"ffn_l{l}_s{sub}",
    )(*xs, mods, npre, npost, wi, wo)
    return tuple(out) if split_out else out[0]


def _proj_kernel(x_ref, mod_ref, npre_ref, w_ref, *rest, n_carried):
    qkv_ref, f_ref, g_ref, k_ref, v_ref, wb_ref = rest[n_carried:]

    @pl.when(pl.program_id(0) == 0)
    def _():
        wb_ref[...] = w_ref[...].astype(BF)

    x = x_ref[...]
    shift = mod_ref[3:4, :]
    scale = mod_ref[4:5, :]
    h = (_rms(x, npre_ref[1:2, :]) * (1.0 + scale) + shift).astype(BF)
    z = _dot(h, wb_ref[...])
    qkv_ref[...] = z[:, :3 * WA].astype(BF)
    f_ref[...] = z[:, 3 * WA:3 * WA + WB].astype(BF)
    g_ref[...] = z[:, 3 * WA + WB:].astype(BF)
    k_ref[...] = z[:, WA:2 * WA].reshape(k_ref.shape)
    v_ref[...] = z[:, 2 * WA:3 * WA].reshape(v_ref.shape)


def _proj(x, mods, npre, w_in, carried, l, tm=512):
    n_tiles = N_TOK // tm
    nl = N_LAT // tm
    bpt = tm // CTX_SEQ
    kv_shape = jax.ShapeDtypeStruct((N_CTX_B, DEPTH, CTX_SEQ, WA), F32)
    kv_spec = pl.BlockSpec((bpt, None, CTX_SEQ, WA), lambda t: (_ctx_tile(t, nl), l, 0, 0))
    n_fixed = 4
    return pl.pallas_call(
        functools.partial(_proj_kernel, n_carried=len(carried)),
        grid=(n_tiles,),
        in_specs=[
            pl.BlockSpec((tm, D), lambda t: (t, 0)),
            pl.BlockSpec((None, None, 9, D), lambda t: (l, _mod_row(t, tm), 0, 0)),
            pl.BlockSpec((None, 3, D), lambda t: (l, 0, 0)),
            pl.BlockSpec((None, D, PROJ_W), lambda t: (l, 0, 0), pipeline_mode=pl.Buffered(1)),
        ] + [pl.BlockSpec(memory_space=pl.ANY)] * len(carried),
        out_specs=[
            pl.BlockSpec((tm, 3 * WA), lambda t: (t, 0)),
            pl.BlockSpec((tm, WB), lambda t: (t, 0)),
            pl.BlockSpec((tm, 2 * WC), lambda t: (t, 0)),
            kv_spec,
            kv_spec,
        ],
        out_shape=[
            jax.ShapeDtypeStruct((N_TOK, 3 * WA), BF),
            jax.ShapeDtypeStruct((N_TOK, WB), BF),
            jax.ShapeDtypeStruct((N_TOK, 2 * WC), BF),
            kv_shape,
            kv_shape,
        ],
        scratch_shapes=[pltpu.VMEM((D, PROJ_W), BF)],
        input_output_aliases={n_fixed + i: 3 + i for i in range(len(carried))},
        compiler_params=pltpu.CompilerParams(
            dimension_semantics=("arbitrary",), vmem_limit_bytes=VMEM_LIMIT),
        name=f"proj_l{l}",
    )(x, mods, npre, w_in, *carried)


def _fourier(f, fc_ref, fn_ref):
    n = fn_ref.shape[0]
    xcs = _dot(f, fc_ref[...]).astype(BF)
    outs = []
    for b in range(f.shape[0] // n):
        xb = xcs[b * n:(b + 1) * n]
        stacked = jnp.concatenate([xb[:, :WB], xb[:, WB:]], axis=0)
        outs.append(_dot(fn_ref[...], stacked))
    return outs[0] if len(outs) == 1 else jnp.concatenate(outs, axis=0)


def _gmlp(g, gn_ref, ws_ref, gb_ref, avg_ref):
    gl = jax.nn.gelu(g.astype(F32), approximate=True)
    u = gl[:, :WC]
    v = gl[:, WC:]
    avg = avg_ref[...]
    mu = _split3_dot(v, avg)
    d = v - mu
    var = _split3_dot(d * d, avg)
    vn = (d * lax.rsqrt(var + EPS) * gn_ref[...]).astype(BF)
    lane_group = lax.broadcasted_iota(jnp.int32, (CHUNK, WC), 1) // GC
    outs = []
    for c in range(g.shape[0] // CHUNK):
        vc = vn[c * CHUNK:(c + 1) * CHUNK, :]
        sp = _dot(ws_ref[0].astype(BF), vc)
        for grp in range(1, NGRP):
            sp = jnp.where(lane_group == grp, _dot(ws_ref[grp].astype(BF), vc), sp)
        outs.append(u[c * CHUNK:(c + 1) * CHUNK, :] * (sp + gb_ref[...]))
    return jnp.concatenate(outs, axis=0)


def _cast_weight_once(w_ref, wb_ref):
    @pl.when(pl.program_id(0) == 0)
    def _():
        wb_ref[...] = w_ref[...].astype(BF)


def _mix_out(x, o, gate, npost, wb_ref):
    y = _dot(o, wb_ref[...])
    return x + gate * _rms(y, npost)


def _ctx_mixer_kernel(x_ref, qkv_ref, f_ref, g_ref, mod_ref, npost_ref, wo_ref, fc_ref, fn_ref,
                      gn_ref, ws_ref, gb_ref, avg_ref, o_ref, mix_ref, wb_ref):
    _cast_weight_once(wo_ref, wb_ref)
    gw = HEADS_PER_GROUP * HD
    pair_w = 2 * HD
    first_of_pair = lax.broadcasted_iota(jnp.int32, (CTX_SEQ, pair_w), 1) < HD
    own = (lax.broadcasted_iota(jnp.int32, (HEADS_PER_GROUP * CTX_SEQ, gw), 0) // CTX_SEQ
           == lax.broadcasted_iota(jnp.int32, (HEADS_PER_GROUP * CTX_SEQ, gw), 1) // HD)
    lane_head = lax.broadcasted_iota(jnp.int32, (CTX_SEQ, gw), 1) // HD
    zero = jnp.zeros((), BF)
    for b in range(x_ref.shape[0] // CTX_SEQ):
        rows = slice(b * CTX_SEQ, (b + 1) * CTX_SEQ)
        scores = []
        for h in range(NH):
            lanes = slice((h // 2) * pair_w, (h // 2 + 1) * pair_w)
            q2 = qkv_ref[rows, lanes]
            k2 = qkv_ref[rows, WA + (h // 2) * pair_w:WA + (h // 2 + 1) * pair_w]
            qh = jnp.where(first_of_pair, q2, zero) if h % 2 == 0 else jnp.where(first_of_pair, zero, q2)
            scores.append(_dot_nt(qh, k2) * (HD ** -0.5))
        probs, dens = [], []
        for s in scores:
            p = jnp.exp(s - jnp.max(s, axis=-1, keepdims=True))
            dens.append(jnp.sum(p, axis=-1, keepdims=True))
            probs.append(p.astype(BF))
        for grp in range(N_HEAD_GROUPS):
            hs = range(grp * HEADS_PER_GROUP, (grp + 1) * HEADS_PER_GROUP)
            v4 = qkv_ref[rows, 2 * WA + grp * gw:2 * WA + (grp + 1) * gw]
            vbd = jnp.where(own, jnp.concatenate([v4] * HEADS_PER_GROUP, axis=0), zero)
            o = _dot(jnp.concatenate([probs[h] for h in hs], axis=1), vbd)
            den = dens[hs[0]]
            for i in range(1, HEADS_PER_GROUP):
                den = jnp.where(lane_head == i, dens[hs[i]], den)
            mix_ref[rows, grp * gw:(grp + 1) * gw] = (o / den).astype(BF)
    mix_ref[:, WA:WA + WB] = _fourier(f_ref[...], fc_ref, fn_ref).astype(BF)
    mix_ref[:, WA + WB:] = _gmlp(g_ref[...], gn_ref, ws_ref, gb_ref, avg_ref).astype(BF)
    o_ref[...] = _mix_out(x_ref[...], mix_ref[...], mod_ref[5:6, :], npost_ref[1:2, :], wb_ref)


def _const_spec(shape):
    nd = len(shape)
    return pl.BlockSpec(shape, lambda *_: (0,) * nd)


def _ctx_mixer(x, qkv, f, g, mods, npost, w_out, fc, fn, gn, ws, gb, avg, l):
    tm = CTX_PER_STEP * CTX_SEQ
    off = N_LAT // tm
    row = lambda b: (b + off, 0)
    return pl.pallas_call(
        _ctx_mixer_kernel,
        grid=(N_CTX_B // CTX_PER_STEP,),
        in_specs=[
            pl.BlockSpec((tm, D), row),
            pl.BlockSpec((tm, 3 * WA), row),
            pl.BlockSpec((tm, WB), row),
            pl.BlockSpec((tm, 2 * WC), row),
            pl.BlockSpec((None, None, 9, D), lambda b: (l, CTX_MOD_ROW, 0, 0)),
            pl.BlockSpec((None, 3, D), lambda b: (l, 0, 0)),
            pl.BlockSpec((None, D, D), lambda b: (l, 0, 0), pipeline_mode=pl.Buffered(1)),
            _const_spec(fc.shape),
            _const_spec(fn.shape),
            pl.BlockSpec((None, 1, WC), lambda b: (l, 0, 0)),
            pl.BlockSpec((None, NGRP, CHUNK, CHUNK), lambda b: (l, 0, 0, 0)),
            pl.BlockSpec((None, CHUNK, WC), lambda b: (l, 0, 0)),
            _const_spec(avg.shape),
        ],
        out_specs=pl.BlockSpec((tm, D), row),
        out_shape=jax.ShapeDtypeStruct((N_TOK, D), F32),
        input_output_aliases={0: 0},
        scratch_shapes=[pltpu.VMEM((tm, D), BF), pltpu.VMEM((D, D), BF)],
        compiler_params=pltpu.CompilerParams(
            dimension_semantics=("arbitrary",), vmem_limit_bytes=VMEM_LIMIT),
        name=f"ctx_mixer_l{l}",
    )(x, qkv, f, g, mods, npost, w_out, fc, fn, gn, ws, gb, avg)


def _window_start(r):
    return min(max(r - WIN_ROWS // 2, 0), GRID_ROWS - WIN_ROWS)


def _nbr_attn_kernel(qkv_ref, ck_ref, cv_ref, bias_ref, o_ref):
    gw = HEADS_PER_GROUP * HD
    row_head = lax.broadcasted_iota(jnp.int32, (HEADS_PER_GROUP * GRID_W, gw), 0) // GRID_W
    lane_head = lax.broadcasted_iota(jnp.int32, (HEADS_PER_GROUP * GRID_W, gw), 1) // HD
    own = row_head == lane_head
    for hg in range(N_HEAD_GROUPS):
        ck = ck_ref[:, hg * gw:(hg + 1) * gw].astype(BF)
        cv = cv_ref[:, hg * gw:(hg + 1) * gw].astype(BF)
        for r in range(GRID_ROWS):
            ws = _window_start(r)
            q = qkv_ref[r * GRID_W:(r + 1) * GRID_W, hg * gw:(hg + 1) * gw]
            qs = jnp.where(own, jnp.concatenate([q] * HEADS_PER_GROUP, axis=0), jnp.zeros((), BF))
            kw = qkv_ref[ws * GRID_W:(ws + WIN_ROWS) * GRID_W, WA + hg * gw:WA + (hg + 1) * gw]
            vw = qkv_ref[ws * GRID_W:(ws + WIN_ROWS) * GRID_W, 2 * WA + hg * gw:2 * WA + (hg + 1) * gw]
            s_loc = _dot_nt(qs, kw) * (HD ** -0.5) + bias_ref[hg, ws - r + N_DELTA - 1].astype(F32)
            s_ctx = _dot_nt(qs, ck) * (HD ** -0.5)
            m = jnp.maximum(jnp.max(s_loc, axis=-1, keepdims=True),
                            jnp.max(s_ctx, axis=-1, keepdims=True))
            p_loc = jnp.exp(s_loc - m)
            p_ctx = jnp.exp(s_ctx - m)
            den = jnp.sum(p_loc, axis=-1, keepdims=True) + jnp.sum(p_ctx, axis=-1, keepdims=True)
            o = (_dot(p_loc.astype(BF), vw) + _dot(p_ctx.astype(BF), cv)) / den
            o = jnp.where(own, o, 0.0)
            o = (o[0:GRID_W] + o[GRID_W:2 * GRID_W]) + (o[2 * GRID_W:3 * GRID_W] + o[3 * GRID_W:])
            o_ref[r * GRID_W:(r + 1) * GRID_W, hg * gw:(hg + 1) * gw] = o.astype(BF)


def _nbr_attn(qkv, cache_k, cache_v, bias, l):
    return pl.pallas_call(
        _nbr_attn_kernel,
        grid=(N_LAT_B,),
        in_specs=[
            pl.BlockSpec((LAT_SEQ, 3 * WA), lambda b: (b, 0)),
            pl.BlockSpec((None, None, PAST, WA), lambda b: (b, l, 0, 0)),
            pl.BlockSpec((None, None, PAST, WA), lambda b: (b, l, 0, 0)),
            pl.BlockSpec((None,) + bias.shape[1:], lambda b: (l, 0, 0, 0, 0)),
        ],
        out_specs=pl.BlockSpec((LAT_SEQ, WA), lambda b: (b, 0)),
        out_shape=jax.ShapeDtypeStruct((N_LAT, WA), BF),
        compiler_params=pltpu.CompilerParams(
            dimension_semantics=("arbitrary",), vmem_limit_bytes=VMEM_LIMIT),
        name=f"nbr_attn_l{l}",
    )(qkv, cache_k, cache_v, bias)


def _lat_mixer_kernel(x_ref, oa_ref, f_ref, g_ref, mod_ref, npost_ref, wo_ref, fc_ref, fn_ref,
                      gn_ref, ws_ref, gb_ref, avg_ref, o_ref, mix_ref, wb_ref):
    _cast_weight_once(wo_ref, wb_ref)
    mix_ref[:, :WA] = oa_ref[...]
    mix_ref[:, WA:WA + WB] = _fourier(f_ref[...], fc_ref, fn_ref).astype(BF)
    mix_ref[:, WA + WB:] = _gmlp(g_ref[...], gn_ref, ws_ref, gb_ref, avg_ref).astype(BF)
    o_ref[...] = _mix_out(x_ref[...], mix_ref[...], mod_ref[5:6, :], npost_ref[1:2, :], wb_ref)


def _lat_mixer(x, oa, f, g, mods, npost, w_out, fc, fn, gn, ws, gb, avg, l):
    row = lambda b: (b, 0)
    return pl.pallas_call(
        _lat_mixer_kernel,
        grid=(N_LAT_B,),
        in_specs=[
            pl.BlockSpec((LAT_SEQ, D), row),
            pl.BlockSpec((LAT_SEQ, WA), row),
            pl.BlockSpec((LAT_SEQ, WB), row),
            pl.BlockSpec((LAT_SEQ, 2 * WC), row),
            pl.BlockSpec((None, None, 9, D), lambda b: (l, b, 0, 0)),
            pl.BlockSpec((None, 3, D), lambda b: (l, 0, 0)),
            pl.BlockSpec((None, D, D), lambda b: (l, 0, 0), pipeline_mode=pl.Buffered(1)),
            _const_spec(fc.shape),
            pl.BlockSpec(fn.shape, lambda b: (0, 0), pipeline_mode=pl.Buffered(1)),
            pl.BlockSpec((None, 1, WC), lambda b: (l, 0, 0)),
            pl.BlockSpec((None, NGRP, CHUNK, CHUNK), lambda b: (l, 0, 0, 0)),
            pl.BlockSpec((None, CHUNK, WC), lambda b: (l, 0, 0)),
            _const_spec(avg.shape),
        ],
        out_specs=pl.BlockSpec((LAT_SEQ, D), row),
        out_shape=jax.ShapeDtypeStruct((N_TOK, D), F32),
        input_output_aliases={0: 0},
        scratch_shapes=[pltpu.VMEM((LAT_SEQ, D), BF), pltpu.VMEM((D, D), BF)],
        compiler_params=pltpu.CompilerParams(
            dimension_semantics=("arbitrary",), vmem_limit_bytes=VMEM_LIMIT),
        name=f"lat_mixer_l{l}",
    )(x, oa, f, g, mods, npost, w_out, fc, fn, gn, ws, gb, avg)


def _dft_tables(n):
    idx = np.arange(n)
    ang = 2.0 * np.pi * ((idx[:, None] * idx[None, :]) % n) / n
    return (np.concatenate([np.cos(ang), -np.sin(ang)], axis=1) / np.sqrt(n)).astype(np.float32)


def _channel_dft_table():
    idx = np.arange(GC)
    ang = 2.0 * np.pi * ((idx[:, None] * idx[None, :]) % GC) / GC
    eye = np.eye(NGRP)
    return (np.concatenate([np.kron(eye, np.cos(ang)), np.kron(eye, np.sin(ang))], axis=1)
            / np.sqrt(GC)).astype(np.float32)


def _group_average_table():
    return np.kron(np.eye(NGRP), np.full((GC, GC), 1.0 / GC)).astype(np.float32)


def kernel(x_prompt, x_sample, cache_k, cache_v, c, c_ctx, ada_w, ada_b, norm_pre, norm_post,
           ffn_w_in, ffn_w_out, w_in, w_out, rpb, gmlp_norm, gmlp_w, gmlp_b):
    xs = (x_sample.reshape(N_LAT, D), x_prompt.reshape(N_CTX, D))
    cs =jnp.concatenate([c, c_ctx[None, :], jnp.zeros((MOD_ROWS - N_LAT_B - 1, D), F32)], axis=0)
    mods = _ada_mods(cs, ada_w, ada_b)
    bias = _bias_table(rpb)

    gn = gmlp_norm.reshape(DEPTH, 1, WC)
    gb = jnp.repeat(jnp.transpose(gmlp_b, (0, 2, 1)), GC, axis=2)
    ck = cache_k.reshape(N_LAT_B, DEPTH, PAST, WA)
    cv = cache_v.reshape(N_LAT_B, DEPTH, PAST, WA)

    fc = jnp.asarray(_channel_dft_table()).astype(BF)
    fn_ctx = jnp.asarray(_dft_tables(CTX_SEQ)).astype(BF)
    fn_lat = jnp.asarray(_dft_tables(LAT_SEQ)).astype(BF)
    avg = jnp.asarray(_group_average_table()).astype(BF)

    carried = (jnp.zeros((N_CTX_B, DEPTH, CTX_SEQ, WA), F32),) * 2
    for l in range(DEPTH):
        x = _ffn(xs, mods, norm_pre, norm_post, ffn_w_in, ffn_w_out, l, 0)
        qkv, f, g, new_k, new_v = _proj(x, mods, norm_pre, w_in, carried, l)
        carried = (new_k, new_v)
        oa = _nbr_attn(qkv, ck, cv, bias, l)
        x = _ctx_mixer(x, qkv, f, g, mods, norm_post, w_out, fc, fn_ctx, gn, gmlp_w, gb, avg, l)
        x = _lat_mixer(x, oa, f, g, mods, norm_post, w_out, fc, fn_lat, gn, gmlp_w, gb, avg, l)
        xs = _ffn((x,), mods, norm_pre, norm_post, ffn_w_in, ffn_w_out, l, 1,
                  split_out=(l == DEPTH - 1))
        if l < DEPTH - 1:
            xs = (xs,)
    xl, xc = xs

    kv_out = (N_CTX_B, DEPTH, CTX_SEQ, NH, HD)
    return (xc.reshape(N_CTX_B, CTX_SEQ, D), xl.reshape(N_LAT_B, LAT_SEQ, D),
            new_k.reshape(kv_out), new_v.reshape(kv_out))
```

```python
import functools

import numpy as np
import jax
import jax.numpy as jnp
from jax import lax
from jax.experimental import pallas as pl
from jax.experimental.pallas import tpu as pltpu

D = 1024
DEPTH = 4
N_CTX_B = 16
CTX_SEQ = 256
N_LAT_B = 4
LAT_SEQ = 1024
N_LAT = N_LAT_B * LAT_SEQ
N_CTX = N_CTX_B * CTX_SEQ
N_TOK = N_LAT + N_CTX
PAST = 256
GRID_W = 64
GRID_ROWS = LAT_SEQ // GRID_W
HD = 64
NH = 8
WA = 512
WB = 256
WC = 256
NGRP = 4
GC = 64
CHUNK = 128
PROJ_W = 3 * WA + WB + 2 * WC
D_FF = 2816
FF_CHUNK = 256
N_FF_CHUNKS = D_FF // FF_CHUNK
WIN_ROWS = 8
WIN_COLS = 16
N_ROW_OFF = 2 * WIN_ROWS - 1
N_COL_OFF = 2 * WIN_COLS - 1
N_DELTA = 8
EPS = 1e-6
NEG = -1e30
MOD_ROWS = 8
CTX_MOD_ROW = 4
CTX_PER_STEP = 2
HEADS_PER_GROUP = 4
N_HEAD_GROUPS = NH // HEADS_PER_GROUP

BF = jnp.bfloat16
F32 = jnp.float32
QK_SCALE = HD ** -0.5

VMEM_LIMIT = 56 * 1024 * 1024


def _dot(a, b):
    return jnp.dot(a, b, preferred_element_type=F32)


def _dot_nt(a, b):
    return lax.dot_general(a, b, (((1,), (1,)), ((), ())), preferred_element_type=F32)


def _split3_dot(a, b):
    a0 = a.astype(BF)
    r1 = a - a0.astype(F32)
    a1 = r1.astype(BF)
    a2 = (r1 - a1.astype(F32)).astype(BF)
    return _dot(a0, b) + _dot(a1, b) + _dot(a2, b)


def _rms(x, g):
    return x * lax.rsqrt(jnp.mean(x * x, axis=-1, keepdims=True) + EPS) * g


def _mod_row(tile, tm):
    return jnp.minimum((tile * tm) // LAT_SEQ, CTX_MOD_ROW)


def _ada_kernel(cs_ref, w_ref, b_ref, o_ref):
    cs = cs_ref[...]
    s = cs * jax.nn.sigmoid(cs)
    o_ref[...] = _dot(s.astype(BF), w_ref[...].astype(BF)) + b_ref[...]


def _ada_mods(cs, ada_w, ada_b):
    n_out = 9 * D
    tn = D
    out = pl.pallas_call(
        _ada_kernel,
        grid=(DEPTH, n_out // tn),
        in_specs=[
            pl.BlockSpec((MOD_ROWS, D), lambda l, n: (0, 0)),
            pl.BlockSpec((None, D, tn), lambda l, n: (l, 0, n)),
            pl.BlockSpec((None, 1, tn), lambda l, n: (l, 0, n)),
        ],
        out_specs=pl.BlockSpec((None, MOD_ROWS, tn), lambda l, n: (l, 0, n)),
        out_shape=jax.ShapeDtypeStruct((DEPTH, MOD_ROWS, n_out), F32),
        compiler_params=pltpu.CompilerParams(
            dimension_semantics=("arbitrary", "arbitrary"), vmem_limit_bytes=VMEM_LIMIT),
        name="ada_mods",
    )(cs, ada_w, ada_b.reshape(DEPTH, 1, n_out))
    return out.reshape(DEPTH, MOD_ROWS, 9, D)


RPB_LANES = 128
RPB_HALF = RPB_LANES // 2


def _bias_kernel(rpb_ref, o_ref):
    qq = lax.broadcasted_iota(jnp.int32, (GRID_W, RPB_LANES), 0)
    lane = lax.broadcasted_iota(jnp.int32, (GRID_W, RPB_LANES), 1)
    second = lane >= GRID_W
    kk = jnp.where(second, lane - GRID_W, lane)
    col_off = jnp.clip(kk - qq + (WIN_COLS - 1), 0, N_COL_OFF - 1)
    gather_idx = jnp.where(second, col_off + RPB_HALF, col_off)
    col_start = jnp.clip(qq - WIN_COLS // 2, 0, GRID_W - WIN_COLS)
    visible = (kk >= col_start) & (kk < col_start + WIN_COLS)
    low_half = lax.broadcasted_iota(jnp.int32, (1, RPB_LANES), 1) < RPB_HALF
    for i in range(HEADS_PER_GROUP):
        for ro in range(N_ROW_OFF - 1):
            row0 = rpb_ref[i, ro:ro + 1, :]
            row1 = pltpu.roll(rpb_ref[i, ro + 1:ro + 2, :], RPB_HALF, 1)
            both = jnp.broadcast_to(jnp.where(low_half, row0, row1), (GRID_W, RPB_LANES))
            t = jnp.take_along_axis(both, gather_idx, axis=1)
            t = jnp.where(visible, t, NEG).astype(o_ref.dtype)
            for a in range(0, WIN_ROWS, 2):
                delta_idx = ro - a
                if 0 <= delta_idx < N_DELTA:
                    o_ref[delta_idx, i * GRID_W:(i + 1) * GRID_W, a * GRID_W:(a + 2) * GRID_W] = t


def _bias_table(rpb):
    rpb_padded = jnp.pad(rpb, ((0, 0), (0, 0), (0, 0), (0, RPB_LANES - N_COL_OFF)))
    return pl.pallas_call(
        _bias_kernel,
        grid=(DEPTH, N_HEAD_GROUPS),
        in_specs=[pl.BlockSpec((None, HEADS_PER_GROUP, N_ROW_OFF, RPB_LANES),
                               lambda l, g: (l, g, 0, 0))],
        out_specs=pl.BlockSpec((None, None, N_DELTA, HEADS_PER_GROUP * GRID_W, WIN_ROWS * GRID_W),
                               lambda l, g: (l, g, 0, 0, 0)),
        out_shape=jax.ShapeDtypeStruct(
            (DEPTH, N_HEAD_GROUPS, N_DELTA, HEADS_PER_GROUP * GRID_W, WIN_ROWS * GRID_W), BF),
        compiler_params=pltpu.CompilerParams(
            dimension_semantics=("arbitrary", "arbitrary"), vmem_limit_bytes=VMEM_LIMIT),
        name="rpb_table",
    )(rpb_padded)


def _lat_tile(t, n_lat_tiles):
    return jnp.minimum(t, n_lat_tiles - 1)


def _ctx_tile(t, n_lat_tiles):
    return jnp.maximum(t - n_lat_tiles, 0)


def _ffn_kernel(*refs, l, sub, split_in, split_out, n_lat_tiles, n_tiles):
    refs = list(refs)
    n_x = 2 if split_in else 1
    xprev_refs = [refs.pop(0) for _ in range(n_x)]
    xnext_refs = [refs.pop(0) for _ in range(n_x)]
    mprev_ref, mnext_ref, npre_ref, npost_ref, wi_hbm, wo_hbm = [refs.pop(0) for _ in range(6)]
    o_refs = [refs.pop(0) for _ in range(2 if split_out else 1)]
    wg_ref, wu_ref, wo_ref, sg_ref, su_ref, so_ref, h_ref, acc_ref, sem = refs

    mo = 0 if sub == 0 else 6
    ni = 0 if sub == 0 else 2
    t = pl.program_id(0)

    def load(x_refs, tile):
        if split_in:
            return jnp.where(jnp.int32(tile) < n_lat_tiles, x_refs[0][...], x_refs[1][...])
        return x_refs[0][...]

    def store(out, tile):
        if not split_out:
            o_refs[0][...] = out
            return
        tile = jnp.int32(tile)

        @pl.when(tile < n_lat_tiles)
        def _():
            o_refs[0][...] = out

        @pl.when(tile >= n_lat_tiles)
        def _():
            o_refs[1][...] = out

    def modulated(x, mod_ref):
        shift = mod_ref[mo:mo + 1, :]
        scale = mod_ref[mo + 1:mo + 2, :]
        return (_rms(x, npre_ref[ni:ni + 1, :]) * (1.0 + scale) + shift).astype(BF)

    def finish(tile, slot):
        gate = mprev_ref[mo + 2:mo + 3, :]
        x = load(xprev_refs, tile)
        store(x + 0.5 * gate * _rms(acc_ref[slot], npost_ref[ni:ni + 1, :]), tile)

    def chunk_update(h, wg, wu, wo):
        g = _dot(h, wg)
        u = _dot(h, wu)
        a = (g * jax.nn.sigmoid(g) * u).astype(BF)
        return _dot(a, wo)

    def chunk_copies(j, slot):
        cols = pl.ds(pl.multiple_of(j * FF_CHUNK, FF_CHUNK), FF_CHUNK)
        up = pl.ds(pl.multiple_of(D_FF + j * FF_CHUNK, FF_CHUNK), FF_CHUNK)
        return (
            pltpu.make_async_copy(wi_hbm.at[l, sub, :, cols], sg_ref.at[slot], sem.at[0, slot]),
            pltpu.make_async_copy(wi_hbm.at[l, sub, :, up], su_ref.at[slot], sem.at[1, slot]),
            pltpu.make_async_copy(wo_hbm.at[l, sub, cols, :], so_ref.at[slot], sem.at[2, slot]),
        )

    @pl.when(t == 0)
    def _():
        for cp in chunk_copies(0, 0):
            cp.start()
        h_ref[0] = modulated(load(xprev_refs, 0), mprev_ref)
        acc_ref[0] = jnp.zeros(acc_ref.shape[1:], F32)

        def body(j, carry):
            slot = lax.rem(j, 2)

            @pl.when(j + 1 < N_FF_CHUNKS)
            def _():
                for cp in chunk_copies(j + 1, 1 - slot):
                    cp.start()

            for cp in chunk_copies(j, slot):
                cp.wait()
            wg = sg_ref[slot].astype(BF)
            wu = su_ref[slot].astype(BF)
            wo = so_ref[slot].astype(BF)
            wg_ref[j] = wg
            wu_ref[j] = wu
            wo_ref[j] = wo
            acc_ref[0] += chunk_update(h_ref[0], wg, wu, wo)
            return carry

        lax.fori_loop(0, N_FF_CHUNKS, body, 0)
        h_ref[1] = modulated(load(xnext_refs, 1), mnext_ref)

    @pl.when((t > 0) & (t < n_tiles))
    def _():
        slot = lax.rem(t, 2)
        h = h_ref[slot]
        acc = jnp.zeros(acc_ref.shape[1:], F32)
        for j in range(N_FF_CHUNKS):
            acc = acc + chunk_update(h, wg_ref[j], wu_ref[j], wo_ref[j])
        acc_ref[slot] = acc
        h_ref[1 - slot] = modulated(load(xnext_refs, jnp.minimum(t + 1, n_tiles - 1)), mnext_ref)
        finish(t - 1, 1 - slot)

    @pl.when(t == n_tiles)
    def _():
        finish(n_tiles - 1, (n_tiles - 1) % 2)


def _ffn(xs, mods, npre, npost, wi, wo, l, sub, split_out=False, tm=512):
    n_tiles = N_TOK // tm
    nl = N_LAT // tm
    split_in = len(xs) == 2
    prev_tile = lambda t: jnp.maximum(t - 1, 0)
    next_tile = lambda t: jnp.minimum(t + 1, n_tiles - 1)

    def x_specs(split, tile):
        if split:
            return [pl.BlockSpec((tm, D), lambda t: (_lat_tile(tile(t), nl), 0)),
                    pl.BlockSpec((tm, D), lambda t: (_ctx_tile(tile(t), nl), 0))]
        return [pl.BlockSpec((tm, D), lambda t: (tile(t), 0))]

    def mod_spec(tile):
        return pl.BlockSpec((None, None, 9, D), lambda t: (l, _mod_row(tile(t), tm), 0, 0))

    split_shapes = [jax.ShapeDtypeStruct((N_LAT, D), F32), jax.ShapeDtypeStruct((N_CTX, D), F32)]
    joint_shapes = [jax.ShapeDtypeStruct((N_TOK, D), F32)]
    out = pl.pallas_call(
        functools.partial(_ffn_kernel, l=l, sub=sub, split_in=split_in, split_out=split_out,
                          n_lat_tiles=nl, n_tiles=n_tiles),
        grid=(n_tiles + 1,),
        in_specs=x_specs(split_in, prev_tile) + x_specs(split_in, next_tile) + [
            mod_spec(prev_tile),
            mod_spec(next_tile),
            pl.BlockSpec((None, 3, D), lambda t: (l, 0, 0)),
            pl.BlockSpec((None, 3, D), lambda t: (l, 0, 0)),
            pl.BlockSpec(memory_space=pl.ANY),
            pl.BlockSpec(memory_space=pl.ANY),
        ],
        out_specs=x_specs(split_out, prev_tile),
        out_shape=split_shapes if split_out else joint_shapes,
        scratch_shapes=[
            pltpu.VMEM((N_FF_CHUNKS, D, FF_CHUNK), BF),
            pltpu.VMEM((N_FF_CHUNKS, D, FF_CHUNK), BF),
            pltpu.VMEM((N_FF_CHUNKS, FF_CHUNK, D), BF),
            pltpu.VMEM((2, D, FF_CHUNK), F32),
            pltpu.VMEM((2, D, FF_CHUNK), F32),
            pltpu.VMEM((2, FF_CHUNK, D), F32),
            pltpu.VMEM((2, tm, D), BF),
            pltpu.VMEM((2, tm, D), F32),
            pltpu.SemaphoreType.DMA((3, 2)),
        ],
        compiler_params=pltpu.CompilerParams(
            dimension_semantics=("arbitrary",), vmem_limit_bytes=VMEM_LIMIT),
        name=f"ffn_l{l}_s{sub}",
    )(*xs, *xs, mods, mods, npre, npost, wi, wo)
    return tuple(out) if split_out else out[0]


def _proj_kernel(x_ref, mod_ref, npre_ref, w_ref, *rest, n_carried):
    qkv_ref, f_ref, g_ref, k_ref, v_ref, wb_ref = rest[n_carried:]

    @pl.when(pl.program_id(0) == 0)
    def _():
        wb_ref[...] = w_ref[...].astype(BF)

    x = x_ref[...]
    shift = mod_ref[3:4, :]
    scale = mod_ref[4:5, :]
    h = (_rms(x, npre_ref[1:2, :]) * (1.0 + scale) + shift).astype(BF)
    z = _dot(h, wb_ref[...])
    qkv_ref[...] = z[:, :3 * WA].astype(BF)
    f_ref[...] = z[:, 3 * WA:3 * WA + WB].astype(BF)
    g_ref[...] = z[:, 3 * WA + WB:].astype(BF)
    k_ref[...] = z[:, WA:2 * WA].reshape(k_ref.shape)
    v_ref[...] = z[:, 2 * WA:3 * WA].reshape(v_ref.shape)


def _proj(x, mods, npre, w_in, carried, l, tm=512):
    n_tiles = N_TOK // tm
    nl = N_LAT // tm
    bpt = tm // CTX_SEQ
    kv_shape = jax.ShapeDtypeStruct((N_CTX_B, DEPTH, CTX_SEQ, WA), F32)
    kv_spec = pl.BlockSpec((bpt, None, CTX_SEQ, WA), lambda t: (_ctx_tile(t, nl), l, 0, 0))
    n_fixed = 4
    return pl.pallas_call(
        functools.partial(_proj_kernel, n_carried=len(carried)),
        grid=(n_tiles,),
        in_specs=[
            pl.BlockSpec((tm, D), lambda t: (t, 0)),
            pl.BlockSpec((None, None, 9, D), lambda t: (l, _mod_row(t, tm), 0, 0)),
            pl.BlockSpec((None, 3, D), lambda t: (l, 0, 0)),
            pl.BlockSpec((None, D, PROJ_W), lambda t: (l, 0, 0), pipeline_mode=pl.Buffered(1)),
        ] + [pl.BlockSpec(memory_space=pl.ANY)] * len(carried),
        out_specs=[
            pl.BlockSpec((tm, 3 * WA), lambda t: (t, 0)),
            pl.BlockSpec((tm, WB), lambda t: (t, 0)),
            pl.BlockSpec((tm, 2 * WC), lambda t: (t, 0)),
            kv_spec,
            kv_spec,
        ],
        out_shape=[
            jax.ShapeDtypeStruct((N_TOK, 3 * WA), BF),
            jax.ShapeDtypeStruct((N_TOK, WB), BF),
            jax.ShapeDtypeStruct((N_TOK, 2 * WC), BF),
            kv_shape,
            kv_shape,
        ],
        scratch_shapes=[pltpu.VMEM((D, PROJ_W), BF)],
        input_output_aliases={n_fixed + i: 3 + i for i in range(len(carried))},
        compiler_params=pltpu.CompilerParams(
            dimension_semantics=("arbitrary",), vmem_limit_bytes=VMEM_LIMIT),
        name=f"proj_l{l}",
    )(x, mods, npre, w_in, *carried)


def _fourier(f, fc_ref, fn_ref):
    n = fn_ref.shape[0]
    xcs = _dot(f, fc_ref[...]).astype(BF)
    outs = []
    for b in range(f.shape[0] // n):
        xb = xcs[b * n:(b + 1) * n]
        stacked = jnp.concatenate([xb[:, :WB], xb[:, WB:]], axis=0)
        outs.append(_dot(fn_ref[...], stacked))
    return outs[0] if len(outs) == 1 else jnp.concatenate(outs, axis=0)


def _gmlp(g, gn_ref, ws_ref, gb_ref, avg_ref):
    gl = jax.nn.gelu(g.astype(F32), approximate=True)
    u = gl[:, :WC]
    v = gl[:, WC:]
    avg = avg_ref[...]
    mu = _split3_dot(v, avg)
    d = v - mu
    var = _split3_dot(d * d, avg)
    vn = (d * lax.rsqrt(var + EPS) * gn_ref[...]).astype(BF)
    lane_group = lax.broadcasted_iota(jnp.int32, (CHUNK, WC), 1) // GC
    outs = []
    for c in range(g.shape[0] // CHUNK):
        vc = vn[c * CHUNK:(c + 1) * CHUNK, :]
        sp = _dot(ws_ref[0].astype(BF), vc)
        for grp in range(1, NGRP):
            sp = jnp.where(lane_group == grp, _dot(ws_ref[grp].astype(BF), vc), sp)
        outs.append(u[c * CHUNK:(c + 1) * CHUNK, :] * (sp + gb_ref[...]))
    return jnp.concatenate(outs, axis=0)


def _cast_weight_once(w_ref, wb_ref):
    @pl.when(pl.program_id(0) == 0)
    def _():
        wb_ref[...] = w_ref[...].astype(BF)


def _mix_out(x, o, gate, npost, wb_ref):
    y = _dot(o, wb_ref[...])
    return x + gate * _rms(y, npost)


def _ctx_mixer_kernel(x_ref, qkv_ref, f_ref, g_ref, mod_ref, npost_ref, wo_ref, fc_ref, fn_ref,
                      gn_ref, ws_ref, gb_ref, avg_ref, o_ref, mix_ref, wb_ref):
    _cast_weight_once(wo_ref, wb_ref)
    gw = HEADS_PER_GROUP * HD
    pair_w = 2 * HD
    first_of_pair = lax.broadcasted_iota(jnp.int32, (CTX_SEQ, pair_w), 1) < HD
    own = (lax.broadcasted_iota(jnp.int32, (HEADS_PER_GROUP * CTX_SEQ, gw), 0) // CTX_SEQ
           == lax.broadcasted_iota(jnp.int32, (HEADS_PER_GROUP * CTX_SEQ, gw), 1) // HD)
    lane_head = lax.broadcasted_iota(jnp.int32, (CTX_SEQ, gw), 1) // HD
    zero = jnp.zeros((), BF)
    for b in range(x_ref.shape[0] // CTX_SEQ):
        rows = slice(b * CTX_SEQ, (b + 1) * CTX_SEQ)
        scores = []
        for h in range(NH):
            lanes = slice((h // 2) * pair_w, (h // 2 + 1) * pair_w)
            q2 = qkv_ref[rows, lanes] * QK_SCALE
            k2 = qkv_ref[rows, WA + (h // 2) * pair_w:WA + (h // 2 + 1) * pair_w]
            qh = jnp.where(first_of_pair, q2, zero) if h % 2 == 0 else jnp.where(first_of_pair, zero, q2)
            scores.append(_dot_nt(qh, k2))
        probs, dens = [], []
        for s in scores:
            p = jnp.exp(s - jnp.max(s, axis=-1, keepdims=True))
            dens.append(jnp.sum(p, axis=-1, keepdims=True))
            probs.append(p.astype(BF))
        for grp in range(N_HEAD_GROUPS):
            hs = range(grp * HEADS_PER_GROUP, (grp + 1) * HEADS_PER_GROUP)
            v4 = qkv_ref[rows, 2 * WA + grp * gw:2 * WA + (grp + 1) * gw]
            vbd = jnp.where(own, jnp.concatenate([v4] * HEADS_PER_GROUP, axis=0), zero)
            o = _dot(jnp.concatenate([probs[h] for h in hs], axis=1), vbd)
            den = dens[hs[0]]
            for i in range(1, HEADS_PER_GROUP):
                den = jnp.where(lane_head == i, dens[hs[i]], den)
            mix_ref[rows, grp * gw:(grp + 1) * gw] = (o / den).astype(BF)
    mix_ref[:, WA:WA + WB] = _fourier(f_ref[...], fc_ref, fn_ref).astype(BF)
    mix_ref[:, WA + WB:] = _gmlp(g_ref[...], gn_ref, ws_ref, gb_ref, avg_ref).astype(BF)
    o_ref[...] = _mix_out(x_ref[...], mix_ref[...], mod_ref[5:6, :], npost_ref[1:2, :], wb_ref)


def _const_spec(shape):
    nd = len(shape)
    return pl.BlockSpec(shape, lambda *_: (0,) * nd)


def _ctx_mixer(x, qkv, f, g, mods, npost, w_out, fc, fn, gn, ws, gb, avg, l):
    tm = CTX_PER_STEP * CTX_SEQ
    off = N_LAT // tm
    row = lambda b: (b + off, 0)
    return pl.pallas_call(
        _ctx_mixer_kernel,
        grid=(N_CTX_B // CTX_PER_STEP,),
        in_specs=[
            pl.BlockSpec((tm, D), row),
            pl.BlockSpec((tm, 3 * WA), row),
            pl.BlockSpec((tm, WB), row),
            pl.BlockSpec((tm, 2 * WC), row),
            pl.BlockSpec((None, None, 9, D), lambda b: (l, CTX_MOD_ROW, 0, 0)),
            pl.BlockSpec((None, 3, D), lambda b: (l, 0, 0)),
            pl.BlockSpec((None, D, D), lambda b: (l, 0, 0), pipeline_mode=pl.Buffered(1)),
            _const_spec(fc.shape),
            _const_spec(fn.shape),
            pl.BlockSpec((None, 1, WC), lambda b: (l, 0, 0)),
            pl.BlockSpec((None, NGRP, CHUNK, CHUNK), lambda b: (l, 0, 0, 0)),
            pl.BlockSpec((None, CHUNK, WC), lambda b: (l, 0, 0)),
            _const_spec(avg.shape),
        ],
        out_specs=pl.BlockSpec((tm, D), row),
        out_shape=jax.ShapeDtypeStruct((N_TOK, D), F32),
        input_output_aliases={0: 0},
        scratch_shapes=[pltpu.VMEM((tm, D), BF), pltpu.VMEM((D, D), BF)],
        compiler_params=pltpu.CompilerParams(
            dimension_semantics=("arbitrary",), vmem_limit_bytes=VMEM_LIMIT),
        name=f"ctx_mixer_l{l}",
    )(x, qkv, f, g, mods, npost, w_out, fc, fn, gn, ws, gb, avg)


def _window_start(r):
    return min(max(r - WIN_ROWS // 2, 0), GRID_ROWS - WIN_ROWS)


def _nbr_attn_kernel(qkv_ref, ck_ref, cv_ref, bias_ref, o_ref):
    gw = HEADS_PER_GROUP * HD
    row_head = lax.broadcasted_iota(jnp.int32, (HEADS_PER_GROUP * GRID_W, gw), 0) // GRID_W
    lane_head = lax.broadcasted_iota(jnp.int32, (HEADS_PER_GROUP * GRID_W, gw), 1) // HD
    own = row_head == lane_head
    for hg in range(N_HEAD_GROUPS):
        ck = ck_ref[:, hg * gw:(hg + 1) * gw].astype(BF)
        cv = cv_ref[:, hg * gw:(hg + 1) * gw].astype(BF)
        for r in range(GRID_ROWS):
            ws = _window_start(r)
            q = qkv_ref[r * GRID_W:(r + 1) * GRID_W, hg * gw:(hg + 1) * gw] * QK_SCALE
            qs = jnp.where(own, jnp.concatenate([q] * HEADS_PER_GROUP, axis=0), jnp.zeros((), BF))
            kw = qkv_ref[ws * GRID_W:(ws + WIN_ROWS) * GRID_W, WA + hg * gw:WA + (hg + 1) * gw]
            vw = qkv_ref[ws * GRID_W:(ws + WIN_ROWS) * GRID_W, 2 * WA + hg * gw:2 * WA + (hg + 1) * gw]
            s_loc = _dot_nt(qs, kw) + bias_ref[hg, ws - r + N_DELTA - 1].astype(F32)
            s_ctx = _dot_nt(qs, ck)
            m = jnp.maximum(jnp.max(s_loc, axis=-1, keepdims=True),
                            jnp.max(s_ctx, axis=-1, keepdims=True))
            p_loc = jnp.exp(s_loc - m)
            p_ctx = jnp.exp(s_ctx - m)
            den = jnp.sum(p_loc, axis=-1, keepdims=True) + jnp.sum(p_ctx, axis=-1, keepdims=True)
            o = (_dot(p_loc.astype(BF), vw) + _dot(p_ctx.astype(BF), cv)) / den
            o = jnp.where(own, o, 0.0)
            o = (o[0:GRID_W] + o[GRID_W:2 * GRID_W]) + (o[2 * GRID_W:3 * GRID_W] + o[3 * GRID_W:])
            o_ref[r * GRID_W:(r + 1) * GRID_W, hg * gw:(hg + 1) * gw] = o.astype(BF)


def _nbr_attn(qkv, cache_k, cache_v, bias, l):
    return pl.pallas_call(
        _nbr_attn_kernel,
        grid=(N_LAT_B,),
        in_specs=[
            pl.BlockSpec((LAT_SEQ, 3 * WA), lambda b: (b, 0)),
            pl.BlockSpec((None, None, PAST, WA), lambda b: (b, l, 0, 0)),
            pl.BlockSpec((None, None, PAST, WA), lambda b: (b, l, 0, 0)),
            pl.BlockSpec((None,) + bias.shape[1:], lambda b: (l, 0, 0, 0, 0)),
        ],
        out_specs=pl.BlockSpec((LAT_SEQ, WA), lambda b: (b, 0)),
        out_shape=jax.ShapeDtypeStruct((N_LAT, WA), BF),
        compiler_params=pltpu.CompilerParams(
            dimension_semantics=("arbitrary",), vmem_limit_bytes=VMEM_LIMIT),
        name=f"nbr_attn_l{l}",
    )(qkv, cache_k, cache_v, bias)


def _lat_mixer_kernel(x_ref, oa_ref, f_ref, g_ref, mod_ref, npost_ref, wo_ref, fc_ref, fn_ref,
                      gn_ref, ws_ref, gb_ref, avg_ref, o_ref, mix_ref, wb_ref):
    _cast_weight_once(wo_ref, wb_ref)
    mix_ref[:, :WA] = oa_ref[...]
    mix_ref[:, WA:WA + WB] = _fourier(f_ref[...], fc_ref, fn_ref).astype(BF)
    mix_ref[:, WA + WB:] = _gmlp(g_ref[...], gn_ref, ws_ref, gb_ref, avg_ref).astype(BF)
    o_ref[...] = _mix_out(x_ref[...], mix_ref[...], mod_ref[5:6, :], npost_ref[1:2, :], wb_ref)


def _lat_mixer(x, oa, f, g, mods, npost, w_out, fc, fn, gn, ws, gb, avg, l):
    row = lambda b: (b, 0)
    return pl.pallas_call(
        _lat_mixer_kernel,
        grid=(N_LAT_B,),
        in_specs=[
            pl.BlockSpec((LAT_SEQ, D), row),
            pl.BlockSpec((LAT_SEQ, WA), row),
            pl.BlockSpec((LAT_SEQ, WB), row),
            pl.BlockSpec((LAT_SEQ, 2 * WC), row),
            pl.BlockSpec((None, None, 9, D), lambda b: (l, b, 0, 0)),
            pl.BlockSpec((None, 3, D), lambda b: (l, 0, 0)),
            pl.BlockSpec((None, D, D), lambda b: (l, 0, 0), pipeline_mode=pl.Buffered(1)),
            _const_spec(fc.shape),
            pl.BlockSpec(fn.shape, lambda b: (0, 0), pipeline_mode=pl.Buffered(1)),
            pl.BlockSpec((None, 1, WC), lambda b: (l, 0, 0)),
            pl.BlockSpec((None, NGRP, CHUNK, CHUNK), lambda b: (l, 0, 0, 0)),
            pl.BlockSpec((None, CHUNK, WC), lambda b: (l, 0, 0)),
            _const_spec(avg.shape),
        ],
        out_specs=pl.BlockSpec((LAT_SEQ, D), row),
        out_shape=jax.ShapeDtypeStruct((N_TOK, D), F32),
        input_output_aliases={0: 0},
        scratch_shapes=[pltpu.VMEM((LAT_SEQ, D), BF), pltpu.VMEM((D, D), BF)],
        compiler_params=pltpu.CompilerParams(
            dimension_semantics=("arbitrary",), vmem_limit_bytes=VMEM_LIMIT),
        name=f"lat_mixer_l{l}",
    )(x, oa, f, g, mods, npost, w_out, fc, fn, gn, ws, gb, avg)


def _dft_tables(n):
    idx = np.arange(n)
    ang = 2.0 * np.pi * ((idx[:, None] * idx[None, :]) % n) / n
    return (np.concatenate([np.cos(ang), -np.sin(ang)], axis=1) / np.sqrt(n)).astype(np.float32)


def _channel_dft_table():
    idx = np.arange(GC)
    ang = 2.0 * np.pi * ((idx[:, None] * idx[None, :]) % GC) / GC
    eye = np.eye(NGRP)
    return (np.concatenate([np.kron(eye, np.cos(ang)), np.kron(eye, np.sin(ang))], axis=1)
            / np.sqrt(GC)).astype(np.float32)


def _group_average_table():
    return np.kron(np.eye(NGRP), np.full((GC, GC), 1.0 / GC)).astype(np.float32)


def kernel(x_prompt, x_sample, cache_k, cache_v, c, c_ctx, ada_w, ada_b, norm_pre, norm_post,
           ffn_w_in, ffn_w_out, w_in, w_out, rpb, gmlp_norm, gmlp_w, gmlp_b):
    xs = (x_sample.reshape(N_LAT, D), x_prompt.reshape(N_CTX, D))
    cs =jnp.concatenate([c, c_ctx[None, :], jnp.zeros((MOD_ROWS - N_LAT_B - 1, D), F32)], axis=0)
    mods = _ada_mods(cs, ada_w, ada_b)
    bias = _bias_table(rpb)

    gn = gmlp_norm.reshape(DEPTH, 1, WC)
    gb = jnp.repeat(jnp.transpose(gmlp_b, (0, 2, 1)), GC, axis=2)
    ck = cache_k.reshape(N_LAT_B, DEPTH, PAST, WA)
    cv = cache_v.reshape(N_LAT_B, DEPTH, PAST, WA)

    fc = jnp.asarray(_channel_dft_table()).astype(BF)
    fn_ctx = jnp.asarray(_dft_tables(CTX_SEQ)).astype(BF)
    fn_lat = jnp.asarray(_dft_tables(LAT_SEQ)).astype(BF)
    avg = jnp.asarray(_group_average_table()).astype(BF)

    carried = (jnp.zeros((N_CTX_B, DEPTH, CTX_SEQ, WA), F32),) * 2
    for l in range(DEPTH):
        x = _ffn(xs, mods, norm_pre, norm_post, ffn_w_in, ffn_w_out, l, 0)
        qkv, f, g, new_k, new_v = _proj(x, mods, norm_pre, w_in, carried, l)
        carried = (new_k, new_v)
        oa = _nbr_attn(qkv, ck, cv, bias, l)
        x = _ctx_mixer(x, qkv, f, g, mods, norm_post, w_out, fc, fn_ctx, gn, gmlp_w, gb, avg, l)
        x = _lat_mixer(x, oa, f, g, mods, norm_post, w_out, fc, fn_lat, gn, gmlp_w, gb, avg, l)
        xs = _ffn((x,), mods, norm_pre, norm_post, ffn_w_in, ffn_w_out, l, 1,
                  split_out=(l == DEPTH - 1))
        if l < DEPTH - 1:
            xs = (xs,)
    xl, xc = xs

    kv_out = (N_CTX_B, DEPTH, CTX_SEQ, NH, HD)
    return (xc.reshape(N_CTX_B, CTX_SEQ, D), xl.reshape(N_LAT_B, LAT_SEQ, D),
            new_k.reshape(kv_out), new_v.reshape(kv_out))
```

```python
import functools

import numpy as np
import jax
import jax.numpy as jnp
from jax import lax
from jax.experimental import pallas as pl
from jax.experimental.pallas import tpu as pltpu

D = 1024
DEPTH = 4
N_CTX_B = 16
CTX_SEQ = 256
N_LAT_B = 4
LAT_SEQ = 1024
N_LAT = N_LAT_B * LAT_SEQ
N_CTX = N_CTX_B * CTX_SEQ
N_TOK = N_LAT + N_CTX
PAST = 256
GRID_W = 64
GRID_ROWS = LAT_SEQ // GRID_W
HD = 64
NH = 8
WA = 512
WB = 256
WC = 256
NGRP = 4
GC = 64
CHUNK = 128
PROJ_W = 3 * WA + WB + 2 * WC
D_FF = 2816
FF_CHUNK = 256
N_FF_CHUNKS = D_FF // FF_CHUNK
FFN_TILE = 1024
FFN_SUBTILE = 512
WIN_ROWS = 8
WIN_COLS = 16
N_ROW_OFF = 2 * WIN_ROWS - 1
N_COL_OFF = 2 * WIN_COLS - 1
N_DELTA = 8
EPS = 1e-6
NEG = -1e30
MOD_ROWS = 8
CTX_MOD_ROW = 4
CTX_PER_STEP = 2
HEADS_PER_GROUP = 4
N_HEAD_GROUPS = NH // HEADS_PER_GROUP

BF = jnp.bfloat16
F32 = jnp.float32
QK_SCALE = HD ** -0.5

VMEM_LIMIT = 56 * 1024 * 1024


def _dot(a, b):
    return jnp.dot(a, b, preferred_element_type=F32)


def _dot_nt(a, b):
    return lax.dot_general(a, b, (((1,), (1,)), ((), ())), preferred_element_type=F32)


def _split3_dot(a, b):
    a0 = a.astype(BF)
    r1 = a - a0.astype(F32)
    a1 = r1.astype(BF)
    a2 = (r1 - a1.astype(F32)).astype(BF)
    return _dot(a0, b) + _dot(a1, b) + _dot(a2, b)


def _rms(x, g):
    return x * lax.rsqrt(jnp.mean(x * x, axis=-1, keepdims=True) + EPS) * g


def _mod_row(tile, tm):
    return jnp.minimum((tile * tm) // LAT_SEQ, CTX_MOD_ROW)


def _ada_kernel(cs_ref, w_ref, b_ref, o_ref):
    cs = cs_ref[...]
    s = cs * jax.nn.sigmoid(cs)
    o_ref[...] = _dot(s.astype(BF), w_ref[...].astype(BF)) + b_ref[...]


def _ada_mods(cs, ada_w, ada_b):
    n_out = 9 * D
    tn = D
    out = pl.pallas_call(
        _ada_kernel,
        grid=(DEPTH, n_out // tn),
        in_specs=[
            pl.BlockSpec((MOD_ROWS, D), lambda l, n: (0, 0)),
            pl.BlockSpec((None, D, tn), lambda l, n: (l, 0, n)),
            pl.BlockSpec((None, 1, tn), lambda l, n: (l, 0, n)),
        ],
        out_specs=pl.BlockSpec((None, MOD_ROWS, tn), lambda l, n: (l, 0, n)),
        out_shape=jax.ShapeDtypeStruct((DEPTH, MOD_ROWS, n_out), F32),
        compiler_params=pltpu.CompilerParams(
            dimension_semantics=("arbitrary", "arbitrary"), vmem_limit_bytes=VMEM_LIMIT),
        name="ada_mods",
    )(cs, ada_w, ada_b.reshape(DEPTH, 1, n_out))
    return out.reshape(DEPTH, MOD_ROWS, 9, D)


RPB_LANES = 128
RPB_HALF = RPB_LANES // 2


def _bias_kernel(rpb_ref, o_ref):
    qq = lax.broadcasted_iota(jnp.int32, (GRID_W, RPB_LANES), 0)
    lane = lax.broadcasted_iota(jnp.int32, (GRID_W, RPB_LANES), 1)
    second = lane >= GRID_W
    kk = jnp.where(second, lane - GRID_W, lane)
    col_off = jnp.clip(kk - qq + (WIN_COLS - 1), 0, N_COL_OFF - 1)
    gather_idx = jnp.where(second, col_off + RPB_HALF, col_off)
    col_start = jnp.clip(qq - WIN_COLS // 2, 0, GRID_W - WIN_COLS)
    visible = (kk >= col_start) & (kk < col_start + WIN_COLS)
    low_half = lax.broadcasted_iota(jnp.int32, (1, RPB_LANES), 1) < RPB_HALF
    for i in range(HEADS_PER_GROUP):
        for ro in range(N_ROW_OFF - 1):
            row0 = rpb_ref[i, ro:ro + 1, :]
            row1 = pltpu.roll(rpb_ref[i, ro + 1:ro + 2, :], RPB_HALF, 1)
            both = jnp.broadcast_to(jnp.where(low_half, row0, row1), (GRID_W, RPB_LANES))
            t = jnp.take_along_axis(both, gather_idx, axis=1)
            t = jnp.where(visible, t, NEG).astype(o_ref.dtype)
            for a in range(0, WIN_ROWS, 2):
                delta_idx = ro - a
                if 0 <= delta_idx < N_DELTA:
                    o_ref[delta_idx, i * GRID_W:(i + 1) * GRID_W, a * GRID_W:(a + 2) * GRID_W] = t


def _bias_table(rpb):
    rpb_padded = jnp.pad(rpb, ((0, 0), (0, 0), (0, 0), (0, RPB_LANES - N_COL_OFF)))
    return pl.pallas_call(
        _bias_kernel,
        grid=(DEPTH, N_HEAD_GROUPS),
        in_specs=[pl.BlockSpec((None, HEADS_PER_GROUP, N_ROW_OFF, RPB_LANES),
                               lambda l, g: (l, g, 0, 0))],
        out_specs=pl.BlockSpec((None, None, N_DELTA, HEADS_PER_GROUP * GRID_W, WIN_ROWS * GRID_W),
                               lambda l, g: (l, g, 0, 0, 0)),
        out_shape=jax.ShapeDtypeStruct(
            (DEPTH, N_HEAD_GROUPS, N_DELTA, HEADS_PER_GROUP * GRID_W, WIN_ROWS * GRID_W), BF),
        compiler_params=pltpu.CompilerParams(
            dimension_semantics=("arbitrary", "arbitrary"), vmem_limit_bytes=VMEM_LIMIT),
        name="rpb_table",
    )(rpb_padded)


def _lat_tile(t, n_lat_tiles):
    return jnp.minimum(t, n_lat_tiles - 1)


def _ctx_tile(t, n_lat_tiles):
    return jnp.maximum(t - n_lat_tiles, 0)


def _ffn_kernel(*refs, l, sub, split_in, split_out, n_lat_tiles):
    refs = list(refs)
    x_refs = [refs.pop(0) for _ in range(2 if split_in else 1)]
    mod_ref, npre_ref, npost_ref, wi_hbm, wo_hbm = [refs.pop(0) for _ in range(5)]
    o_refs = [refs.pop(0) for _ in range(2 if split_out else 1)]
    wg_ref, wu_ref, wo_ref, sg_ref, su_ref, so_ref, h_ref, acc_ref, sem = refs

    mo = 0 if sub == 0 else 6
    ni = 0 if sub == 0 else 2
    t = pl.program_id(0)
    is_lat = t < n_lat_tiles

    def load_x(rows=slice(None)):
        if split_in:
            return jnp.where(is_lat, x_refs[0][rows, :], x_refs[1][rows, :])
        return x_refs[0][rows, :]

    def store(out, rows=slice(None)):
        if not split_out:
            o_refs[0][rows, :] = out
            return

        @pl.when(is_lat)
        def _():
            o_refs[0][rows, :] = out

        @pl.when(jnp.logical_not(is_lat))
        def _():
            o_refs[1][rows, :] = out

    def modulated(x):
        shift = mod_ref[mo:mo + 1, :]
        scale = mod_ref[mo + 1:mo + 2, :]
        return (_rms(x, npre_ref[ni:ni + 1, :]) * (1.0 + scale) + shift).astype(BF)

    def residual(x, acc):
        gate = mod_ref[mo + 2:mo + 3, :]
        return x + 0.5 * gate * _rms(acc, npost_ref[ni:ni + 1, :])

    def chunk_update(h, wg, wu, wo):
        g = _dot(h, wg)
        u = _dot(h, wu)
        a = (g * jax.nn.sigmoid(g) * u).astype(BF)
        return _dot(a, wo)

    def chunk_copies(j, slot):
        cols = pl.ds(pl.multiple_of(j * FF_CHUNK, FF_CHUNK), FF_CHUNK)
        up = pl.ds(pl.multiple_of(D_FF + j * FF_CHUNK, FF_CHUNK), FF_CHUNK)
        return (
            pltpu.make_async_copy(wi_hbm.at[l, sub, :, cols], sg_ref.at[slot], sem.at[0, slot]),
            pltpu.make_async_copy(wi_hbm.at[l, sub, :, up], su_ref.at[slot], sem.at[1, slot]),
            pltpu.make_async_copy(wo_hbm.at[l, sub, cols, :], so_ref.at[slot], sem.at[2, slot]),
        )

    @pl.when(t == 0)
    def _():
        for cp in chunk_copies(0, 0):
            cp.start()
        h_ref[...] = modulated(load_x())
        acc_ref[...] = jnp.zeros(acc_ref.shape, F32)

        def body(j, carry):
            slot = lax.rem(j, 2)

            @pl.when(j + 1 < N_FF_CHUNKS)
            def _():
                for cp in chunk_copies(j + 1, 1 - slot):
                    cp.start()

            for cp in chunk_copies(j, slot):
                cp.wait()
            wg = sg_ref[slot].astype(BF)
            wu = su_ref[slot].astype(BF)
            wo = so_ref[slot].astype(BF)
            wg_ref[j] = wg
            wu_ref[j] = wu
            wo_ref[j] = wo
            acc_ref[...] += chunk_update(h_ref[...], wg, wu, wo)
            return carry

        lax.fori_loop(0, N_FF_CHUNKS, body, 0)
        store(residual(load_x(), acc_ref[...]))

    @pl.when(t > 0)
    def _():
        for s in range(acc_ref.shape[0] // FFN_SUBTILE):
            rows = slice(s * FFN_SUBTILE, (s + 1) * FFN_SUBTILE)
            x = load_x(rows)
            h = modulated(x)
            acc = jnp.zeros(x.shape, F32)
            for j in range(N_FF_CHUNKS):
                acc = acc + chunk_update(h, wg_ref[j], wu_ref[j], wo_ref[j])
            store(residual(x, acc), rows)


def _ffn(xs, mods, npre, npost, wi, wo, l, sub, split_out=False):
    split_in = len(xs) == 2
    tm = FFN_SUBTILE if (split_in or split_out) else FFN_TILE
    n_tiles = N_TOK // tm
    nl = N_LAT // tm
    lat_idx = lambda t: (_lat_tile(t, nl), 0)
    ctx_idx = lambda t: (_ctx_tile(t, nl), 0)
    split_specs = [pl.BlockSpec((tm, D), lat_idx), pl.BlockSpec((tm, D), ctx_idx)]
    joint_specs = [pl.BlockSpec((tm, D), lambda t: (t, 0))]
    split_shapes = [jax.ShapeDtypeStruct((N_LAT, D), F32), jax.ShapeDtypeStruct((N_CTX, D), F32)]
    joint_shapes = [jax.ShapeDtypeStruct((N_TOK, D), F32)]
    out = pl.pallas_call(
        functools.partial(_ffn_kernel, l=l, sub=sub, split_in=split_in, split_out=split_out,
                          n_lat_tiles=nl),
        grid=(n_tiles,),
        in_specs=(split_specs if split_in else joint_specs) + [
            pl.BlockSpec((None, None, 9, D), lambda t: (l, _mod_row(t, tm), 0, 0)),
            pl.BlockSpec((None, 3, D), lambda t: (l, 0, 0)),
            pl.BlockSpec((None, 3, D), lambda t: (l, 0, 0)),
            pl.BlockSpec(memory_space=pl.ANY),
            pl.BlockSpec(memory_space=pl.ANY),
        ],
        out_specs=split_specs if split_out else joint_specs,
        out_shape=split_shapes if split_out else joint_shapes,
        scratch_shapes=[
            pltpu.VMEM((N_FF_CHUNKS, D, FF_CHUNK), BF),
            pltpu.VMEM((N_FF_CHUNKS, D, FF_CHUNK), BF),
            pltpu.VMEM((N_FF_CHUNKS, FF_CHUNK, D), BF),
            pltpu.VMEM((2, D, FF_CHUNK), F32),
            pltpu.VMEM((2, D, FF_CHUNK), F32),
            pltpu.VMEM((2, FF_CHUNK, D), F32),
            pltpu.VMEM((tm, D), BF),
            pltpu.VMEM((tm, D), F32),
            pltpu.SemaphoreType.DMA((3, 2)),
        ],
        compiler_params=pltpu.CompilerParams(
            dimension_semantics=("arbitrary",), vmem_limit_bytes=VMEM_LIMIT),
        name=f"ffn_l{l}_s{sub}",
    )(*xs, mods, npre, npost, wi, wo)
    return tuple(out) if split_out else out[0]


def _proj_kernel(x_ref, mod_ref, npre_ref, w_ref, *rest, n_carried):
    qkv_ref, f_ref, g_ref, k_ref, v_ref, wb_ref = rest[n_carried:]

    @pl.when(pl.program_id(0) == 0)
    def _():
        wb_ref[...] = w_ref[...].astype(BF)

    x = x_ref[...]
    shift = mod_ref[3:4, :]
    scale = mod_ref[4:5, :]
    h = (_rms(x, npre_ref[1:2, :]) * (1.0 + scale) + shift).astype(BF)
    z = _dot(h, wb_ref[...])
    qkv_ref[...] = z[:, :3 * WA].astype(BF)
    f_ref[...] = z[:, 3 * WA:3 * WA + WB].astype(BF)
    g_ref[...] = z[:, 3 * WA + WB:].astype(BF)
    k_ref[...] = z[:, WA:2 * WA].reshape(k_ref.shape)
    v_ref[...] = z[:, 2 * WA:3 * WA].reshape(v_ref.shape)


def _proj(x, mods, npre, w_in, carried, l, tm=512):
    n_tiles = N_TOK // tm
    nl = N_LAT // tm
    bpt = tm // CTX_SEQ
    kv_shape = jax.ShapeDtypeStruct((N_CTX_B, DEPTH, CTX_SEQ, WA), F32)
    kv_spec = pl.BlockSpec((bpt, None, CTX_SEQ, WA), lambda t: (_ctx_tile(t, nl), l, 0, 0))
    n_fixed = 4
    return pl.pallas_call(
        functools.partial(_proj_kernel, n_carried=len(carried)),
        grid=(n_tiles,),
        in_specs=[
            pl.BlockSpec((tm, D), lambda t: (t, 0)),
            pl.BlockSpec((None, None, 9, D), lambda t: (l, _mod_row(t, tm), 0, 0)),
            pl.BlockSpec((None, 3, D), lambda t: (l, 0, 0)),
            pl.BlockSpec((None, D, PROJ_W), lambda t: (l, 0, 0), pipeline_mode=pl.Buffered(1)),
        ] + [pl.BlockSpec(memory_space=pl.ANY)] * len(carried),
        out_specs=[
            pl.BlockSpec((tm, 3 * WA), lambda t: (t, 0)),
            pl.BlockSpec((tm, WB), lambda t: (t, 0)),
            pl.BlockSpec((tm, 2 * WC), lambda t: (t, 0)),
            kv_spec,
            kv_spec,
        ],
        out_shape=[
            jax.ShapeDtypeStruct((N_TOK, 3 * WA), BF),
            jax.ShapeDtypeStruct((N_TOK, WB), BF),
            jax.ShapeDtypeStruct((N_TOK, 2 * WC), BF),
            kv_shape,
            kv_shape,
        ],
        scratch_shapes=[pltpu.VMEM((D, PROJ_W), BF)],
        input_output_aliases={n_fixed + i: 3 + i for i in range(len(carried))},
        compiler_params=pltpu.CompilerParams(
            dimension_semantics=("arbitrary",), vmem_limit_bytes=VMEM_LIMIT),
        name=f"proj_l{l}",
    )(x, mods, npre, w_in, *carried)


def _fourier(f, fc_ref, fn_ref):
    n = fn_ref.shape[0]
    xcs = _dot(f, fc_ref[...]).astype(BF)
    outs = []
    for b in range(f.shape[0] // n):
        xb = xcs[b * n:(b + 1) * n]
        stacked = jnp.concatenate([xb[:, :WB], xb[:, WB:]], axis=0)
        outs.append(_dot(fn_ref[...], stacked))
    return outs[0] if len(outs) == 1 else jnp.concatenate(outs, axis=0)


def _gmlp(g, gn_ref, ws_ref, gb_ref, avg_ref):
    gl = jax.nn.gelu(g.astype(F32), approximate=True)
    u = gl[:, :WC]
    v = gl[:, WC:]
    avg = avg_ref[...]
    mu = _split3_dot(v, avg)
    d = v - mu
    var = _split3_dot(d * d, avg)
    vn = (d * lax.rsqrt(var + EPS) * gn_ref[...]).astype(BF)
    lane_group = lax.broadcasted_iota(jnp.int32, (CHUNK, WC), 1) // GC
    outs = []
    for c in range(g.shape[0] // CHUNK):
        vc = vn[c * CHUNK:(c + 1) * CHUNK, :]
        sp = _dot(ws_ref[0].astype(BF), vc)
        for grp in range(1, NGRP):
            sp = jnp.where(lane_group == grp, _dot(ws_ref[grp].astype(BF), vc), sp)
        outs.append(u[c * CHUNK:(c + 1) * CHUNK, :] * (sp + gb_ref[...]))
    return jnp.concatenate(outs, axis=0)


def _cast_weight_once(w_ref, wb_ref):
    @pl.when(pl.program_id(0) == 0)
    def _():
        wb_ref[...] = w_ref[...].astype(BF)


def _mix_out(x, o, gate, npost, wb_ref):
    y = _dot(o, wb_ref[...])
    return x + gate * _rms(y, npost)


def _ctx_mixer_kernel(x_ref, qkv_ref, f_ref, g_ref, mod_ref, npost_ref, wo_ref, fc_ref, fn_ref,
                      gn_ref, ws_ref, gb_ref, avg_ref, o_ref, mix_ref, wb_ref):
    _cast_weight_once(wo_ref, wb_ref)
    gw = HEADS_PER_GROUP * HD
    pair_w = 2 * HD
    first_of_pair = lax.broadcasted_iota(jnp.int32, (CTX_SEQ, pair_w), 1) < HD
    own = (lax.broadcasted_iota(jnp.int32, (HEADS_PER_GROUP * CTX_SEQ, gw), 0) // CTX_SEQ
           == lax.broadcasted_iota(jnp.int32, (HEADS_PER_GROUP * CTX_SEQ, gw), 1) // HD)
    lane_head = lax.broadcasted_iota(jnp.int32, (CTX_SEQ, gw), 1) // HD
    zero = jnp.zeros((), BF)
    for b in range(x_ref.shape[0] // CTX_SEQ):
        rows = slice(b * CTX_SEQ, (b + 1) * CTX_SEQ)
        scores = []
        for h in range(NH):
            lanes = slice((h // 2) * pair_w, (h // 2 + 1) * pair_w)
            q2 = qkv_ref[rows, lanes] * QK_SCALE
            k2 = qkv_ref[rows, WA + (h // 2) * pair_w:WA + (h // 2 + 1) * pair_w]
            qh = jnp.where(first_of_pair, q2, zero) if h % 2 == 0 else jnp.where(first_of_pair, zero, q2)
            scores.append(_dot_nt(qh, k2))
        probs, dens = [], []
        for s in scores:
            p = jnp.exp(s - jnp.max(s, axis=-1, keepdims=True))
            dens.append(jnp.sum(p, axis=-1, keepdims=True))
            probs.append(p.astype(BF))
        for grp in range(N_HEAD_GROUPS):
            hs = range(grp * HEADS_PER_GROUP, (grp + 1) * HEADS_PER_GROUP)
            v4 = qkv_ref[rows, 2 * WA + grp * gw:2 * WA + (grp + 1) * gw]
            vbd = jnp.where(own, jnp.concatenate([v4] * HEADS_PER_GROUP, axis=0), zero)
            o = _dot(jnp.concatenate([probs[h] for h in hs], axis=1), vbd)
            den = dens[hs[0]]
            for i in range(1, HEADS_PER_GROUP):
                den = jnp.where(lane_head == i, dens[hs[i]], den)
            mix_ref[rows, grp * gw:(grp + 1) * gw] = (o / den).astype(BF)
    mix_ref[:, WA:WA + WB] = _fourier(f_ref[...], fc_ref, fn_ref).astype(BF)
    mix_ref[:, WA + WB:] = _gmlp(g_ref[...], gn_ref, ws_ref, gb_ref, avg_ref).astype(BF)
    o_ref[...] = _mix_out(x_ref[...], mix_ref[...], mod_ref[5:6, :], npost_ref[1:2, :], wb_ref)


def _const_spec(shape):
    nd = len(shape)
    return pl.BlockSpec(shape, lambda *_: (0,) * nd)


def _ctx_mixer(x, qkv, f, g, mods, npost, w_out, fc, fn, gn, ws, gb, avg, l):
    tm = CTX_PER_STEP * CTX_SEQ
    off = N_LAT // tm
    row = lambda b: (b + off, 0)
    return pl.pallas_call(
        _ctx_mixer_kernel,
        grid=(N_CTX_B // CTX_PER_STEP,),
        in_specs=[
            pl.BlockSpec((tm, D), row),
            pl.BlockSpec((tm, 3 * WA), row),
            pl.BlockSpec((tm, WB), row),
            pl.BlockSpec((tm, 2 * WC), row),
            pl.BlockSpec((None, None, 9, D), lambda b: (l, CTX_MOD_ROW, 0, 0)),
            pl.BlockSpec((None, 3, D), lambda b: (l, 0, 0)),
            pl.BlockSpec((None, D, D), lambda b: (l, 0, 0), pipeline_mode=pl.Buffered(1)),
            _const_spec(fc.shape),
            _const_spec(fn.shape),
            pl.BlockSpec((None, 1, WC), lambda b: (l, 0, 0)),
            pl.BlockSpec((None, NGRP, CHUNK, CHUNK), lambda b: (l, 0, 0, 0)),
            pl.BlockSpec((None, CHUNK, WC), lambda b: (l, 0, 0)),
            _const_spec(avg.shape),
        ],
        out_specs=pl.BlockSpec((tm, D), row),
        out_shape=jax.ShapeDtypeStruct((N_TOK, D), F32),
        input_output_aliases={0: 0},
        scratch_shapes=[pltpu.VMEM((tm, D), BF), pltpu.VMEM((D, D), BF)],
        compiler_params=pltpu.CompilerParams(
            dimension_semantics=("arbitrary",), vmem_limit_bytes=VMEM_LIMIT),
        name=f"ctx_mixer_l{l}",
    )(x, qkv, f, g, mods, npost, w_out, fc, fn, gn, ws, gb, avg)


def _window_start(r):
    return min(max(r - WIN_ROWS // 2, 0), GRID_ROWS - WIN_ROWS)


def _nbr_attn_kernel(qkv_ref, ck_ref, cv_ref, bias_ref, o_ref):
    gw = HEADS_PER_GROUP * HD
    row_head = lax.broadcasted_iota(jnp.int32, (HEADS_PER_GROUP * GRID_W, gw), 0) // GRID_W
    lane_head = lax.broadcasted_iota(jnp.int32, (HEADS_PER_GROUP * GRID_W, gw), 1) // HD
    own = row_head == lane_head
    for hg in range(N_HEAD_GROUPS):
        ck = ck_ref[:, hg * gw:(hg + 1) * gw].astype(BF)
        cv = cv_ref[:, hg * gw:(hg + 1) * gw].astype(BF)
        for r in range(GRID_ROWS):
            ws = _window_start(r)
            q = qkv_ref[r * GRID_W:(r + 1) * GRID_W, hg * gw:(hg + 1) * gw] * QK_SCALE
            qs = jnp.where(own, jnp.concatenate([q] * HEADS_PER_GROUP, axis=0), jnp.zeros((), BF))
            kw = qkv_ref[ws * GRID_W:(ws + WIN_ROWS) * GRID_W, WA + hg * gw:WA + (hg + 1) * gw]
            vw = qkv_ref[ws * GRID_W:(ws + WIN_ROWS) * GRID_W, 2 * WA + hg * gw:2 * WA + (hg + 1) * gw]
            s_loc = _dot_nt(qs, kw) + bias_ref[hg, ws - r + N_DELTA - 1].astype(F32)
            s_ctx = _dot_nt(qs, ck)
            m = jnp.maximum(jnp.max(s_loc, axis=-1, keepdims=True),
                            jnp.max(s_ctx, axis=-1, keepdims=True))
            p_loc = jnp.exp(s_loc - m)
            p_ctx = jnp.exp(s_ctx - m)
            den = jnp.sum(p_loc, axis=-1, keepdims=True) + jnp.sum(p_ctx, axis=-1, keepdims=True)
            o = (_dot(p_loc.astype(BF), vw) + _dot(p_ctx.astype(BF), cv)) / den
            o = jnp.where(own, o, 0.0)
            o = (o[0:GRID_W] + o[GRID_W:2 * GRID_W]) + (o[2 * GRID_W:3 * GRID_W] + o[3 * GRID_W:])
            o_ref[r * GRID_W:(r + 1) * GRID_W, hg * gw:(hg + 1) * gw] = o.astype(BF)


def _nbr_attn(qkv, cache_k, cache_v, bias, l):
    return pl.pallas_call(
        _nbr_attn_kernel,
        grid=(N_LAT_B,),
        in_specs=[
            pl.BlockSpec((LAT_SEQ, 3 * WA), lambda b: (b, 0)),
            pl.BlockSpec((None, None, PAST, WA), lambda b: (b, l, 0, 0)),
            pl.BlockSpec((None, None, PAST, WA), lambda b: (b, l, 0, 0)),
            pl.BlockSpec((None,) + bias.shape[1:], lambda b: (l, 0, 0, 0, 0)),
        ],
        out_specs=pl.BlockSpec((LAT_SEQ, WA), lambda b: (b, 0)),
        out_shape=jax.ShapeDtypeStruct((N_LAT, WA), BF),
        compiler_params=pltpu.CompilerParams(
            dimension_semantics=("arbitrary",), vmem_limit_bytes=VMEM_LIMIT),
        name=f"nbr_attn_l{l}",
    )(qkv, cache_k, cache_v, bias)


def _lat_mixer_kernel(x_ref, oa_ref, f_ref, g_ref, mod_ref, npost_ref, wo_ref, fc_ref, fn_ref,
                      gn_ref, ws_ref, gb_ref, avg_ref, o_ref, mix_ref, wb_ref):
    _cast_weight_once(wo_ref, wb_ref)
    mix_ref[:, :WA] = oa_ref[...]
    mix_ref[:, WA:WA + WB] = _fourier(f_ref[...], fc_ref, fn_ref).astype(BF)
    mix_ref[:, WA + WB:] = _gmlp(g_ref[...], gn_ref, ws_ref, gb_ref, avg_ref).astype(BF)
    o_ref[...] = _mix_out(x_ref[...], mix_ref[...], mod_ref[5:6, :], npost_ref[1:2, :], wb_ref)


def _lat_mixer(x, oa, f, g, mods, npost, w_out, fc, fn, gn, ws, gb, avg, l):
    row = lambda b: (b, 0)
    return pl.pallas_call(
        _lat_mixer_kernel,
        grid=(N_LAT_B,),
        in_specs=[
            pl.BlockSpec((LAT_SEQ, D), row),
            pl.BlockSpec((LAT_SEQ, WA), row),
            pl.BlockSpec((LAT_SEQ, WB), row),
            pl.BlockSpec((LAT_SEQ, 2 * WC), row),
            pl.BlockSpec((None, None, 9, D), lambda b: (l, b, 0, 0)),
            pl.BlockSpec((None, 3, D), lambda b: (l, 0, 0)),
            pl.BlockSpec((None, D, D), lambda b: (l, 0, 0), pipeline_mode=pl.Buffered(1)),
            _const_spec(fc.shape),
            pl.BlockSpec(fn.shape, lambda b: (0, 0), pipeline_mode=pl.Buffered(1)),
            pl.BlockSpec((None, 1, WC), lambda b: (l, 0, 0)),
            pl.BlockSpec((None, NGRP, CHUNK, CHUNK), lambda b: (l, 0, 0, 0)),
            pl.BlockSpec((None, CHUNK, WC), lambda b: (l, 0, 0)),
            _const_spec(avg.shape),
        ],
        out_specs=pl.BlockSpec((LAT_SEQ, D), row),
        out_shape=jax.ShapeDtypeStruct((N_TOK, D), F32),
        input_output_aliases={0: 0},
        scratch_shapes=[pltpu.VMEM((LAT_SEQ, D), BF), pltpu.VMEM((D, D), BF)],
        compiler_params=pltpu.CompilerParams(
            dimension_semantics=("arbitrary",), vmem_limit_bytes=VMEM_LIMIT),
        name=f"lat_mixer_l{l}",
    )(x, oa, f, g, mods, npost, w_out, fc, fn, gn, ws, gb, avg)


def _dft_tables(n):
    idx = np.arange(n)
    ang = 2.0 * np.pi * ((idx[:, None] * idx[None, :]) % n) / n
    return (np.concatenate([np.cos(ang), -np.sin(ang)], axis=1) / np.sqrt(n)).astype(np.float32)


def _channel_dft_table():
    idx = np.arange(GC)
    ang = 2.0 * np.pi * ((idx[:, None] * idx[None, :]) % GC) / GC
    eye = np.eye(NGRP)
    return (np.concatenate([np.kron(eye, np.cos(ang)), np.kron(eye, np.sin(ang))], axis=1)
            / np.sqrt(GC)).astype(np.float32)


def _group_average_table():
    return np.kron(np.eye(NGRP), np.full((GC, GC), 1.0 / GC)).astype(np.float32)


def kernel(x_prompt, x_sample, cache_k, cache_v, c, c_ctx, ada_w, ada_b, norm_pre, norm_post,
           ffn_w_in, ffn_w_out, w_in, w_out, rpb, gmlp_norm, gmlp_w, gmlp_b):
    xs = (x_sample.reshape(N_LAT, D), x_prompt.reshape(N_CTX, D))
    cs = jnp.concatenate([c, c_ctx[None, :], jnp.zeros((MOD_ROWS - N_LAT_B - 1, D), F32)], axis=0)
    mods = _ada_mods(cs, ada_w, ada_b)
    bias = _bias_table(rpb)

    gn = gmlp_norm.reshape(DEPTH, 1, WC)
    gb = jnp.repeat(jnp.transpose(gmlp_b, (0, 2, 1)), GC, axis=2)
    ck = cache_k.reshape(N_LAT_B, DEPTH, PAST, WA)
    cv = cache_v.reshape(N_LAT_B, DEPTH, PAST, WA)

    fc = jnp.asarray(_channel_dft_table()).astype(BF)
    fn_ctx = jnp.asarray(_dft_tables(CTX_SEQ)).astype(BF)
    fn_lat = jnp.asarray(_dft_tables(LAT_SEQ)).astype(BF)
    avg = jnp.asarray(_group_average_table()).astype(BF)

    carried = (jnp.zeros((N_CTX_B, DEPTH, CTX_SEQ, WA), F32),) * 2
    for l in range(DEPTH):
        x = _ffn(xs, mods, norm_pre, norm_post, ffn_w_in, ffn_w_out, l, 0)
        qkv, f, g, new_k, new_v = _proj(x, mods, norm_pre, w_in, carried, l)
        carried = (new_k, new_v)
        oa = _nbr_attn(qkv, ck, cv, bias, l)
        x = _ctx_mixer(x, qkv, f, g, mods, norm_post, w_out, fc, fn_ctx, gn, gmlp_w, gb, avg, l)
        x = _lat_mixer(x, oa, f, g, mods, norm_post, w_out, fc, fn_lat, gn, gmlp_w, gb, avg, l)
        xs = _ffn((x,), mods, norm_pre, norm_post, ffn_w_in, ffn_w_out, l, 1,
                  split_out=(l == DEPTH - 1))
        if l < DEPTH - 1:
            xs = (xs,)
    xl, xc = xs

    kv_out = (N_CTX_B, DEPTH, CTX_SEQ, NH, HD)
    return (xc.reshape(N_CTX_B, CTX_SEQ, D), xl.reshape(N_LAT_B, LAT_SEQ, D),
            new_k.reshape(kv_out), new_v.reshape(kv_out))
```

```python
import functools

import numpy as np
import jax
import jax.numpy as jnp
from jax import lax
from jax.experimental import pallas as pl
from jax.experimental.pallas import tpu as pltpu

D = 1024
DEPTH = 4
N_CTX_B = 16
CTX_SEQ = 256
N_LAT_B = 4
LAT_SEQ = 1024
N_LAT = N_LAT_B * LAT_SEQ
N_CTX = N_CTX_B * CTX_SEQ
N_TOK = N_LAT + N_CTX
PAST = 256
GRID_W = 64
GRID_ROWS = LAT_SEQ // GRID_W
HD = 64
NH = 8
WA = 512
WB = 256
WC = 256
NGRP = 4
GC = 64
CHUNK = 128
PROJ_W = 3 * WA + WB + 2 * WC
D_FF = 2816
FF_CHUNK = 256
N_FF_CHUNKS = D_FF // FF_CHUNK
FFN_TILE = 1024
FFN_SUBTILE = 512
WIN_ROWS = 8
WIN_COLS = 16
N_ROW_OFF = 2 * WIN_ROWS - 1
N_COL_OFF = 2 * WIN_COLS - 1
N_DELTA = 8
EPS = 1e-6
NEG = -1e30
MOD_ROWS = 8
CTX_MOD_ROW = 4
CTX_PER_STEP = 2
HEADS_PER_GROUP = 4
N_HEAD_GROUPS = NH // HEADS_PER_GROUP

BF = jnp.bfloat16
F32 = jnp.float32
QK_SCALE = HD ** -0.5

VMEM_LIMIT = 56 * 1024 * 1024


def _dot(a, b):
    return jnp.dot(a, b, preferred_element_type=F32)


def _dot_nt(a, b):
    return lax.dot_general(a, b, (((1,), (1,)), ((), ())), preferred_element_type=F32)


def _split3_dot(a, b):
    a0 = a.astype(BF)
    r1 = a - a0.astype(F32)
    a1 = r1.astype(BF)
    a2 = (r1 - a1.astype(F32)).astype(BF)
    return _dot(a0, b) + _dot(a1, b) + _dot(a2, b)


def _rms(x, g):
    return x * lax.rsqrt(jnp.mean(x * x, axis=-1, keepdims=True) + EPS) * g


def _mod_row(tile, tm):
    return jnp.minimum((tile * tm) // LAT_SEQ, CTX_MOD_ROW)


def _ada_kernel(cs_ref, w_ref, b_ref, o_ref):
    cs = cs_ref[...]
    s = cs * jax.nn.sigmoid(cs)
    o_ref[...] = _dot(s.astype(BF), w_ref[...].astype(BF)) + b_ref[...]


def _ada_mods(cs, ada_w, ada_b):
    n_out = 9 * D
    tn = D
    out = pl.pallas_call(
        _ada_kernel,
        grid=(DEPTH, n_out // tn),
        in_specs=[
            pl.BlockSpec((MOD_ROWS, D), lambda l, n: (0, 0)),
            pl.BlockSpec((None, D, tn), lambda l, n: (l, 0, n)),
            pl.BlockSpec((None, 1, tn), lambda l, n: (l, 0, n)),
        ],
        out_specs=pl.BlockSpec((None, MOD_ROWS, tn), lambda l, n: (l, 0, n)),
        out_shape=jax.ShapeDtypeStruct((DEPTH, MOD_ROWS, n_out), F32),
        compiler_params=pltpu.CompilerParams(
            dimension_semantics=("arbitrary", "arbitrary"), vmem_limit_bytes=VMEM_LIMIT),
        name="ada_mods",
    )(cs, ada_w, ada_b.reshape(DEPTH, 1, n_out))
    return out.reshape(DEPTH, MOD_ROWS, 9, D)


RPB_LANES = 128
RPB_HALF = RPB_LANES // 2


def _bias_kernel(rpb_ref, o_ref):
    qq = lax.broadcasted_iota(jnp.int32, (GRID_W, RPB_LANES), 0)
    lane = lax.broadcasted_iota(jnp.int32, (GRID_W, RPB_LANES), 1)
    second = lane >= GRID_W
    kk = jnp.where(second, lane - GRID_W, lane)
    col_off = jnp.clip(kk - qq + (WIN_COLS - 1), 0, N_COL_OFF - 1)
    gather_idx = jnp.where(second, col_off + RPB_HALF, col_off)
    col_start = jnp.clip(qq - WIN_COLS // 2, 0, GRID_W - WIN_COLS)
    visible = (kk >= col_start) & (kk < col_start + WIN_COLS)
    low_half = lax.broadcasted_iota(jnp.int32, (1, RPB_LANES), 1) < RPB_HALF
    for i in range(HEADS_PER_GROUP):
        for ro in range(N_ROW_OFF - 1):
            row0 = rpb_ref[i, ro:ro + 1, :]
            row1 = pltpu.roll(rpb_ref[i, ro + 1:ro + 2, :], RPB_HALF, 1)
            both = jnp.broadcast_to(jnp.where(low_half, row0, row1), (GRID_W, RPB_LANES))
            t = jnp.take_along_axis(both, gather_idx, axis=1)
            t = jnp.where(visible, t, NEG).astype(o_ref.dtype)
            for a in range(0, WIN_ROWS, 2):
                delta_idx = ro - a
                if 0 <= delta_idx < N_DELTA:
                    o_ref[delta_idx, i * GRID_W:(i + 1) * GRID_W, a * GRID_W:(a + 2) * GRID_W] = t


def _bias_table(rpb):
    rpb_padded = jnp.pad(rpb, ((0, 0), (0, 0), (0, 0), (0, RPB_LANES - N_COL_OFF)))
    return pl.pallas_call(
        _bias_kernel,
        grid=(DEPTH, N_HEAD_GROUPS),
        in_specs=[pl.BlockSpec((None, HEADS_PER_GROUP, N_ROW_OFF, RPB_LANES),
                               lambda l, g: (l, g, 0, 0))],
        out_specs=pl.BlockSpec((None, None, N_DELTA, HEADS_PER_GROUP * GRID_W, WIN_ROWS * GRID_W),
                               lambda l, g: (l, g, 0, 0, 0)),
        out_shape=jax.ShapeDtypeStruct(
            (DEPTH, N_HEAD_GROUPS, N_DELTA, HEADS_PER_GROUP * GRID_W, WIN_ROWS * GRID_W), BF),
        compiler_params=pltpu.CompilerParams(
            dimension_semantics=("arbitrary", "arbitrary"), vmem_limit_bytes=VMEM_LIMIT),
        name="rpb_table",
    )(rpb_padded)


def _lat_tile(t, n_lat_tiles):
    return jnp.minimum(t, n_lat_tiles - 1)


def _ctx_tile(t, n_lat_tiles):
    return jnp.maximum(t - n_lat_tiles, 0)


def _ffn_kernel(*refs, l, sub, split_in, split_out, n_lat_tiles):
    refs = list(refs)
    x_refs = [refs.pop(0) for _ in range(2 if split_in else 1)]
    mod_ref, npre_ref, npost_ref, wi_hbm, wo_hbm = [refs.pop(0) for _ in range(5)]
    o_refs = [refs.pop(0) for _ in range(2 if split_out else 1)]
    wg_ref, wu_ref, wo_ref, sg_ref, su_ref, so_ref, h_ref, acc_ref, sem = refs

    mo = 0 if sub == 0 else 6
    ni = 0 if sub == 0 else 2
    t = pl.program_id(0)
    is_lat = t < n_lat_tiles

    def load_x(rows=slice(None)):
        if split_in:
            return jnp.where(is_lat, x_refs[0][rows, :], x_refs[1][rows, :])
        return x_refs[0][rows, :]

    def store(out, rows=slice(None)):
        if not split_out:
            o_refs[0][rows, :] = out
            return

        @pl.when(is_lat)
        def _():
            o_refs[0][rows, :] = out

        @pl.when(jnp.logical_not(is_lat))
        def _():
            o_refs[1][rows, :] = out

    def modulated(x):
        shift = mod_ref[mo:mo + 1, :]
        scale = mod_ref[mo + 1:mo + 2, :]
        return (_rms(x, npre_ref[ni:ni + 1, :]) * (1.0 + scale) + shift).astype(BF)

    def residual(x, acc):
        gate = mod_ref[mo + 2:mo + 3, :]
        return x + 0.5 * gate * _rms(acc, npost_ref[ni:ni + 1, :])

    def chunk_update(h, wg, wu, wo):
        g = _dot(h, wg)
        u = _dot(h, wu)
        a = (g * jax.nn.sigmoid(g) * u).astype(BF)
        return _dot(a, wo)

    def chunk_copies(j, slot):
        cols = pl.ds(pl.multiple_of(j * FF_CHUNK, FF_CHUNK), FF_CHUNK)
        up = pl.ds(pl.multiple_of(D_FF + j * FF_CHUNK, FF_CHUNK), FF_CHUNK)
        return (
            pltpu.make_async_copy(wi_hbm.at[l, sub, :, cols], sg_ref.at[slot], sem.at[0, slot]),
            pltpu.make_async_copy(wi_hbm.at[l, sub, :, up], su_ref.at[slot], sem.at[1, slot]),
            pltpu.make_async_copy(wo_hbm.at[l, sub, cols, :], so_ref.at[slot], sem.at[2, slot]),
        )

    @pl.when(t == 0)
    def _():
        for cp in chunk_copies(0, 0):
            cp.start()
        h_ref[...] = modulated(load_x())
        acc_ref[...] = jnp.zeros(acc_ref.shape, F32)

        def body(j, carry):
            slot = lax.rem(j, 2)

            @pl.when(j + 1 < N_FF_CHUNKS)
            def _():
                for cp in chunk_copies(j + 1, 1 - slot):
                    cp.start()

            for cp in chunk_copies(j, slot):
                cp.wait()
            wg = sg_ref[slot].astype(BF)
            wu = su_ref[slot].astype(BF)
            wo = so_ref[slot].astype(BF)
            wg_ref[j] = wg
            wu_ref[j] = wu
            wo_ref[j] = wo
            acc_ref[...] += chunk_update(h_ref[...], wg, wu, wo)
            return carry

        lax.fori_loop(0, N_FF_CHUNKS, body, 0)
        store(residual(load_x(), acc_ref[...]))

    @pl.when(t > 0)
    def _():
        def sub_tile(s, carry):
            rows = pl.ds(pl.multiple_of(s * FFN_SUBTILE, FFN_SUBTILE), FFN_SUBTILE)
            x = load_x(rows)
            h = modulated(x)
            acc = jnp.zeros(x.shape, F32)
            for j in range(N_FF_CHUNKS):
                acc = acc + chunk_update(h, wg_ref[j], wu_ref[j], wo_ref[j])
            store(residual(x, acc), rows)
            return carry

        n_sub = acc_ref.shape[0] // FFN_SUBTILE
        if n_sub == 1:
            sub_tile(0, 0)
        else:
            lax.fori_loop(0, n_sub, sub_tile, 0)


def _ffn(xs, mods, npre, npost, wi, wo, l, sub, split_out=False):
    split_in = len(xs) == 2
    tm = FFN_SUBTILE if (split_in or split_out) else FFN_TILE
    n_tiles = N_TOK // tm
    nl = N_LAT // tm
    lat_idx = lambda t: (_lat_tile(t, nl), 0)
    ctx_idx = lambda t: (_ctx_tile(t, nl), 0)
    split_specs = [pl.BlockSpec((tm, D), lat_idx), pl.BlockSpec((tm, D), ctx_idx)]
    joint_specs = [pl.BlockSpec((tm, D), lambda t: (t, 0))]
    split_shapes = [jax.ShapeDtypeStruct((N_LAT, D), F32), jax.ShapeDtypeStruct((N_CTX, D), F32)]
    joint_shapes = [jax.ShapeDtypeStruct((N_TOK, D), F32)]
    out = pl.pallas_call(
        functools.partial(_ffn_kernel, l=l, sub=sub, split_in=split_in, split_out=split_out,
                          n_lat_tiles=nl),
        grid=(n_tiles,),
        in_specs=(split_specs if split_in else joint_specs) + [
            pl.BlockSpec((None, None, 9, D), lambda t: (l, _mod_row(t, tm), 0, 0)),
            pl.BlockSpec((None, 3, D), lambda t: (l, 0, 0)),
            pl.BlockSpec((None, 3, D), lambda t: (l, 0, 0)),
            pl.BlockSpec(memory_space=pl.ANY),
            pl.BlockSpec(memory_space=pl.ANY),
        ],
        out_specs=split_specs if split_out else joint_specs,
        out_shape=split_shapes if split_out else joint_shapes,
        scratch_shapes=[
            pltpu.VMEM((N_FF_CHUNKS, D, FF_CHUNK), BF),
            pltpu.VMEM((N_FF_CHUNKS, D, FF_CHUNK), BF),
            pltpu.VMEM((N_FF_CHUNKS, FF_CHUNK, D), BF),
            pltpu.VMEM((2, D, FF_CHUNK), F32),
            pltpu.VMEM((2, D, FF_CHUNK), F32),
            pltpu.VMEM((2, FF_CHUNK, D), F32),
            pltpu.VMEM((tm, D), BF),
            pltpu.VMEM((tm, D), F32),
            pltpu.SemaphoreType.DMA((3, 2)),
        ],
        compiler_params=pltpu.CompilerParams(
            dimension_semantics=("arbitrary",), vmem_limit_bytes=VMEM_LIMIT),
        name=f"ffn_l{l}_s{sub}",
    )(*xs, mods, npre, npost, wi, wo)
    return tuple(out) if split_out else out[0]


def _proj_kernel(x_ref, mod_ref, npre_ref, w_ref, *rest, n_carried):
    qkv_ref, f_ref, g_ref, k_ref, v_ref, wb_ref = rest[n_carried:]

    @pl.when(pl.program_id(0) == 0)
    def _():
        wb_ref[...] = w_ref[...].astype(BF)

    x = x_ref[...]
    shift = mod_ref[3:4, :]
    scale = mod_ref[4:5, :]
    h = (_rms(x, npre_ref[1:2, :]) * (1.0 + scale) + shift).astype(BF)
    z = _dot(h, wb_ref[...])
    qkv_ref[...] = z[:, :3 * WA].astype(BF)
    f_ref[...] = z[:, 3 * WA:3 * WA + WB].astype(BF)
    g_ref[...] = z[:, 3 * WA + WB:].astype(BF)
    k_ref[...] = z[:, WA:2 * WA].reshape(k_ref.shape)
    v_ref[...] = z[:, 2 * WA:3 * WA].reshape(v_ref.shape)


def _proj(x, mods, npre, w_in, carried, l, tm=1024):
    n_tiles = N_TOK // tm
    nl = N_LAT // tm
    bpt = tm // CTX_SEQ
    kv_shape = jax.ShapeDtypeStruct((N_CTX_B, DEPTH, CTX_SEQ, WA), F32)
    kv_spec = pl.BlockSpec((bpt, None, CTX_SEQ, WA), lambda t: (_ctx_tile(t, nl), l, 0, 0))
    n_fixed = 4
    return pl.pallas_call(
        functools.partial(_proj_kernel, n_carried=len(carried)),
        grid=(n_tiles,),
        in_specs=[
            pl.BlockSpec((tm, D), lambda t: (t, 0)),
            pl.BlockSpec((None, None, 9, D), lambda t: (l, _mod_row(t, tm), 0, 0)),
            pl.BlockSpec((None, 3, D), lambda t: (l, 0, 0)),
            pl.BlockSpec((None, D, PROJ_W), lambda t: (l, 0, 0), pipeline_mode=pl.Buffered(1)),
        ] + [pl.BlockSpec(memory_space=pl.ANY)] * len(carried),
        out_specs=[
            pl.BlockSpec((tm, 3 * WA), lambda t: (t, 0)),
            pl.BlockSpec((tm, WB), lambda t: (t, 0)),
            pl.BlockSpec((tm, 2 * WC), lambda t: (t, 0)),
            kv_spec,
            kv_spec,
        ],
        out_shape=[
            jax.ShapeDtypeStruct((N_TOK, 3 * WA), BF),
            jax.ShapeDtypeStruct((N_TOK, WB), BF),
            jax.ShapeDtypeStruct((N_TOK, 2 * WC), BF),
            kv_shape,
            kv_shape,
        ],
        scratch_shapes=[pltpu.VMEM((D, PROJ_W), BF)],
        input_output_aliases={n_fixed + i: 3 + i for i in range(len(carried))},
        compiler_params=pltpu.CompilerParams(
            dimension_semantics=("arbitrary",), vmem_limit_bytes=VMEM_LIMIT),
        name=f"proj_l{l}",
    )(x, mods, npre, w_in, *carried)


def _fourier(f, fc_ref, fn_ref):
    n = fn_ref.shape[0]
    xcs = _dot(f, fc_ref[...]).astype(BF)
    outs = []
    for b in range(f.shape[0] // n):
        xb = xcs[b * n:(b + 1) * n]
        stacked = jnp.concatenate([xb[:, :WB], xb[:, WB:]], axis=0)
        outs.append(_dot(fn_ref[...], stacked))
    return outs[0] if len(outs) == 1 else jnp.concatenate(outs, axis=0)


def _gmlp(g, gn_ref, ws_ref, gb_ref, avg_ref):
    gl = jax.nn.gelu(g.astype(F32), approximate=True)
    u = gl[:, :WC]
    v = gl[:, WC:]
    avg = avg_ref[...]
    mu = _split3_dot(v, avg)
    d = v - mu
    var = _split3_dot(d * d, avg)
    vn = (d * lax.rsqrt(var + EPS) * gn_ref[...]).astype(BF)
    lane_group = lax.broadcasted_iota(jnp.int32, (CHUNK, WC), 1) // GC
    outs = []
    for c in range(g.shape[0] // CHUNK):
        vc = vn[c * CHUNK:(c + 1) * CHUNK, :]
        sp = _dot(ws_ref[0].astype(BF), vc)
        for grp in range(1, NGRP):
            sp = jnp.where(lane_group == grp, _dot(ws_ref[grp].astype(BF), vc), sp)
        outs.append(u[c * CHUNK:(c + 1) * CHUNK, :] * (sp + gb_ref[...]))
    return jnp.concatenate(outs, axis=0)


def _cast_weight_once(w_ref, wb_ref):
    @pl.when(pl.program_id(0) == 0)
    def _():
        wb_ref[...] = w_ref[...].astype(BF)


def _mix_out(x, o, gate, npost, wb_ref):
    y = _dot(o, wb_ref[...])
    return x + gate * _rms(y, npost)


def _ctx_mixer_kernel(x_ref, qkv_ref, f_ref, g_ref, mod_ref, npost_ref, wo_ref, fc_ref, fn_ref,
                      gn_ref, ws_ref, gb_ref, avg_ref, o_ref, mix_ref, wb_ref):
    _cast_weight_once(wo_ref, wb_ref)
    gw = HEADS_PER_GROUP * HD
    pair_w = 2 * HD
    first_of_pair = lax.broadcasted_iota(jnp.int32, (CTX_SEQ, pair_w), 1) < HD
    own = (lax.broadcasted_iota(jnp.int32, (HEADS_PER_GROUP * CTX_SEQ, gw), 0) // CTX_SEQ
           == lax.broadcasted_iota(jnp.int32, (HEADS_PER_GROUP * CTX_SEQ, gw), 1) // HD)
    lane_head = lax.broadcasted_iota(jnp.int32, (CTX_SEQ, gw), 1) // HD
    zero = jnp.zeros((), BF)
    for b in range(x_ref.shape[0] // CTX_SEQ):
        rows = slice(b * CTX_SEQ, (b + 1) * CTX_SEQ)
        scores = []
        for h in range(NH):
            lanes = slice((h // 2) * pair_w, (h // 2 + 1) * pair_w)
            q2 = qkv_ref[rows, lanes] * QK_SCALE
            k2 = qkv_ref[rows, WA + (h // 2) * pair_w:WA + (h // 2 + 1) * pair_w]
            qh = jnp.where(first_of_pair, q2, zero) if h % 2 == 0 else jnp.where(first_of_pair, zero, q2)
            scores.append(_dot_nt(qh, k2))
        probs, dens = [], []
        for s in scores:
            p = jnp.exp(s - jnp.max(s, axis=-1, keepdims=True))
            dens.append(jnp.sum(p, axis=-1, keepdims=True))
            probs.append(p.astype(BF))
        for grp in range(N_HEAD_GROUPS):
            hs = range(grp * HEADS_PER_GROUP, (grp + 1) * HEADS_PER_GROUP)
            v4 = qkv_ref[rows, 2 * WA + grp * gw:2 * WA + (grp + 1) * gw]
            vbd = jnp.where(own, jnp.concatenate([v4] * HEADS_PER_GROUP, axis=0), zero)
            o = _dot(jnp.concatenate([probs[h] for h in hs], axis=1), vbd)
            den = dens[hs[0]]
            for i in range(1, HEADS_PER_GROUP):
                den = jnp.where(lane_head == i, dens[hs[i]], den)
            mix_ref[rows, grp * gw:(grp + 1) * gw] = (o / den).astype(BF)
    mix_ref[:, WA:WA + WB] = _fourier(f_ref[...], fc_ref, fn_ref).astype(BF)
    mix_ref[:, WA + WB:] = _gmlp(g_ref[...], gn_ref, ws_ref, gb_ref, avg_ref).astype(BF)
    o_ref[...] = _mix_out(x_ref[...], mix_ref[...], mod_ref[5:6, :], npost_ref[1:2, :], wb_ref)


def _const_spec(shape):
    nd = len(shape)
    return pl.BlockSpec(shape, lambda *_: (0,) * nd)


def _ctx_mixer(x, qkv, f, g, mods, npost, w_out, fc, fn, gn, ws, gb, avg, l):
    tm = CTX_PER_STEP * CTX_SEQ
    off = N_LAT // tm
    row = lambda b: (b + off, 0)
    return pl.pallas_call(
        _ctx_mixer_kernel,
        grid=(N_CTX_B // CTX_PER_STEP,),
        in_specs=[
            pl.BlockSpec((tm, D), row),
            pl.BlockSpec((tm, 3 * WA), row),
            pl.BlockSpec((tm, WB), row),
            pl.BlockSpec((tm, 2 * WC), row),
            pl.BlockSpec((None, None, 9, D), lambda b: (l, CTX_MOD_ROW, 0, 0)),
            pl.BlockSpec((None, 3, D), lambda b: (l, 0, 0)),
            pl.BlockSpec((None, D, D), lambda b: (l, 0, 0), pipeline_mode=pl.Buffered(1)),
            _const_spec(fc.shape),
            _const_spec(fn.shape),
            pl.BlockSpec((None, 1, WC), lambda b: (l, 0, 0)),
            pl.BlockSpec((None, NGRP, CHUNK, CHUNK), lambda b: (l, 0, 0, 0)),
            pl.BlockSpec((None, CHUNK, WC), lambda b: (l, 0, 0)),
            _const_spec(avg.shape),
        ],
        out_specs=pl.BlockSpec((tm, D), row),
        out_shape=jax.ShapeDtypeStruct((N_TOK, D), F32),
        input_output_aliases={0: 0},
        scratch_shapes=[pltpu.VMEM((tm, D), BF), pltpu.VMEM((D, D), BF)],
        compiler_params=pltpu.CompilerParams(
            dimension_semantics=("arbitrary",), vmem_limit_bytes=VMEM_LIMIT),
        name=f"ctx_mixer_l{l}",
    )(x, qkv, f, g, mods, npost, w_out, fc, fn, gn, ws, gb, avg)


def _window_start(r):
    return min(max(r - WIN_ROWS // 2, 0), GRID_ROWS - WIN_ROWS)


def _nbr_attn_kernel(qkv_ref, ck_ref, cv_ref, bias_ref, o_ref):
    gw = HEADS_PER_GROUP * HD
    row_head = lax.broadcasted_iota(jnp.int32, (HEADS_PER_GROUP * GRID_W, gw), 0) // GRID_W
    lane_head = lax.broadcasted_iota(jnp.int32, (HEADS_PER_GROUP * GRID_W, gw), 1) // HD
    own = row_head == lane_head
    for hg in range(N_HEAD_GROUPS):
        ck = ck_ref[:, hg * gw:(hg + 1) * gw].astype(BF)
        cv = cv_ref[:, hg * gw:(hg + 1) * gw].astype(BF)
        for r in range(GRID_ROWS):
            ws = _window_start(r)
            q = qkv_ref[r * GRID_W:(r + 1) * GRID_W, hg * gw:(hg + 1) * gw] * QK_SCALE
            qs = jnp.where(own, jnp.concatenate([q] * HEADS_PER_GROUP, axis=0), jnp.zeros((), BF))
            kw = qkv_ref[ws * GRID_W:(ws + WIN_ROWS) * GRID_W, WA + hg * gw:WA + (hg + 1) * gw]
            vw = qkv_ref[ws * GRID_W:(ws + WIN_ROWS) * GRID_W, 2 * WA + hg * gw:2 * WA + (hg + 1) * gw]
            s_loc = _dot_nt(qs, kw) + bias_ref[hg, ws - r + N_DELTA - 1].astype(F32)
            s_ctx = _dot_nt(qs, ck)
            m = jnp.maximum(jnp.max(s_loc, axis=-1, keepdims=True),
                            jnp.max(s_ctx, axis=-1, keepdims=True))
            p_loc = jnp.exp(s_loc - m)
            p_ctx = jnp.exp(s_ctx - m)
            den = jnp.sum(p_loc, axis=-1, keepdims=True) + jnp.sum(p_ctx, axis=-1, keepdims=True)
            o = (_dot(p_loc.astype(BF), vw) + _dot(p_ctx.astype(BF), cv)) / den
            o = jnp.where(own, o, 0.0)
            o = (o[0:GRID_W] + o[GRID_W:2 * GRID_W]) + (o[2 * GRID_W:3 * GRID_W] + o[3 * GRID_W:])
            o_ref[r * GRID_W:(r + 1) * GRID_W, hg * gw:(hg + 1) * gw] = o.astype(BF)


def _nbr_attn(qkv, cache_k, cache_v, bias, l):
    return pl.pallas_call(
        _nbr_attn_kernel,
        grid=(N_LAT_B,),
        in_specs=[
            pl.BlockSpec((LAT_SEQ, 3 * WA), lambda b: (b, 0)),
            pl.BlockSpec((None, None, PAST, WA), lambda b: (b, l, 0, 0)),
            pl.BlockSpec((None, None, PAST, WA), lambda b: (b, l, 0, 0)),
            pl.BlockSpec((None,) + bias.shape[1:], lambda b: (l, 0, 0, 0, 0)),
        ],
        out_specs=pl.BlockSpec((LAT_SEQ, WA), lambda b: (b, 0)),
        out_shape=jax.ShapeDtypeStruct((N_LAT, WA), BF),
        compiler_params=pltpu.CompilerParams(
            dimension_semantics=("arbitrary",), vmem_limit_bytes=VMEM_LIMIT),
        name=f"nbr_attn_l{l}",
    )(qkv, cache_k, cache_v, bias)


def _lat_mixer_kernel(x_ref, oa_ref, f_ref, g_ref, mod_ref, npost_ref, wo_ref, fc_ref, fn_ref,
                      gn_ref, ws_ref, gb_ref, avg_ref, o_ref, mix_ref, wb_ref):
    _cast_weight_once(wo_ref, wb_ref)
    mix_ref[:, :WA] = oa_ref[...]
    mix_ref[:, WA:WA + WB] = _fourier(f_ref[...], fc_ref, fn_ref).astype(BF)
    mix_ref[:, WA + WB:] = _gmlp(g_ref[...], gn_ref, ws_ref, gb_ref, avg_ref).astype(BF)
    o_ref[...] = _mix_out(x_ref[...], mix_ref[...], mod_ref[5:6, :], npost_ref[1:2, :], wb_ref)


def _lat_mixer(x, oa, f, g, mods, npost, w_out, fc, fn, gn, ws, gb, avg, l):
    row = lambda b: (b, 0)
    return pl.pallas_call(
        _lat_mixer_kernel,
        grid=(N_LAT_B,),
        in_specs=[
            pl.BlockSpec((LAT_SEQ, D), row),
            pl.BlockSpec((LAT_SEQ, WA), row),
            pl.BlockSpec((LAT_SEQ, WB), row),
            pl.BlockSpec((LAT_SEQ, 2 * WC), row),
            pl.BlockSpec((None, None, 9, D), lambda b: (l, b, 0, 0)),
            pl.BlockSpec((None, 3, D), lambda b: (l, 0, 0)),
            pl.BlockSpec((None, D, D), lambda b: (l, 0, 0), pipeline_mode=pl.Buffered(1)),
            _const_spec(fc.shape),
            pl.BlockSpec(fn.shape, lambda b: (0, 0), pipeline_mode=pl.Buffered(1)),
            pl.BlockSpec((None, 1, WC), lambda b: (l, 0, 0)),
            pl.BlockSpec((None, NGRP, CHUNK, CHUNK), lambda b: (l, 0, 0, 0)),
            pl.BlockSpec((None, CHUNK, WC), lambda b: (l, 0, 0)),
            _const_spec(avg.shape),
        ],
        out_specs=pl.BlockSpec((LAT_SEQ, D), row),
        out_shape=jax.ShapeDtypeStruct((N_TOK, D), F32),
        input_output_aliases={0: 0},
        scratch_shapes=[pltpu.VMEM((LAT_SEQ, D), BF), pltpu.VMEM((D, D), BF)],
        compiler_params=pltpu.CompilerParams(
            dimension_semantics=("arbitrary",), vmem_limit_bytes=VMEM_LIMIT),
        name=f"lat_mixer_l{l}",
    )(x, oa, f, g, mods, npost, w_out, fc, fn, gn, ws, gb, avg)


def _dft_tables(n):
    idx = np.arange(n)
    ang = 2.0 * np.pi * ((idx[:, None] * idx[None, :]) % n) / n
    return (np.concatenate([np.cos(ang), -np.sin(ang)], axis=1) / np.sqrt(n)).astype(np.float32)


def _channel_dft_table():
    idx = np.arange(GC)
    ang = 2.0 * np.pi * ((idx[:, None] * idx[None, :]) % GC) / GC
    eye = np.eye(NGRP)
    return (np.concatenate([np.kron(eye, np.cos(ang)), np.kron(eye, np.sin(ang))], axis=1)
            / np.sqrt(GC)).astype(np.float32)


def _group_average_table():
    return np.kron(np.eye(NGRP), np.full((GC, GC), 1.0 / GC)).astype(np.float32)


def kernel(x_prompt, x_sample, cache_k, cache_v, c, c_ctx, ada_w, ada_b, norm_pre, norm_post,
           ffn_w_in, ffn_w_out, w_in, w_out, rpb, gmlp_norm, gmlp_w, gmlp_b):
    xs = (x_sample.reshape(N_LAT, D), x_prompt.reshape(N_CTX, D))
    cs = jnp.concatenate([c, c_ctx[None, :], jnp.zeros((MOD_ROWS - N_LAT_B - 1, D), F32)], axis=0)
    mods = _ada_mods(cs, ada_w, ada_b)
    bias = _bias_table(rpb)

    gn = gmlp_norm.reshape(DEPTH, 1, WC)
    gb = jnp.repeat(jnp.transpose(gmlp_b, (0, 2, 1)), GC, axis=2)
    ck = cache_k.reshape(N_LAT_B, DEPTH, PAST, WA)
    cv = cache_v.reshape(N_LAT_B, DEPTH, PAST, WA)

    fc = jnp.asarray(_channel_dft_table()).astype(BF)
    fn_ctx = jnp.asarray(_dft_tables(CTX_SEQ)).astype(BF)
    fn_lat = jnp.asarray(_dft_tables(LAT_SEQ)).astype(BF)
    avg = jnp.asarray(_group_average_table()).astype(BF)

    carried = (jnp.zeros((N_CTX_B, DEPTH, CTX_SEQ, WA), F32),) * 2
    for l in range(DEPTH):
        x = _ffn(xs, mods, norm_pre, norm_post, ffn_w_in, ffn_w_out, l, 0)
        qkv, f, g, new_k, new_v = _proj(x, mods, norm_pre, w_in, carried, l)
        carried = (new_k, new_v)
        oa = _nbr_attn(qkv, ck, cv, bias, l)
        x = _ctx_mixer(x, qkv, f, g, mods, norm_post, w_out, fc, fn_ctx, gn, gmlp_w, gb, avg, l)
        x = _lat_mixer(x, oa, f, g, mods, norm_post, w_out, fc, fn_lat, gn, gmlp_w, gb, avg, l)
        xs = _ffn((x,), mods, norm_pre, norm_post, ffn_w_in, ffn_w_out, l, 1,
                  split_out=(l == DEPTH - 1))
        if l < DEPTH - 1:
            xs = (xs,)
    xl, xc = xs

    kv_out = (N_CTX_B, DEPTH, CTX_SEQ, NH, HD)
    return (xc.reshape(N_CTX_B, CTX_SEQ, D), xl.reshape(N_LAT_B, LAT_SEQ, D),
            new_k.reshape(kv_out), new_v.reshape(kv_out))
```

```python
import functools

import numpy as np
import jax
import jax.numpy as jnp
from jax import lax
from jax.experimental import pallas as pl
from jax.experimental.pallas import tpu as pltpu

D = 1024
DEPTH = 4
N_CTX_B = 16
CTX_SEQ = 256
N_LAT_B = 4
LAT_SEQ = 1024
N_LAT = N_LAT_B * LAT_SEQ
N_CTX = N_CTX_B * CTX_SEQ
N_TOK = N_LAT + N_CTX
PAST = 256
GRID_W = 64
GRID_ROWS = LAT_SEQ // GRID_W
HD = 64
NH = 8
WA = 512
WB = 256
WC = 256
NGRP = 4
GC = 64
CHUNK = 128
PROJ_W = 3 * WA + WB + 2 * WC
D_FF = 2816
FF_CHUNK = 256
N_FF_CHUNKS = D_FF // FF_CHUNK
FFN_TILE = 1024
FFN_SUBTILE = 512
WIN_ROWS = 8
WIN_COLS = 16
N_ROW_OFF = 2 * WIN_ROWS - 1
N_COL_OFF = 2 * WIN_COLS - 1
N_DELTA = 8
EPS = 1e-6
NEG = -1e30
MOD_ROWS = 8
CTX_MOD_ROW = 4
CTX_PER_STEP = 2
HEADS_PER_GROUP = 4
N_HEAD_GROUPS = NH // HEADS_PER_GROUP

BF = jnp.bfloat16
F32 = jnp.float32
QK_SCALE = HD ** -0.5

VMEM_LIMIT = 56 * 1024 * 1024


def _dot(a, b):
    return jnp.dot(a, b, preferred_element_type=F32)


def _dot_nt(a, b):
    return lax.dot_general(a, b, (((1,), (1,)), ((), ())), preferred_element_type=F32)


def _split3_dot(a, b):
    a0 = a.astype(BF)
    r1 = a - a0.astype(F32)
    a1 = r1.astype(BF)
    a2 = (r1 - a1.astype(F32)).astype(BF)
    return _dot(a0, b) + _dot(a1, b) + _dot(a2, b)


def _rms(x, g):
    return x * lax.rsqrt(jnp.mean(x * x, axis=-1, keepdims=True) + EPS) * g


def _mod_row(tile, tm):
    return jnp.minimum((tile * tm) // LAT_SEQ, CTX_MOD_ROW)


def _ada_kernel(cs_ref, w_ref, b_ref, o_ref):
    cs = cs_ref[...]
    s = cs * jax.nn.sigmoid(cs)
    o_ref[...] = _dot(s.astype(BF), w_ref[...].astype(BF)) + b_ref[...]


def _ada_mods(cs, ada_w, ada_b):
    n_out = 9 * D
    tn = D
    out = pl.pallas_call(
        _ada_kernel,
        grid=(DEPTH, n_out // tn),
        in_specs=[
            pl.BlockSpec((MOD_ROWS, D), lambda l, n: (0, 0)),
            pl.BlockSpec((None, D, tn), lambda l, n: (l, 0, n)),
            pl.BlockSpec((None, 1, tn), lambda l, n: (l, 0, n)),
        ],
        out_specs=pl.BlockSpec((None, MOD_ROWS, tn), lambda l, n: (l, 0, n)),
        out_shape=jax.ShapeDtypeStruct((DEPTH, MOD_ROWS, n_out), F32),
        compiler_params=pltpu.CompilerParams(
            dimension_semantics=("arbitrary", "arbitrary"), vmem_limit_bytes=VMEM_LIMIT),
        name="ada_mods",
    )(cs, ada_w, ada_b.reshape(DEPTH, 1, n_out))
    return out.reshape(DEPTH, MOD_ROWS, 9, D)


RPB_LANES = 128
RPB_HALF = RPB_LANES // 2


def _bias_kernel(rpb_ref, o_ref):
    qq = lax.broadcasted_iota(jnp.int32, (GRID_W, RPB_LANES), 0)
    lane = lax.broadcasted_iota(jnp.int32, (GRID_W, RPB_LANES), 1)
    second = lane >= GRID_W
    kk = jnp.where(second, lane - GRID_W, lane)
    col_off = jnp.clip(kk - qq + (WIN_COLS - 1), 0, N_COL_OFF - 1)
    gather_idx = jnp.where(second, col_off + RPB_HALF, col_off)
    col_start = jnp.clip(qq - WIN_COLS // 2, 0, GRID_W - WIN_COLS)
    visible = (kk >= col_start) & (kk < col_start + WIN_COLS)
    low_half = lax.broadcasted_iota(jnp.int32, (1, RPB_LANES), 1) < RPB_HALF
    for i in range(HEADS_PER_GROUP):
        for ro in range(N_ROW_OFF - 1):
            row0 = rpb_ref[i, ro:ro + 1, :]
            row1 = pltpu.roll(rpb_ref[i, ro + 1:ro + 2, :], RPB_HALF, 1)
            both = jnp.broadcast_to(jnp.where(low_half, row0, row1), (GRID_W, RPB_LANES))
            t = jnp.take_along_axis(both, gather_idx, axis=1)
            t = jnp.where(visible, t, NEG).astype(o_ref.dtype)
            for a in range(0, WIN_ROWS, 2):
                delta_idx = ro - a
                if 0 <= delta_idx < N_DELTA:
                    o_ref[delta_idx, i * GRID_W:(i + 1) * GRID_W, a * GRID_W:(a + 2) * GRID_W] = t


def _bias_table(rpb):
    rpb_padded = jnp.pad(rpb, ((0, 0), (0, 0), (0, 0), (0, RPB_LANES - N_COL_OFF)))
    return pl.pallas_call(
        _bias_kernel,
        grid=(DEPTH, N_HEAD_GROUPS),
        in_specs=[pl.BlockSpec((None, HEADS_PER_GROUP, N_ROW_OFF, RPB_LANES),
                               lambda l, g: (l, g, 0, 0))],
        out_specs=pl.BlockSpec((None, None, N_DELTA, HEADS_PER_GROUP * GRID_W, WIN_ROWS * GRID_W),
                               lambda l, g: (l, g, 0, 0, 0)),
        out_shape=jax.ShapeDtypeStruct(
            (DEPTH, N_HEAD_GROUPS, N_DELTA, HEADS_PER_GROUP * GRID_W, WIN_ROWS * GRID_W), BF),
        compiler_params=pltpu.CompilerParams(
            dimension_semantics=("arbitrary", "arbitrary"), vmem_limit_bytes=VMEM_LIMIT),
        name="rpb_table",
    )(rpb_padded)


def _lat_tile(t, n_lat_tiles):
    return jnp.minimum(t, n_lat_tiles - 1)


def _ctx_tile(t, n_lat_tiles):
    return jnp.maximum(t - n_lat_tiles, 0)


def _ffn_kernel(*refs, l, sub, split_in, split_out, n_lat_tiles):
    refs = list(refs)
    x_refs = [refs.pop(0) for _ in range(2 if split_in else 1)]
    mod_ref, npre_ref, npost_ref, wi_hbm, wo_hbm = [refs.pop(0) for _ in range(5)]
    o_refs = [refs.pop(0) for _ in range(2 if split_out else 1)]
    wg_ref, wu_ref, wo_ref, sg_ref, su_ref, so_ref, h_ref, acc_ref, act_ref, sem = refs

    mo = 0 if sub == 0 else 6
    ni = 0 if sub == 0 else 2
    t = pl.program_id(0)
    is_lat = t < n_lat_tiles

    def load_x(rows=slice(None)):
        if split_in:
            return jnp.where(is_lat, x_refs[0][rows, :], x_refs[1][rows, :])
        return x_refs[0][rows, :]

    def store(out, rows=slice(None)):
        if not split_out:
            o_refs[0][rows, :] = out
            return

        @pl.when(is_lat)
        def _():
            o_refs[0][rows, :] = out

        @pl.when(jnp.logical_not(is_lat))
        def _():
            o_refs[1][rows, :] = out

    def modulated(x):
        shift = mod_ref[mo:mo + 1, :]
        scale = mod_ref[mo + 1:mo + 2, :]
        return (_rms(x, npre_ref[ni:ni + 1, :]) * (1.0 + scale) + shift).astype(BF)

    def residual(x, acc):
        gate = mod_ref[mo + 2:mo + 3, :]
        return x + 0.5 * gate * _rms(acc, npost_ref[ni:ni + 1, :])

    def chunk_update(h, wg, wu, wo):
        g = _dot(h, wg)
        u = _dot(h, wu)
        a = (g * jax.nn.sigmoid(g) * u).astype(BF)
        return _dot(a, wo)

    def chunk_copies(j, slot):
        cols = pl.ds(pl.multiple_of(j * FF_CHUNK, FF_CHUNK), FF_CHUNK)
        up = pl.ds(pl.multiple_of(D_FF + j * FF_CHUNK, FF_CHUNK), FF_CHUNK)
        return (
            pltpu.make_async_copy(wi_hbm.at[l, sub, :, cols], sg_ref.at[slot], sem.at[0, slot]),
            pltpu.make_async_copy(wi_hbm.at[l, sub, :, up], su_ref.at[slot], sem.at[1, slot]),
            pltpu.make_async_copy(wo_hbm.at[l, sub, cols, :], so_ref.at[slot], sem.at[2, slot]),
        )

    @pl.when(t == 0)
    def _():
        for cp in chunk_copies(0, 0):
            cp.start()
        h_ref[...] = modulated(load_x())
        acc_ref[...] = jnp.zeros(acc_ref.shape, F32)

        def body(j, carry):
            slot = lax.rem(j, 2)

            @pl.when(j + 1 < N_FF_CHUNKS)
            def _():
                for cp in chunk_copies(j + 1, 1 - slot):
                    cp.start()

            for cp in chunk_copies(j, slot):
                cp.wait()
            wg = sg_ref[slot].astype(BF)
            wu = su_ref[slot].astype(BF)
            wo = so_ref[slot].astype(BF)
            wg_ref[j] = wg
            wu_ref[j] = wu
            wo_ref[pl.ds(pl.multiple_of(j * FF_CHUNK, FF_CHUNK), FF_CHUNK), :] = wo
            acc_ref[...] += chunk_update(h_ref[...], wg, wu, wo)
            return carry

        lax.fori_loop(0, N_FF_CHUNKS, body, 0)
        store(residual(load_x(), acc_ref[...]))

    @pl.when(t > 0)
    def _():
        def sub_tile(s, carry):
            rows = pl.ds(pl.multiple_of(s * FFN_SUBTILE, FFN_SUBTILE), FFN_SUBTILE)
            x = load_x(rows)
            h = modulated(x)
            for j in range(N_FF_CHUNKS):
                g = _dot(h, wg_ref[j])
                u = _dot(h, wu_ref[j])
                act_ref[:, j * FF_CHUNK:(j + 1) * FF_CHUNK] = (g * jax.nn.sigmoid(g) * u).astype(BF)
            acc = _dot(act_ref[...], wo_ref[...])
            store(residual(x, acc), rows)
            return carry

        n_sub = acc_ref.shape[0] // FFN_SUBTILE
        if n_sub == 1:
            sub_tile(0, 0)
        else:
            lax.fori_loop(0, n_sub, sub_tile, 0)


def _ffn(xs, mods, npre, npost, wi, wo, l, sub, split_out=False):
    split_in = len(xs) == 2
    tm = FFN_SUBTILE if (split_in or split_out) else FFN_TILE
    n_tiles = N_TOK // tm
    nl = N_LAT // tm
    lat_idx = lambda t: (_lat_tile(t, nl), 0)
    ctx_idx = lambda t: (_ctx_tile(t, nl), 0)
    split_specs = [pl.BlockSpec((tm, D), lat_idx), pl.BlockSpec((tm, D), ctx_idx)]
    joint_specs = [pl.BlockSpec((tm, D), lambda t: (t, 0))]
    split_shapes = [jax.ShapeDtypeStruct((N_LAT, D), F32), jax.ShapeDtypeStruct((N_CTX, D), F32)]
    joint_shapes = [jax.ShapeDtypeStruct((N_TOK, D), F32)]
    out = pl.pallas_call(
        functools.partial(_ffn_kernel, l=l, sub=sub, split_in=split_in, split_out=split_out,
                          n_lat_tiles=nl),
        grid=(n_tiles,),
        in_specs=(split_specs if split_in else joint_specs) + [
            pl.BlockSpec((None, None, 9, D), lambda t: (l, _mod_row(t, tm), 0, 0)),
            pl.BlockSpec((None, 3, D), lambda t: (l, 0, 0)),
            pl.BlockSpec((None, 3, D), lambda t: (l, 0, 0)),
            pl.BlockSpec(memory_space=pl.ANY),
            pl.BlockSpec(memory_space=pl.ANY),
        ],
        out_specs=split_specs if split_out else joint_specs,
        out_shape=split_shapes if split_out else joint_shapes,
        scratch_shapes=[
            pltpu.VMEM((N_FF_CHUNKS, D, FF_CHUNK), BF),
            pltpu.VMEM((N_FF_CHUNKS, D, FF_CHUNK), BF),
            pltpu.VMEM((D_FF, D), BF),
            pltpu.VMEM((2, D, FF_CHUNK), F32),
            pltpu.VMEM((2, D, FF_CHUNK), F32),
            pltpu.VMEM((2, FF_CHUNK, D), F32),
            pltpu.VMEM((tm, D), BF),
            pltpu.VMEM((tm, D), F32),
            pltpu.VMEM((FFN_SUBTILE, D_FF), BF),
            pltpu.SemaphoreType.DMA((3, 2)),
        ],
        compiler_params=pltpu.CompilerParams(
            dimension_semantics=("arbitrary",), vmem_limit_bytes=VMEM_LIMIT),
        name=f"ffn_l{l}_s{sub}",
    )(*xs, mods, npre, npost, wi, wo)
    return tuple(out) if split_out else out[0]


def _proj_kernel(x_ref, mod_ref, npre_ref, w_ref, *rest, n_carried):
    qkv_ref, f_ref, g_ref, k_ref, v_ref, wb_ref = rest[n_carried:]

    @pl.when(pl.program_id(0) == 0)
    def _():
        wb_ref[...] = w_ref[...].astype(BF)

    x = x_ref[...]
    shift = mod_ref[3:4, :]
    scale = mod_ref[4:5, :]
    h = (_rms(x, npre_ref[1:2, :]) * (1.0 + scale) + shift).astype(BF)
    z = _dot(h, wb_ref[...])
    qkv_ref[...] = z[:, :3 * WA].astype(BF)
    f_ref[...] = z[:, 3 * WA:3 * WA + WB].astype(BF)
    g_ref[...] = z[:, 3 * WA + WB:].astype(BF)
    k_ref[...] = z[:, WA:2 * WA].reshape(k_ref.shape)
    v_ref[...] = z[:, 2 * WA:3 * WA].reshape(v_ref.shape)


def _proj(x, mods, npre, w_in, carried, l, tm=1024):
    n_tiles = N_TOK // tm
    nl = N_LAT // tm
    bpt = tm // CTX_SEQ
    kv_shape = jax.ShapeDtypeStruct((N_CTX_B, DEPTH, CTX_SEQ, WA), F32)
    kv_spec = pl.BlockSpec((bpt, None, CTX_SEQ, WA), lambda t: (_ctx_tile(t, nl), l, 0, 0))
    n_fixed = 4
    return pl.pallas_call(
        functools.partial(_proj_kernel, n_carried=len(carried)),
        grid=(n_tiles,),
        in_specs=[
            pl.BlockSpec((tm, D), lambda t: (t, 0)),
            pl.BlockSpec((None, None, 9, D), lambda t: (l, _mod_row(t, tm), 0, 0)),
            pl.BlockSpec((None, 3, D), lambda t: (l, 0, 0)),
            pl.BlockSpec((None, D, PROJ_W), lambda t: (l, 0, 0), pipeline_mode=pl.Buffered(1)),
        ] + [pl.BlockSpec(memory_space=pl.ANY)] * len(carried),
        out_specs=[
            pl.BlockSpec((tm, 3 * WA), lambda t: (t, 0)),
            pl.BlockSpec((tm, WB), lambda t: (t, 0)),
            pl.BlockSpec((tm, 2 * WC), lambda t: (t, 0)),
            kv_spec,
            kv_spec,
        ],
        out_shape=[
            jax.ShapeDtypeStruct((N_TOK, 3 * WA), BF),
            jax.ShapeDtypeStruct((N_TOK, WB), BF),
            jax.ShapeDtypeStruct((N_TOK, 2 * WC), BF),
            kv_shape,
            kv_shape,
        ],
        scratch_shapes=[pltpu.VMEM((D, PROJ_W), BF)],
        input_output_aliases={n_fixed + i: 3 + i for i in range(len(carried))},
        compiler_params=pltpu.CompilerParams(
            dimension_semantics=("arbitrary",), vmem_limit_bytes=VMEM_LIMIT),
        name=f"proj_l{l}",
    )(x, mods, npre, w_in, *carried)


def _fourier(f, fc_ref, fn_ref):
    n = fn_ref.shape[0]
    xcs = _dot(f, fc_ref[...]).astype(BF)
    outs = []
    for b in range(f.shape[0] // n):
        xb = xcs[b * n:(b + 1) * n]
        stacked = jnp.concatenate([xb[:, :WB], xb[:, WB:]], axis=0)
        outs.append(_dot(fn_ref[...], stacked))
    return outs[0] if len(outs) == 1 else jnp.concatenate(outs, axis=0)


def _gmlp(g, gn_ref, ws_ref, gb_ref, avg_ref):
    gl = jax.nn.gelu(g.astype(F32), approximate=True)
    u = gl[:, :WC]
    v = gl[:, WC:]
    avg = avg_ref[...]
    mu = _split3_dot(v, avg)
    d = v - mu
    var = _split3_dot(d * d, avg)
    vn = (d * lax.rsqrt(var + EPS) * gn_ref[...]).astype(BF)
    lane_group = lax.broadcasted_iota(jnp.int32, (CHUNK, WC), 1) // GC
    outs = []
    for c in range(g.shape[0] // CHUNK):
        vc = vn[c * CHUNK:(c + 1) * CHUNK, :]
        sp = _dot(ws_ref[0].astype(BF), vc)
        for grp in range(1, NGRP):
            sp = jnp.where(lane_group == grp, _dot(ws_ref[grp].astype(BF), vc), sp)
        outs.append(u[c * CHUNK:(c + 1) * CHUNK, :] * (sp + gb_ref[...]))
    return jnp.concatenate(outs, axis=0)


def _cast_weight_once(w_ref, wb_ref):
    @pl.when(pl.program_id(0) == 0)
    def _():
        wb_ref[...] = w_ref[...].astype(BF)


def _mix_out(x, o, gate, npost, wb_ref):
    y = _dot(o, wb_ref[...])
    return x + gate * _rms(y, npost)


def _ctx_mixer_kernel(x_ref, qkv_ref, f_ref, g_ref, mod_ref, npost_ref, wo_ref, fc_ref, fn_ref,
                      gn_ref, ws_ref, gb_ref, avg_ref, o_ref, mix_ref, wb_ref):
    _cast_weight_once(wo_ref, wb_ref)
    gw = HEADS_PER_GROUP * HD
    pair_w = 2 * HD
    first_of_pair = lax.broadcasted_iota(jnp.int32, (CTX_SEQ, pair_w), 1) < HD
    own = (lax.broadcasted_iota(jnp.int32, (HEADS_PER_GROUP * CTX_SEQ, gw), 0) // CTX_SEQ
           == lax.broadcasted_iota(jnp.int32, (HEADS_PER_GROUP * CTX_SEQ, gw), 1) // HD)
    lane_head = lax.broadcasted_iota(jnp.int32, (CTX_SEQ, gw), 1) // HD
    zero = jnp.zeros((), BF)
    for b in range(x_ref.shape[0] // CTX_SEQ):
        rows = slice(b * CTX_SEQ, (b + 1) * CTX_SEQ)
        scores = []
        for h in range(NH):
            lanes = slice((h // 2) * pair_w, (h // 2 + 1) * pair_w)
            q2 = qkv_ref[rows, lanes] * QK_SCALE
            k2 = qkv_ref[rows, WA + (h // 2) * pair_w:WA + (h // 2 + 1) * pair_w]
            qh = jnp.where(first_of_pair, q2, zero) if h % 2 == 0 else jnp.where(first_of_pair, zero, q2)
            scores.append(_dot_nt(qh, k2))
        probs, dens = [], []
        for s in scores:
            p = jnp.exp(s - jnp.max(s, axis=-1, keepdims=True))
            dens.append(jnp.sum(p, axis=-1, keepdims=True))
            probs.append(p.astype(BF))
        for grp in range(N_HEAD_GROUPS):
            hs = range(grp * HEADS_PER_GROUP, (grp + 1) * HEADS_PER_GROUP)
            v4 = qkv_ref[rows, 2 * WA + grp * gw:2 * WA + (grp + 1) * gw]
            vbd = jnp.where(own, jnp.concatenate([v4] * HEADS_PER_GROUP, axis=0), zero)
            o = _dot(jnp.concatenate([probs[h] for h in hs], axis=1), vbd)
            den = dens[hs[0]]
            for i in range(1, HEADS_PER_GROUP):
                den = jnp.where(lane_head == i, dens[hs[i]], den)
            mix_ref[rows, grp * gw:(grp + 1) * gw] = (o / den).astype(BF)
    mix_ref[:, WA:WA + WB] = _fourier(f_ref[...], fc_ref, fn_ref).astype(BF)
    mix_ref[:, WA + WB:] = _gmlp(g_ref[...], gn_ref, ws_ref, gb_ref, avg_ref).astype(BF)
    o_ref[...] = _mix_out(x_ref[...], mix_ref[...], mod_ref[5:6, :], npost_ref[1:2, :], wb_ref)


def _const_spec(shape):
    nd = len(shape)
    return pl.BlockSpec(shape, lambda *_: (0,) * nd)


def _ctx_mixer(x, qkv, f, g, mods, npost, w_out, fc, fn, gn, ws, gb, avg, l):
    tm = CTX_PER_STEP * CTX_SEQ
    off = N_LAT // tm
    row = lambda b: (b + off, 0)
    return pl.pallas_call(
        _ctx_mixer_kernel,
        grid=(N_CTX_B // CTX_PER_STEP,),
        in_specs=[
            pl.BlockSpec((tm, D), row),
            pl.BlockSpec((tm, 3 * WA), row),
            pl.BlockSpec((tm, WB), row),
            pl.BlockSpec((tm, 2 * WC), row),
            pl.BlockSpec((None, None, 9, D), lambda b: (l, CTX_MOD_ROW, 0, 0)),
            pl.BlockSpec((None, 3, D), lambda b: (l, 0, 0)),
            pl.BlockSpec((None, D, D), lambda b: (l, 0, 0), pipeline_mode=pl.Buffered(1)),
            _const_spec(fc.shape),
            _const_spec(fn.shape),
            pl.BlockSpec((None, 1, WC), lambda b: (l, 0, 0)),
            pl.BlockSpec((None, NGRP, CHUNK, CHUNK), lambda b: (l, 0, 0, 0)),
            pl.BlockSpec((None, CHUNK, WC), lambda b: (l, 0, 0)),
            _const_spec(avg.shape),
        ],
        out_specs=pl.BlockSpec((tm, D), row),
        out_shape=jax.ShapeDtypeStruct((N_TOK, D), F32),
        input_output_aliases={0: 0},
        scratch_shapes=[pltpu.VMEM((tm, D), BF), pltpu.VMEM((D, D), BF)],
        compiler_params=pltpu.CompilerParams(
            dimension_semantics=("arbitrary",), vmem_limit_bytes=VMEM_LIMIT),
        name=f"ctx_mixer_l{l}",
    )(x, qkv, f, g, mods, npost, w_out, fc, fn, gn, ws, gb, avg)


def _window_start(r):
    return min(max(r - WIN_ROWS // 2, 0), GRID_ROWS - WIN_ROWS)


def _nbr_attn_kernel(qkv_ref, ck_ref, cv_ref, bias_ref, o_ref):
    gw = HEADS_PER_GROUP * HD
    row_head = lax.broadcasted_iota(jnp.int32, (HEADS_PER_GROUP * GRID_W, gw), 0) // GRID_W
    lane_head = lax.broadcasted_iota(jnp.int32, (HEADS_PER_GROUP * GRID_W, gw), 1) // HD
    own = row_head == lane_head
    for hg in range(N_HEAD_GROUPS):
        ck = ck_ref[:, hg * gw:(hg + 1) * gw].astype(BF)
        cv = cv_ref[:, hg * gw:(hg + 1) * gw].astype(BF)
        for r in range(GRID_ROWS):
            ws = _window_start(r)
            q = qkv_ref[r * GRID_W:(r + 1) * GRID_W, hg * gw:(hg + 1) * gw] * QK_SCALE
            qs = jnp.where(own, jnp.concatenate([q] * HEADS_PER_GROUP, axis=0), jnp.zeros((), BF))
            kw = qkv_ref[ws * GRID_W:(ws + WIN_ROWS) * GRID_W, WA + hg * gw:WA + (hg + 1) * gw]
            vw = qkv_ref[ws * GRID_W:(ws + WIN_ROWS) * GRID_W, 2 * WA + hg * gw:2 * WA + (hg + 1) * gw]
            s_loc = _dot_nt(qs, kw) + bias_ref[hg, ws - r + N_DELTA - 1].astype(F32)
            s_ctx = _dot_nt(qs, ck)
            m = jnp.maximum(jnp.max(s_loc, axis=-1, keepdims=True),
                            jnp.max(s_ctx, axis=-1, keepdims=True))
            p_loc = jnp.exp(s_loc - m)
            p_ctx = jnp.exp(s_ctx - m)
            den = jnp.sum(p_loc, axis=-1, keepdims=True) + jnp.sum(p_ctx, axis=-1, keepdims=True)
            o = (_dot(p_loc.astype(BF), vw) + _dot(p_ctx.astype(BF), cv)) / den
            o = jnp.where(own, o, 0.0)
            o = (o[0:GRID_W] + o[GRID_W:2 * GRID_W]) + (o[2 * GRID_W:3 * GRID_W] + o[3 * GRID_W:])
            o_ref[r * GRID_W:(r + 1) * GRID_W, hg * gw:(hg + 1) * gw] = o.astype(BF)


def _nbr_attn(qkv, cache_k, cache_v, bias, l):
    return pl.pallas_call(
        _nbr_attn_kernel,
        grid=(N_LAT_B,),
        in_specs=[
            pl.BlockSpec((LAT_SEQ, 3 * WA), lambda b: (b, 0)),
            pl.BlockSpec((None, None, PAST, WA), lambda b: (b, l, 0, 0)),
            pl.BlockSpec((None, None, PAST, WA), lambda b: (b, l, 0, 0)),
            pl.BlockSpec((None,) + bias.shape[1:], lambda b: (l, 0, 0, 0, 0)),
        ],
        out_specs=pl.BlockSpec((LAT_SEQ, WA), lambda b: (b, 0)),
        out_shape=jax.ShapeDtypeStruct((N_LAT, WA), BF),
        compiler_params=pltpu.CompilerParams(
            dimension_semantics=("arbitrary",), vmem_limit_bytes=VMEM_LIMIT),
        name=f"nbr_attn_l{l}",
    )(qkv, cache_k, cache_v, bias)


def _lat_mixer_kernel(x_ref, oa_ref, f_ref, g_ref, mod_ref, npost_ref, wo_ref, fc_ref, fn_ref,
                      gn_ref, ws_ref, gb_ref, avg_ref, o_ref, mix_ref, wb_ref):
    _cast_weight_once(wo_ref, wb_ref)
    mix_ref[:, :WA] = oa_ref[...]
    mix_ref[:, WA:WA + WB] = _fourier(f_ref[...], fc_ref, fn_ref).astype(BF)
    mix_ref[:, WA + WB:] = _gmlp(g_ref[...], gn_ref, ws_ref, gb_ref, avg_ref).astype(BF)
    o_ref[...] = _mix_out(x_ref[...], mix_ref[...], mod_ref[5:6, :], npost_ref[1:2, :], wb_ref)


def _lat_mixer(x, oa, f, g, mods, npost, w_out, fc, fn, gn, ws, gb, avg, l):
    row = lambda b: (b, 0)
    return pl.pallas_call(
        _lat_mixer_kernel,
        grid=(N_LAT_B,),
        in_specs=[
            pl.BlockSpec((LAT_SEQ, D), row),
            pl.BlockSpec((LAT_SEQ, WA), row),
            pl.BlockSpec((LAT_SEQ, WB), row),
            pl.BlockSpec((LAT_SEQ, 2 * WC), row),
            pl.BlockSpec((None, None, 9, D), lambda b: (l, b, 0, 0)),
            pl.BlockSpec((None, 3, D), lambda b: (l, 0, 0)),
            pl.BlockSpec((None, D, D), lambda b: (l, 0, 0), pipeline_mode=pl.Buffered(1)),
            _const_spec(fc.shape),
            pl.BlockSpec(fn.shape, lambda b: (0, 0), pipeline_mode=pl.Buffered(1)),
            pl.BlockSpec((None, 1, WC), lambda b: (l, 0, 0)),
            pl.BlockSpec((None, NGRP, CHUNK, CHUNK), lambda b: (l, 0, 0, 0)),
            pl.BlockSpec((None, CHUNK, WC), lambda b: (l, 0, 0)),
            _const_spec(avg.shape),
        ],
        out_specs=pl.BlockSpec((LAT_SEQ, D), row),
        out_shape=jax.ShapeDtypeStruct((N_TOK, D), F32),
        input_output_aliases={0: 0},
        scratch_shapes=[pltpu.VMEM((LAT_SEQ, D), BF), pltpu.VMEM((D, D), BF)],
        compiler_params=pltpu.CompilerParams(
            dimension_semantics=("arbitrary",), vmem_limit_bytes=VMEM_LIMIT),
        name=f"lat_mixer_l{l}",
    )(x, oa, f, g, mods, npost, w_out, fc, fn, gn, ws, gb, avg)


def _dft_tables(n):
    idx = np.arange(n)
    ang = 2.0 * np.pi * ((idx[:, None] * idx[None, :]) % n) / n
    return (np.concatenate([np.cos(ang), -np.sin(ang)], axis=1) / np.sqrt(n)).astype(np.float32)


def _channel_dft_table():
    idx = np.arange(GC)
    ang = 2.0 * np.pi * ((idx[:, None] * idx[None, :]) % GC) / GC
    eye = np.eye(NGRP)
    return (np.concatenate([np.kron(eye, np.cos(ang)), np.kron(eye, np.sin(ang))], axis=1)
            / np.sqrt(GC)).astype(np.float32)


def _group_average_table():
    return np.kron(np.eye(NGRP), np.full((GC, GC), 1.0 / GC)).astype(np.float32)


def kernel(x_prompt, x_sample, cache_k, cache_v, c, c_ctx, ada_w, ada_b, norm_pre, norm_post,
           ffn_w_in, ffn_w_out, w_in, w_out, rpb, gmlp_norm, gmlp_w, gmlp_b):
    xs = (x_sample.reshape(N_LAT, D), x_prompt.reshape(N_CTX, D))
    cs = jnp.concatenate([c, c_ctx[None, :], jnp.zeros((MOD_ROWS - N_LAT_B - 1, D), F32)], axis=0)
    mods = _ada_mods(cs, ada_w, ada_b)
    bias = _bias_table(rpb)

    gn = gmlp_norm.reshape(DEPTH, 1, WC)
    gb = jnp.repeat(jnp.transpose(gmlp_b, (0, 2, 1)), GC, axis=2)
    ck = cache_k.reshape(N_LAT_B, DEPTH, PAST, WA)
    cv = cache_v.reshape(N_LAT_B, DEPTH, PAST, WA)

    fc = jnp.asarray(_channel_dft_table()).astype(BF)
    fn_ctx = jnp.asarray(_dft_tables(CTX_SEQ)).astype(BF)
    fn_lat = jnp.asarray(_dft_tables(LAT_SEQ)).astype(BF)
    avg = jnp.asarray(_group_average_table()).astype(BF)

    carried = (jnp.zeros((N_CTX_B, DEPTH, CTX_SEQ, WA), F32),) * 2
    for l in range(DEPTH):
        x = _ffn(xs, mods, norm_pre, norm_post, ffn_w_in, ffn_w_out, l, 0)
        qkv, f, g, new_k, new_v = _proj(x, mods, norm_pre, w_in, carried, l)
        carried = (new_k, new_v)
        oa = _nbr_attn(qkv, ck, cv, bias, l)
        x = _ctx_mixer(x, qkv, f, g, mods, norm_post, w_out, fc, fn_ctx, gn, gmlp_w, gb, avg, l)
        x = _lat_mixer(x, oa, f, g, mods, norm_post, w_out, fc, fn_lat, gn, gmlp_w, gb, avg, l)
        xs = _ffn((x,), mods, norm_pre, norm_post, ffn_w_in, ffn_w_out, l, 1,
                  split_out=(l == DEPTH - 1))
        if l < DEPTH - 1:
            xs = (xs,)
    xl, xc = xs

    kv_out = (N_CTX_B, DEPTH, CTX_SEQ, NH, HD)
    return (xc.reshape(N_CTX_B, CTX_SEQ, D), xl.reshape(N_LAT_B, LAT_SEQ, D),
            new_k.reshape(kv_out), new_v.reshape(kv_out))
```

```python
import functools

import numpy as np
import jax
import jax.numpy as jnp
from jax import lax
from jax.experimental import pallas as pl
from jax.experimental.pallas import tpu as pltpu

D = 1024
DEPTH = 4
N_CTX_B = 16
CTX_SEQ = 256
N_LAT_B = 4
LAT_SEQ = 1024
N_LAT = N_LAT_B * LAT_SEQ
N_CTX = N_CTX_B * CTX_SEQ
N_TOK = N_LAT + N_CTX
PAST = 256
GRID_W = 64
GRID_ROWS = LAT_SEQ // GRID_W
HD = 64
NH = 8
WA = 512
WB = 256
WC = 256
NGRP = 4
GC = 64
CHUNK = 128
PROJ_W = 3 * WA + WB + 2 * WC
D_FF = 2816
FF_CHUNK = 256
N_FF_CHUNKS = D_FF // FF_CHUNK
FFN_TILE = 1024
FFN_SUBTILE = 512
WIN_ROWS = 8
WIN_COLS = 16
N_ROW_OFF = 2 * WIN_ROWS - 1
N_COL_OFF = 2 * WIN_COLS - 1
N_DELTA = 8
EPS = 1e-6
NEG = -1e30
MOD_ROWS = 8
CTX_MOD_ROW = 4
CTX_PER_STEP = 2
HEADS_PER_GROUP = 4
N_HEAD_GROUPS = NH // HEADS_PER_GROUP

BF = jnp.bfloat16
F32 = jnp.float32
QK_SCALE = HD ** -0.5

VMEM_LIMIT = 56 * 1024 * 1024


def _dot(a, b):
    return jnp.dot(a, b, preferred_element_type=F32)


def _dot_nt(a, b):
    return lax.dot_general(a, b, (((1,), (1,)), ((), ())), preferred_element_type=F32)


def _split3_dot(a, b):
    a0 = a.astype(BF)
    r1 = a - a0.astype(F32)
    a1 = r1.astype(BF)
    a2 = (r1 - a1.astype(F32)).astype(BF)
    return _dot(a0, b) + _dot(a1, b) + _dot(a2, b)


def _rms(x, g):
    return x * lax.rsqrt(jnp.mean(x * x, axis=-1, keepdims=True) + EPS) * g


def _mod_row(tile, tm):
    return jnp.minimum((tile * tm) // LAT_SEQ, CTX_MOD_ROW)


def _ada_kernel(cs_ref, w_ref, b_ref, o_ref):
    cs = cs_ref[...]
    s = cs * jax.nn.sigmoid(cs)
    o_ref[...] = _dot(s.astype(BF), w_ref[...].astype(BF)) + b_ref[...]


def _ada_mods(cs, ada_w, ada_b):
    n_out = 9 * D
    tn = D
    out = pl.pallas_call(
        _ada_kernel,
        grid=(DEPTH, n_out // tn),
        in_specs=[
            pl.BlockSpec((MOD_ROWS, D), lambda l, n: (0, 0)),
            pl.BlockSpec((None, D, tn), lambda l, n: (l, 0, n)),
            pl.BlockSpec((None, 1, tn), lambda l, n: (l, 0, n)),
        ],
        out_specs=pl.BlockSpec((None, MOD_ROWS, tn), lambda l, n: (l, 0, n)),
        out_shape=jax.ShapeDtypeStruct((DEPTH, MOD_ROWS, n_out), F32),
        compiler_params=pltpu.CompilerParams(
            dimension_semantics=("arbitrary", "arbitrary"), vmem_limit_bytes=VMEM_LIMIT),
        name="ada_mods",
    )(cs, ada_w, ada_b.reshape(DEPTH, 1, n_out))
    return out.reshape(DEPTH, MOD_ROWS, 9, D)


RPB_LANES = 128
RPB_HALF = RPB_LANES // 2


def _bias_kernel(rpb_ref, o_ref):
    qq = lax.broadcasted_iota(jnp.int32, (GRID_W, RPB_LANES), 0)
    lane = lax.broadcasted_iota(jnp.int32, (GRID_W, RPB_LANES), 1)
    second = lane >= GRID_W
    kk = jnp.where(second, lane - GRID_W, lane)
    col_start = jnp.clip(qq - WIN_COLS // 2, 0, GRID_W - WIN_COLS)
    visible = (kk >= col_start) & (kk < col_start + WIN_COLS)
    m = lax.broadcasted_iota(jnp.int32, (N_ROW_OFF, RPB_LANES), 1)
    diff = jnp.where(m < RPB_HALF, m, m - RPB_LANES)
    ext_idx = jnp.clip(diff + (WIN_COLS - 1), 0, N_COL_OFF - 1)
    for i in range(HEADS_PER_GROUP):
        ext = jnp.take_along_axis(rpb_ref[i], ext_idx, axis=1)
        for ro in range(N_ROW_OFF - 1):
            e0 = jnp.broadcast_to(ext[ro:ro + 1, :], (GRID_W, RPB_LANES))
            e1 = jnp.broadcast_to(ext[ro + 1:ro + 2, :], (GRID_W, RPB_LANES))
            t = jnp.where(second,
                          pltpu.roll(e1, GRID_W, 1, stride=1, stride_axis=0),
                          pltpu.roll(e0, 0, 1, stride=1, stride_axis=0))
            t = jnp.where(visible, t, NEG).astype(o_ref.dtype)
            for a in range(0, WIN_ROWS, 2):
                delta_idx = ro - a
                if 0 <= delta_idx < N_DELTA:
                    o_ref[delta_idx, i * GRID_W:(i + 1) * GRID_W, a * GRID_W:(a + 2) * GRID_W] = t


def _bias_table(rpb):
    rpb_padded = jnp.pad(rpb, ((0, 0), (0, 0), (0, 0), (0, RPB_LANES - N_COL_OFF)))
    return pl.pallas_call(
        _bias_kernel,
        grid=(DEPTH, N_HEAD_GROUPS),
        in_specs=[pl.BlockSpec((None, HEADS_PER_GROUP, N_ROW_OFF, RPB_LANES),
                               lambda l, g: (l, g, 0, 0))],
        out_specs=pl.BlockSpec((None, None, N_DELTA, HEADS_PER_GROUP * GRID_W, WIN_ROWS * GRID_W),
                               lambda l, g: (l, g, 0, 0, 0)),
        out_shape=jax.ShapeDtypeStruct(
            (DEPTH, N_HEAD_GROUPS, N_DELTA, HEADS_PER_GROUP * GRID_W, WIN_ROWS * GRID_W), BF),
        compiler_params=pltpu.CompilerParams(
            dimension_semantics=("arbitrary", "arbitrary"), vmem_limit_bytes=VMEM_LIMIT),
        name="rpb_table",
    )(rpb_padded)


def _lat_tile(t, n_lat_tiles):
    return jnp.minimum(t, n_lat_tiles - 1)


def _ctx_tile(t, n_lat_tiles):
    return jnp.maximum(t - n_lat_tiles, 0)


def _ffn_kernel(*refs, l, sub, split_in, split_out, n_lat_tiles):
    refs = list(refs)
    x_refs = [refs.pop(0) for _ in range(2 if split_in else 1)]
    mod_ref, npre_ref, npost_ref, wi_hbm, wo_hbm = [refs.pop(0) for _ in range(5)]
    o_refs = [refs.pop(0) for _ in range(2 if split_out else 1)]
    wg_ref, wu_ref, wo_ref, sg_ref, su_ref, so_ref, h_ref, acc_ref, act_ref, sem = refs

    mo = 0 if sub == 0 else 6
    ni = 0 if sub == 0 else 2
    t = pl.program_id(0)
    is_lat = t < n_lat_tiles

    def load_x(rows=slice(None)):
        if split_in:
            return jnp.where(is_lat, x_refs[0][rows, :], x_refs[1][rows, :])
        return x_refs[0][rows, :]

    def store(out, rows=slice(None)):
        if not split_out:
            o_refs[0][rows, :] = out
            return

        @pl.when(is_lat)
        def _():
            o_refs[0][rows, :] = out

        @pl.when(jnp.logical_not(is_lat))
        def _():
            o_refs[1][rows, :] = out

    def modulated(x):
        shift = mod_ref[mo:mo + 1, :]
        scale = mod_ref[mo + 1:mo + 2, :]
        return (_rms(x, npre_ref[ni:ni + 1, :]) * (1.0 + scale) + shift).astype(BF)

    def residual(x, acc):
        gate = mod_ref[mo + 2:mo + 3, :]
        return x + 0.5 * gate * _rms(acc, npost_ref[ni:ni + 1, :])

    def chunk_update(h, wg, wu, wo):
        g = _dot(h, wg)
        u = _dot(h, wu)
        a = (g * jax.nn.sigmoid(g) * u).astype(BF)
        return _dot(a, wo)

    def chunk_copies(j, slot):
        cols = pl.ds(pl.multiple_of(j * FF_CHUNK, FF_CHUNK), FF_CHUNK)
        up = pl.ds(pl.multiple_of(D_FF + j * FF_CHUNK, FF_CHUNK), FF_CHUNK)
        return (
            pltpu.make_async_copy(wi_hbm.at[l, sub, :, cols], sg_ref.at[slot], sem.at[0, slot]),
            pltpu.make_async_copy(wi_hbm.at[l, sub, :, up], su_ref.at[slot], sem.at[1, slot]),
            pltpu.make_async_copy(wo_hbm.at[l, sub, cols, :], so_ref.at[slot], sem.at[2, slot]),
        )

    @pl.when(t == 0)
    def _():
        for cp in chunk_copies(0, 0):
            cp.start()
        h_ref[...] = modulated(load_x())
        acc_ref[...] = jnp.zeros(acc_ref.shape, F32)

        def body(j, carry):
            slot = lax.rem(j, 2)

            @pl.when(j + 1 < N_FF_CHUNKS)
            def _():
                for cp in chunk_copies(j + 1, 1 - slot):
                    cp.start()

            for cp in chunk_copies(j, slot):
                cp.wait()
            wg = sg_ref[slot].astype(BF)
            wu = su_ref[slot].astype(BF)
            wo = so_ref[slot].astype(BF)
            wg_ref[j] = wg
            wu_ref[j] = wu
            wo_ref[pl.ds(pl.multiple_of(j * FF_CHUNK, FF_CHUNK), FF_CHUNK), :] = wo
            acc_ref[...] += chunk_update(h_ref[...], wg, wu, wo)
            return carry

        lax.fori_loop(0, N_FF_CHUNKS, body, 0)
        store(residual(load_x(), acc_ref[...]))

    @pl.when(t > 0)
    def _():
        def sub_tile(s, carry):
            rows = pl.ds(pl.multiple_of(s * FFN_SUBTILE, FFN_SUBTILE), FFN_SUBTILE)
            x = load_x(rows)
            h = modulated(x)
            for j in range(N_FF_CHUNKS):
                g = _dot(h, wg_ref[j])
                u = _dot(h, wu_ref[j])
                act_ref[:, j * FF_CHUNK:(j + 1) * FF_CHUNK] = (g * jax.nn.sigmoid(g) * u).astype(BF)
            acc = _dot(act_ref[...], wo_ref[...])
            store(residual(x, acc), rows)
            return carry

        n_sub = acc_ref.shape[0] // FFN_SUBTILE
        if n_sub == 1:
            sub_tile(0, 0)
        else:
            lax.fori_loop(0, n_sub, sub_tile, 0)


def _ffn(xs, mods, npre, npost, wi, wo, l, sub, split_out=False):
    split_in = len(xs) == 2
    tm = FFN_SUBTILE if (split_in or split_out) else FFN_TILE
    n_tiles = N_TOK // tm
    nl = N_LAT // tm
    lat_idx = lambda t: (_lat_tile(t, nl), 0)
    ctx_idx = lambda t: (_ctx_tile(t, nl), 0)
    split_specs = [pl.BlockSpec((tm, D), lat_idx), pl.BlockSpec((tm, D), ctx_idx)]
    joint_specs = [pl.BlockSpec((tm, D), lambda t: (t, 0))]
    split_shapes = [jax.ShapeDtypeStruct((N_LAT, D), F32), jax.ShapeDtypeStruct((N_CTX, D), F32)]
    joint_shapes = [jax.ShapeDtypeStruct((N_TOK, D), F32)]
    out = pl.pallas_call(
        functools.partial(_ffn_kernel, l=l, sub=sub, split_in=split_in, split_out=split_out,
                          n_lat_tiles=nl),
        grid=(n_tiles,),
        in_specs=(split_specs if split_in else joint_specs) + [
            pl.BlockSpec((None, None, 9, D), lambda t: (l, _mod_row(t, tm), 0, 0)),
            pl.BlockSpec((None, 3, D), lambda t: (l, 0, 0)),
            pl.BlockSpec((None, 3, D), lambda t: (l, 0, 0)),
            pl.BlockSpec(memory_space=pl.ANY),
            pl.BlockSpec(memory_space=pl.ANY),
        ],
        out_specs=split_specs if split_out else joint_specs,
        out_shape=split_shapes if split_out else joint_shapes,
        scratch_shapes=[
            pltpu.VMEM((N_FF_CHUNKS, D, FF_CHUNK), BF),
            pltpu.VMEM((N_FF_CHUNKS, D, FF_CHUNK), BF),
            pltpu.VMEM((D_FF, D), BF),
            pltpu.VMEM((2, D, FF_CHUNK), F32),
            pltpu.VMEM((2, D, FF_CHUNK), F32),
            pltpu.VMEM((2, FF_CHUNK, D), F32),
            pltpu.VMEM((tm, D), BF),
            pltpu.VMEM((tm, D), F32),
            pltpu.VMEM((FFN_SUBTILE, D_FF), BF),
            pltpu.SemaphoreType.DMA((3, 2)),
        ],
        compiler_params=pltpu.CompilerParams(
            dimension_semantics=("arbitrary",), vmem_limit_bytes=VMEM_LIMIT),
        name=f"ffn_l{l}_s{sub}",
    )(*xs, mods, npre, npost, wi, wo)
    return tuple(out) if split_out else out[0]


def _proj_kernel(x_ref, mod_ref, npre_ref, w_ref, *rest, n_carried):
    qkv_ref, f_ref, g_ref, k_ref, v_ref, wb_ref = rest[n_carried:]

    @pl.when(pl.program_id(0) == 0)
    def _():
        wb_ref[...] = w_ref[...].astype(BF)

    x = x_ref[...]
    shift = mod_ref[3:4, :]
    scale = mod_ref[4:5, :]
    h = (_rms(x, npre_ref[1:2, :]) * (1.0 + scale) + shift).astype(BF)
    z = _dot(h, wb_ref[...])
    qkv_ref[...] = z[:, :3 * WA].astype(BF)
    f_ref[...] = z[:, 3 * WA:3 * WA + WB].astype(BF)
    g_ref[...] = z[:, 3 * WA + WB:].astype(BF)
    k_ref[...] = z[:, WA:2 * WA].reshape(k_ref.shape)
    v_ref[...] = z[:, 2 * WA:3 * WA].reshape(v_ref.shape)


def _proj(x, mods, npre, w_in, carried, l, tm=1024):
    n_tiles = N_TOK // tm
    nl = N_LAT // tm
    bpt = tm // CTX_SEQ
    kv_shape = jax.ShapeDtypeStruct((N_CTX_B, DEPTH, CTX_SEQ, WA), F32)
    kv_spec = pl.BlockSpec((bpt, None, CTX_SEQ, WA), lambda t: (_ctx_tile(t, nl), l, 0, 0))
    n_fixed = 4
    return pl.pallas_call(
        functools.partial(_proj_kernel, n_carried=len(carried)),
        grid=(n_tiles,),
        in_specs=[
            pl.BlockSpec((tm, D), lambda t: (t, 0)),
            pl.BlockSpec((None, None, 9, D), lambda t: (l, _mod_row(t, tm), 0, 0)),
            pl.BlockSpec((None, 3, D), lambda t: (l, 0, 0)),
            pl.BlockSpec((None, D, PROJ_W), lambda t: (l, 0, 0), pipeline_mode=pl.Buffered(1)),
        ] + [pl.BlockSpec(memory_space=pl.ANY)] * len(carried),
        out_specs=[
            pl.BlockSpec((tm, 3 * WA), lambda t: (t, 0)),
            pl.BlockSpec((tm, WB), lambda t: (t, 0)),
            pl.BlockSpec((tm, 2 * WC), lambda t: (t, 0)),
            kv_spec,
            kv_spec,
        ],
        out_shape=[
            jax.ShapeDtypeStruct((N_TOK, 3 * WA), BF),
            jax.ShapeDtypeStruct((N_TOK, WB), BF),
            jax.ShapeDtypeStruct((N_TOK, 2 * WC), BF),
            kv_shape,
            kv_shape,
        ],
        scratch_shapes=[pltpu.VMEM((D, PROJ_W), BF)],
        input_output_aliases={n_fixed + i: 3 + i for i in range(len(carried))},
        compiler_params=pltpu.CompilerParams(
            dimension_semantics=("arbitrary",), vmem_limit_bytes=VMEM_LIMIT),
        name=f"proj_l{l}",
    )(x, mods, npre, w_in, *carried)


def _fourier(f, fc_ref, fn_ref):
    n = fn_ref.shape[0]
    xcs = _dot(f, fc_ref[...]).astype(BF)
    outs = []
    for b in range(f.shape[0] // n):
        xb = xcs[b * n:(b + 1) * n]
        stacked = jnp.concatenate([xb[:, :WB], xb[:, WB:]], axis=0)
        outs.append(_dot(fn_ref[...], stacked))
    return outs[0] if len(outs) == 1 else jnp.concatenate(outs, axis=0)


def _gmlp(g, gn_ref, ws_ref, gb_ref, avg_ref):
    gl = jax.nn.gelu(g.astype(F32), approximate=True)
    u = gl[:, :WC]
    v = gl[:, WC:]
    avg = avg_ref[...]
    mu = _split3_dot(v, avg)
    d = v - mu
    var = _split3_dot(d * d, avg)
    vn = (d * lax.rsqrt(var + EPS) * gn_ref[...]).astype(BF)
    lane_group = lax.broadcasted_iota(jnp.int32, (CHUNK, WC), 1) // GC
    outs = []
    for c in range(g.shape[0] // CHUNK):
        vc = vn[c * CHUNK:(c + 1) * CHUNK, :]
        sp = _dot(ws_ref[0].astype(BF), vc)
        for grp in range(1, NGRP):
            sp = jnp.where(lane_group == grp, _dot(ws_ref[grp].astype(BF), vc), sp)
        outs.append(u[c * CHUNK:(c + 1) * CHUNK, :] * (sp + gb_ref[...]))
    return jnp.concatenate(outs, axis=0)


def _cast_weight_once(w_ref, wb_ref):
    @pl.when(pl.program_id(0) == 0)
    def _():
        wb_ref[...] = w_ref[...].astype(BF)


def _mix_out(x, o, gate, npost, wb_ref):
    y = _dot(o, wb_ref[...])
    return x + gate * _rms(y, npost)


def _ctx_mixer_kernel(x_ref, qkv_ref, f_ref, g_ref, mod_ref, npost_ref, wo_ref, fc_ref, fn_ref,
                      gn_ref, ws_ref, gb_ref, avg_ref, o_ref, mix_ref, wb_ref):
    _cast_weight_once(wo_ref, wb_ref)
    gw = HEADS_PER_GROUP * HD
    pair_w = 2 * HD
    first_of_pair = lax.broadcasted_iota(jnp.int32, (CTX_SEQ, pair_w), 1) < HD
    own = (lax.broadcasted_iota(jnp.int32, (HEADS_PER_GROUP * CTX_SEQ, gw), 0) // CTX_SEQ
           == lax.broadcasted_iota(jnp.int32, (HEADS_PER_GROUP * CTX_SEQ, gw), 1) // HD)
    lane_head = lax.broadcasted_iota(jnp.int32, (CTX_SEQ, gw), 1) // HD
    zero = jnp.zeros((), BF)
    for b in range(x_ref.shape[0] // CTX_SEQ):
        rows = slice(b * CTX_SEQ, (b + 1) * CTX_SEQ)
        scores = []
        for h in range(NH):
            lanes = slice((h // 2) * pair_w, (h // 2 + 1) * pair_w)
            q2 = qkv_ref[rows, lanes] * QK_SCALE
            k2 = qkv_ref[rows, WA + (h // 2) * pair_w:WA + (h // 2 + 1) * pair_w]
            qh = jnp.where(first_of_pair, q2, zero) if h % 2 == 0 else jnp.where(first_of_pair, zero, q2)
            scores.append(_dot_nt(qh, k2))
        probs, dens = [], []
        for s in scores:
            p = jnp.exp(s - jnp.max(s, axis=-1, keepdims=True))
            dens.append(jnp.sum(p, axis=-1, keepdims=True))
            probs.append(p.astype(BF))
        for grp in range(N_HEAD_GROUPS):
            hs = range(grp * HEADS_PER_GROUP, (grp + 1) * HEADS_PER_GROUP)
            v4 = qkv_ref[rows, 2 * WA + grp * gw:2 * WA + (grp + 1) * gw]
            vbd = jnp.where(own, jnp.concatenate([v4] * HEADS_PER_GROUP, axis=0), zero)
            o = _dot(jnp.concatenate([probs[h] for h in hs], axis=1), vbd)
            den = dens[hs[0]]
            for i in range(1, HEADS_PER_GROUP):
                den = jnp.where(lane_head == i, dens[hs[i]], den)
            mix_ref[rows, grp * gw:(grp + 1) * gw] = (o / den).astype(BF)
    mix_ref[:, WA:WA + WB] = _fourier(f_ref[...], fc_ref, fn_ref).astype(BF)
    mix_ref[:, WA + WB:] = _gmlp(g_ref[...], gn_ref, ws_ref, gb_ref, avg_ref).astype(BF)
    o_ref[...] = _mix_out(x_ref[...], mix_ref[...], mod_ref[5:6, :], npost_ref[1:2, :], wb_ref)


def _const_spec(shape):
    nd = len(shape)
    return pl.BlockSpec(shape, lambda *_: (0,) * nd)


def _ctx_mixer(x, qkv, f, g, mods, npost, w_out, fc, fn, gn, ws, gb, avg, l):
    tm = CTX_PER_STEP * CTX_SEQ
    off = N_LAT // tm
    row = lambda b: (b + off, 0)
    return pl.pallas_call(
        _ctx_mixer_kernel,
        grid=(N_CTX_B // CTX_PER_STEP,),
        in_specs=[
            pl.BlockSpec((tm, D), row),
            pl.BlockSpec((tm, 3 * WA), row),
            pl.BlockSpec((tm, WB), row),
            pl.BlockSpec((tm, 2 * WC), row),
            pl.BlockSpec((None, None, 9, D), lambda b: (l, CTX_MOD_ROW, 0, 0)),
            pl.BlockSpec((None, 3, D), lambda b: (l, 0, 0)),
            pl.BlockSpec((None, D, D), lambda b: (l, 0, 0), pipeline_mode=pl.Buffered(1)),
            _const_spec(fc.shape),
            _const_spec(fn.shape),
            pl.BlockSpec((None, 1, WC), lambda b: (l, 0, 0)),
            pl.BlockSpec((None, NGRP, CHUNK, CHUNK), lambda b: (l, 0, 0, 0)),
            pl.BlockSpec((None, CHUNK, WC), lambda b: (l, 0, 0)),
            _const_spec(avg.shape),
        ],
        out_specs=pl.BlockSpec((tm, D), row),
        out_shape=jax.ShapeDtypeStruct((N_TOK, D), F32),
        input_output_aliases={0: 0},
        scratch_shapes=[pltpu.VMEM((tm, D), BF), pltpu.VMEM((D, D), BF)],
        compiler_params=pltpu.CompilerParams(
            dimension_semantics=("arbitrary",), vmem_limit_bytes=VMEM_LIMIT),
        name=f"ctx_mixer_l{l}",
    )(x, qkv, f, g, mods, npost, w_out, fc, fn, gn, ws, gb, avg)


def _window_start(r):
    return min(max(r - WIN_ROWS // 2, 0), GRID_ROWS - WIN_ROWS)


def _nbr_attn_kernel(qkv_ref, ck_ref, cv_ref, bias_ref, o_ref):
    gw = HEADS_PER_GROUP * HD
    row_head = lax.broadcasted_iota(jnp.int32, (HEADS_PER_GROUP * GRID_W, gw), 0) // GRID_W
    lane_head = lax.broadcasted_iota(jnp.int32, (HEADS_PER_GROUP * GRID_W, gw), 1) // HD
    own = row_head == lane_head
    for hg in range(N_HEAD_GROUPS):
        heads = range(hg * HEADS_PER_GROUP, (hg + 1) * HEADS_PER_GROUP)
        ck = jnp.concatenate([ck_ref[:, h, :] for h in heads], axis=1).astype(BF)
        cv = jnp.concatenate([cv_ref[:, h, :] for h in heads], axis=1).astype(BF)
        for r in range(GRID_ROWS):
            ws = _window_start(r)
            q = qkv_ref[r * GRID_W:(r + 1) * GRID_W, hg * gw:(hg + 1) * gw] * QK_SCALE
            qs = jnp.where(own, jnp.concatenate([q] * HEADS_PER_GROUP, axis=0), jnp.zeros((), BF))
            kw = qkv_ref[ws * GRID_W:(ws + WIN_ROWS) * GRID_W, WA + hg * gw:WA + (hg + 1) * gw]
            vw = qkv_ref[ws * GRID_W:(ws + WIN_ROWS) * GRID_W, 2 * WA + hg * gw:2 * WA + (hg + 1) * gw]
            s_loc = _dot_nt(qs, kw) + bias_ref[hg, ws - r + N_DELTA - 1].astype(F32)
            s_ctx = _dot_nt(qs, ck)
            m = jnp.maximum(jnp.max(s_loc, axis=-1, keepdims=True),
                            jnp.max(s_ctx, axis=-1, keepdims=True))
            p_loc = jnp.exp(s_loc - m)
            p_ctx = jnp.exp(s_ctx - m)
            den = jnp.sum(p_loc, axis=-1, keepdims=True) + jnp.sum(p_ctx, axis=-1, keepdims=True)
            o = (_dot(p_loc.astype(BF), vw) + _dot(p_ctx.astype(BF), cv)) / den
            o = jnp.where(own, o, 0.0)
            o = (o[0:GRID_W] + o[GRID_W:2 * GRID_W]) + (o[2 * GRID_W:3 * GRID_W] + o[3 * GRID_W:])
            o_ref[r * GRID_W:(r + 1) * GRID_W, hg * gw:(hg + 1) * gw] = o.astype(BF)


def _nbr_attn(qkv, cache_k, cache_v, bias, l):
    return pl.pallas_call(
        _nbr_attn_kernel,
        grid=(N_LAT_B,),
        in_specs=[
            pl.BlockSpec((LAT_SEQ, 3 * WA), lambda b: (b, 0)),
            pl.BlockSpec((None, None, PAST, NH, HD), lambda b: (b, l, 0, 0, 0)),
            pl.BlockSpec((None, None, PAST, NH, HD), lambda b: (b, l, 0, 0, 0)),
            pl.BlockSpec((None,) + bias.shape[1:], lambda b: (l, 0, 0, 0, 0)),
        ],
        out_specs=pl.BlockSpec((LAT_SEQ, WA), lambda b: (b, 0)),
        out_shape=jax.ShapeDtypeStruct((N_LAT, WA), BF),
        compiler_params=pltpu.CompilerParams(
            dimension_semantics=("arbitrary",), vmem_limit_bytes=VMEM_LIMIT),
        name=f"nbr_attn_l{l}",
    )(qkv, cache_k, cache_v, bias)


def _lat_mixer_kernel(x_ref, oa_ref, f_ref, g_ref, mod_ref, npost_ref, wo_ref, fc_ref, fn_ref,
                      gn_ref, ws_ref, gb_ref, avg_ref, o_ref, mix_ref, wb_ref):
    _cast_weight_once(wo_ref, wb_ref)
    mix_ref[:, :WA] = oa_ref[...]
    mix_ref[:, WA:WA + WB] = _fourier(f_ref[...], fc_ref, fn_ref).astype(BF)
    mix_ref[:, WA + WB:] = _gmlp(g_ref[...], gn_ref, ws_ref, gb_ref, avg_ref).astype(BF)
    o_ref[...] = _mix_out(x_ref[...], mix_ref[...], mod_ref[5:6, :], npost_ref[1:2, :], wb_ref)


def _lat_mixer(x, oa, f, g, mods, npost, w_out, fc, fn, gn, ws, gb, avg, l):
    row = lambda b: (b, 0)
    return pl.pallas_call(
        _lat_mixer_kernel,
        grid=(N_LAT_B,),
        in_specs=[
            pl.BlockSpec((LAT_SEQ, D), row),
            pl.BlockSpec((LAT_SEQ, WA), row),
            pl.BlockSpec((LAT_SEQ, WB), row),
            pl.BlockSpec((LAT_SEQ, 2 * WC), row),
            pl.BlockSpec((None, None, 9, D), lambda b: (l, b, 0, 0)),
            pl.BlockSpec((None, 3, D), lambda b: (l, 0, 0)),
            pl.BlockSpec((None, D, D), lambda b: (l, 0, 0), pipeline_mode=pl.Buffered(1)),
            _const_spec(fc.shape),
            pl.BlockSpec(fn.shape, lambda b: (0, 0), pipeline_mode=pl.Buffered(1)),
            pl.BlockSpec((None, 1, WC), lambda b: (l, 0, 0)),
            pl.BlockSpec((None, NGRP, CHUNK, CHUNK), lambda b: (l, 0, 0, 0)),
            pl.BlockSpec((None, CHUNK, WC), lambda b: (l, 0, 0)),
            _const_spec(avg.shape),
        ],
        out_specs=pl.BlockSpec((LAT_SEQ, D), row),
        out_shape=jax.ShapeDtypeStruct((N_TOK, D), F32),
        input_output_aliases={0: 0},
        scratch_shapes=[pltpu.VMEM((LAT_SEQ, D), BF), pltpu.VMEM((D, D), BF)],
        compiler_params=pltpu.CompilerParams(
            dimension_semantics=("arbitrary",), vmem_limit_bytes=VMEM_LIMIT),
        name=f"lat_mixer_l{l}",
    )(x, oa, f, g, mods, npost, w_out, fc, fn, gn, ws, gb, avg)


def _dft_tables(n):
    idx = np.arange(n)
    ang = 2.0 * np.pi * ((idx[:, None] * idx[None, :]) % n) / n
    return (np.concatenate([np.cos(ang), -np.sin(ang)], axis=1) / np.sqrt(n)).astype(np.float32)


def _channel_dft_table():
    idx = np.arange(GC)
    ang = 2.0 * np.pi * ((idx[:, None] * idx[None, :]) % GC) / GC
    eye = np.eye(NGRP)
    return (np.concatenate([np.kron(eye, np.cos(ang)), np.kron(eye, np.sin(ang))], axis=1)
            / np.sqrt(GC)).astype(np.float32)


def _group_average_table():
    return np.kron(np.eye(NGRP), np.full((GC, GC), 1.0 / GC)).astype(np.float32)


def kernel(x_prompt, x_sample, cache_k, cache_v, c, c_ctx, ada_w, ada_b, norm_pre, norm_post,
           ffn_w_in, ffn_w_out, w_in, w_out, rpb, gmlp_norm, gmlp_w, gmlp_b):
    xs = (x_sample.reshape(N_LAT, D), x_prompt.reshape(N_CTX, D))
    cs = jnp.concatenate([c, c_ctx[None, :], jnp.zeros((MOD_ROWS - N_LAT_B - 1, D), F32)], axis=0)
    mods = _ada_mods(cs, ada_w, ada_b)
    bias = _bias_table(rpb)

    gn = gmlp_norm.reshape(DEPTH, 1, WC)
    gb = jnp.repeat(jnp.transpose(gmlp_b, (0, 2, 1)), GC, axis=2)

    fc = jnp.asarray(_channel_dft_table()).astype(BF)
    fn_ctx = jnp.asarray(_dft_tables(CTX_SEQ)).astype(BF)
    fn_lat = jnp.asarray(_dft_tables(LAT_SEQ)).astype(BF)
    avg = jnp.asarray(_group_average_table()).astype(BF)

    carried = (jnp.zeros((N_CTX_B, DEPTH, CTX_SEQ, WA), F32),) * 2
    for l in range(DEPTH):
        x = _ffn(xs, mods, norm_pre, norm_post, ffn_w_in, ffn_w_out, l, 0)
        qkv, f, g, new_k, new_v = _proj(x, mods, norm_pre, w_in, carried, l)
        carried = (new_k, new_v)
        oa = _nbr_attn(qkv, cache_k, cache_v, bias, l)
        x = _ctx_mixer(x, qkv, f, g, mods, norm_post, w_out, fc, fn_ctx, gn, gmlp_w, gb, avg, l)
        x = _lat_mixer(x, oa, f, g, mods, norm_post, w_out, fc, fn_lat, gn, gmlp_w, gb, avg, l)
        xs = _ffn((x,), mods, norm_pre, norm_post, ffn_w_in, ffn_w_out, l, 1,
                  split_out=(l == DEPTH - 1))
        if l < DEPTH - 1:
            xs = (xs,)
    xl, xc = xs

    kv_out = (N_CTX_B, DEPTH, CTX_SEQ, NH, HD)
    return (xc.reshape(N_CTX_B, CTX_SEQ, D), xl.reshape(N_LAT_B, LAT_SEQ, D),
            new_k.reshape(kv_out), new_v.reshape(kv_out))
```

```python
import functools

import numpy as np
import jax
import jax.numpy as jnp
from jax import lax
from jax.experimental import pallas as pl
from jax.experimental.pallas import tpu as pltpu

D = 1024
DEPTH = 4
N_CTX_B = 16
CTX_SEQ = 256
N_LAT_B = 4
LAT_SEQ = 1024
N_LAT = N_LAT_B * LAT_SEQ
N_CTX = N_CTX_B * CTX_SEQ
N_TOK = N_LAT + N_CTX
PAST = 256
GRID_W = 64
GRID_ROWS = LAT_SEQ // GRID_W
HD = 64
NH = 8
WA = 512
WB = 256
WC = 256
NGRP = 4
GC = 64
CHUNK = 128
PROJ_W = 3 * WA + WB + 2 * WC
D_FF = 2816
FF_CHUNK = 256
N_FF_CHUNKS = D_FF // FF_CHUNK
FFN_TILE = 1024
FFN_SUBTILE = 512
WIN_ROWS = 8
WIN_COLS = 16
N_ROW_OFF = 2 * WIN_ROWS - 1
N_COL_OFF = 2 * WIN_COLS - 1
N_DELTA = 8
EPS = 1e-6
NEG = -1e30
ADA_TILE = 2304
MOD_ROWS = 8
CTX_MOD_ROW = 4
CTX_PER_STEP = 2
HEADS_PER_GROUP = 4
N_HEAD_GROUPS = NH // HEADS_PER_GROUP

BF = jnp.bfloat16
F32 = jnp.float32
QK_SCALE = HD ** -0.5

VMEM_LIMIT = 56 * 1024 * 1024


def _dot(a, b):
    return jnp.dot(a, b, preferred_element_type=F32)


def _dot_nt(a, b):
    return lax.dot_general(a, b, (((1,), (1,)), ((), ())), preferred_element_type=F32)


def _split2_dot(a, b):
    a0 = a.astype(BF)
    a1 = (a - a0.astype(F32)).astype(BF)
    return _dot(a0, b) + _dot(a1, b)


def _rms(x, g):
    return x * lax.rsqrt(jnp.mean(x * x, axis=-1, keepdims=True) + EPS) * g


def _mod_row(tile, tm):
    return jnp.minimum((tile * tm) // LAT_SEQ, CTX_MOD_ROW)


def _ada_kernel(cs_ref, w_ref, b_ref, o_ref):
    cs = cs_ref[...]
    s = cs * jax.nn.sigmoid(cs)
    o_ref[...] = _dot(s.astype(BF), w_ref[...].astype(BF)) + b_ref[...]


def _ada_mods(cs, ada_w, ada_b):
    n_out = 9 * D
    tn = ADA_TILE
    out = pl.pallas_call(
        _ada_kernel,
        grid=(DEPTH, n_out // tn),
        in_specs=[
            pl.BlockSpec((MOD_ROWS, D), lambda l, n: (0, 0)),
            pl.BlockSpec((None, D, tn), lambda l, n: (l, 0, n)),
            pl.BlockSpec((None, 1, tn), lambda l, n: (l, 0, n)),
        ],
        out_specs=pl.BlockSpec((None, MOD_ROWS, tn), lambda l, n: (l, 0, n)),
        out_shape=jax.ShapeDtypeStruct((DEPTH, MOD_ROWS, n_out), F32),
        compiler_params=pltpu.CompilerParams(
            dimension_semantics=("arbitrary", "arbitrary"), vmem_limit_bytes=VMEM_LIMIT),
        name="ada_mods",
    )(cs, ada_w, ada_b.reshape(DEPTH, 1, n_out))
    return out.reshape(DEPTH, MOD_ROWS, 9, D)


RPB_LANES = 128
RPB_HALF = RPB_LANES // 2


def _bias_kernel(rpb_ref, o_ref):
    qq = lax.broadcasted_iota(jnp.int32, (GRID_W, RPB_LANES), 0)
    lane = lax.broadcasted_iota(jnp.int32, (GRID_W, RPB_LANES), 1)
    second = lane >= GRID_W
    kk = jnp.where(second, lane - GRID_W, lane)
    col_start = jnp.clip(qq - WIN_COLS // 2, 0, GRID_W - WIN_COLS)
    visible = (kk >= col_start) & (kk < col_start + WIN_COLS)
    m = lax.broadcasted_iota(jnp.int32, (N_ROW_OFF, RPB_LANES), 1)
    diff = jnp.where(m < RPB_HALF, m, m - RPB_LANES)
    ext_idx = jnp.clip(diff + (WIN_COLS - 1), 0, N_COL_OFF - 1)
    for i in range(HEADS_PER_GROUP):
        ext = jnp.take_along_axis(rpb_ref[i], ext_idx, axis=1)
        for ro in range(N_ROW_OFF - 1):
            e0 = jnp.broadcast_to(ext[ro:ro + 1, :], (GRID_W, RPB_LANES))
            e1 = jnp.broadcast_to(ext[ro + 1:ro + 2, :], (GRID_W, RPB_LANES))
            t = jnp.where(second,
                          pltpu.roll(e1, GRID_W, 1, stride=1, stride_axis=0),
                          pltpu.roll(e0, 0, 1, stride=1, stride_axis=0))
            t = jnp.where(visible, t, NEG).astype(o_ref.dtype)
            for a in range(0, WIN_ROWS, 2):
                delta_idx = ro - a
                if 0 <= delta_idx < N_DELTA:
                    o_ref[delta_idx, i * GRID_W:(i + 1) * GRID_W, a * GRID_W:(a + 2) * GRID_W] = t


def _bias_table(rpb):
    rpb_padded = jnp.pad(rpb, ((0, 0), (0, 0), (0, 0), (0, RPB_LANES - N_COL_OFF)))
    return pl.pallas_call(
        _bias_kernel,
        grid=(DEPTH, N_HEAD_GROUPS),
        in_specs=[pl.BlockSpec((None, HEADS_PER_GROUP, N_ROW_OFF, RPB_LANES),
                               lambda l, g: (l, g, 0, 0))],
        out_specs=pl.BlockSpec((None, None, N_DELTA, HEADS_PER_GROUP * GRID_W, WIN_ROWS * GRID_W),
                               lambda l, g: (l, g, 0, 0, 0)),
        out_shape=jax.ShapeDtypeStruct(
            (DEPTH, N_HEAD_GROUPS, N_DELTA, HEADS_PER_GROUP * GRID_W, WIN_ROWS * GRID_W), BF),
        compiler_params=pltpu.CompilerParams(
            dimension_semantics=("arbitrary", "arbitrary"), vmem_limit_bytes=VMEM_LIMIT),
        name="rpb_table",
    )(rpb_padded)


def _lat_tile(t, n_lat_tiles):
    return jnp.minimum(t, n_lat_tiles - 1)


def _ctx_tile(t, n_lat_tiles):
    return jnp.maximum(t - n_lat_tiles, 0)


def _ffn_kernel(*refs, l, sub, split_in, split_out, n_lat_tiles):
    refs = list(refs)
    x_refs = [refs.pop(0) for _ in range(2 if split_in else 1)]
    mod_ref, npre_ref, npost_ref, wi_hbm, wo_hbm = [refs.pop(0) for _ in range(5)]
    o_refs = [refs.pop(0) for _ in range(2 if split_out else 1)]
    wg_ref, wu_ref, wo_ref, sg_ref, su_ref, so_ref, h_ref, acc_ref, act_ref, sem = refs

    mo = 0 if sub == 0 else 6
    ni = 0 if sub == 0 else 2
    t = pl.program_id(0)
    is_lat = t < n_lat_tiles

    def load_x(rows=slice(None)):
        if split_in:
            return jnp.where(is_lat, x_refs[0][rows, :], x_refs[1][rows, :])
        return x_refs[0][rows, :]

    def store(out, rows=slice(None)):
        if not split_out:
            o_refs[0][rows, :] = out
            return

        @pl.when(is_lat)
        def _():
            o_refs[0][rows, :] = out

        @pl.when(jnp.logical_not(is_lat))
        def _():
            o_refs[1][rows, :] = out

    def modulated(x):
        shift = mod_ref[mo:mo + 1, :]
        scale = mod_ref[mo + 1:mo + 2, :]
        return (_rms(x, npre_ref[ni:ni + 1, :]) * (1.0 + scale) + shift).astype(BF)

    def residual(x, acc):
        gate = mod_ref[mo + 2:mo + 3, :]
        return x + 0.5 * gate * _rms(acc, npost_ref[ni:ni + 1, :])

    def chunk_update(h, wg, wu, wo):
        g = _dot(h, wg)
        u = _dot(h, wu)
        a = (g * jax.nn.sigmoid(g) * u).astype(BF)
        return _dot(a, wo)

    def chunk_copies(j, slot):
        cols = pl.ds(pl.multiple_of(j * FF_CHUNK, FF_CHUNK), FF_CHUNK)
        up = pl.ds(pl.multiple_of(D_FF + j * FF_CHUNK, FF_CHUNK), FF_CHUNK)
        return (
            pltpu.make_async_copy(wi_hbm.at[l, sub, :, cols], sg_ref.at[slot], sem.at[0, slot]),
            pltpu.make_async_copy(wi_hbm.at[l, sub, :, up], su_ref.at[slot], sem.at[1, slot]),
            pltpu.make_async_copy(wo_hbm.at[l, sub, cols, :], so_ref.at[slot], sem.at[2, slot]),
        )

    @pl.when(t == 0)
    def _():
        for cp in chunk_copies(0, 0):
            cp.start()
        h_ref[...] = modulated(load_x())
        acc_ref[...] = jnp.zeros(acc_ref.shape, F32)

        def body(j, carry):
            slot = lax.rem(j, 2)

            @pl.when(j + 1 < N_FF_CHUNKS)
            def _():
                for cp in chunk_copies(j + 1, 1 - slot):
                    cp.start()

            for cp in chunk_copies(j, slot):
                cp.wait()
            wg = sg_ref[slot].astype(BF)
            wu = su_ref[slot].astype(BF)
            wo = so_ref[slot].astype(BF)
            wg_ref[j] = wg
            wu_ref[j] = wu
            wo_ref[pl.ds(pl.multiple_of(j * FF_CHUNK, FF_CHUNK), FF_CHUNK), :] = wo
            acc_ref[...] += chunk_update(h_ref[...], wg, wu, wo)
            return carry

        lax.fori_loop(0, N_FF_CHUNKS, body, 0)
        store(residual(load_x(), acc_ref[...]))

    @pl.when(t > 0)
    def _():
        def sub_tile(s, carry):
            rows = pl.ds(pl.multiple_of(s * FFN_SUBTILE, FFN_SUBTILE), FFN_SUBTILE)
            x = load_x(rows)
            h = modulated(x)
            for j in range(N_FF_CHUNKS):
                g = _dot(h, wg_ref[j])
                u = _dot(h, wu_ref[j])
                act_ref[:, j * FF_CHUNK:(j + 1) * FF_CHUNK] = (g * jax.nn.sigmoid(g) * u).astype(BF)
            acc = _dot(act_ref[...], wo_ref[...])
            store(residual(x, acc), rows)
            return carry

        n_sub = acc_ref.shape[0] // FFN_SUBTILE
        if n_sub == 1:
            sub_tile(0, 0)
        else:
            lax.fori_loop(0, n_sub, sub_tile, 0)


def _ffn(xs, mods, npre, npost, wi, wo, l, sub, split_out=False):
    split_in = len(xs) == 2
    tm = FFN_SUBTILE if (split_in or split_out) else FFN_TILE
    n_tiles = N_TOK // tm
    nl = N_LAT // tm
    lat_idx = lambda t: (_lat_tile(t, nl), 0)
    ctx_idx = lambda t: (_ctx_tile(t, nl), 0)
    split_specs = [pl.BlockSpec((tm, D), lat_idx), pl.BlockSpec((tm, D), ctx_idx)]
    joint_specs = [pl.BlockSpec((tm, D), lambda t: (t, 0))]
    split_shapes = [jax.ShapeDtypeStruct((N_LAT, D), F32), jax.ShapeDtypeStruct((N_CTX, D), F32)]
    joint_shapes = [jax.ShapeDtypeStruct((N_TOK, D), F32)]
    out = pl.pallas_call(
        functools.partial(_ffn_kernel, l=l, sub=sub, split_in=split_in, split_out=split_out,
                          n_lat_tiles=nl),
        grid=(n_tiles,),
        in_specs=(split_specs if split_in else joint_specs) + [
            pl.BlockSpec((None, None, 9, D), lambda t: (l, _mod_row(t, tm), 0, 0)),
            pl.BlockSpec((None, 3, D), lambda t: (l, 0, 0)),
            pl.BlockSpec((None, 3, D), lambda t: (l, 0, 0)),
            pl.BlockSpec(memory_space=pl.ANY),
            pl.BlockSpec(memory_space=pl.ANY),
        ],
        out_specs=split_specs if split_out else joint_specs,
        out_shape=split_shapes if split_out else joint_shapes,
        scratch_shapes=[
            pltpu.VMEM((N_FF_CHUNKS, D, FF_CHUNK), BF),
            pltpu.VMEM((N_FF_CHUNKS, D, FF_CHUNK), BF),
            pltpu.VMEM((D_FF, D), BF),
            pltpu.VMEM((2, D, FF_CHUNK), F32),
            pltpu.VMEM((2, D, FF_CHUNK), F32),
            pltpu.VMEM((2, FF_CHUNK, D), F32),
            pltpu.VMEM((tm, D), BF),
            pltpu.VMEM((tm, D), F32),
            pltpu.VMEM((FFN_SUBTILE, D_FF), BF),
            pltpu.SemaphoreType.DMA((3, 2)),
        ],
        compiler_params=pltpu.CompilerParams(
            dimension_semantics=("arbitrary",), vmem_limit_bytes=VMEM_LIMIT),
        name=f"ffn_l{l}_s{sub}",
    )(*xs, mods, npre, npost, wi, wo)
    return tuple(out) if split_out else out[0]


def _proj_kernel(x_ref, mod_ref, npre_ref, w_ref, *rest, n_carried):
    qkv_ref, f_ref, g_ref, k_ref, v_ref, wb_ref = rest[n_carried:]

    @pl.when(pl.program_id(0) == 0)
    def _():
        wb_ref[...] = w_ref[...].astype(BF)

    x = x_ref[...]
    shift = mod_ref[3:4, :]
    scale = mod_ref[4:5, :]
    h = (_rms(x, npre_ref[1:2, :]) * (1.0 + scale) + shift).astype(BF)
    z = _dot(h, wb_ref[...])
    qkv_ref[...] = z[:, :3 * WA].astype(BF)
    f_ref[...] = z[:, 3 * WA:3 * WA + WB].astype(BF)
    g_ref[...] = z[:, 3 * WA + WB:].astype(BF)
    k_ref[...] = z[:, WA:2 * WA].reshape(k_ref.shape)
    v_ref[...] = z[:, 2 * WA:3 * WA].reshape(v_ref.shape)


def _proj(x, mods, npre, w_in, carried, l, tm=1024):
    n_tiles = N_TOK // tm
    nl = N_LAT // tm
    bpt = tm // CTX_SEQ
    kv_shape = jax.ShapeDtypeStruct((N_CTX_B, DEPTH, CTX_SEQ, WA), F32)
    kv_spec = pl.BlockSpec((bpt, None, CTX_SEQ, WA), lambda t: (_ctx_tile(t, nl), l, 0, 0))
    n_fixed = 4
    return pl.pallas_call(
        functools.partial(_proj_kernel, n_carried=len(carried)),
        grid=(n_tiles,),
        in_specs=[
            pl.BlockSpec((tm, D), lambda t: (t, 0)),
            pl.BlockSpec((None, None, 9, D), lambda t: (l, _mod_row(t, tm), 0, 0)),
            pl.BlockSpec((None, 3, D), lambda t: (l, 0, 0)),
            pl.BlockSpec((None, D, PROJ_W), lambda t: (l, 0, 0), pipeline_mode=pl.Buffered(1)),
        ] + [pl.BlockSpec(memory_space=pl.ANY)] * len(carried),
        out_specs=[
            pl.BlockSpec((tm, 3 * WA), lambda t: (t, 0)),
            pl.BlockSpec((tm, WB), lambda t: (t, 0)),
            pl.BlockSpec((tm, 2 * WC), lambda t: (t, 0)),
            kv_spec,
            kv_spec,
        ],
        out_shape=[
            jax.ShapeDtypeStruct((N_TOK, 3 * WA), BF),
            jax.ShapeDtypeStruct((N_TOK, WB), BF),
            jax.ShapeDtypeStruct((N_TOK, 2 * WC), BF),
            kv_shape,
            kv_shape,
        ],
        scratch_shapes=[pltpu.VMEM((D, PROJ_W), BF)],
        input_output_aliases={n_fixed + i: 3 + i for i in range(len(carried))},
        compiler_params=pltpu.CompilerParams(
            dimension_semantics=("arbitrary",), vmem_limit_bytes=VMEM_LIMIT),
        name=f"proj_l{l}",
    )(x, mods, npre, w_in, *carried)


def _fourier(f, fc_ref, fn_ref):
    n = fn_ref.shape[0]
    xcs = _dot(f, fc_ref[...]).astype(BF)
    outs = []
    for b in range(f.shape[0] // n):
        xb = xcs[b * n:(b + 1) * n]
        stacked = jnp.concatenate([xb[:, :WB], xb[:, WB:]], axis=0)
        outs.append(_dot(fn_ref[...], stacked))
    return outs[0] if len(outs) == 1 else jnp.concatenate(outs, axis=0)


def _gmlp(g, gn_ref, ws_ref, gb_ref, avg_ref):
    gl = jax.nn.gelu(g.astype(F32), approximate=True)
    u = gl[:, :WC]
    v = gl[:, WC:]
    avg = avg_ref[...]
    mu = _split2_dot(v, avg)
    d = v - mu
    var = _dot((d * d).astype(BF), avg)
    vn = (d * lax.rsqrt(var + EPS) * gn_ref[...]).astype(BF)
    lane_group = lax.broadcasted_iota(jnp.int32, (CHUNK, WC), 1) // GC
    outs = []
    for c in range(g.shape[0] // CHUNK):
        vc = vn[c * CHUNK:(c + 1) * CHUNK, :]
        sp = _dot(ws_ref[0].astype(BF), vc)
        for grp in range(1, NGRP):
            sp = jnp.where(lane_group == grp, _dot(ws_ref[grp].astype(BF), vc), sp)
        outs.append(u[c * CHUNK:(c + 1) * CHUNK, :] * (sp + gb_ref[...]))
    return jnp.concatenate(outs, axis=0)


def _cast_weight_once(w_ref, wb_ref):
    @pl.when(pl.program_id(0) == 0)
    def _():
        wb_ref[...] = w_ref[...].astype(BF)


def _mix_out(x, o, gate, npost, wb_ref):
    y = _dot(o, wb_ref[...])
    return x + gate * _rms(y, npost)


def _ctx_mixer_kernel(x_ref, qkv_ref, f_ref, g_ref, mod_ref, npost_ref, wo_ref, fc_ref, fn_ref,
                      gn_ref, ws_ref, gb_ref, avg_ref, o_ref, mix_ref, wb_ref):
    _cast_weight_once(wo_ref, wb_ref)
    gw = HEADS_PER_GROUP * HD
    pair_w = 2 * HD
    first_of_pair = lax.broadcasted_iota(jnp.int32, (CTX_SEQ, pair_w), 1) < HD
    own = (lax.broadcasted_iota(jnp.int32, (HEADS_PER_GROUP * CTX_SEQ, gw), 0) // CTX_SEQ
           == lax.broadcasted_iota(jnp.int32, (HEADS_PER_GROUP * CTX_SEQ, gw), 1) // HD)
    lane_head = lax.broadcasted_iota(jnp.int32, (CTX_SEQ, gw), 1) // HD
    zero = jnp.zeros((), BF)
    for b in range(x_ref.shape[0] // CTX_SEQ):
        rows = slice(b * CTX_SEQ, (b + 1) * CTX_SEQ)
        scores = []
        for h in range(NH):
            lanes = slice((h // 2) * pair_w, (h // 2 + 1) * pair_w)
            q2 = qkv_ref[rows, lanes] * QK_SCALE
            k2 = qkv_ref[rows, WA + (h // 2) * pair_w:WA + (h // 2 + 1) * pair_w]
            qh = jnp.where(first_of_pair, q2, zero) if h % 2 == 0 else jnp.where(first_of_pair, zero, q2)
            scores.append(_dot_nt(qh, k2))
        probs, dens = [], []
        for s in scores:
            p = jnp.exp(s - jnp.max(s, axis=-1, keepdims=True))
            dens.append(jnp.sum(p, axis=-1, keepdims=True))
            probs.append(p.astype(BF))
        for grp in range(N_HEAD_GROUPS):
            hs = range(grp * HEADS_PER_GROUP, (grp + 1) * HEADS_PER_GROUP)
            v4 = qkv_ref[rows, 2 * WA + grp * gw:2 * WA + (grp + 1) * gw]
            vbd = jnp.where(own, jnp.concatenate([v4] * HEADS_PER_GROUP, axis=0), zero)
            o = _dot(jnp.concatenate([probs[h] for h in hs], axis=1), vbd)
            den = dens[hs[0]]
            for i in range(1, HEADS_PER_GROUP):
                den = jnp.where(lane_head == i, dens[hs[i]], den)
            mix_ref[rows, grp * gw:(grp + 1) * gw] = (o / den).astype(BF)
    mix_ref[:, WA:WA + WB] = _fourier(f_ref[...], fc_ref, fn_ref).astype(BF)
    mix_ref[:, WA + WB:] = _gmlp(g_ref[...], gn_ref, ws_ref, gb_ref, avg_ref).astype(BF)
    o_ref[...] = _mix_out(x_ref[...], mix_ref[...], mod_ref[5:6, :], npost_ref[1:2, :], wb_ref)


def _const_spec(shape):
    nd = len(shape)
    return pl.BlockSpec(shape, lambda *_: (0,) * nd)


def _ctx_mixer(x, qkv, f, g, mods, npost, w_out, fc, fn, gn, ws, gb, avg, l):
    tm = CTX_PER_STEP * CTX_SEQ
    off = N_LAT // tm
    row = lambda b: (b + off, 0)
    return pl.pallas_call(
        _ctx_mixer_kernel,
        grid=(N_CTX_B // CTX_PER_STEP,),
        in_specs=[
            pl.BlockSpec((tm, D), row),
            pl.BlockSpec((tm, 3 * WA), row),
            pl.BlockSpec((tm, WB), row),
            pl.BlockSpec((tm, 2 * WC), row),
            pl.BlockSpec((None, None, 9, D), lambda b: (l, CTX_MOD_ROW, 0, 0)),
            pl.BlockSpec((None, 3, D), lambda b: (l, 0, 0)),
            pl.BlockSpec((None, D, D), lambda b: (l, 0, 0), pipeline_mode=pl.Buffered(1)),
            _const_spec(fc.shape),
            _const_spec(fn.shape),
            pl.BlockSpec((None, 1, WC), lambda b: (l, 0, 0)),
            pl.BlockSpec((None, NGRP, CHUNK, CHUNK), lambda b: (l, 0, 0, 0)),
            pl.BlockSpec((None, CHUNK, WC), lambda b: (l, 0, 0)),
            _const_spec(avg.shape),
        ],
        out_specs=pl.BlockSpec((tm, D), row),
        out_shape=jax.ShapeDtypeStruct((N_TOK, D), F32),
        input_output_aliases={0: 0},
        scratch_shapes=[pltpu.VMEM((tm, D), BF), pltpu.VMEM((D, D), BF)],
        compiler_params=pltpu.CompilerParams(
            dimension_semantics=("arbitrary",), vmem_limit_bytes=VMEM_LIMIT),
        name=f"ctx_mixer_l{l}",
    )(x, qkv, f, g, mods, npost, w_out, fc, fn, gn, ws, gb, avg)


def _window_start(r):
    return min(max(r - WIN_ROWS // 2, 0), GRID_ROWS - WIN_ROWS)


def _nbr_attn_kernel(qkv_ref, ck_ref, cv_ref, bias_ref, o_ref):
    gw = HEADS_PER_GROUP * HD
    row_head = lax.broadcasted_iota(jnp.int32, (HEADS_PER_GROUP * GRID_W, gw), 0) // GRID_W
    lane_head = lax.broadcasted_iota(jnp.int32, (HEADS_PER_GROUP * GRID_W, gw), 1) // HD
    own = row_head == lane_head
    for hg in range(N_HEAD_GROUPS):
        ck = ck_ref[:, hg * gw:(hg + 1) * gw].astype(BF)
        cv = cv_ref[:, hg * gw:(hg + 1) * gw].astype(BF)
        for r in range(GRID_ROWS):
            ws = _window_start(r)
            q = qkv_ref[r * GRID_W:(r + 1) * GRID_W, hg * gw:(hg + 1) * gw] * QK_SCALE
            qs = jnp.where(own, jnp.concatenate([q] * HEADS_PER_GROUP, axis=0), jnp.zeros((), BF))
            kw = qkv_ref[ws * GRID_W:(ws + WIN_ROWS) * GRID_W, WA + hg * gw:WA + (hg + 1) * gw]
            vw = qkv_ref[ws * GRID_W:(ws + WIN_ROWS) * GRID_W, 2 * WA + hg * gw:2 * WA + (hg + 1) * gw]
            s_loc = _dot_nt(qs, kw) + bias_ref[hg, ws - r + N_DELTA - 1].astype(F32)
            s_ctx = _dot_nt(qs, ck)
            m = jnp.maximum(jnp.max(s_loc, axis=-1, keepdims=True),
                            jnp.max(s_ctx, axis=-1, keepdims=True))
            p_loc = jnp.exp(s_loc - m)
            p_ctx = jnp.exp(s_ctx - m)
            den = jnp.sum(p_loc, axis=-1, keepdims=True) + jnp.sum(p_ctx, axis=-1, keepdims=True)
            o = (_dot(p_loc.astype(BF), vw) + _dot(p_ctx.astype(BF), cv)) / den
            o = jnp.where(own, o, 0.0)
            o = (o[0:GRID_W] + o[GRID_W:2 * GRID_W]) + (o[2 * GRID_W:3 * GRID_W] + o[3 * GRID_W:])
            o_ref[r * GRID_W:(r + 1) * GRID_W, hg * gw:(hg + 1) * gw] = o.astype(BF)


def _nbr_attn(qkv, cache_k, cache_v, bias, l):
    return pl.pallas_call(
        _nbr_attn_kernel,
        grid=(N_LAT_B,),
        in_specs=[
            pl.BlockSpec((LAT_SEQ, 3 * WA), lambda b: (b, 0)),
            pl.BlockSpec((None, None, PAST, WA), lambda b: (b, l, 0, 0)),
            pl.BlockSpec((None, None, PAST, WA), lambda b: (b, l, 0, 0)),
            pl.BlockSpec((None,) + bias.shape[1:], lambda b: (l, 0, 0, 0, 0)),
        ],
        out_specs=pl.BlockSpec((LAT_SEQ, WA), lambda b: (b, 0)),
        out_shape=jax.ShapeDtypeStruct((N_LAT, WA), BF),
        compiler_params=pltpu.CompilerParams(
            dimension_semantics=("arbitrary",), vmem_limit_bytes=VMEM_LIMIT),
        name=f"nbr_attn_l{l}",
    )(qkv, cache_k, cache_v, bias)


def _lat_mixer_kernel(x_ref, oa_ref, f_ref, g_ref, mod_ref, npost_ref, wo_ref, fc_ref, fn_ref,
                      gn_ref, ws_ref, gb_ref, avg_ref, o_ref, mix_ref, wb_ref):
    _cast_weight_once(wo_ref, wb_ref)
    mix_ref[:, :WA] = oa_ref[...]
    mix_ref[:, WA:WA + WB] = _fourier(f_ref[...], fc_ref, fn_ref).astype(BF)
    mix_ref[:, WA + WB:] = _gmlp(g_ref[...], gn_ref, ws_ref, gb_ref, avg_ref).astype(BF)
    o_ref[...] = _mix_out(x_ref[...], mix_ref[...], mod_ref[5:6, :], npost_ref[1:2, :], wb_ref)


def _lat_mixer(x, oa, f, g, mods, npost, w_out, fc, fn, gn, ws, gb, avg, l):
    row = lambda b: (b, 0)
    return pl.pallas_call(
        _lat_mixer_kernel,
        grid=(N_LAT_B,),
        in_specs=[
            pl.BlockSpec((LAT_SEQ, D), row),
            pl.BlockSpec((LAT_SEQ, WA), row),
            pl.BlockSpec((LAT_SEQ, WB), row),
            pl.BlockSpec((LAT_SEQ, 2 * WC), row),
            pl.BlockSpec((None, None, 9, D), lambda b: (l, b, 0, 0)),
            pl.BlockSpec((None, 3, D), lambda b: (l, 0, 0)),
            pl.BlockSpec((None, D, D), lambda b: (l, 0, 0), pipeline_mode=pl.Buffered(1)),
            _const_spec(fc.shape),
            pl.BlockSpec(fn.shape, lambda b: (0, 0), pipeline_mode=pl.Buffered(1)),
            pl.BlockSpec((None, 1, WC), lambda b: (l, 0, 0)),
            pl.BlockSpec((None, NGRP, CHUNK, CHUNK), lambda b: (l, 0, 0, 0)),
            pl.BlockSpec((None, CHUNK, WC), lambda b: (l, 0, 0)),
            _const_spec(avg.shape),
        ],
        out_specs=pl.BlockSpec((LAT_SEQ, D), row),
        out_shape=jax.ShapeDtypeStruct((N_TOK, D), F32),
        input_output_aliases={0: 0},
        scratch_shapes=[pltpu.VMEM((LAT_SEQ, D), BF), pltpu.VMEM((D, D), BF)],
        compiler_params=pltpu.CompilerParams(
            dimension_semantics=("arbitrary",), vmem_limit_bytes=VMEM_LIMIT),
        name=f"lat_mixer_l{l}",
    )(x, oa, f, g, mods, npost, w_out, fc, fn, gn, ws, gb, avg)


def _dft_tables(n):
    idx = np.arange(n)
    ang = 2.0 * np.pi * ((idx[:, None] * idx[None, :]) % n) / n
    return (np.concatenate([np.cos(ang), -np.sin(ang)], axis=1) / np.sqrt(n)).astype(np.float32)


def _channel_dft_table():
    idx = np.arange(GC)
    ang = 2.0 * np.pi * ((idx[:, None] * idx[None, :]) % GC) / GC
    eye = np.eye(NGRP)
    return (np.concatenate([np.kron(eye, np.cos(ang)), np.kron(eye, np.sin(ang))], axis=1)
            / np.sqrt(GC)).astype(np.float32)


def _group_average_table():
    return np.kron(np.eye(NGRP), np.full((GC, GC), 1.0 / GC)).astype(np.float32)


def kernel(x_prompt, x_sample, cache_k, cache_v, c, c_ctx, ada_w, ada_b, norm_pre, norm_post,
           ffn_w_in, ffn_w_out, w_in, w_out, rpb, gmlp_norm, gmlp_w, gmlp_b):
    xs = (x_sample.reshape(N_LAT, D), x_prompt.reshape(N_CTX, D))
    cs = jnp.concatenate([c, c_ctx[None, :], jnp.zeros((MOD_ROWS - N_LAT_B - 1, D), F32)], axis=0)
    mods = _ada_mods(cs, ada_w, ada_b)
    bias = _bias_table(rpb)

    gn = gmlp_norm.reshape(DEPTH, 1, WC)
    gb = jnp.repeat(jnp.transpose(gmlp_b, (0, 2, 1)), GC, axis=2)
    ck = cache_k.reshape(N_LAT_B, DEPTH, PAST, WA)
    cv = cache_v.reshape(N_LAT_B, DEPTH, PAST, WA)

    fc = jnp.asarray(_channel_dft_table()).astype(BF)
    fn_ctx = jnp.asarray(_dft_tables(CTX_SEQ)).astype(BF)
    fn_lat = jnp.asarray(_dft_tables(LAT_SEQ)).astype(BF)
    avg = jnp.asarray(_group_average_table()).astype(BF)

    carried = (jnp.zeros((N_CTX_B, DEPTH, CTX_SEQ, WA), F32),) * 2
    for l in range(DEPTH):
        x = _ffn(xs, mods, norm_pre, norm_post, ffn_w_in, ffn_w_out, l, 0)
        qkv, f, g, new_k, new_v = _proj(x, mods, norm_pre, w_in, carried, l)
        carried = (new_k, new_v)
        oa = _nbr_attn(qkv, ck, cv, bias, l)
        x = _ctx_mixer(x, qkv, f, g, mods, norm_post, w_out, fc, fn_ctx, gn, gmlp_w, gb, avg, l)
        x = _lat_mixer(x, oa, f, g, mods, norm_post, w_out, fc, fn_lat, gn, gmlp_w, gb, avg, l)
        xs = _ffn((x,), mods, norm_pre, norm_post, ffn_w_in, ffn_w_out, l, 1,
                  split_out=(l == DEPTH - 1))
        if l < DEPTH - 1:
            xs = (xs,)
    xl, xc = xs

    kv_out = (N_CTX_B, DEPTH, CTX_SEQ, NH, HD)
    return (xc.reshape(N_CTX_B, CTX_SEQ, D), xl.reshape(N_LAT_B, LAT_SEQ, D),
            new_k.reshape(kv_out), new_v.reshape(kv_out))
```

```python
import functools

import numpy as np
import jax
import jax.numpy as jnp
from jax import lax
from jax.experimental import pallas as pl
from jax.experimental.pallas import tpu as pltpu

D = 1024
DEPTH = 4
N_CTX_B = 16
CTX_SEQ = 256
N_LAT_B = 4
LAT_SEQ = 1024
N_LAT = N_LAT_B * LAT_SEQ
N_CTX = N_CTX_B * CTX_SEQ
N_TOK = N_LAT + N_CTX
PAST = 256
GRID_W = 64
GRID_ROWS = LAT_SEQ // GRID_W
HD = 64
NH = 8
WA = 512
WB = 256
WC = 256
NGRP = 4
GC = 64
CHUNK = 128
PROJ_W = 3 * WA + WB + 2 * WC
D_FF = 2816
FF_CHUNK = 256
N_FF_CHUNKS = D_FF // FF_CHUNK
FFN_TILE = 1024
FFN_SUBTILE = 512
WIN_ROWS = 8
WIN_COLS = 16
N_ROW_OFF = 2 * WIN_ROWS - 1
N_COL_OFF = 2 * WIN_COLS - 1
N_DELTA = 8
EPS = 1e-6
NEG = -1e30
ADA_TILE = 2304
MOD_ROWS = 8
CTX_MOD_ROW = 4
CTX_PER_STEP = 2
HEADS_PER_GROUP = 4
N_HEAD_GROUPS = NH // HEADS_PER_GROUP

BF = jnp.bfloat16
F32 = jnp.float32
QK_SCALE = HD ** -0.5

VMEM_LIMIT = 56 * 1024 * 1024


def _dot(a, b):
    return jnp.dot(a, b, preferred_element_type=F32)


def _dot_nt(a, b):
    return lax.dot_general(a, b, (((1,), (1,)), ((), ())), preferred_element_type=F32)


def _split2_dot(a, b):
    a0 = a.astype(BF)
    a1 = (a - a0.astype(F32)).astype(BF)
    return _dot(a0, b) + _dot(a1, b)


def _rms(x, g):
    return x * lax.rsqrt(jnp.mean(x * x, axis=-1, keepdims=True) + EPS) * g


def _mod_row(tile, tm):
    return jnp.minimum((tile * tm) // LAT_SEQ, CTX_MOD_ROW)


def _ada_kernel(cs_ref, w_ref, b_ref, o_ref):
    cs = cs_ref[...]
    s = cs * jax.nn.sigmoid(cs)
    o_ref[...] = _dot(s.astype(BF), w_ref[...].astype(BF)) + b_ref[...]


def _ada_mods(cs, ada_w, ada_b):
    n_out = 9 * D
    tn = ADA_TILE
    out = pl.pallas_call(
        _ada_kernel,
        grid=(DEPTH, n_out // tn),
        in_specs=[
            pl.BlockSpec((MOD_ROWS, D), lambda l, n: (0, 0)),
            pl.BlockSpec((None, D, tn), lambda l, n: (l, 0, n)),
            pl.BlockSpec((None, 1, tn), lambda l, n: (l, 0, n)),
        ],
        out_specs=pl.BlockSpec((None, MOD_ROWS, tn), lambda l, n: (l, 0, n)),
        out_shape=jax.ShapeDtypeStruct((DEPTH, MOD_ROWS, n_out), F32),
        compiler_params=pltpu.CompilerParams(
            dimension_semantics=("arbitrary", "arbitrary"), vmem_limit_bytes=VMEM_LIMIT),
        name="ada_mods",
    )(cs, ada_w, ada_b.reshape(DEPTH, 1, n_out))
    return out.reshape(DEPTH, MOD_ROWS, 9, D)


RPB_LANES = 128
RPB_HALF = RPB_LANES // 2


def _bias_kernel(rpb_ref, o_ref):
    qq = lax.broadcasted_iota(jnp.int32, (GRID_W, RPB_LANES), 0)
    lane = lax.broadcasted_iota(jnp.int32, (GRID_W, RPB_LANES), 1)
    second = lane >= GRID_W
    kk = jnp.where(second, lane - GRID_W, lane)
    col_start = jnp.clip(qq - WIN_COLS // 2, 0, GRID_W - WIN_COLS)
    visible = (kk >= col_start) & (kk < col_start + WIN_COLS)
    m = lax.broadcasted_iota(jnp.int32, (N_ROW_OFF, RPB_LANES), 1)
    diff = jnp.where(m < RPB_HALF, m, m - RPB_LANES)
    ext_idx = jnp.clip(diff + (WIN_COLS - 1), 0, N_COL_OFF - 1)
    for i in range(HEADS_PER_GROUP):
        ext = jnp.take_along_axis(rpb_ref[i], ext_idx, axis=1)
        for ro in range(N_ROW_OFF - 1):
            e0 = jnp.broadcast_to(ext[ro:ro + 1, :], (GRID_W, RPB_LANES))
            e1 = jnp.broadcast_to(ext[ro + 1:ro + 2, :], (GRID_W, RPB_LANES))
            t = jnp.where(second,
                          pltpu.roll(e1, GRID_W, 1, stride=1, stride_axis=0),
                          pltpu.roll(e0, 0, 1, stride=1, stride_axis=0))
            t = jnp.where(visible, t, NEG).astype(o_ref.dtype)
            for a in range(0, WIN_ROWS, 2):
                delta_idx = ro - a
                if 0 <= delta_idx < N_DELTA:
                    o_ref[delta_idx, i * GRID_W:(i + 1) * GRID_W, a * GRID_W:(a + 2) * GRID_W] = t


def _bias_table(rpb):
    rpb_padded = jnp.pad(rpb, ((0, 0), (0, 0), (0, 0), (0, RPB_LANES - N_COL_OFF)))
    return pl.pallas_call(
        _bias_kernel,
        grid=(DEPTH, N_HEAD_GROUPS),
        in_specs=[pl.BlockSpec((None, HEADS_PER_GROUP, N_ROW_OFF, RPB_LANES),
                               lambda l, g: (l, g, 0, 0))],
        out_specs=pl.BlockSpec((None, None, N_DELTA, HEADS_PER_GROUP * GRID_W, WIN_ROWS * GRID_W),
                               lambda l, g: (l, g, 0, 0, 0)),
        out_shape=jax.ShapeDtypeStruct(
            (DEPTH, N_HEAD_GROUPS, N_DELTA, HEADS_PER_GROUP * GRID_W, WIN_ROWS * GRID_W), BF),
        compiler_params=pltpu.CompilerParams(
            dimension_semantics=("arbitrary", "arbitrary"), vmem_limit_bytes=VMEM_LIMIT),
        name="rpb_table",
    )(rpb_padded)


def _lat_tile(t, n_lat_tiles):
    return jnp.minimum(t, n_lat_tiles - 1)


def _ctx_tile(t, n_lat_tiles):
    return jnp.maximum(t - n_lat_tiles, 0)


def _ffn_kernel(*refs, l, sub, split_in, split_out, n_lat_tiles, n_cast, n_zero):
    refs = list(refs)
    x_refs = [refs.pop(0) for _ in range(2 if split_in else 1)]
    mod_ref, npre_ref, npost_ref, wi_hbm, wo_hbm = [refs.pop(0) for _ in range(5)]
    cast_in_refs = [refs.pop(0) for _ in range(n_cast)]
    o_refs = [refs.pop(0) for _ in range(2 if split_out else 1)]
    cast_out_refs = [refs.pop(0) for _ in range(n_cast)]
    zero_refs = [refs.pop(0) for _ in range(n_zero)]
    wg_ref, wu_ref, wo_ref, sg_ref, su_ref, so_ref, h_ref, acc_ref, act_ref, sem = refs

    mo = 0 if sub == 0 else 6
    ni = 0 if sub == 0 else 2
    t = pl.program_id(0)
    is_lat = t < n_lat_tiles

    for src, dst in zip(cast_in_refs, cast_out_refs):
        dst[...] = src[...].astype(BF)
    for z in zero_refs:
        z[...] = jnp.zeros(z.shape, z.dtype)

    def load_x(rows=slice(None)):
        if split_in:
            return jnp.where(is_lat, x_refs[0][rows, :], x_refs[1][rows, :])
        return x_refs[0][rows, :]

    def store(out, rows=slice(None)):
        if not split_out:
            o_refs[0][rows, :] = out
            return

        @pl.when(is_lat)
        def _():
            o_refs[0][rows, :] = out

        @pl.when(jnp.logical_not(is_lat))
        def _():
            o_refs[1][rows, :] = out

    def modulated(x):
        shift = mod_ref[mo:mo + 1, :]
        scale = mod_ref[mo + 1:mo + 2, :]
        return (_rms(x, npre_ref[ni:ni + 1, :]) * (1.0 + scale) + shift).astype(BF)

    def residual(x, acc):
        gate = mod_ref[mo + 2:mo + 3, :]
        return x + 0.5 * gate * _rms(acc, npost_ref[ni:ni + 1, :])

    def chunk_update(h, wg, wu, wo):
        g = _dot(h, wg)
        u = _dot(h, wu)
        a = (g * jax.nn.sigmoid(g) * u).astype(BF)
        return _dot(a, wo)

    def chunk_copies(j, slot):
        cols = pl.ds(pl.multiple_of(j * FF_CHUNK, FF_CHUNK), FF_CHUNK)
        up = pl.ds(pl.multiple_of(D_FF + j * FF_CHUNK, FF_CHUNK), FF_CHUNK)
        return (
            pltpu.make_async_copy(wi_hbm.at[l, sub, :, cols], sg_ref.at[slot], sem.at[0, slot]),
            pltpu.make_async_copy(wi_hbm.at[l, sub, :, up], su_ref.at[slot], sem.at[1, slot]),
            pltpu.make_async_copy(wo_hbm.at[l, sub, cols, :], so_ref.at[slot], sem.at[2, slot]),
        )

    @pl.when(t == 0)
    def _():
        for cp in chunk_copies(0, 0):
            cp.start()
        h_ref[...] = modulated(load_x())
        acc_ref[...] = jnp.zeros(acc_ref.shape, F32)

        def body(j, carry):
            slot = lax.rem(j, 2)

            @pl.when(j + 1 < N_FF_CHUNKS)
            def _():
                for cp in chunk_copies(j + 1, 1 - slot):
                    cp.start()

            for cp in chunk_copies(j, slot):
                cp.wait()
            wg = sg_ref[slot].astype(BF)
            wu = su_ref[slot].astype(BF)
            wo = so_ref[slot].astype(BF)
            wg_ref[j] = wg
            wu_ref[j] = wu
            wo_ref[pl.ds(pl.multiple_of(j * FF_CHUNK, FF_CHUNK), FF_CHUNK), :] = wo
            acc_ref[...] += chunk_update(h_ref[...], wg, wu, wo)
            return carry

        lax.fori_loop(0, N_FF_CHUNKS, body, 0)
        store(residual(load_x(), acc_ref[...]))

    @pl.when(t > 0)
    def _():
        def sub_tile(s, carry):
            rows = pl.ds(pl.multiple_of(s * FFN_SUBTILE, FFN_SUBTILE), FFN_SUBTILE)
            x = load_x(rows)
            h = modulated(x)
            for j in range(N_FF_CHUNKS):
                g = _dot(h, wg_ref[j])
                u = _dot(h, wu_ref[j])
                act_ref[:, j * FF_CHUNK:(j + 1) * FF_CHUNK] = (g * jax.nn.sigmoid(g) * u).astype(BF)
            acc = _dot(act_ref[...], wo_ref[...])
            store(residual(x, acc), rows)
            return carry

        n_sub = acc_ref.shape[0] // FFN_SUBTILE
        if n_sub == 1:
            sub_tile(0, 0)
        else:
            lax.fori_loop(0, n_sub, sub_tile, 0)


def _ffn(xs, mods, npre, npost, wi, wo, l, sub, split_out=False, cast_weights=(), zero_shapes=()):
    split_in = len(xs) == 2
    tm = FFN_SUBTILE if (split_in or split_out) else FFN_TILE
    n_tiles = N_TOK // tm
    nl = N_LAT // tm
    lat_idx = lambda t: (_lat_tile(t, nl), 0)
    ctx_idx = lambda t: (_ctx_tile(t, nl), 0)
    split_specs = [pl.BlockSpec((tm, D), lat_idx), pl.BlockSpec((tm, D), ctx_idx)]
    joint_specs = [pl.BlockSpec((tm, D), lambda t: (t, 0))]
    split_shapes = [jax.ShapeDtypeStruct((N_LAT, D), F32), jax.ShapeDtypeStruct((N_CTX, D), F32)]
    joint_shapes = [jax.ShapeDtypeStruct((N_TOK, D), F32)]
    cast_in_specs = [pl.BlockSpec((None, w.shape[1] // n_tiles, w.shape[2]), lambda t: (l, t, 0))
                     for w in cast_weights]
    cast_out_specs = [pl.BlockSpec((w.shape[1] // n_tiles, w.shape[2]), lambda t: (t, 0))
                      for w in cast_weights]
    cast_out_shapes = [jax.ShapeDtypeStruct(w.shape[1:], BF) for w in cast_weights]
    zero_specs = [pl.BlockSpec((s[0] // n_tiles,) + tuple(s[1:]),
                               lambda t, nd=len(s): (t,) + (0,) * (nd - 1),
                               pipeline_mode=pl.Buffered(1)) for s in zero_shapes]
    zero_out_shapes = [jax.ShapeDtypeStruct(tuple(s), F32) for s in zero_shapes]
    n_act = 2 if split_out else 1
    out = pl.pallas_call(
        functools.partial(_ffn_kernel, l=l, sub=sub, split_in=split_in, split_out=split_out,
                          n_lat_tiles=nl, n_cast=len(cast_weights), n_zero=len(zero_shapes)),
        grid=(n_tiles,),
        in_specs=(split_specs if split_in else joint_specs) + [
            pl.BlockSpec((None, None, 9, D), lambda t: (l, _mod_row(t, tm), 0, 0)),
            pl.BlockSpec((None, 3, D), lambda t: (l, 0, 0)),
            pl.BlockSpec((None, 3, D), lambda t: (l, 0, 0)),
            pl.BlockSpec(memory_space=pl.ANY),
            pl.BlockSpec(memory_space=pl.ANY),
        ] + cast_in_specs,
        out_specs=(split_specs if split_out else joint_specs) + cast_out_specs + zero_specs,
        out_shape=(split_shapes if split_out else joint_shapes) + cast_out_shapes + zero_out_shapes,
        scratch_shapes=[
            pltpu.VMEM((N_FF_CHUNKS, D, FF_CHUNK), BF),
            pltpu.VMEM((N_FF_CHUNKS, D, FF_CHUNK), BF),
            pltpu.VMEM((D_FF, D), BF),
            pltpu.VMEM((2, D, FF_CHUNK), F32),
            pltpu.VMEM((2, D, FF_CHUNK), F32),
            pltpu.VMEM((2, FF_CHUNK, D), F32),
            pltpu.VMEM((tm, D), BF),
            pltpu.VMEM((tm, D), F32),
            pltpu.VMEM((FFN_SUBTILE, D_FF), BF),
            pltpu.SemaphoreType.DMA((3, 2)),
        ],
        compiler_params=pltpu.CompilerParams(
            dimension_semantics=("arbitrary",), vmem_limit_bytes=VMEM_LIMIT),
        name=f"ffn_l{l}_s{sub}",
    )(*xs, mods, npre, npost, wi, wo, *cast_weights)
    acts = tuple(out[:n_act]) if split_out else out[0]
    n_cast = len(cast_weights)
    return acts, tuple(out[n_act:n_act + n_cast]), tuple(out[n_act + n_cast:])


def _proj_kernel(x_ref, mod_ref, npre_ref, w_ref, *rest, n_carried):
    qkv_ref, f_ref, g_ref, k_ref, v_ref = rest[n_carried:]
    x = x_ref[...]
    shift = mod_ref[3:4, :]
    scale = mod_ref[4:5, :]
    h = (_rms(x, npre_ref[1:2, :]) * (1.0 + scale) + shift).astype(BF)
    z = _dot(h, w_ref[...])
    qkv_ref[...] = z[:, :3 * WA].astype(BF)
    f_ref[...] = z[:, 3 * WA:3 * WA + WB].astype(BF)
    g_ref[...] = z[:, 3 * WA + WB:].astype(BF)
    k_ref[...] = z[:, WA:2 * WA].reshape(k_ref.shape)
    v_ref[...] = z[:, 2 * WA:3 * WA].reshape(v_ref.shape)


def _proj(x, mods, npre, w_in, carried, l, tm=1024):
    n_tiles = N_TOK // tm
    nl = N_LAT // tm
    bpt = tm // CTX_SEQ
    kv_shape = jax.ShapeDtypeStruct((N_CTX_B, DEPTH, CTX_SEQ, WA), F32)
    kv_spec = pl.BlockSpec((bpt, None, CTX_SEQ, WA), lambda t: (_ctx_tile(t, nl), l, 0, 0))
    n_fixed = 4
    return pl.pallas_call(
        functools.partial(_proj_kernel, n_carried=len(carried)),
        grid=(n_tiles,),
        in_specs=[
            pl.BlockSpec((tm, D), lambda t: (t, 0)),
            pl.BlockSpec((None, None, 9, D), lambda t: (l, _mod_row(t, tm), 0, 0)),
            pl.BlockSpec((None, 3, D), lambda t: (l, 0, 0)),
            pl.BlockSpec((D, PROJ_W), lambda t: (0, 0), pipeline_mode=pl.Buffered(1)),
        ] + [pl.BlockSpec(memory_space=pl.ANY)] * len(carried),
        out_specs=[
            pl.BlockSpec((tm, 3 * WA), lambda t: (t, 0)),
            pl.BlockSpec((tm, WB), lambda t: (t, 0)),
            pl.BlockSpec((tm, 2 * WC), lambda t: (t, 0)),
            kv_spec,
            kv_spec,
        ],
        out_shape=[
            jax.ShapeDtypeStruct((N_TOK, 3 * WA), BF),
            jax.ShapeDtypeStruct((N_TOK, WB), BF),
            jax.ShapeDtypeStruct((N_TOK, 2 * WC), BF),
            kv_shape,
            kv_shape,
        ],
        input_output_aliases={n_fixed + i: 3 + i for i in range(len(carried))},
        compiler_params=pltpu.CompilerParams(
            dimension_semantics=("arbitrary",), vmem_limit_bytes=VMEM_LIMIT),
        name=f"proj_l{l}",
    )(x, mods, npre, w_in, *carried)


def _fourier(f, fc_ref, fn_ref):
    n = fn_ref.shape[0]
    xcs = _dot(f, fc_ref[...]).astype(BF)
    outs = []
    for b in range(f.shape[0] // n):
        xb = xcs[b * n:(b + 1) * n]
        stacked = jnp.concatenate([xb[:, :WB], xb[:, WB:]], axis=0)
        outs.append(_dot(fn_ref[...], stacked))
    return outs[0] if len(outs) == 1 else jnp.concatenate(outs, axis=0)


def _gmlp(g, gn_ref, ws_ref, gb_ref, avg_ref):
    gl = jax.nn.gelu(g.astype(F32), approximate=True)
    u = gl[:, :WC]
    v = gl[:, WC:]
    avg = avg_ref[...]
    mu = _split2_dot(v, avg)
    d = v - mu
    var = _dot((d * d).astype(BF), avg)
    vn = (d * lax.rsqrt(var + EPS) * gn_ref[...]).astype(BF)
    lane_group = lax.broadcasted_iota(jnp.int32, (CHUNK, WC), 1) // GC
    outs = []
    for c in range(g.shape[0] // CHUNK):
        vc = vn[c * CHUNK:(c + 1) * CHUNK, :]
        sp = _dot(ws_ref[0].astype(BF), vc)
        for grp in range(1, NGRP):
            sp = jnp.where(lane_group == grp, _dot(ws_ref[grp].astype(BF), vc), sp)
        outs.append(u[c * CHUNK:(c + 1) * CHUNK, :] * (sp + gb_ref[...]))
    return jnp.concatenate(outs, axis=0)


def _mix_out(x, o, gate, npost, wo_ref):
    y = _dot(o, wo_ref[...])
    return x + gate * _rms(y, npost)


def _ctx_mixer_kernel(x_ref, qkv_ref, f_ref, g_ref, mod_ref, npost_ref, wo_ref, fc_ref, fn_ref,
                      gn_ref, ws_ref, gb_ref, avg_ref, o_ref, mix_ref):
    gw = HEADS_PER_GROUP * HD
    pair_w = 2 * HD
    first_of_pair = lax.broadcasted_iota(jnp.int32, (CTX_SEQ, pair_w), 1) < HD
    own = (lax.broadcasted_iota(jnp.int32, (HEADS_PER_GROUP * CTX_SEQ, gw), 0) // CTX_SEQ
           == lax.broadcasted_iota(jnp.int32, (HEADS_PER_GROUP * CTX_SEQ, gw), 1) // HD)
    lane_head = lax.broadcasted_iota(jnp.int32, (CTX_SEQ, gw), 1) // HD
    zero = jnp.zeros((), BF)
    for b in range(x_ref.shape[0] // CTX_SEQ):
        rows = slice(b * CTX_SEQ, (b + 1) * CTX_SEQ)
        scores = []
        for h in range(NH):
            lanes = slice((h // 2) * pair_w, (h // 2 + 1) * pair_w)
            q2 = qkv_ref[rows, lanes] * QK_SCALE
            k2 = qkv_ref[rows, WA + (h // 2) * pair_w:WA + (h // 2 + 1) * pair_w]
            qh = jnp.where(first_of_pair, q2, zero) if h % 2 == 0 else jnp.where(first_of_pair, zero, q2)
            scores.append(_dot_nt(qh, k2))
        probs, dens = [], []
        for s in scores:
            p = jnp.exp(s - jnp.max(s, axis=-1, keepdims=True))
            dens.append(jnp.sum(p, axis=-1, keepdims=True))
            probs.append(p.astype(BF))
        for grp in range(N_HEAD_GROUPS):
            hs = range(grp * HEADS_PER_GROUP, (grp + 1) * HEADS_PER_GROUP)
            v4 = qkv_ref[rows, 2 * WA + grp * gw:2 * WA + (grp + 1) * gw]
            vbd = jnp.where(own, jnp.concatenate([v4] * HEADS_PER_GROUP, axis=0), zero)
            o = _dot(jnp.concatenate([probs[h] for h in hs], axis=1), vbd)
            den = dens[hs[0]]
            for i in range(1, HEADS_PER_GROUP):
                den = jnp.where(lane_head == i, dens[hs[i]], den)
            mix_ref[rows, grp * gw:(grp + 1) * gw] = (o / den).astype(BF)
    mix_ref[:, WA:WA + WB] = _fourier(f_ref[...], fc_ref, fn_ref).astype(BF)
    mix_ref[:, WA + WB:] = _gmlp(g_ref[...], gn_ref, ws_ref, gb_ref, avg_ref).astype(BF)
    o_ref[...] = _mix_out(x_ref[...], mix_ref[...], mod_ref[5:6, :], npost_ref[1:2, :], wo_ref)


def _const_spec(shape):
    nd = len(shape)
    return pl.BlockSpec(shape, lambda *_: (0,) * nd)


def _ctx_mixer(x, qkv, f, g, mods, npost, w_out, fc, fn, gn, ws, gb, avg, l):
    tm = CTX_PER_STEP * CTX_SEQ
    off = N_LAT // tm
    row = lambda b: (b + off, 0)
    return pl.pallas_call(
        _ctx_mixer_kernel,
        grid=(N_CTX_B // CTX_PER_STEP,),
        in_specs=[
            pl.BlockSpec((tm, D), row),
            pl.BlockSpec((tm, 3 * WA), row),
            pl.BlockSpec((tm, WB), row),
            pl.BlockSpec((tm, 2 * WC), row),
            pl.BlockSpec((None, None, 9, D), lambda b: (l, CTX_MOD_ROW, 0, 0)),
            pl.BlockSpec((None, 3, D), lambda b: (l, 0, 0)),
            pl.BlockSpec((D, D), lambda b: (0, 0), pipeline_mode=pl.Buffered(1)),
            _const_spec(fc.shape),
            _const_spec(fn.shape),
            pl.BlockSpec((None, 1, WC), lambda b: (l, 0, 0)),
            pl.BlockSpec((None, NGRP, CHUNK, CHUNK), lambda b: (l, 0, 0, 0)),
            pl.BlockSpec((None, CHUNK, WC), lambda b: (l, 0, 0)),
            _const_spec(avg.shape),
        ],
        out_specs=pl.BlockSpec((tm, D), row),
        out_shape=jax.ShapeDtypeStruct((N_TOK, D), F32),
        input_output_aliases={0: 0},
        scratch_shapes=[pltpu.VMEM((tm, D), BF)],
        compiler_params=pltpu.CompilerParams(
            dimension_semantics=("arbitrary",), vmem_limit_bytes=VMEM_LIMIT),
        name=f"ctx_mixer_l{l}",
    )(x, qkv, f, g, mods, npost, w_out, fc, fn, gn, ws, gb, avg)


def _window_start(r):
    return min(max(r - WIN_ROWS // 2, 0), GRID_ROWS - WIN_ROWS)


def _nbr_attn_kernel(qkv_ref, ck_ref, cv_ref, bias_ref, o_ref):
    gw = HEADS_PER_GROUP * HD
    row_head = lax.broadcasted_iota(jnp.int32, (HEADS_PER_GROUP * GRID_W, gw), 0) // GRID_W
    lane_head = lax.broadcasted_iota(jnp.int32, (HEADS_PER_GROUP * GRID_W, gw), 1) // HD
    own = row_head == lane_head
    for hg in range(N_HEAD_GROUPS):
        ck = ck_ref[:, hg * gw:(hg + 1) * gw].astype(BF)
        cv = cv_ref[:, hg * gw:(hg + 1) * gw].astype(BF)
        for r in range(GRID_ROWS):
            ws = _window_start(r)
            q = qkv_ref[r * GRID_W:(r + 1) * GRID_W, hg * gw:(hg + 1) * gw] * QK_SCALE
            qs = jnp.where(own, jnp.concatenate([q] * HEADS_PER_GROUP, axis=0), jnp.zeros((), BF))
            kw = qkv_ref[ws * GRID_W:(ws + WIN_ROWS) * GRID_W, WA + hg * gw:WA + (hg + 1) * gw]
            vw = qkv_ref[ws * GRID_W:(ws + WIN_ROWS) * GRID_W, 2 * WA + hg * gw:2 * WA + (hg + 1) * gw]
            s_loc = _dot_nt(qs, kw) + bias_ref[hg, ws - r + N_DELTA - 1].astype(F32)
            s_ctx = _dot_nt(qs, ck)
            m = jnp.maximum(jnp.max(s_loc, axis=-1, keepdims=True),
                            jnp.max(s_ctx, axis=-1, keepdims=True))
            p_loc = jnp.exp(s_loc - m)
            p_ctx = jnp.exp(s_ctx - m)
            den = jnp.sum(p_loc, axis=-1, keepdims=True) + jnp.sum(p_ctx, axis=-1, keepdims=True)
            o = (_dot(p_loc.astype(BF), vw) + _dot(p_ctx.astype(BF), cv)) / den
            o = jnp.where(own, o, 0.0)
            o = (o[0:GRID_W] + o[GRID_W:2 * GRID_W]) + (o[2 * GRID_W:3 * GRID_W] + o[3 * GRID_W:])
            o_ref[r * GRID_W:(r + 1) * GRID_W, hg * gw:(hg + 1) * gw] = o.astype(BF)


def _nbr_attn(qkv, cache_k, cache_v, bias, l):
    return pl.pallas_call(
        _nbr_attn_kernel,
        grid=(N_LAT_B,),
        in_specs=[
            pl.BlockSpec((LAT_SEQ, 3 * WA), lambda b: (b, 0)),
            pl.BlockSpec((None, None, PAST, WA), lambda b: (b, l, 0, 0)),
            pl.BlockSpec((None, None, PAST, WA), lambda b: (b, l, 0, 0)),
            pl.BlockSpec((None,) + bias.shape[1:], lambda b: (l, 0, 0, 0, 0)),
        ],
        out_specs=pl.BlockSpec((LAT_SEQ, WA), lambda b: (b, 0)),
        out_shape=jax.ShapeDtypeStruct((N_LAT, WA), BF),
        compiler_params=pltpu.CompilerParams(
            dimension_semantics=("arbitrary",), vmem_limit_bytes=VMEM_LIMIT),
        name=f"nbr_attn_l{l}",
    )(qkv, cache_k, cache_v, bias)


def _lat_mixer_kernel(x_ref, oa_ref, f_ref, g_ref, mod_ref, npost_ref, wo_ref, fc_ref, fn_ref,
                      gn_ref, ws_ref, gb_ref, avg_ref, o_ref, mix_ref):
    mix_ref[:, :WA] = oa_ref[...]
    mix_ref[:, WA:WA + WB] = _fourier(f_ref[...], fc_ref, fn_ref).astype(BF)
    mix_ref[:, WA + WB:] = _gmlp(g_ref[...], gn_ref, ws_ref, gb_ref, avg_ref).astype(BF)
    o_ref[...] = _mix_out(x_ref[...], mix_ref[...], mod_ref[5:6, :], npost_ref[1:2, :], wo_ref)


def _lat_mixer(x, oa, f, g, mods, npost, w_out, fc, fn, gn, ws, gb, avg, l):
    row = lambda b: (b, 0)
    return pl.pallas_call(
        _lat_mixer_kernel,
        grid=(N_LAT_B,),
        in_specs=[
            pl.BlockSpec((LAT_SEQ, D), row),
            pl.BlockSpec((LAT_SEQ, WA), row),
            pl.BlockSpec((LAT_SEQ, WB), row),
            pl.BlockSpec((LAT_SEQ, 2 * WC), row),
            pl.BlockSpec((None, None, 9, D), lambda b: (l, b, 0, 0)),
            pl.BlockSpec((None, 3, D), lambda b: (l, 0, 0)),
            pl.BlockSpec((D, D), lambda b: (0, 0), pipeline_mode=pl.Buffered(1)),
            _const_spec(fc.shape),
            pl.BlockSpec(fn.shape, lambda b: (0, 0), pipeline_mode=pl.Buffered(1)),
            pl.BlockSpec((None, 1, WC), lambda b: (l, 0, 0)),
            pl.BlockSpec((None, NGRP, CHUNK, CHUNK), lambda b: (l, 0, 0, 0)),
            pl.BlockSpec((None, CHUNK, WC), lambda b: (l, 0, 0)),
            _const_spec(avg.shape),
        ],
        out_specs=pl.BlockSpec((LAT_SEQ, D), row),
        out_shape=jax.ShapeDtypeStruct((N_TOK, D), F32),
        input_output_aliases={0: 0},
        scratch_shapes=[pltpu.VMEM((LAT_SEQ, D), BF)],
        compiler_params=pltpu.CompilerParams(
            dimension_semantics=("arbitrary",), vmem_limit_bytes=VMEM_LIMIT),
        name=f"lat_mixer_l{l}",
    )(x, oa, f, g, mods, npost, w_out, fc, fn, gn, ws, gb, avg)


def _dft_tables(n):
    idx = np.arange(n)
    ang = 2.0 * np.pi * ((idx[:, None] * idx[None, :]) % n) / n
    return (np.concatenate([np.cos(ang), -np.sin(ang)], axis=1) / np.sqrt(n)).astype(np.float32)


def _channel_dft_table():
    idx = np.arange(GC)
    ang = 2.0 * np.pi * ((idx[:, None] * idx[None, :]) % GC) / GC
    eye = np.eye(NGRP)
    return (np.concatenate([np.kron(eye, np.cos(ang)), np.kron(eye, np.sin(ang))], axis=1)
            / np.sqrt(GC)).astype(np.float32)


def _group_average_table():
    return np.kron(np.eye(NGRP), np.full((GC, GC), 1.0 / GC)).astype(np.float32)


def kernel(x_prompt, x_sample, cache_k, cache_v, c, c_ctx, ada_w, ada_b, norm_pre, norm_post,
           ffn_w_in, ffn_w_out, w_in, w_out, rpb, gmlp_norm, gmlp_w, gmlp_b):
    xs = (x_sample.reshape(N_LAT, D), x_prompt.reshape(N_CTX, D))
    cs = jnp.concatenate([c, c_ctx[None, :], jnp.zeros((MOD_ROWS - N_LAT_B - 1, D), F32)], axis=0)
    mods = _ada_mods(cs, ada_w, ada_b)
    bias = _bias_table(rpb)

    gn = gmlp_norm.reshape(DEPTH, 1, WC)
    gb = jnp.repeat(jnp.transpose(gmlp_b, (0, 2, 1)), GC, axis=2)
    ck = cache_k.reshape(N_LAT_B, DEPTH, PAST, WA)
    cv = cache_v.reshape(N_LAT_B, DEPTH, PAST, WA)

    fc = jnp.asarray(_channel_dft_table()).astype(BF)
    fn_ctx = jnp.asarray(_dft_tables(CTX_SEQ)).astype(BF)
    fn_lat = jnp.asarray(_dft_tables(LAT_SEQ)).astype(BF)
    avg = jnp.asarray(_group_average_table()).astype(BF)

    kv_carrier = (N_CTX_B, DEPTH, CTX_SEQ, WA)
    carried = None
    for l in range(DEPTH):
        x, (w_in_b, w_out_b), zeros = _ffn(
            xs, mods, norm_pre, norm_post, ffn_w_in, ffn_w_out, l, 0,
            cast_weights=(w_in, w_out), zero_shapes=(kv_carrier,) * 2 if l == 0 else ())
        if l == 0:
            carried = zeros
        qkv, f, g, new_k, new_v = _proj(x, mods, norm_pre, w_in_b, carried, l)
        carried = (new_k, new_v)
        oa = _nbr_attn(qkv, ck, cv, bias, l)
        x = _ctx_mixer(x, qkv, f, g, mods, norm_post, w_out_b, fc, fn_ctx, gn, gmlp_w, gb, avg, l)
        x = _lat_mixer(x, oa, f, g, mods, norm_post, w_out_b, fc, fn_lat, gn, gmlp_w, gb, avg, l)
        xs, _, _ = _ffn((x,), mods, norm_pre, norm_post, ffn_w_in, ffn_w_out, l, 1,
                        split_out=(l == DEPTH - 1))
        if l < DEPTH - 1:
            xs = (xs,)
    xl, xc = xs

    kv_out = (N_CTX_B, DEPTH, CTX_SEQ, NH, HD)
    return (xc.reshape(N_CTX_B, CTX_SEQ, D), xl.reshape(N_LAT_B, LAT_SEQ, D),
            new_k.reshape(kv_out), new_v.reshape(kv_out))
```

```python
import functools

import numpy as np
import jax
import jax.numpy as jnp
from jax import lax
from jax.experimental import pallas as pl
from jax.experimental.pallas import tpu as pltpu

D = 1024
DEPTH = 4
N_CTX_B = 16
CTX_SEQ = 256
N_LAT_B = 4
LAT_SEQ = 1024
N_LAT = N_LAT_B * LAT_SEQ
N_CTX = N_CTX_B * CTX_SEQ
N_TOK = N_LAT + N_CTX
PAST = 256
GRID_W = 64
GRID_ROWS = LAT_SEQ // GRID_W
HD = 64
NH = 8
WA = 512
WB = 256
WC = 256
NGRP = 4
GC = 64
CHUNK = 128
PROJ_W = 3 * WA + WB + 2 * WC
D_FF = 2816
FF_CHUNK = 256
N_FF_CHUNKS = D_FF // FF_CHUNK
FFN_TILE = 1024
FFN_SUBTILE = 512
WIN_ROWS = 8
WIN_COLS = 16
N_ROW_OFF = 2 * WIN_ROWS - 1
N_COL_OFF = 2 * WIN_COLS - 1
N_DELTA = 8
EPS = 1e-6
NEG = -1e30
ADA_TILE = 2304
MOD_ROWS = 8
CTX_MOD_ROW = 4
CTX_PER_STEP = 4
HEADS_PER_GROUP = 4
N_HEAD_GROUPS = NH // HEADS_PER_GROUP

BF = jnp.bfloat16
F32 = jnp.float32
QK_SCALE = HD ** -0.5

VMEM_LIMIT = 56 * 1024 * 1024


def _dot(a, b):
    return jnp.dot(a, b, preferred_element_type=F32)


def _dot_nt(a, b):
    return lax.dot_general(a, b, (((1,), (1,)), ((), ())), preferred_element_type=F32)


def _split2_dot(a, b):
    a0 = a.astype(BF)
    a1 = (a - a0.astype(F32)).astype(BF)
    return _dot(a0, b) + _dot(a1, b)


def _rms(x, g):
    return x * lax.rsqrt(jnp.mean(x * x, axis=-1, keepdims=True) + EPS) * g


def _mod_row(tile, tm):
    return jnp.minimum((tile * tm) // LAT_SEQ, CTX_MOD_ROW)


def _ada_kernel(cs_ref, w_ref, b_ref, o_ref):
    cs = cs_ref[...]
    s = cs * jax.nn.sigmoid(cs)
    o_ref[...] = _dot(s.astype(BF), w_ref[...].astype(BF)) + b_ref[...]


def _ada_mods(cs, ada_w, ada_b):
    n_out = 9 * D
    tn = ADA_TILE
    out = pl.pallas_call(
        _ada_kernel,
        grid=(DEPTH, n_out // tn),
        in_specs=[
            pl.BlockSpec((MOD_ROWS, D), lambda l, n: (0, 0)),
            pl.BlockSpec((None, D, tn), lambda l, n: (l, 0, n)),
            pl.BlockSpec((None, 1, tn), lambda l, n: (l, 0, n)),
        ],
        out_specs=pl.BlockSpec((None, MOD_ROWS, tn), lambda l, n: (l, 0, n)),
        out_shape=jax.ShapeDtypeStruct((DEPTH, MOD_ROWS, n_out), F32),
        compiler_params=pltpu.CompilerParams(
            dimension_semantics=("arbitrary", "arbitrary"), vmem_limit_bytes=VMEM_LIMIT),
        name="ada_mods",
    )(cs, ada_w, ada_b.reshape(DEPTH, 1, n_out))
    return out.reshape(DEPTH, MOD_ROWS, 9, D)


RPB_LANES = 128
RPB_HALF = RPB_LANES // 2


def _bias_kernel(rpb_ref, o_ref):
    qq = lax.broadcasted_iota(jnp.int32, (GRID_W, RPB_LANES), 0)
    lane = lax.broadcasted_iota(jnp.int32, (GRID_W, RPB_LANES), 1)
    second = lane >= GRID_W
    kk = jnp.where(second, lane - GRID_W, lane)
    col_start = jnp.clip(qq - WIN_COLS // 2, 0, GRID_W - WIN_COLS)
    visible = (kk >= col_start) & (kk < col_start + WIN_COLS)
    m = lax.broadcasted_iota(jnp.int32, (N_ROW_OFF, RPB_LANES), 1)
    diff = jnp.where(m < RPB_HALF, m, m - RPB_LANES)
    ext_idx = jnp.clip(diff + (WIN_COLS - 1), 0, N_COL_OFF - 1)
    for i in range(HEADS_PER_GROUP):
        ext = jnp.take_along_axis(rpb_ref[i], ext_idx, axis=1)
        for ro in range(N_ROW_OFF - 1):
            e0 = jnp.broadcast_to(ext[ro:ro + 1, :], (GRID_W, RPB_LANES))
            e1 = jnp.broadcast_to(ext[ro + 1:ro + 2, :], (GRID_W, RPB_LANES))
            t = jnp.where(second,
                          pltpu.roll(e1, GRID_W, 1, stride=1, stride_axis=0),
                          pltpu.roll(e0, 0, 1, stride=1, stride_axis=0))
            t = jnp.where(visible, t, NEG).astype(o_ref.dtype)
            for a in range(0, WIN_ROWS, 2):
                delta_idx = ro - a
                if 0 <= delta_idx < N_DELTA:
                    o_ref[delta_idx, i * GRID_W:(i + 1) * GRID_W, a * GRID_W:(a + 2) * GRID_W] = t


def _bias_table(rpb):
    rpb_padded = jnp.pad(rpb, ((0, 0), (0, 0), (0, 0), (0, RPB_LANES - N_COL_OFF)))
    return pl.pallas_call(
        _bias_kernel,
        grid=(DEPTH, N_HEAD_GROUPS),
        in_specs=[pl.BlockSpec((None, HEADS_PER_GROUP, N_ROW_OFF, RPB_LANES),
                               lambda l, g: (l, g, 0, 0))],
        out_specs=pl.BlockSpec((None, None, N_DELTA, HEADS_PER_GROUP * GRID_W, WIN_ROWS * GRID_W),
                               lambda l, g: (l, g, 0, 0, 0)),
        out_shape=jax.ShapeDtypeStruct(
            (DEPTH, N_HEAD_GROUPS, N_DELTA, HEADS_PER_GROUP * GRID_W, WIN_ROWS * GRID_W), BF),
        compiler_params=pltpu.CompilerParams(
            dimension_semantics=("arbitrary", "arbitrary"), vmem_limit_bytes=VMEM_LIMIT),
        name="rpb_table",
    )(rpb_padded)


def _lat_tile(t, n_lat_tiles):
    return jnp.minimum(t, n_lat_tiles - 1)


def _ctx_tile(t, n_lat_tiles):
    return jnp.maximum(t - n_lat_tiles, 0)


def _ffn_kernel(*refs, l, sub, split_in, split_out, n_lat_tiles, n_cast, n_zero):
    refs = list(refs)
    x_refs = [refs.pop(0) for _ in range(2 if split_in else 1)]
    mod_ref, npre_ref, npost_ref, wi_hbm, wo_hbm = [refs.pop(0) for _ in range(5)]
    cast_in_refs = [refs.pop(0) for _ in range(n_cast)]
    o_refs = [refs.pop(0) for _ in range(2 if split_out else 1)]
    cast_out_refs = [refs.pop(0) for _ in range(n_cast)]
    zero_refs = [refs.pop(0) for _ in range(n_zero)]
    wg_ref, wu_ref, wo_ref, sg_ref, su_ref, so_ref, h_ref, acc_ref, act_ref, sem = refs

    mo = 0 if sub == 0 else 6
    ni = 0 if sub == 0 else 2
    t = pl.program_id(0)
    is_lat = t < n_lat_tiles

    for src, dst in zip(cast_in_refs, cast_out_refs):
        dst[...] = src[...].astype(BF)
    for z in zero_refs:
        z[...] = jnp.zeros(z.shape, z.dtype)

    def load_x(rows=slice(None)):
        if split_in:
            return jnp.where(is_lat, x_refs[0][rows, :], x_refs[1][rows, :])
        return x_refs[0][rows, :]

    def store(out, rows=slice(None)):
        if not split_out:
            o_refs[0][rows, :] = out
            return

        @pl.when(is_lat)
        def _():
            o_refs[0][rows, :] = out

        @pl.when(jnp.logical_not(is_lat))
        def _():
            o_refs[1][rows, :] = out

    def modulated(x):
        shift = mod_ref[mo:mo + 1, :]
        scale = mod_ref[mo + 1:mo + 2, :]
        return (_rms(x, npre_ref[ni:ni + 1, :]) * (1.0 + scale) + shift).astype(BF)

    def residual(x, acc):
        gate = mod_ref[mo + 2:mo + 3, :]
        return x + 0.5 * gate * _rms(acc, npost_ref[ni:ni + 1, :])

    def chunk_update(h, wg, wu, wo):
        g = _dot(h, wg)
        u = _dot(h, wu)
        a = (g * jax.nn.sigmoid(g) * u).astype(BF)
        return _dot(a, wo)

    def chunk_copies(j, slot):
        cols = pl.ds(pl.multiple_of(j * FF_CHUNK, FF_CHUNK), FF_CHUNK)
        up = pl.ds(pl.multiple_of(D_FF + j * FF_CHUNK, FF_CHUNK), FF_CHUNK)
        return (
            pltpu.make_async_copy(wi_hbm.at[l, sub, :, cols], sg_ref.at[slot], sem.at[0, slot]),
            pltpu.make_async_copy(wi_hbm.at[l, sub, :, up], su_ref.at[slot], sem.at[1, slot]),
            pltpu.make_async_copy(wo_hbm.at[l, sub, cols, :], so_ref.at[slot], sem.at[2, slot]),
        )

    @pl.when(t == 0)
    def _():
        for cp in chunk_copies(0, 0):
            cp.start()
        h_ref[...] = modulated(load_x())
        acc_ref[...] = jnp.zeros(acc_ref.shape, F32)

        def body(j, carry):
            slot = lax.rem(j, 2)

            @pl.when(j + 1 < N_FF_CHUNKS)
            def _():
                for cp in chunk_copies(j + 1, 1 - slot):
                    cp.start()

            for cp in chunk_copies(j, slot):
                cp.wait()
            wg = sg_ref[slot].astype(BF)
            wu = su_ref[slot].astype(BF)
            wo = so_ref[slot].astype(BF)
            wg_ref[j] = wg
            wu_ref[j] = wu
            wo_ref[pl.ds(pl.multiple_of(j * FF_CHUNK, FF_CHUNK), FF_CHUNK), :] = wo
            acc_ref[...] += chunk_update(h_ref[...], wg, wu, wo)
            return carry

        lax.fori_loop(0, N_FF_CHUNKS, body, 0)
        store(residual(load_x(), acc_ref[...]))

    @pl.when(t > 0)
    def _():
        def sub_tile(s, carry):
            rows = pl.ds(pl.multiple_of(s * FFN_SUBTILE, FFN_SUBTILE), FFN_SUBTILE)
            x = load_x(rows)
            h = modulated(x)
            for j in range(N_FF_CHUNKS):
                g = _dot(h, wg_ref[j])
                u = _dot(h, wu_ref[j])
                act_ref[:, j * FF_CHUNK:(j + 1) * FF_CHUNK] = (g * jax.nn.sigmoid(g) * u).astype(BF)
            acc = _dot(act_ref[...], wo_ref[...])
            store(residual(x, acc), rows)
            return carry

        n_sub = acc_ref.shape[0] // FFN_SUBTILE
        if n_sub == 1:
            sub_tile(0, 0)
        else:
            lax.fori_loop(0, n_sub, sub_tile, 0)


def _ffn(xs, mods, npre, npost, wi, wo, l, sub, split_out=False, cast_weights=(), zero_shapes=()):
    split_in = len(xs) == 2
    tm = FFN_SUBTILE if (split_in or split_out) else FFN_TILE
    n_tiles = N_TOK // tm
    nl = N_LAT // tm
    lat_idx = lambda t: (_lat_tile(t, nl), 0)
    ctx_idx = lambda t: (_ctx_tile(t, nl), 0)
    split_specs = [pl.BlockSpec((tm, D), lat_idx), pl.BlockSpec((tm, D), ctx_idx)]
    joint_specs = [pl.BlockSpec((tm, D), lambda t: (t, 0))]
    split_shapes = [jax.ShapeDtypeStruct((N_LAT, D), F32), jax.ShapeDtypeStruct((N_CTX, D), F32)]
    joint_shapes = [jax.ShapeDtypeStruct((N_TOK, D), F32)]
    cast_in_specs = [pl.BlockSpec((None, w.shape[1] // n_tiles, w.shape[2]), lambda t: (l, t, 0))
                     for w in cast_weights]
    cast_out_specs = [pl.BlockSpec((w.shape[1] // n_tiles, w.shape[2]), lambda t: (t, 0))
                      for w in cast_weights]
    cast_out_shapes = [jax.ShapeDtypeStruct(w.shape[1:], BF) for w in cast_weights]
    zero_specs = [pl.BlockSpec((s[0] // n_tiles,) + tuple(s[1:]),
                               lambda t, nd=len(s): (t,) + (0,) * (nd - 1)) for s in zero_shapes]
    zero_out_shapes = [jax.ShapeDtypeStruct(tuple(s), F32) for s in zero_shapes]
    n_act = 2 if split_out else 1
    out = pl.pallas_call(
        functools.partial(_ffn_kernel, l=l, sub=sub, split_in=split_in, split_out=split_out,
                          n_lat_tiles=nl, n_cast=len(cast_weights), n_zero=len(zero_shapes)),
        grid=(n_tiles,),
        in_specs=(split_specs if split_in else joint_specs) + [
            pl.BlockSpec((None, None, 9, D), lambda t: (l, _mod_row(t, tm), 0, 0)),
            pl.BlockSpec((None, 3, D), lambda t: (l, 0, 0)),
            pl.BlockSpec((None, 3, D), lambda t: (l, 0, 0)),
            pl.BlockSpec(memory_space=pl.ANY),
            pl.BlockSpec(memory_space=pl.ANY),
        ] + cast_in_specs,
        out_specs=(split_specs if split_out else joint_specs) + cast_out_specs + zero_specs,
        out_shape=(split_shapes if split_out else joint_shapes) + cast_out_shapes + zero_out_shapes,
        scratch_shapes=[
            pltpu.VMEM((N_FF_CHUNKS, D, FF_CHUNK), BF),
            pltpu.VMEM((N_FF_CHUNKS, D, FF_CHUNK), BF),
            pltpu.VMEM((D_FF, D), BF),
            pltpu.VMEM((2, D, FF_CHUNK), F32),
            pltpu.VMEM((2, D, FF_CHUNK), F32),
            pltpu.VMEM((2, FF_CHUNK, D), F32),
            pltpu.VMEM((tm, D), BF),
            pltpu.VMEM((tm, D), F32),
            pltpu.VMEM((FFN_SUBTILE, D_FF), BF),
            pltpu.SemaphoreType.DMA((3, 2)),
        ],
        compiler_params=pltpu.CompilerParams(
            dimension_semantics=("arbitrary",), vmem_limit_bytes=VMEM_LIMIT),
        name=f"ffn_l{l}_s{sub}",
    )(*xs, mods, npre, npost, wi, wo, *cast_weights)
    acts = tuple(out[:n_act]) if split_out else out[0]
    n_cast = len(cast_weights)
    return acts, tuple(out[n_act:n_act + n_cast]), tuple(out[n_act + n_cast:])


def _proj_kernel(x_ref, mod_ref, npre_ref, w_ref, *rest, n_carried):
    qkv_ref, f_ref, g_ref, k_ref, v_ref = rest[n_carried:]
    x = x_ref[...]
    shift = mod_ref[3:4, :]
    scale = mod_ref[4:5, :]
    h = (_rms(x, npre_ref[1:2, :]) * (1.0 + scale) + shift).astype(BF)
    z = _dot(h, w_ref[...])
    qkv_ref[...] = z[:, :3 * WA].astype(BF)
    f_ref[...] = z[:, 3 * WA:3 * WA + WB].astype(BF)
    g_ref[...] = z[:, 3 * WA + WB:].astype(BF)
    k_ref[...] = z[:, WA:2 * WA].reshape(k_ref.shape)
    v_ref[...] = z[:, 2 * WA:3 * WA].reshape(v_ref.shape)


def _proj(x, mods, npre, w_in, carried, l, tm=1024):
    n_tiles = N_TOK // tm
    nl = N_LAT // tm
    bpt = tm // CTX_SEQ
    kv_shape = jax.ShapeDtypeStruct((N_CTX_B, DEPTH, CTX_SEQ, WA), F32)
    kv_spec = pl.BlockSpec((bpt, None, CTX_SEQ, WA), lambda t: (_ctx_tile(t, nl), l, 0, 0))
    n_fixed = 4
    return pl.pallas_call(
        functools.partial(_proj_kernel, n_carried=len(carried)),
        grid=(n_tiles,),
        in_specs=[
            pl.BlockSpec((tm, D), lambda t: (t, 0)),
            pl.BlockSpec((None, None, 9, D), lambda t: (l, _mod_row(t, tm), 0, 0)),
            pl.BlockSpec((None, 3, D), lambda t: (l, 0, 0)),
            pl.BlockSpec((D, PROJ_W), lambda t: (0, 0), pipeline_mode=pl.Buffered(1)),
        ] + [pl.BlockSpec(memory_space=pl.ANY)] * len(carried),
        out_specs=[
            pl.BlockSpec((tm, 3 * WA), lambda t: (t, 0)),
            pl.BlockSpec((tm, WB), lambda t: (t, 0)),
            pl.BlockSpec((tm, 2 * WC), lambda t: (t, 0)),
            kv_spec,
            kv_spec,
        ],
        out_shape=[
            jax.ShapeDtypeStruct((N_TOK, 3 * WA), BF),
            jax.ShapeDtypeStruct((N_TOK, WB), BF),
            jax.ShapeDtypeStruct((N_TOK, 2 * WC), BF),
            kv_shape,
            kv_shape,
        ],
        input_output_aliases={n_fixed + i: 3 + i for i in range(len(carried))},
        compiler_params=pltpu.CompilerParams(
            dimension_semantics=("arbitrary",), vmem_limit_bytes=VMEM_LIMIT),
        name=f"proj_l{l}",
    )(x, mods, npre, w_in, *carried)


def _fourier(f, fc_ref, fn_ref):
    n = fn_ref.shape[0]
    xcs = _dot(f, fc_ref[...]).astype(BF)
    outs = []
    for b in range(f.shape[0] // n):
        xb = xcs[b * n:(b + 1) * n]
        stacked = jnp.concatenate([xb[:, :WB], xb[:, WB:]], axis=0)
        outs.append(_dot(fn_ref[...], stacked))
    return outs[0] if len(outs) == 1 else jnp.concatenate(outs, axis=0)


def _gmlp(g, gn_ref, ws_ref, gb_ref, avg_ref):
    gl = jax.nn.gelu(g.astype(F32), approximate=True)
    u = gl[:, :WC]
    v = gl[:, WC:]
    avg = avg_ref[...]
    mu = _split2_dot(v, avg)
    d = v - mu
    var = _dot((d * d).astype(BF), avg)
    vn = (d * lax.rsqrt(var + EPS) * gn_ref[...]).astype(BF)
    lane_group = lax.broadcasted_iota(jnp.int32, (CHUNK, WC), 1) // GC
    outs = []
    for c in range(g.shape[0] // CHUNK):
        vc = vn[c * CHUNK:(c + 1) * CHUNK, :]
        sp = _dot(ws_ref[0].astype(BF), vc)
        for grp in range(1, NGRP):
            sp = jnp.where(lane_group == grp, _dot(ws_ref[grp].astype(BF), vc), sp)
        outs.append(u[c * CHUNK:(c + 1) * CHUNK, :] * (sp + gb_ref[...]))
    return jnp.concatenate(outs, axis=0)


def _mix_out(x, o, gate, npost, wo_ref):
    y = _dot(o, wo_ref[...])
    return x + gate * _rms(y, npost)


def _ctx_mixer_kernel(x_ref, qkv_ref, f_ref, g_ref, mod_ref, npost_ref, wo_ref, fc_ref, fn_ref,
                      gn_ref, ws_ref, gb_ref, avg_ref, o_ref, mix_ref):
    gw = HEADS_PER_GROUP * HD
    pair_w = 2 * HD
    first_of_pair = lax.broadcasted_iota(jnp.int32, (CTX_SEQ, pair_w), 1) < HD
    own = (lax.broadcasted_iota(jnp.int32, (HEADS_PER_GROUP * CTX_SEQ, gw), 0) // CTX_SEQ
           == lax.broadcasted_iota(jnp.int32, (HEADS_PER_GROUP * CTX_SEQ, gw), 1) // HD)
    lane_head = lax.broadcasted_iota(jnp.int32, (CTX_SEQ, gw), 1) // HD
    zero = jnp.zeros((), BF)
    for b in range(x_ref.shape[0] // CTX_SEQ):
        rows = slice(b * CTX_SEQ, (b + 1) * CTX_SEQ)
        scores = []
        for h in range(NH):
            lanes = slice((h // 2) * pair_w, (h // 2 + 1) * pair_w)
            q2 = qkv_ref[rows, lanes] * QK_SCALE
            k2 = qkv_ref[rows, WA + (h // 2) * pair_w:WA + (h // 2 + 1) * pair_w]
            qh = jnp.where(first_of_pair, q2, zero) if h % 2 == 0 else jnp.where(first_of_pair, zero, q2)
            scores.append(_dot_nt(qh, k2))
        probs, dens = [], []
        for s in scores:
            p = jnp.exp(s - jnp.max(s, axis=-1, keepdims=True))
            dens.append(jnp.sum(p, axis=-1, keepdims=True))
            probs.append(p.astype(BF))
        for grp in range(N_HEAD_GROUPS):
            hs = range(grp * HEADS_PER_GROUP, (grp + 1) * HEADS_PER_GROUP)
            v4 = qkv_ref[rows, 2 * WA + grp * gw:2 * WA + (grp + 1) * gw]
            vbd = jnp.where(own, jnp.concatenate([v4] * HEADS_PER_GROUP, axis=0), zero)
            o = _dot(jnp.concatenate([probs[h] for h in hs], axis=1), vbd)
            den = dens[hs[0]]
            for i in range(1, HEADS_PER_GROUP):
                den = jnp.where(lane_head == i, dens[hs[i]], den)
            mix_ref[rows, grp * gw:(grp + 1) * gw] = (o / den).astype(BF)
    mix_ref[:, WA:WA + WB] = _fourier(f_ref[...], fc_ref, fn_ref).astype(BF)
    mix_ref[:, WA + WB:] = _gmlp(g_ref[...], gn_ref, ws_ref, gb_ref, avg_ref).astype(BF)
    o_ref[...] = _mix_out(x_ref[...], mix_ref[...], mod_ref[5:6, :], npost_ref[1:2, :], wo_ref)


def _const_spec(shape):
    nd = len(shape)
    return pl.BlockSpec(shape, lambda *_: (0,) * nd)


def _ctx_mixer(x, qkv, f, g, mods, npost, w_out, fc, fn, gn, ws, gb, avg, l):
    tm = CTX_PER_STEP * CTX_SEQ
    off = N_LAT // tm
    row = lambda b: (b + off, 0)
    return pl.pallas_call(
        _ctx_mixer_kernel,
        grid=(N_CTX_B // CTX_PER_STEP,),
        in_specs=[
            pl.BlockSpec((tm, D), row),
            pl.BlockSpec((tm, 3 * WA), row),
            pl.BlockSpec((tm, WB), row),
            pl.BlockSpec((tm, 2 * WC), row),
            pl.BlockSpec((None, None, 9, D), lambda b: (l, CTX_MOD_ROW, 0, 0)),
            pl.BlockSpec((None, 3, D), lambda b: (l, 0, 0)),
            pl.BlockSpec((D, D), lambda b: (0, 0), pipeline_mode=pl.Buffered(1)),
            _const_spec(fc.shape),
            _const_spec(fn.shape),
            pl.BlockSpec((None, 1, WC), lambda b: (l, 0, 0)),
            pl.BlockSpec((None, NGRP, CHUNK, CHUNK), lambda b: (l, 0, 0, 0)),
            pl.BlockSpec((None, CHUNK, WC), lambda b: (l, 0, 0)),
            _const_spec(avg.shape),
        ],
        out_specs=pl.BlockSpec((tm, D), row),
        out_shape=jax.ShapeDtypeStruct((N_TOK, D), F32),
        input_output_aliases={0: 0},
        scratch_shapes=[pltpu.VMEM((tm, D), BF)],
        compiler_params=pltpu.CompilerParams(
            dimension_semantics=("arbitrary",), vmem_limit_bytes=VMEM_LIMIT),
        name=f"ctx_mixer_l{l}",
    )(x, qkv, f, g, mods, npost, w_out, fc, fn, gn, ws, gb, avg)


def _window_start(r):
    return min(max(r - WIN_ROWS // 2, 0), GRID_ROWS - WIN_ROWS)


def _nbr_attn_kernel(qkv_ref, ck_ref, cv_ref, bias_ref, o_ref):
    gw = HEADS_PER_GROUP * HD
    row_head = lax.broadcasted_iota(jnp.int32, (HEADS_PER_GROUP * GRID_W, gw), 0) // GRID_W
    lane_head = lax.broadcasted_iota(jnp.int32, (HEADS_PER_GROUP * GRID_W, gw), 1) // HD
    own = row_head == lane_head
    for hg in range(N_HEAD_GROUPS):
        ck = ck_ref[:, hg * gw:(hg + 1) * gw].astype(BF)
        cv = cv_ref[:, hg * gw:(hg + 1) * gw].astype(BF)
        for r in range(GRID_ROWS):
            ws = _window_start(r)
            q = qkv_ref[r * GRID_W:(r + 1) * GRID_W, hg * gw:(hg + 1) * gw] * QK_SCALE
            qs = jnp.where(own, jnp.concatenate([q] * HEADS_PER_GROUP, axis=0), jnp.zeros((), BF))
            kw = qkv_ref[ws * GRID_W:(ws + WIN_ROWS) * GRID_W, WA + hg * gw:WA + (hg + 1) * gw]
            vw = qkv_ref[ws * GRID_W:(ws + WIN_ROWS) * GRID_W, 2 * WA + hg * gw:2 * WA + (hg + 1) * gw]
            s_loc = _dot_nt(qs, kw) + bias_ref[hg, ws - r + N_DELTA - 1].astype(F32)
            s_ctx = _dot_nt(qs, ck)
            m = jnp.maximum(jnp.max(s_loc, axis=-1, keepdims=True),
                            jnp.max(s_ctx, axis=-1, keepdims=True))
            p_loc = jnp.exp(s_loc - m)
            p_ctx = jnp.exp(s_ctx - m)
            den = jnp.sum(p_loc, axis=-1, keepdims=True) + jnp.sum(p_ctx, axis=-1, keepdims=True)
            o = (_dot(p_loc.astype(BF), vw) + _dot(p_ctx.astype(BF), cv)) / den
            o = jnp.where(own, o, 0.0)
            o = (o[0:GRID_W] + o[GRID_W:2 * GRID_W]) + (o[2 * GRID_W:3 * GRID_W] + o[3 * GRID_W:])
            o_ref[r * GRID_W:(r + 1) * GRID_W, hg * gw:(hg + 1) * gw] = o.astype(BF)


def _nbr_attn(qkv, cache_k, cache_v, bias, l):
    return pl.pallas_call(
        _nbr_attn_kernel,
        grid=(N_LAT_B,),
        in_specs=[
            pl.BlockSpec((LAT_SEQ, 3 * WA), lambda b: (b, 0)),
            pl.BlockSpec((None, None, PAST, WA), lambda b: (b, l, 0, 0)),
            pl.BlockSpec((None, None, PAST, WA), lambda b: (b, l, 0, 0)),
            pl.BlockSpec((None,) + bias.shape[1:], lambda b: (l, 0, 0, 0, 0)),
        ],
        out_specs=pl.BlockSpec((LAT_SEQ, WA), lambda b: (b, 0)),
        out_shape=jax.ShapeDtypeStruct((N_LAT, WA), BF),
        compiler_params=pltpu.CompilerParams(
            dimension_semantics=("arbitrary",), vmem_limit_bytes=VMEM_LIMIT),
        name=f"nbr_attn_l{l}",
    )(qkv, cache_k, cache_v, bias)


def _lat_mixer_kernel(x_ref, oa_ref, f_ref, g_ref, mod_ref, npost_ref, wo_ref, fc_ref, fn_ref,
                      gn_ref, ws_ref, gb_ref, avg_ref, o_ref, mix_ref):
    mix_ref[:, :WA] = oa_ref[...]
    mix_ref[:, WA:WA + WB] = _fourier(f_ref[...], fc_ref, fn_ref).astype(BF)
    mix_ref[:, WA + WB:] = _gmlp(g_ref[...], gn_ref, ws_ref, gb_ref, avg_ref).astype(BF)
    o_ref[...] = _mix_out(x_ref[...], mix_ref[...], mod_ref[5:6, :], npost_ref[1:2, :], wo_ref)


def _lat_mixer(x, oa, f, g, mods, npost, w_out, fc, fn, gn, ws, gb, avg, l):
    row = lambda b: (b, 0)
    return pl.pallas_call(
        _lat_mixer_kernel,
        grid=(N_LAT_B,),
        in_specs=[
            pl.BlockSpec((LAT_SEQ, D), row),
            pl.BlockSpec((LAT_SEQ, WA), row),
            pl.BlockSpec((LAT_SEQ, WB), row),
            pl.BlockSpec((LAT_SEQ, 2 * WC), row),
            pl.BlockSpec((None, None, 9, D), lambda b: (l, b, 0, 0)),
            pl.BlockSpec((None, 3, D), lambda b: (l, 0, 0)),
            pl.BlockSpec((D, D), lambda b: (0, 0), pipeline_mode=pl.Buffered(1)),
            _const_spec(fc.shape),
            pl.BlockSpec(fn.shape, lambda b: (0, 0), pipeline_mode=pl.Buffered(1)),
            pl.BlockSpec((None, 1, WC), lambda b: (l, 0, 0)),
            pl.BlockSpec((None, NGRP, CHUNK, CHUNK), lambda b: (l, 0, 0, 0)),
            pl.BlockSpec((None, CHUNK, WC), lambda b: (l, 0, 0)),
            _const_spec(avg.shape),
        ],
        out_specs=pl.BlockSpec((LAT_SEQ, D), row),
        out_shape=jax.ShapeDtypeStruct((N_TOK, D), F32),
        input_output_aliases={0: 0},
        scratch_shapes=[pltpu.VMEM((LAT_SEQ, D), BF)],
        compiler_params=pltpu.CompilerParams(
            dimension_semantics=("arbitrary",), vmem_limit_bytes=VMEM_LIMIT),
        name=f"lat_mixer_l{l}",
    )(x, oa, f, g, mods, npost, w_out, fc, fn, gn, ws, gb, avg)


def _dft_tables(n):
    idx = np.arange(n)
    ang = 2.0 * np.pi * ((idx[:, None] * idx[None, :]) % n) / n
    return (np.concatenate([np.cos(ang), -np.sin(ang)], axis=1) / np.sqrt(n)).astype(np.float32)


def _channel_dft_table():
    idx = np.arange(GC)
    ang = 2.0 * np.pi * ((idx[:, None] * idx[None, :]) % GC) / GC
    eye = np.eye(NGRP)
    return (np.concatenate([np.kron(eye, np.cos(ang)), np.kron(eye, np.sin(ang))], axis=1)
            / np.sqrt(GC)).astype(np.float32)


def _group_average_table():
    return np.kron(np.eye(NGRP), np.full((GC, GC), 1.0 / GC)).astype(np.float32)


def kernel(x_prompt, x_sample, cache_k, cache_v, c, c_ctx, ada_w, ada_b, norm_pre, norm_post,
           ffn_w_in, ffn_w_out, w_in, w_out, rpb, gmlp_norm, gmlp_w, gmlp_b):
    xs = (x_sample.reshape(N_LAT, D), x_prompt.reshape(N_CTX, D))
    cs = jnp.concatenate([c, c_ctx[None, :], jnp.zeros((MOD_ROWS - N_LAT_B - 1, D), F32)], axis=0)
    mods = _ada_mods(cs, ada_w, ada_b)
    bias = _bias_table(rpb)

    gn = gmlp_norm.reshape(DEPTH, 1, WC)
    gb = jnp.repeat(jnp.transpose(gmlp_b, (0, 2, 1)), GC, axis=2)
    ck = cache_k.reshape(N_LAT_B, DEPTH, PAST, WA)
    cv = cache_v.reshape(N_LAT_B, DEPTH, PAST, WA)

    fc = jnp.asarray(_channel_dft_table()).astype(BF)
    fn_ctx = jnp.asarray(_dft_tables(CTX_SEQ)).astype(BF)
    fn_lat = jnp.asarray(_dft_tables(LAT_SEQ)).astype(BF)
    avg = jnp.asarray(_group_average_table()).astype(BF)

    kv_carrier = (N_CTX_B, DEPTH, CTX_SEQ, WA)
    carried = None
    for l in range(DEPTH):
        x, (w_in_b, w_out_b), zeros = _ffn(
            xs, mods, norm_pre, norm_post, ffn_w_in, ffn_w_out, l, 0,
            cast_weights=(w_in, w_out), zero_shapes=(kv_carrier,) * 2 if l == 0 else ())
        if l == 0:
            carried = zeros
        qkv, f, g, new_k, new_v = _proj(x, mods, norm_pre, w_in_b, carried, l)
        carried = (new_k, new_v)
        oa = _nbr_attn(qkv, ck, cv, bias, l)
        x = _ctx_mixer(x, qkv, f, g, mods, norm_post, w_out_b, fc, fn_ctx, gn, gmlp_w, gb, avg, l)
        x = _lat_mixer(x, oa, f, g, mods, norm_post, w_out_b, fc, fn_lat, gn, gmlp_w, gb, avg, l)
        xs, _, _ = _ffn((x,), mods, norm_pre, norm_post, ffn_w_in, ffn_w_out, l, 1,
                        split_out=(l == DEPTH - 1))
        if l < DEPTH - 1:
            xs = (xs,)
    xl, xc = xs

    kv_out = (N_CTX_B, DEPTH, CTX_SEQ, NH, HD)
    return (xc.reshape(N_CTX_B, CTX_SEQ, D), xl.reshape(N_LAT_B, LAT_SEQ, D),
            new_k.reshape(kv_out), new_v.reshape(kv_out))
```

```python
import functools

import numpy as np
import jax
import jax.numpy as jnp
from jax import lax
from jax.experimental import pallas as pl
from jax.experimental.pallas import tpu as pltpu

D = 1024
DEPTH = 4
N_CTX_B = 16
CTX_SEQ = 256
N_LAT_B = 4
LAT_SEQ = 1024
N_LAT = N_LAT_B * LAT_SEQ
N_CTX = N_CTX_B * CTX_SEQ
N_TOK = N_LAT + N_CTX
PAST = 256
GRID_W = 64
GRID_ROWS = LAT_SEQ // GRID_W
HD = 64
NH = 8
WA = 512
WB = 256
WC = 256
NGRP = 4
GC = 64
CHUNK = 128
PROJ_W = 3 * WA + WB + 2 * WC
D_FF = 2816
FF_CHUNK = 256
N_FF_CHUNKS = D_FF // FF_CHUNK
FFN_TILE = 1024
FFN_SUBTILE = 512
WIN_ROWS = 8
WIN_COLS = 16
N_ROW_OFF = 2 * WIN_ROWS - 1
N_COL_OFF = 2 * WIN_COLS - 1
N_DELTA = 8
EPS = 1e-6
NEG = -1e30
ADA_TILE = 2304
MOD_ROWS = 8
CTX_MOD_ROW = 4
CTX_PER_STEP = 4
HEADS_PER_GROUP = 4
N_HEAD_GROUPS = NH // HEADS_PER_GROUP

BF = jnp.bfloat16
F32 = jnp.float32
QK_SCALE = HD ** -0.5

VMEM_LIMIT = 56 * 1024 * 1024


def _dot(a, b):
    return jnp.dot(a, b, preferred_element_type=F32)


def _dot_nt(a, b):
    return lax.dot_general(a, b, (((1,), (1,)), ((), ())), preferred_element_type=F32)


def _split2_dot(a, b):
    a0 = a.astype(BF)
    a1 = (a - a0.astype(F32)).astype(BF)
    return _dot(a0, b) + _dot(a1, b)


def _rms(x, g):
    return x * lax.rsqrt(jnp.mean(x * x, axis=-1, keepdims=True) + EPS) * g


def _mod_row(tile, tm):
    return jnp.minimum((tile * tm) // LAT_SEQ, CTX_MOD_ROW)


def _ada_kernel(cs_ref, w_ref, b_ref, o_ref):
    cs = cs_ref[...]
    s = cs * jax.nn.sigmoid(cs)
    o_ref[...] = _dot(s.astype(BF), w_ref[...].astype(BF)) + b_ref[...]


def _ada_mods(cs, ada_w, ada_b):
    n_out = 9 * D
    tn = ADA_TILE
    out = pl.pallas_call(
        _ada_kernel,
        grid=(DEPTH, n_out // tn),
        in_specs=[
            pl.BlockSpec((MOD_ROWS, D), lambda l, n: (0, 0)),
            pl.BlockSpec((None, D, tn), lambda l, n: (l, 0, n)),
            pl.BlockSpec((None, 1, tn), lambda l, n: (l, 0, n)),
        ],
        out_specs=pl.BlockSpec((None, MOD_ROWS, tn), lambda l, n: (l, 0, n)),
        out_shape=jax.ShapeDtypeStruct((DEPTH, MOD_ROWS, n_out), F32),
        compiler_params=pltpu.CompilerParams(
            dimension_semantics=("arbitrary", "arbitrary"), vmem_limit_bytes=VMEM_LIMIT),
        name="ada_mods",
    )(cs, ada_w, ada_b.reshape(DEPTH, 1, n_out))
    return out.reshape(DEPTH, MOD_ROWS, 9, D)


RPB_LANES = 128
RPB_HALF = RPB_LANES // 2


def _bias_kernel(rpb_ref, o_ref):
    qq = lax.broadcasted_iota(jnp.int32, (GRID_W, RPB_LANES), 0)
    lane = lax.broadcasted_iota(jnp.int32, (GRID_W, RPB_LANES), 1)
    second = lane >= GRID_W
    kk = jnp.where(second, lane - GRID_W, lane)
    col_start = jnp.clip(qq - WIN_COLS // 2, 0, GRID_W - WIN_COLS)
    visible = (kk >= col_start) & (kk < col_start + WIN_COLS)
    m = lax.broadcasted_iota(jnp.int32, (N_ROW_OFF, RPB_LANES), 1)
    diff = jnp.where(m < RPB_HALF, m, m - RPB_LANES)
    ext_idx = jnp.clip(diff + (WIN_COLS - 1), 0, N_COL_OFF - 1)
    for i in range(HEADS_PER_GROUP):
        ext = jnp.take_along_axis(rpb_ref[i], ext_idx, axis=1)
        for ro in range(N_ROW_OFF - 1):
            e0 = jnp.broadcast_to(ext[ro:ro + 1, :], (GRID_W, RPB_LANES))
            e1 = jnp.broadcast_to(ext[ro + 1:ro + 2, :], (GRID_W, RPB_LANES))
            t = jnp.where(second,
                          pltpu.roll(e1, GRID_W, 1, stride=1, stride_axis=0),
                          pltpu.roll(e0, 0, 1, stride=1, stride_axis=0))
            t = jnp.where(visible, t, NEG).astype(o_ref.dtype)
            for a in range(0, WIN_ROWS, 2):
                delta_idx = ro - a
                if 0 <= delta_idx < N_DELTA:
                    o_ref[delta_idx, i * GRID_W:(i + 1) * GRID_W, a * GRID_W:(a + 2) * GRID_W] = t


def _bias_table(rpb):
    rpb_padded = jnp.pad(rpb, ((0, 0), (0, 0), (0, 0), (0, RPB_LANES - N_COL_OFF)))
    return pl.pallas_call(
        _bias_kernel,
        grid=(DEPTH, N_HEAD_GROUPS),
        in_specs=[pl.BlockSpec((None, HEADS_PER_GROUP, N_ROW_OFF, RPB_LANES),
                               lambda l, g: (l, g, 0, 0))],
        out_specs=pl.BlockSpec((None, None, N_DELTA, HEADS_PER_GROUP * GRID_W, WIN_ROWS * GRID_W),
                               lambda l, g: (l, g, 0, 0, 0)),
        out_shape=jax.ShapeDtypeStruct(
            (DEPTH, N_HEAD_GROUPS, N_DELTA, HEADS_PER_GROUP * GRID_W, WIN_ROWS * GRID_W), BF),
        compiler_params=pltpu.CompilerParams(
            dimension_semantics=("arbitrary", "arbitrary"), vmem_limit_bytes=VMEM_LIMIT),
        name="rpb_table",
    )(rpb_padded)


def _lat_tile(t, n_lat_tiles):
    return jnp.minimum(t, n_lat_tiles - 1)


def _ctx_tile(t, n_lat_tiles):
    return jnp.maximum(t - n_lat_tiles, 0)


def _ffn_kernel(*refs, l, sub, split_in, split_out, n_lat_tiles, n_cast, n_zero):
    refs = list(refs)
    x_refs = [refs.pop(0) for _ in range(2 if split_in else 1)]
    mod_ref, npre_ref, npost_ref, wi_hbm, wo_hbm = [refs.pop(0) for _ in range(5)]
    cast_in_refs = [refs.pop(0) for _ in range(n_cast)]
    o_refs = [refs.pop(0) for _ in range(2 if split_out else 1)]
    cast_out_refs = [refs.pop(0) for _ in range(n_cast)]
    zero_refs = [refs.pop(0) for _ in range(n_zero)]
    wg_ref, wu_ref, wo_ref, sg_ref, su_ref, so_ref, h_ref, acc_ref, act_ref, sem = refs

    mo = 0 if sub == 0 else 6
    ni = 0 if sub == 0 else 2
    t = pl.program_id(0)
    is_lat = t < n_lat_tiles

    for src, dst in zip(cast_in_refs, cast_out_refs):
        dst[...] = src[...].astype(BF)
    for z in zero_refs:
        z[...] = jnp.zeros(z.shape, z.dtype)

    def load_x(rows=slice(None)):
        if split_in:
            return jnp.where(is_lat, x_refs[0][rows, :], x_refs[1][rows, :])
        return x_refs[0][rows, :]

    def store(out, rows=slice(None)):
        if not split_out:
            o_refs[0][rows, :] = out
            return

        @pl.when(is_lat)
        def _():
            o_refs[0][rows, :] = out

        @pl.when(jnp.logical_not(is_lat))
        def _():
            o_refs[1][rows, :] = out

    def modulated(x):
        shift = mod_ref[mo:mo + 1, :]
        scale = mod_ref[mo + 1:mo + 2, :]
        return (_rms(x, npre_ref[ni:ni + 1, :]) * (1.0 + scale) + shift).astype(BF)

    def residual(x, acc):
        gate = mod_ref[mo + 2:mo + 3, :]
        return x + 0.5 * gate * _rms(acc, npost_ref[ni:ni + 1, :])

    def chunk_update(h, wg, wu, wo):
        g = _dot(h, wg)
        u = _dot(h, wu)
        a = (g * jax.nn.sigmoid(g) * u).astype(BF)
        return _dot(a, wo)

    def chunk_copies(j, slot):
        cols = pl.ds(pl.multiple_of(j * FF_CHUNK, FF_CHUNK), FF_CHUNK)
        up = pl.ds(pl.multiple_of(D_FF + j * FF_CHUNK, FF_CHUNK), FF_CHUNK)
        return (
            pltpu.make_async_copy(wi_hbm.at[l, sub, :, cols], sg_ref.at[slot], sem.at[0, slot]),
            pltpu.make_async_copy(wi_hbm.at[l, sub, :, up], su_ref.at[slot], sem.at[1, slot]),
            pltpu.make_async_copy(wo_hbm.at[l, sub, cols, :], so_ref.at[slot], sem.at[2, slot]),
        )

    @pl.when(t == 0)
    def _():
        for cp in chunk_copies(0, 0):
            cp.start()
        h_ref[...] = modulated(load_x())
        acc_ref[...] = jnp.zeros(acc_ref.shape, F32)

        def body(j, carry):
            slot = lax.rem(j, 2)

            @pl.when(j + 1 < N_FF_CHUNKS)
            def _():
                for cp in chunk_copies(j + 1, 1 - slot):
                    cp.start()

            for cp in chunk_copies(j, slot):
                cp.wait()
            wg = sg_ref[slot].astype(BF)
            wu = su_ref[slot].astype(BF)
            wo = so_ref[slot].astype(BF)
            wg_ref[j] = wg
            wu_ref[j] = wu
            wo_ref[pl.ds(pl.multiple_of(j * FF_CHUNK, FF_CHUNK), FF_CHUNK), :] = wo
            acc_ref[...] += chunk_update(h_ref[...], wg, wu, wo)
            return carry

        lax.fori_loop(0, N_FF_CHUNKS, body, 0)
        store(residual(load_x(), acc_ref[...]))

    @pl.when(t > 0)
    def _():
        def sub_tile(s, carry):
            rows = pl.ds(pl.multiple_of(s * FFN_SUBTILE, FFN_SUBTILE), FFN_SUBTILE)
            x = load_x(rows)
            h = modulated(x)
            for j in range(N_FF_CHUNKS):
                g = _dot(h, wg_ref[j])
                u = _dot(h, wu_ref[j])
                act_ref[:, j * FF_CHUNK:(j + 1) * FF_CHUNK] = (g * jax.nn.sigmoid(g) * u).astype(BF)
            acc = _dot(act_ref[...], wo_ref[...])
            store(residual(x, acc), rows)
            return carry

        n_sub = acc_ref.shape[0] // FFN_SUBTILE
        if n_sub == 1:
            sub_tile(0, 0)
        else:
            lax.fori_loop(0, n_sub, sub_tile, 0)


def _ffn(xs, mods, npre, npost, wi, wo, l, sub, split_out=False, cast_weights=(), zero_shapes=()):
    split_in = len(xs) == 2
    tm = FFN_SUBTILE if (split_in or split_out) else FFN_TILE
    n_tiles = N_TOK // tm
    nl = N_LAT // tm
    lat_idx = lambda t: (_lat_tile(t, nl), 0)
    ctx_idx = lambda t: (_ctx_tile(t, nl), 0)
    split_specs = [pl.BlockSpec((tm, D), lat_idx), pl.BlockSpec((tm, D), ctx_idx)]
    joint_specs = [pl.BlockSpec((tm, D), lambda t: (t, 0))]
    split_shapes = [jax.ShapeDtypeStruct((N_LAT, D), F32), jax.ShapeDtypeStruct((N_CTX, D), F32)]
    joint_shapes = [jax.ShapeDtypeStruct((N_TOK, D), F32)]
    cast_in_specs = [pl.BlockSpec((None, w.shape[1] // n_tiles, w.shape[2]), lambda t: (l, t, 0))
                     for w in cast_weights]
    cast_out_specs = [pl.BlockSpec((w.shape[1] // n_tiles, w.shape[2]), lambda t: (t, 0))
                      for w in cast_weights]
    cast_out_shapes = [jax.ShapeDtypeStruct(w.shape[1:], BF) for w in cast_weights]
    zero_specs = [pl.BlockSpec((s[0] // n_tiles,) + tuple(s[1:]),
                               lambda t, nd=len(s): (t,) + (0,) * (nd - 1)) for s in zero_shapes]
    zero_out_shapes = [jax.ShapeDtypeStruct(tuple(s), F32) for s in zero_shapes]
    n_act = 2 if split_out else 1
    out = pl.pallas_call(
        functools.partial(_ffn_kernel, l=l, sub=sub, split_in=split_in, split_out=split_out,
                          n_lat_tiles=nl, n_cast=len(cast_weights), n_zero=len(zero_shapes)),
        grid=(n_tiles,),
        in_specs=(split_specs if split_in else joint_specs) + [
            pl.BlockSpec((None, None, 9, D), lambda t: (l, _mod_row(t, tm), 0, 0)),
            pl.BlockSpec((None, 3, D), lambda t: (l, 0, 0)),
            pl.BlockSpec((None, 3, D), lambda t: (l, 0, 0)),
            pl.BlockSpec(memory_space=pl.ANY),
            pl.BlockSpec(memory_space=pl.ANY),
        ] + cast_in_specs,
        out_specs=(split_specs if split_out else joint_specs) + cast_out_specs + zero_specs,
        out_shape=(split_shapes if split_out else joint_shapes) + cast_out_shapes + zero_out_shapes,
        scratch_shapes=[
            pltpu.VMEM((N_FF_CHUNKS, D, FF_CHUNK), BF),
            pltpu.VMEM((N_FF_CHUNKS, D, FF_CHUNK), BF),
            pltpu.VMEM((D_FF, D), BF),
            pltpu.VMEM((2, D, FF_CHUNK), F32),
            pltpu.VMEM((2, D, FF_CHUNK), F32),
            pltpu.VMEM((2, FF_CHUNK, D), F32),
            pltpu.VMEM((tm, D), BF),
            pltpu.VMEM((tm, D), F32),
            pltpu.VMEM((FFN_SUBTILE, D_FF), BF),
            pltpu.SemaphoreType.DMA((3, 2)),
        ],
        compiler_params=pltpu.CompilerParams(
            dimension_semantics=("arbitrary",), vmem_limit_bytes=VMEM_LIMIT),
        name=f"ffn_l{l}_s{sub}",
    )(*xs, mods, npre, npost, wi, wo, *cast_weights)
    acts = tuple(out[:n_act]) if split_out else out[0]
    n_cast = len(cast_weights)
    return acts, tuple(out[n_act:n_act + n_cast]), tuple(out[n_act + n_cast:])


def _proj_kernel(x_ref, mod_ref, npre_ref, w_ref, *rest, n_carried):
    qkv_ref, f_ref, g_ref, k_ref, v_ref = rest[n_carried:]
    x = x_ref[...]
    shift = mod_ref[3:4, :]
    scale = mod_ref[4:5, :]
    h = (_rms(x, npre_ref[1:2, :]) * (1.0 + scale) + shift).astype(BF)
    z = _dot(h, w_ref[...])
    qkv_ref[...] = z[:, :3 * WA].astype(BF)
    f_ref[...] = z[:, 3 * WA:3 * WA + WB].astype(BF)
    g_ref[...] = z[:, 3 * WA + WB:].astype(BF)
    k_ref[...] = z[:, WA:2 * WA].reshape(k_ref.shape)
    v_ref[...] = z[:, 2 * WA:3 * WA].reshape(v_ref.shape)


def _proj(x, mods, npre, w_in, carried, l, tm=1024):
    n_tiles = N_TOK // tm
    nl = N_LAT // tm
    bpt = tm // CTX_SEQ
    kv_shape = jax.ShapeDtypeStruct((N_CTX_B, DEPTH, CTX_SEQ, WA), F32)
    kv_spec = pl.BlockSpec((bpt, None, CTX_SEQ, WA), lambda t: (_ctx_tile(t, nl), l, 0, 0))
    n_fixed = 4
    return pl.pallas_call(
        functools.partial(_proj_kernel, n_carried=len(carried)),
        grid=(n_tiles,),
        in_specs=[
            pl.BlockSpec((tm, D), lambda t: (t, 0)),
            pl.BlockSpec((None, None, 9, D), lambda t: (l, _mod_row(t, tm), 0, 0)),
            pl.BlockSpec((None, 3, D), lambda t: (l, 0, 0)),
            pl.BlockSpec((D, PROJ_W), lambda t: (0, 0), pipeline_mode=pl.Buffered(1)),
        ] + [pl.BlockSpec(memory_space=pl.ANY)] * len(carried),
        out_specs=[
            pl.BlockSpec((tm, 3 * WA), lambda t: (t, 0)),
            pl.BlockSpec((tm, WB), lambda t: (t, 0)),
            pl.BlockSpec((tm, 2 * WC), lambda t: (t, 0)),
            kv_spec,
            kv_spec,
        ],
        out_shape=[
            jax.ShapeDtypeStruct((N_TOK, 3 * WA), BF),
            jax.ShapeDtypeStruct((N_TOK, WB), BF),
            jax.ShapeDtypeStruct((N_TOK, 2 * WC), BF),
            kv_shape,
            kv_shape,
        ],
        input_output_aliases={n_fixed + i: 3 + i for i in range(len(carried))},
        compiler_params=pltpu.CompilerParams(
            dimension_semantics=("arbitrary",), vmem_limit_bytes=VMEM_LIMIT),
        name=f"proj_l{l}",
    )(x, mods, npre, w_in, *carried)


def _fourier(f, fc_ref, fn_ref):
    n = fn_ref.shape[0]
    xcs = _dot(f, fc_ref[...]).astype(BF)
    outs = []
    for b in range(f.shape[0] // n):
        xb = xcs[b * n:(b + 1) * n]
        stacked = jnp.concatenate([xb[:, :WB], xb[:, WB:]], axis=0)
        outs.append(_dot(fn_ref[...], stacked))
    return outs[0] if len(outs) == 1 else jnp.concatenate(outs, axis=0)


def _gmlp(g, gn_ref, ws_ref, gb_ref, avg_ref):
    gl = jax.nn.gelu(g.astype(F32), approximate=True)
    u = gl[:, :WC]
    v = gl[:, WC:]
    avg = avg_ref[...]
    mu = _split2_dot(v, avg)
    d = v - mu
    var = _dot((d * d).astype(BF), avg)
    vn = (d * lax.rsqrt(var + EPS) * gn_ref[...]).astype(BF)
    lane_group = lax.broadcasted_iota(jnp.int32, (CHUNK, WC), 1) // GC
    outs = []
    for c in range(g.shape[0] // CHUNK):
        vc = vn[c * CHUNK:(c + 1) * CHUNK, :]
        sp = _dot(ws_ref[0].astype(BF), vc)
        for grp in range(1, NGRP):
            sp = jnp.where(lane_group == grp, _dot(ws_ref[grp].astype(BF), vc), sp)
        outs.append(u[c * CHUNK:(c + 1) * CHUNK, :] * (sp + gb_ref[...]))
    return jnp.concatenate(outs, axis=0)


def _mix_out(x, o, gate, npost, wo_ref):
    y = _dot(o, wo_ref[...])
    return x + gate * _rms(y, npost)


def _ctx_mixer_kernel(x_ref, qkv_ref, f_ref, g_ref, mod_ref, npost_ref, wo_ref, fc_ref, fn_ref,
                      gn_ref, ws_ref, gb_ref, avg_ref, o_ref, mix_ref):
    gw = HEADS_PER_GROUP * HD
    pair_w = 2 * HD
    first_of_pair = lax.broadcasted_iota(jnp.int32, (CTX_SEQ, pair_w), 1) < HD
    own = (lax.broadcasted_iota(jnp.int32, (HEADS_PER_GROUP * CTX_SEQ, gw), 0) // CTX_SEQ
           == lax.broadcasted_iota(jnp.int32, (HEADS_PER_GROUP * CTX_SEQ, gw), 1) // HD)
    lane_head = lax.broadcasted_iota(jnp.int32, (CTX_SEQ, gw), 1) // HD
    zero = jnp.zeros((), BF)
    for b in range(x_ref.shape[0] // CTX_SEQ):
        rows = slice(b * CTX_SEQ, (b + 1) * CTX_SEQ)
        scores = []
        for h in range(NH):
            lanes = slice((h // 2) * pair_w, (h // 2 + 1) * pair_w)
            q2 = qkv_ref[rows, lanes] * QK_SCALE
            k2 = qkv_ref[rows, WA + (h // 2) * pair_w:WA + (h // 2 + 1) * pair_w]
            qh = jnp.where(first_of_pair, q2, zero) if h % 2 == 0 else jnp.where(first_of_pair, zero, q2)
            scores.append(_dot_nt(qh, k2))
        probs, dens = [], []
        for s in scores:
            p = jnp.exp(s - jnp.max(s, axis=-1, keepdims=True))
            dens.append(jnp.sum(p, axis=-1, keepdims=True))
            probs.append(p.astype(BF))
        for grp in range(N_HEAD_GROUPS):
            hs = range(grp * HEADS_PER_GROUP, (grp + 1) * HEADS_PER_GROUP)
            v4 = qkv_ref[rows, 2 * WA + grp * gw:2 * WA + (grp + 1) * gw]
            vbd = jnp.where(own, jnp.concatenate([v4] * HEADS_PER_GROUP, axis=0), zero)
            o = _dot(jnp.concatenate([probs[h] for h in hs], axis=1), vbd)
            den = dens[hs[0]]
            for i in range(1, HEADS_PER_GROUP):
                den = jnp.where(lane_head == i, dens[hs[i]], den)
            mix_ref[rows, grp * gw:(grp + 1) * gw] = (o / den).astype(BF)
    mix_ref[:, WA:WA + WB] = _fourier(f_ref[...], fc_ref, fn_ref).astype(BF)
    mix_ref[:, WA + WB:] = _gmlp(g_ref[...], gn_ref, ws_ref, gb_ref, avg_ref).astype(BF)
    o_ref[...] = _mix_out(x_ref[...], mix_ref[...], mod_ref[5:6, :], npost_ref[1:2, :], wo_ref)


def _const_spec(shape):
    nd = len(shape)
    return pl.BlockSpec(shape, lambda *_: (0,) * nd)


def _ctx_mixer(x, qkv, f, g, mods, npost, w_out, fc, fn, gn, ws, gb, avg, l):
    tm = CTX_PER_STEP * CTX_SEQ
    off = N_LAT // tm
    row = lambda b: (b + off, 0)
    return pl.pallas_call(
        _ctx_mixer_kernel,
        grid=(N_CTX_B // CTX_PER_STEP,),
        in_specs=[
            pl.BlockSpec((tm, D), row),
            pl.BlockSpec((tm, 3 * WA), row),
            pl.BlockSpec((tm, WB), row),
            pl.BlockSpec((tm, 2 * WC), row),
            pl.BlockSpec((None, None, 9, D), lambda b: (l, CTX_MOD_ROW, 0, 0)),
            pl.BlockSpec((None, 3, D), lambda b: (l, 0, 0)),
            pl.BlockSpec((D, D), lambda b: (0, 0), pipeline_mode=pl.Buffered(1)),
            _const_spec(fc.shape),
            _const_spec(fn.shape),
            pl.BlockSpec((None, 1, WC), lambda b: (l, 0, 0)),
            pl.BlockSpec((None, NGRP, CHUNK, CHUNK), lambda b: (l, 0, 0, 0)),
            pl.BlockSpec((None, CHUNK, WC), lambda b: (l, 0, 0)),
            _const_spec(avg.shape),
        ],
        out_specs=pl.BlockSpec((tm, D), row),
        out_shape=jax.ShapeDtypeStruct((N_TOK, D), F32),
        input_output_aliases={0: 0},
        scratch_shapes=[pltpu.VMEM((tm, D), BF)],
        compiler_params=pltpu.CompilerParams(
            dimension_semantics=("arbitrary",), vmem_limit_bytes=VMEM_LIMIT),
        name=f"ctx_mixer_l{l}",
    )(x, qkv, f, g, mods, npost, w_out, fc, fn, gn, ws, gb, avg)


def _window_start(r):
    return min(max(r - WIN_ROWS // 2, 0), GRID_ROWS - WIN_ROWS)


def _nbr_attn_kernel(qkv_ref, ck_ref, cv_ref, bias_ref, o_ref):
    gw = HEADS_PER_GROUP * HD
    row_head = lax.broadcasted_iota(jnp.int32, (HEADS_PER_GROUP * GRID_W, gw), 0) // GRID_W
    lane_head = lax.broadcasted_iota(jnp.int32, (HEADS_PER_GROUP * GRID_W, gw), 1) // HD
    own = row_head == lane_head
    for hg in range(N_HEAD_GROUPS):
        ck = ck_ref[hg * gw:(hg + 1) * gw, :].astype(BF)
        cv = cv_ref[hg * gw:(hg + 1) * gw, :].astype(BF)
        for r in range(GRID_ROWS):
            ws = _window_start(r)
            q = qkv_ref[r * GRID_W:(r + 1) * GRID_W, hg * gw:(hg + 1) * gw] * QK_SCALE
            qs = jnp.where(own, jnp.concatenate([q] * HEADS_PER_GROUP, axis=0), jnp.zeros((), BF))
            kw = qkv_ref[ws * GRID_W:(ws + WIN_ROWS) * GRID_W, WA + hg * gw:WA + (hg + 1) * gw]
            vw = qkv_ref[ws * GRID_W:(ws + WIN_ROWS) * GRID_W, 2 * WA + hg * gw:2 * WA + (hg + 1) * gw]
            s_loc = _dot_nt(qs, kw) + bias_ref[hg, ws - r + N_DELTA - 1].astype(F32)
            s_ctx = _dot(qs, ck)
            m = jnp.maximum(jnp.max(s_loc, axis=-1, keepdims=True),
                            jnp.max(s_ctx, axis=-1, keepdims=True))
            p_loc = jnp.exp(s_loc - m)
            p_ctx = jnp.exp(s_ctx - m)
            den = jnp.sum(p_loc, axis=-1, keepdims=True) + jnp.sum(p_ctx, axis=-1, keepdims=True)
            o = (_dot(p_loc.astype(BF), vw) + _dot_nt(p_ctx.astype(BF), cv)) / den
            o = jnp.where(own, o, 0.0)
            o = (o[0:GRID_W] + o[GRID_W:2 * GRID_W]) + (o[2 * GRID_W:3 * GRID_W] + o[3 * GRID_W:])
            o_ref[r * GRID_W:(r + 1) * GRID_W, hg * gw:(hg + 1) * gw] = o.astype(BF)


def _nbr_attn(qkv, cache_k, cache_v, bias, l):
    return pl.pallas_call(
        _nbr_attn_kernel,
        grid=(N_LAT_B,),
        in_specs=[
            pl.BlockSpec((LAT_SEQ, 3 * WA), lambda b: (b, 0)),
            pl.BlockSpec((None, None, WA, PAST), lambda b: (b, l, 0, 0)),
            pl.BlockSpec((None, None, WA, PAST), lambda b: (b, l, 0, 0)),
            pl.BlockSpec((None,) + bias.shape[1:], lambda b: (l, 0, 0, 0, 0)),
        ],
        out_specs=pl.BlockSpec((LAT_SEQ, WA), lambda b: (b, 0)),
        out_shape=jax.ShapeDtypeStruct((N_LAT, WA), BF),
        compiler_params=pltpu.CompilerParams(
            dimension_semantics=("arbitrary",), vmem_limit_bytes=VMEM_LIMIT),
        name=f"nbr_attn_l{l}",
    )(qkv, cache_k, cache_v, bias)


def _lat_mixer_kernel(x_ref, oa_ref, f_ref, g_ref, mod_ref, npost_ref, wo_ref, fc_ref, fn_ref,
                      gn_ref, ws_ref, gb_ref, avg_ref, o_ref, mix_ref):
    mix_ref[:, :WA] = oa_ref[...]
    mix_ref[:, WA:WA + WB] = _fourier(f_ref[...], fc_ref, fn_ref).astype(BF)
    mix_ref[:, WA + WB:] = _gmlp(g_ref[...], gn_ref, ws_ref, gb_ref, avg_ref).astype(BF)
    o_ref[...] = _mix_out(x_ref[...], mix_ref[...], mod_ref[5:6, :], npost_ref[1:2, :], wo_ref)


def _lat_mixer(x, oa, f, g, mods, npost, w_out, fc, fn, gn, ws, gb, avg, l):
    row = lambda b: (b, 0)
    return pl.pallas_call(
        _lat_mixer_kernel,
        grid=(N_LAT_B,),
        in_specs=[
            pl.BlockSpec((LAT_SEQ, D), row),
            pl.BlockSpec((LAT_SEQ, WA), row),
            pl.BlockSpec((LAT_SEQ, WB), row),
            pl.BlockSpec((LAT_SEQ, 2 * WC), row),
            pl.BlockSpec((None, None, 9, D), lambda b: (l, b, 0, 0)),
            pl.BlockSpec((None, 3, D), lambda b: (l, 0, 0)),
            pl.BlockSpec((D, D), lambda b: (0, 0), pipeline_mode=pl.Buffered(1)),
            _const_spec(fc.shape),
            pl.BlockSpec(fn.shape, lambda b: (0, 0), pipeline_mode=pl.Buffered(1)),
            pl.BlockSpec((None, 1, WC), lambda b: (l, 0, 0)),
            pl.BlockSpec((None, NGRP, CHUNK, CHUNK), lambda b: (l, 0, 0, 0)),
            pl.BlockSpec((None, CHUNK, WC), lambda b: (l, 0, 0)),
            _const_spec(avg.shape),
        ],
        out_specs=pl.BlockSpec((LAT_SEQ, D), row),
        out_shape=jax.ShapeDtypeStruct((N_TOK, D), F32),
        input_output_aliases={0: 0},
        scratch_shapes=[pltpu.VMEM((LAT_SEQ, D), BF)],
        compiler_params=pltpu.CompilerParams(
            dimension_semantics=("arbitrary",), vmem_limit_bytes=VMEM_LIMIT),
        name=f"lat_mixer_l{l}",
    )(x, oa, f, g, mods, npost, w_out, fc, fn, gn, ws, gb, avg)


def _dft_tables(n):
    idx = np.arange(n)
    ang = 2.0 * np.pi * ((idx[:, None] * idx[None, :]) % n) / n
    return (np.concatenate([np.cos(ang), -np.sin(ang)], axis=1) / np.sqrt(n)).astype(np.float32)


def _channel_dft_table():
    idx = np.arange(GC)
    ang = 2.0 * np.pi * ((idx[:, None] * idx[None, :]) % GC) / GC
    eye = np.eye(NGRP)
    return (np.concatenate([np.kron(eye, np.cos(ang)), np.kron(eye, np.sin(ang))], axis=1)
            / np.sqrt(GC)).astype(np.float32)


def _group_average_table():
    return np.kron(np.eye(NGRP), np.full((GC, GC), 1.0 / GC)).astype(np.float32)


def kernel(x_prompt, x_sample, cache_k, cache_v, c, c_ctx, ada_w, ada_b, norm_pre, norm_post,
           ffn_w_in, ffn_w_out, w_in, w_out, rpb, gmlp_norm, gmlp_w, gmlp_b):
    xs = (x_sample.reshape(N_LAT, D), x_prompt.reshape(N_CTX, D))
    cs = jnp.concatenate([c, c_ctx[None, :], jnp.zeros((MOD_ROWS - N_LAT_B - 1, D), F32)], axis=0)
    mods = _ada_mods(cs, ada_w, ada_b)
    bias = _bias_table(rpb)

    gn = gmlp_norm.reshape(DEPTH, 1, WC)
    gb = jnp.repeat(jnp.transpose(gmlp_b, (0, 2, 1)), GC, axis=2)
    ck = jnp.swapaxes(cache_k.reshape(N_LAT_B, DEPTH, PAST, WA), 2, 3)
    cv = jnp.swapaxes(cache_v.reshape(N_LAT_B, DEPTH, PAST, WA), 2, 3)

    fc = jnp.asarray(_channel_dft_table()).astype(BF)
    fn_ctx = jnp.asarray(_dft_tables(CTX_SEQ)).astype(BF)
    fn_lat = jnp.asarray(_dft_tables(LAT_SEQ)).astype(BF)
    avg = jnp.asarray(_group_average_table()).astype(BF)

    kv_carrier = (N_CTX_B, DEPTH, CTX_SEQ, WA)
    carried = None
    for l in range(DEPTH):
        x, (w_in_b, w_out_b), zeros = _ffn(
            xs, mods, norm_pre, norm_post, ffn_w_in, ffn_w_out, l, 0,
            cast_weights=(w_in, w_out), zero_shapes=(kv_carrier,) * 2 if l == 0 else ())
        if l == 0:
            carried = zeros
        qkv, f, g, new_k, new_v = _proj(x, mods, norm_pre, w_in_b, carried, l)
        carried = (new_k, new_v)
        oa = _nbr_attn(qkv, ck, cv, bias, l)
        x = _ctx_mixer(x, qkv, f, g, mods, norm_post, w_out_b, fc, fn_ctx, gn, gmlp_w, gb, avg, l)
        x = _lat_mixer(x, oa, f, g, mods, norm_post, w_out_b, fc, fn_lat, gn, gmlp_w, gb, avg, l)
        xs, _, _ = _ffn((x,), mods, norm_pre, norm_post, ffn_w_in, ffn_w_out, l, 1,
                        split_out=(l == DEPTH - 1))
        if l < DEPTH - 1:
            xs = (xs,)
    xl, xc = xs

    kv_out = (N_CTX_B, DEPTH, CTX_SEQ, NH, HD)
    return (xc.reshape(N_CTX_B, CTX_SEQ, D), xl.reshape(N_LAT_B, LAT_SEQ, D),
            new_k.reshape(kv_out), new_v.reshape(kv_out))
```

```python
import functools

import numpy as np
import jax
import jax.numpy as jnp
from jax import lax
from jax.experimental import pallas as pl
from jax.experimental.pallas import tpu as pltpu

D = 1024
DEPTH = 4
N_CTX_B = 16
CTX_SEQ = 256
N_LAT_B = 4
LAT_SEQ = 1024
N_LAT = N_LAT_B * LAT_SEQ
N_CTX = N_CTX_B * CTX_SEQ
N_TOK = N_LAT + N_CTX
PAST = 256
GRID_W = 64
GRID_ROWS = LAT_SEQ // GRID_W
HD = 64
NH = 8
WA = 512
WB = 256
WC = 256
NGRP = 4
GC = 64
CHUNK = 128
PROJ_W = 3 * WA + WB + 2 * WC
D_FF = 2816
FF_CHUNK = 256
N_FF_CHUNKS = D_FF // FF_CHUNK
FFN_TILE = 1024
FFN_SUBTILE = 512
WIN_ROWS = 8
WIN_COLS = 16
N_ROW_OFF = 2 * WIN_ROWS - 1
N_COL_OFF = 2 * WIN_COLS - 1
N_DELTA = 8
EPS = 1e-6
NEG = -1e30
ADA_TILE = 2304
MOD_ROWS = 8
CTX_MOD_ROW = 4
CTX_PER_STEP = 4
HEADS_PER_GROUP = 4
N_HEAD_GROUPS = NH // HEADS_PER_GROUP

BF = jnp.bfloat16
F32 = jnp.float32
QK_SCALE = HD ** -0.5

VMEM_LIMIT = 56 * 1024 * 1024


def _dot(a, b):
    return jnp.dot(a, b, preferred_element_type=F32)


def _dot_nt(a, b):
    return lax.dot_general(a, b, (((1,), (1,)), ((), ())), preferred_element_type=F32)


def _split2_dot(a, b):
    a0 = a.astype(BF)
    a1 = (a - a0.astype(F32)).astype(BF)
    return _dot(a0, b) + _dot(a1, b)


def _rms(x, g):
    return x * lax.rsqrt(jnp.mean(x * x, axis=-1, keepdims=True) + EPS) * g


def _mod_row(tile, tm):
    return jnp.minimum((tile * tm) // LAT_SEQ, CTX_MOD_ROW)


def _ada_kernel(cs_ref, w_ref, b_ref, o_ref):
    cs = cs_ref[...]
    s = cs * jax.nn.sigmoid(cs)
    o_ref[...] = _dot(s.astype(BF), w_ref[...].astype(BF)) + b_ref[...]


def _ada_mods(cs, ada_w, ada_b):
    n_out = 9 * D
    tn = ADA_TILE
    out = pl.pallas_call(
        _ada_kernel,
        grid=(DEPTH, n_out // tn),
        in_specs=[
            pl.BlockSpec((MOD_ROWS, D), lambda l, n: (0, 0)),
            pl.BlockSpec((None, D, tn), lambda l, n: (l, 0, n)),
            pl.BlockSpec((None, 1, tn), lambda l, n: (l, 0, n)),
        ],
        out_specs=pl.BlockSpec((None, MOD_ROWS, tn), lambda l, n: (l, 0, n)),
        out_shape=jax.ShapeDtypeStruct((DEPTH, MOD_ROWS, n_out), F32),
        compiler_params=pltpu.CompilerParams(
            dimension_semantics=("arbitrary", "arbitrary"), vmem_limit_bytes=VMEM_LIMIT),
        name="ada_mods",
    )(cs, ada_w, ada_b.reshape(DEPTH, 1, n_out))
    return out.reshape(DEPTH, MOD_ROWS, 9, D)


RPB_LANES = 128
RPB_HALF = RPB_LANES // 2


def _bias_kernel(rpb_ref, o_ref):
    qq = lax.broadcasted_iota(jnp.int32, (GRID_W, RPB_LANES), 0)
    lane = lax.broadcasted_iota(jnp.int32, (GRID_W, RPB_LANES), 1)
    second = lane >= GRID_W
    kk = jnp.where(second, lane - GRID_W, lane)
    col_start = jnp.clip(qq - WIN_COLS // 2, 0, GRID_W - WIN_COLS)
    visible = (kk >= col_start) & (kk < col_start + WIN_COLS)
    m = lax.broadcasted_iota(jnp.int32, (N_ROW_OFF, RPB_LANES), 1)
    diff = jnp.where(m < RPB_HALF, m, m - RPB_LANES)
    ext_idx = jnp.clip(diff + (WIN_COLS - 1), 0, N_COL_OFF - 1)
    for i in range(HEADS_PER_GROUP):
        ext = jnp.take_along_axis(rpb_ref[i], ext_idx, axis=1)
        for ro in range(N_ROW_OFF - 1):
            e0 = jnp.broadcast_to(ext[ro:ro + 1, :], (GRID_W, RPB_LANES))
            e1 = jnp.broadcast_to(ext[ro + 1:ro + 2, :], (GRID_W, RPB_LANES))
            t = jnp.where(second,
                          pltpu.roll(e1, GRID_W, 1, stride=1, stride_axis=0),
                          pltpu.roll(e0, 0, 1, stride=1, stride_axis=0))
            t = jnp.where(visible, t, NEG).astype(o_ref.dtype)
            for a in range(0, WIN_ROWS, 2):
                delta_idx = ro - a
                if 0 <= delta_idx < N_DELTA:
                    o_ref[delta_idx, i * GRID_W:(i + 1) * GRID_W, a * GRID_W:(a + 2) * GRID_W] = t


def _bias_table(rpb):
    rpb_padded = jnp.pad(rpb, ((0, 0), (0, 0), (0, 0), (0, RPB_LANES - N_COL_OFF)))
    return pl.pallas_call(
        _bias_kernel,
        grid=(DEPTH, N_HEAD_GROUPS),
        in_specs=[pl.BlockSpec((None, HEADS_PER_GROUP, N_ROW_OFF, RPB_LANES),
                               lambda l, g: (l, g, 0, 0))],
        out_specs=pl.BlockSpec((None, None, N_DELTA, HEADS_PER_GROUP * GRID_W, WIN_ROWS * GRID_W),
                               lambda l, g: (l, g, 0, 0, 0)),
        out_shape=jax.ShapeDtypeStruct(
            (DEPTH, N_HEAD_GROUPS, N_DELTA, HEADS_PER_GROUP * GRID_W, WIN_ROWS * GRID_W), BF),
        compiler_params=pltpu.CompilerParams(
            dimension_semantics=("arbitrary", "arbitrary"), vmem_limit_bytes=VMEM_LIMIT),
        name="rpb_table",
    )(rpb_padded)


def _lat_tile(t, n_lat_tiles):
    return jnp.minimum(t, n_lat_tiles - 1)


def _ctx_tile(t, n_lat_tiles):
    return jnp.maximum(t - n_lat_tiles, 0)


def _ffn_kernel(*refs, l, sub, split_in, split_out, n_lat_tiles, n_cast, n_zero):
    refs = list(refs)
    x_refs = [refs.pop(0) for _ in range(2 if split_in else 1)]
    mod_ref, npre_ref, npost_ref, wi_hbm, wo_hbm = [refs.pop(0) for _ in range(5)]
    cast_in_refs = [refs.pop(0) for _ in range(n_cast)]
    o_refs = [refs.pop(0) for _ in range(2 if split_out else 1)]
    cast_out_refs = [refs.pop(0) for _ in range(n_cast)]
    zero_refs = [refs.pop(0) for _ in range(n_zero)]
    wg_ref, wu_ref, wo_ref, sg_ref, su_ref, so_ref, h_ref, acc_ref, act_ref, sem = refs

    mo = 0 if sub == 0 else 6
    ni = 0 if sub == 0 else 2
    t = pl.program_id(0)
    is_lat = t < n_lat_tiles

    for src, dst in zip(cast_in_refs, cast_out_refs):
        dst[...] = src[...].astype(BF)
    for z in zero_refs:
        z[...] = jnp.zeros(z.shape, z.dtype)

    def load_x(rows=slice(None)):
        if split_in:
            return jnp.where(is_lat, x_refs[0][rows, :], x_refs[1][rows, :])
        return x_refs[0][rows, :]

    def store(out, rows=slice(None)):
        if not split_out:
            o_refs[0][rows, :] = out
            return

        @pl.when(is_lat)
        def _():
            o_refs[0][rows, :] = out

        @pl.when(jnp.logical_not(is_lat))
        def _():
            o_refs[1][rows, :] = out

    def modulated(x):
        shift = mod_ref[mo:mo + 1, :]
        scale = mod_ref[mo + 1:mo + 2, :]
        return (_rms(x, npre_ref[ni:ni + 1, :]) * (1.0 + scale) + shift).astype(BF)

    def residual(x, acc):
        gate = mod_ref[mo + 2:mo + 3, :]
        return x + 0.5 * gate * _rms(acc, npost_ref[ni:ni + 1, :])

    def chunk_update(h, wg, wu, wo):
        g = _dot(h, wg)
        u = _dot(h, wu)
        a = (g * jax.nn.sigmoid(g) * u).astype(BF)
        return _dot(a, wo)

    def chunk_copies(j, slot):
        cols = pl.ds(pl.multiple_of(j * FF_CHUNK, FF_CHUNK), FF_CHUNK)
        up = pl.ds(pl.multiple_of(D_FF + j * FF_CHUNK, FF_CHUNK), FF_CHUNK)
        return (
            pltpu.make_async_copy(wi_hbm.at[l, sub, :, cols], sg_ref.at[slot], sem.at[0, slot]),
            pltpu.make_async_copy(wi_hbm.at[l, sub, :, up], su_ref.at[slot], sem.at[1, slot]),
            pltpu.make_async_copy(wo_hbm.at[l, sub, cols, :], so_ref.at[slot], sem.at[2, slot]),
        )

    @pl.when(t == 0)
    def _():
        for cp in chunk_copies(0, 0):
            cp.start()
        h_ref[...] = modulated(load_x())
        acc_ref[...] = jnp.zeros(acc_ref.shape, F32)

        def body(j, carry):
            slot = lax.rem(j, 2)

            @pl.when(j + 1 < N_FF_CHUNKS)
            def _():
                for cp in chunk_copies(j + 1, 1 - slot):
                    cp.start()

            for cp in chunk_copies(j, slot):
                cp.wait()
            wg = sg_ref[slot].astype(BF)
            wu = su_ref[slot].astype(BF)
            wo = so_ref[slot].astype(BF)
            wg_ref[j] = wg
            wu_ref[j] = wu
            wo_ref[pl.ds(pl.multiple_of(j * FF_CHUNK, FF_CHUNK), FF_CHUNK), :] = wo
            acc_ref[...] += chunk_update(h_ref[...], wg, wu, wo)
            return carry

        lax.fori_loop(0, N_FF_CHUNKS, body, 0)
        store(residual(load_x(), acc_ref[...]))

    @pl.when(t > 0)
    def _():
        def sub_tile(s, carry):
            rows = pl.ds(pl.multiple_of(s * FFN_SUBTILE, FFN_SUBTILE), FFN_SUBTILE)
            x = load_x(rows)
            h = modulated(x)
            for j in range(N_FF_CHUNKS):
                g = _dot(h, wg_ref[j])
                u = _dot(h, wu_ref[j])
                act_ref[:, j * FF_CHUNK:(j + 1) * FF_CHUNK] = (g * jax.nn.sigmoid(g) * u).astype(BF)
            acc = _dot(act_ref[...], wo_ref[...])
            store(residual(x, acc), rows)
            return carry

        n_sub = acc_ref.shape[0] // FFN_SUBTILE
        if n_sub == 1:
            sub_tile(0, 0)
        else:
            lax.fori_loop(0, n_sub, sub_tile, 0)


def _ffn(xs, mods, npre, npost, wi, wo, l, sub, split_out=False, cast_weights=(), zero_shapes=()):
    split_in = len(xs) == 2
    tm = FFN_SUBTILE if (split_in or split_out) else FFN_TILE
    n_tiles = N_TOK // tm
    nl = N_LAT // tm
    lat_idx = lambda t: (_lat_tile(t, nl), 0)
    ctx_idx = lambda t: (_ctx_tile(t, nl), 0)
    split_specs = [pl.BlockSpec((tm, D), lat_idx), pl.BlockSpec((tm, D), ctx_idx)]
    joint_specs = [pl.BlockSpec((tm, D), lambda t: (t, 0))]
    split_shapes = [jax.ShapeDtypeStruct((N_LAT, D), F32), jax.ShapeDtypeStruct((N_CTX, D), F32)]
    joint_shapes = [jax.ShapeDtypeStruct((N_TOK, D), F32)]
    cast_in_specs = [pl.BlockSpec((None, w.shape[1] // n_tiles, w.shape[2]), lambda t: (l, t, 0))
                     for w in cast_weights]
    cast_out_specs = [pl.BlockSpec((w.shape[1] // n_tiles, w.shape[2]), lambda t: (t, 0))
                      for w in cast_weights]
    cast_out_shapes = [jax.ShapeDtypeStruct(w.shape[1:], BF) for w in cast_weights]
    zero_specs = [pl.BlockSpec((s[0] // n_tiles,) + tuple(s[1:]),
                               lambda t, nd=len(s): (t,) + (0,) * (nd - 1)) for s in zero_shapes]
    zero_out_shapes = [jax.ShapeDtypeStruct(tuple(s), F32) for s in zero_shapes]
    n_act = 2 if split_out else 1
    out = pl.pallas_call(
        functools.partial(_ffn_kernel, l=l, sub=sub, split_in=split_in, split_out=split_out,
                          n_lat_tiles=nl, n_cast=len(cast_weights), n_zero=len(zero_shapes)),
        grid=(n_tiles,),
        in_specs=(split_specs if split_in else joint_specs) + [
            pl.BlockSpec((None, None, 9, D), lambda t: (l, _mod_row(t, tm), 0, 0)),
            pl.BlockSpec((None, 3, D), lambda t: (l, 0, 0)),
            pl.BlockSpec((None, 3, D), lambda t: (l, 0, 0)),
            pl.BlockSpec(memory_space=pl.ANY),
            pl.BlockSpec(memory_space=pl.ANY),
        ] + cast_in_specs,
        out_specs=(split_specs if split_out else joint_specs) + cast_out_specs + zero_specs,
        out_shape=(split_shapes if split_out else joint_shapes) + cast_out_shapes + zero_out_shapes,
        scratch_shapes=[
            pltpu.VMEM((N_FF_CHUNKS, D, FF_CHUNK), BF),
            pltpu.VMEM((N_FF_CHUNKS, D, FF_CHUNK), BF),
            pltpu.VMEM((D_FF, D), BF),
            pltpu.VMEM((2, D, FF_CHUNK), F32),
            pltpu.VMEM((2, D, FF_CHUNK), F32),
            pltpu.VMEM((2, FF_CHUNK, D), F32),
            pltpu.VMEM((tm, D), BF),
            pltpu.VMEM((tm, D), F32),
            pltpu.VMEM((FFN_SUBTILE, D_FF), BF),
            pltpu.SemaphoreType.DMA((3, 2)),
        ],
        compiler_params=pltpu.CompilerParams(
            dimension_semantics=("arbitrary",), vmem_limit_bytes=VMEM_LIMIT),
        name=f"ffn_l{l}_s{sub}",
    )(*xs, mods, npre, npost, wi, wo, *cast_weights)
    acts = tuple(out[:n_act]) if split_out else out[0]
    n_cast = len(cast_weights)
    return acts, tuple(out[n_act:n_act + n_cast]), tuple(out[n_act + n_cast:])


def _proj_kernel(x_ref, mod_ref, npre_ref, w_ref, *rest, n_carried):
    qkv_ref, f_ref, g_ref, k_ref, v_ref = rest[n_carried:]
    x = x_ref[...]
    shift = mod_ref[3:4, :]
    scale = mod_ref[4:5, :]
    h = (_rms(x, npre_ref[1:2, :]) * (1.0 + scale) + shift).astype(BF)
    z = _dot(h, w_ref[...])
    qkv_ref[...] = z[:, :3 * WA].astype(BF)
    f_ref[...] = z[:, 3 * WA:3 * WA + WB].astype(BF)
    g_ref[...] = z[:, 3 * WA + WB:].astype(BF)
    for i in range(k_ref.shape[0]):
        rows = slice(i * CTX_SEQ, (i + 1) * CTX_SEQ)
        k_ref[i] = z[rows, WA:2 * WA].T
        v_ref[i] = z[rows, 2 * WA:3 * WA].T


def _proj(x, mods, npre, w_in, carried, l, tm=1024):
    n_tiles = N_TOK // tm
    nl = N_LAT // tm
    bpt = tm // CTX_SEQ
    kv_shape = jax.ShapeDtypeStruct((N_CTX_B, DEPTH, WA, CTX_SEQ), F32)
    kv_spec = pl.BlockSpec((bpt, None, WA, CTX_SEQ), lambda t: (_ctx_tile(t, nl), l, 0, 0))
    n_fixed = 4
    return pl.pallas_call(
        functools.partial(_proj_kernel, n_carried=len(carried)),
        grid=(n_tiles,),
        in_specs=[
            pl.BlockSpec((tm, D), lambda t: (t, 0)),
            pl.BlockSpec((None, None, 9, D), lambda t: (l, _mod_row(t, tm), 0, 0)),
            pl.BlockSpec((None, 3, D), lambda t: (l, 0, 0)),
            pl.BlockSpec((D, PROJ_W), lambda t: (0, 0), pipeline_mode=pl.Buffered(1)),
        ] + [pl.BlockSpec(memory_space=pl.ANY)] * len(carried),
        out_specs=[
            pl.BlockSpec((tm, 3 * WA), lambda t: (t, 0)),
            pl.BlockSpec((tm, WB), lambda t: (t, 0)),
            pl.BlockSpec((tm, 2 * WC), lambda t: (t, 0)),
            kv_spec,
            kv_spec,
        ],
        out_shape=[
            jax.ShapeDtypeStruct((N_TOK, 3 * WA), BF),
            jax.ShapeDtypeStruct((N_TOK, WB), BF),
            jax.ShapeDtypeStruct((N_TOK, 2 * WC), BF),
            kv_shape,
            kv_shape,
        ],
        input_output_aliases={n_fixed + i: 3 + i for i in range(len(carried))},
        compiler_params=pltpu.CompilerParams(
            dimension_semantics=("arbitrary",), vmem_limit_bytes=VMEM_LIMIT),
        name=f"proj_l{l}",
    )(x, mods, npre, w_in, *carried)


def _fourier(f, fc_ref, fn_ref):
    n = fn_ref.shape[0]
    xcs = _dot(f, fc_ref[...]).astype(BF)
    outs = []
    for b in range(f.shape[0] // n):
        xb = xcs[b * n:(b + 1) * n]
        stacked = jnp.concatenate([xb[:, :WB], xb[:, WB:]], axis=0)
        outs.append(_dot(fn_ref[...], stacked))
    return outs[0] if len(outs) == 1 else jnp.concatenate(outs, axis=0)


def _gmlp(g, gn_ref, ws_ref, gb_ref, avg_ref):
    gl = jax.nn.gelu(g.astype(F32), approximate=True)
    u = gl[:, :WC]
    v = gl[:, WC:]
    avg = avg_ref[...]
    mu = _split2_dot(v, avg)
    d = v - mu
    var = _dot((d * d).astype(BF), avg)
    vn = (d * lax.rsqrt(var + EPS) * gn_ref[...]).astype(BF)
    lane_group = lax.broadcasted_iota(jnp.int32, (CHUNK, WC), 1) // GC
    outs = []
    for c in range(g.shape[0] // CHUNK):
        vc = vn[c * CHUNK:(c + 1) * CHUNK, :]
        sp = _dot(ws_ref[0].astype(BF), vc)
        for grp in range(1, NGRP):
            sp = jnp.where(lane_group == grp, _dot(ws_ref[grp].astype(BF), vc), sp)
        outs.append(u[c * CHUNK:(c + 1) * CHUNK, :] * (sp + gb_ref[...]))
    return jnp.concatenate(outs, axis=0)


def _mix_out(x, o, gate, npost, wo_ref):
    y = _dot(o, wo_ref[...])
    return x + gate * _rms(y, npost)


def _ctx_mixer_kernel(x_ref, qkv_ref, f_ref, g_ref, mod_ref, npost_ref, wo_ref, fc_ref, fn_ref,
                      gn_ref, ws_ref, gb_ref, avg_ref, o_ref, mix_ref):
    gw = HEADS_PER_GROUP * HD
    pair_w = 2 * HD
    first_of_pair = lax.broadcasted_iota(jnp.int32, (CTX_SEQ, pair_w), 1) < HD
    own = (lax.broadcasted_iota(jnp.int32, (HEADS_PER_GROUP * CTX_SEQ, gw), 0) // CTX_SEQ
           == lax.broadcasted_iota(jnp.int32, (HEADS_PER_GROUP * CTX_SEQ, gw), 1) // HD)
    lane_head = lax.broadcasted_iota(jnp.int32, (CTX_SEQ, gw), 1) // HD
    zero = jnp.zeros((), BF)
    for b in range(x_ref.shape[0] // CTX_SEQ):
        rows = slice(b * CTX_SEQ, (b + 1) * CTX_SEQ)
        scores = []
        for h in range(NH):
            lanes = slice((h // 2) * pair_w, (h // 2 + 1) * pair_w)
            q2 = qkv_ref[rows, lanes] * QK_SCALE
            k2 = qkv_ref[rows, WA + (h // 2) * pair_w:WA + (h // 2 + 1) * pair_w]
            qh = jnp.where(first_of_pair, q2, zero) if h % 2 == 0 else jnp.where(first_of_pair, zero, q2)
            scores.append(_dot_nt(qh, k2))
        probs, dens = [], []
        for s in scores:
            p = jnp.exp(s - jnp.max(s, axis=-1, keepdims=True))
            dens.append(jnp.sum(p, axis=-1, keepdims=True))
            probs.append(p.astype(BF))
        for grp in range(N_HEAD_GROUPS):
            hs = range(grp * HEADS_PER_GROUP, (grp + 1) * HEADS_PER_GROUP)
            v4 = qkv_ref[rows, 2 * WA + grp * gw:2 * WA + (grp + 1) * gw]
            vbd = jnp.where(own, jnp.concatenate([v4] * HEADS_PER_GROUP, axis=0), zero)
            o = _dot(jnp.concatenate([probs[h] for h in hs], axis=1), vbd)
            den = dens[hs[0]]
            for i in range(1, HEADS_PER_GROUP):
                den = jnp.where(lane_head == i, dens[hs[i]], den)
            mix_ref[rows, grp * gw:(grp + 1) * gw] = (o / den).astype(BF)
    mix_ref[:, WA:WA + WB] = _fourier(f_ref[...], fc_ref, fn_ref).astype(BF)
    mix_ref[:, WA + WB:] = _gmlp(g_ref[...], gn_ref, ws_ref, gb_ref, avg_ref).astype(BF)
    o_ref[...] = _mix_out(x_ref[...], mix_ref[...], mod_ref[5:6, :], npost_ref[1:2, :], wo_ref)


def _const_spec(shape):
    nd = len(shape)
    return pl.BlockSpec(shape, lambda *_: (0,) * nd)


def _ctx_mixer(x, qkv, f, g, mods, npost, w_out, fc, fn, gn, ws, gb, avg, l):
    tm = CTX_PER_STEP * CTX_SEQ
    off = N_LAT // tm
    row = lambda b: (b + off, 0)
    return pl.pallas_call(
        _ctx_mixer_kernel,
        grid=(N_CTX_B // CTX_PER_STEP,),
        in_specs=[
            pl.BlockSpec((tm, D), row),
            pl.BlockSpec((tm, 3 * WA), row),
            pl.BlockSpec((tm, WB), row),
            pl.BlockSpec((tm, 2 * WC), row),
            pl.BlockSpec((None, None, 9, D), lambda b: (l, CTX_MOD_ROW, 0, 0)),
            pl.BlockSpec((None, 3, D), lambda b: (l, 0, 0)),
            pl.BlockSpec((D, D), lambda b: (0, 0), pipeline_mode=pl.Buffered(1)),
            _const_spec(fc.shape),
            _const_spec(fn.shape),
            pl.BlockSpec((None, 1, WC), lambda b: (l, 0, 0)),
            pl.BlockSpec((None, NGRP, CHUNK, CHUNK), lambda b: (l, 0, 0, 0)),
            pl.BlockSpec((None, CHUNK, WC), lambda b: (l, 0, 0)),
            _const_spec(avg.shape),
        ],
        out_specs=pl.BlockSpec((tm, D), row),
        out_shape=jax.ShapeDtypeStruct((N_TOK, D), F32),
        input_output_aliases={0: 0},
        scratch_shapes=[pltpu.VMEM((tm, D), BF)],
        compiler_params=pltpu.CompilerParams(
            dimension_semantics=("arbitrary",), vmem_limit_bytes=VMEM_LIMIT),
        name=f"ctx_mixer_l{l}",
    )(x, qkv, f, g, mods, npost, w_out, fc, fn, gn, ws, gb, avg)


def _window_start(r):
    return min(max(r - WIN_ROWS // 2, 0), GRID_ROWS - WIN_ROWS)


def _nbr_attn_kernel(qkv_ref, ck_ref, cv_ref, bias_ref, o_ref):
    gw = HEADS_PER_GROUP * HD
    row_head = lax.broadcasted_iota(jnp.int32, (HEADS_PER_GROUP * GRID_W, gw), 0) // GRID_W
    lane_head = lax.broadcasted_iota(jnp.int32, (HEADS_PER_GROUP * GRID_W, gw), 1) // HD
    own = row_head == lane_head
    for hg in range(N_HEAD_GROUPS):
        ck = ck_ref[hg * gw:(hg + 1) * gw, :].astype(BF)
        cv = cv_ref[hg * gw:(hg + 1) * gw, :].astype(BF)
        for r in range(GRID_ROWS):
            ws = _window_start(r)
            q = qkv_ref[r * GRID_W:(r + 1) * GRID_W, hg * gw:(hg + 1) * gw] * QK_SCALE
            qs = jnp.where(own, jnp.concatenate([q] * HEADS_PER_GROUP, axis=0), jnp.zeros((), BF))
            kw = qkv_ref[ws * GRID_W:(ws + WIN_ROWS) * GRID_W, WA + hg * gw:WA + (hg + 1) * gw]
            vw = qkv_ref[ws * GRID_W:(ws + WIN_ROWS) * GRID_W, 2 * WA + hg * gw:2 * WA + (hg + 1) * gw]
            s_loc = _dot_nt(qs, kw) + bias_ref[hg, ws - r + N_DELTA - 1].astype(F32)
            s_ctx = _dot(qs, ck)
            m = jnp.maximum(jnp.max(s_loc, axis=-1, keepdims=True),
                            jnp.max(s_ctx, axis=-1, keepdims=True))
            p_loc = jnp.exp(s_loc - m)
            p_ctx = jnp.exp(s_ctx - m)
            den = jnp.sum(p_loc, axis=-1, keepdims=True) + jnp.sum(p_ctx, axis=-1, keepdims=True)
            o = (_dot(p_loc.astype(BF), vw) + _dot_nt(p_ctx.astype(BF), cv)) / den
            o = jnp.where(own, o, 0.0)
            o = (o[0:GRID_W] + o[GRID_W:2 * GRID_W]) + (o[2 * GRID_W:3 * GRID_W] + o[3 * GRID_W:])
            o_ref[r * GRID_W:(r + 1) * GRID_W, hg * gw:(hg + 1) * gw] = o.astype(BF)


def _nbr_attn(qkv, cache_k, cache_v, bias, l):
    return pl.pallas_call(
        _nbr_attn_kernel,
        grid=(N_LAT_B,),
        in_specs=[
            pl.BlockSpec((LAT_SEQ, 3 * WA), lambda b: (b, 0)),
            pl.BlockSpec((None, None, WA, PAST), lambda b: (b, l, 0, 0)),
            pl.BlockSpec((None, None, WA, PAST), lambda b: (b, l, 0, 0)),
            pl.BlockSpec((None,) + bias.shape[1:], lambda b: (l, 0, 0, 0, 0)),
        ],
        out_specs=pl.BlockSpec((LAT_SEQ, WA), lambda b: (b, 0)),
        out_shape=jax.ShapeDtypeStruct((N_LAT, WA), BF),
        compiler_params=pltpu.CompilerParams(
            dimension_semantics=("arbitrary",), vmem_limit_bytes=VMEM_LIMIT),
        name=f"nbr_attn_l{l}",
    )(qkv, cache_k, cache_v, bias)


def _lat_mixer_kernel(x_ref, oa_ref, f_ref, g_ref, mod_ref, npost_ref, wo_ref, fc_ref, fn_ref,
                      gn_ref, ws_ref, gb_ref, avg_ref, o_ref, mix_ref):
    mix_ref[:, :WA] = oa_ref[...]
    mix_ref[:, WA:WA + WB] = _fourier(f_ref[...], fc_ref, fn_ref).astype(BF)
    mix_ref[:, WA + WB:] = _gmlp(g_ref[...], gn_ref, ws_ref, gb_ref, avg_ref).astype(BF)
    o_ref[...] = _mix_out(x_ref[...], mix_ref[...], mod_ref[5:6, :], npost_ref[1:2, :], wo_ref)


def _lat_mixer(x, oa, f, g, mods, npost, w_out, fc, fn, gn, ws, gb, avg, l):
    row = lambda b: (b, 0)
    return pl.pallas_call(
        _lat_mixer_kernel,
        grid=(N_LAT_B,),
        in_specs=[
            pl.BlockSpec((LAT_SEQ, D), row),
            pl.BlockSpec((LAT_SEQ, WA), row),
            pl.BlockSpec((LAT_SEQ, WB), row),
            pl.BlockSpec((LAT_SEQ, 2 * WC), row),
            pl.BlockSpec((None, None, 9, D), lambda b: (l, b, 0, 0)),
            pl.BlockSpec((None, 3, D), lambda b: (l, 0, 0)),
            pl.BlockSpec((D, D), lambda b: (0, 0), pipeline_mode=pl.Buffered(1)),
            _const_spec(fc.shape),
            pl.BlockSpec(fn.shape, lambda b: (0, 0), pipeline_mode=pl.Buffered(1)),
            pl.BlockSpec((None, 1, WC), lambda b: (l, 0, 0)),
            pl.BlockSpec((None, NGRP, CHUNK, CHUNK), lambda b: (l, 0, 0, 0)),
            pl.BlockSpec((None, CHUNK, WC), lambda b: (l, 0, 0)),
            _const_spec(avg.shape),
        ],
        out_specs=pl.BlockSpec((LAT_SEQ, D), row),
        out_shape=jax.ShapeDtypeStruct((N_TOK, D), F32),
        input_output_aliases={0: 0},
        scratch_shapes=[pltpu.VMEM((LAT_SEQ, D), BF)],
        compiler_params=pltpu.CompilerParams(
            dimension_semantics=("arbitrary",), vmem_limit_bytes=VMEM_LIMIT),
        name=f"lat_mixer_l{l}",
    )(x, oa, f, g, mods, npost, w_out, fc, fn, gn, ws, gb, avg)


def _dft_tables(n):
    idx = np.arange(n)
    ang = 2.0 * np.pi * ((idx[:, None] * idx[None, :]) % n) / n
    return (np.concatenate([np.cos(ang), -np.sin(ang)], axis=1) / np.sqrt(n)).astype(np.float32)


def _channel_dft_table():
    idx = np.arange(GC)
    ang = 2.0 * np.pi * ((idx[:, None] * idx[None, :]) % GC) / GC
    eye = np.eye(NGRP)
    return (np.concatenate([np.kron(eye, np.cos(ang)), np.kron(eye, np.sin(ang))], axis=1)
            / np.sqrt(GC)).astype(np.float32)


def _group_average_table():
    return np.kron(np.eye(NGRP), np.full((GC, GC), 1.0 / GC)).astype(np.float32)


def kernel(x_prompt, x_sample, cache_k, cache_v, c, c_ctx, ada_w, ada_b, norm_pre, norm_post,
           ffn_w_in, ffn_w_out, w_in, w_out, rpb, gmlp_norm, gmlp_w, gmlp_b):
    xs = (x_sample.reshape(N_LAT, D), x_prompt.reshape(N_CTX, D))
    cs = jnp.concatenate([c, c_ctx[None, :], jnp.zeros((MOD_ROWS - N_LAT_B - 1, D), F32)], axis=0)
    mods = _ada_mods(cs, ada_w, ada_b)
    bias = _bias_table(rpb)

    gn = gmlp_norm.reshape(DEPTH, 1, WC)
    gb = jnp.repeat(jnp.transpose(gmlp_b, (0, 2, 1)), GC, axis=2)
    ck = jnp.swapaxes(cache_k.reshape(N_LAT_B, DEPTH, PAST, WA), 2, 3)
    cv = jnp.swapaxes(cache_v.reshape(N_LAT_B, DEPTH, PAST, WA), 2, 3)

    fc = jnp.asarray(_channel_dft_table()).astype(BF)
    fn_ctx = jnp.asarray(_dft_tables(CTX_SEQ)).astype(BF)
    fn_lat = jnp.asarray(_dft_tables(LAT_SEQ)).astype(BF)
    avg = jnp.asarray(_group_average_table()).astype(BF)

    kv_carrier = (N_CTX_B, DEPTH, WA, CTX_SEQ)
    carried = None
    for l in range(DEPTH):
        x, (w_in_b, w_out_b), zeros = _ffn(
            xs, mods, norm_pre, norm_post, ffn_w_in, ffn_w_out, l, 0,
            cast_weights=(w_in, w_out), zero_shapes=(kv_carrier,) * 2 if l == 0 else ())
        if l == 0:
            carried = zeros
        qkv, f, g, new_k, new_v = _proj(x, mods, norm_pre, w_in_b, carried, l)
        carried = (new_k, new_v)
        oa = _nbr_attn(qkv, ck, cv, bias, l)
        x = _ctx_mixer(x, qkv, f, g, mods, norm_post, w_out_b, fc, fn_ctx, gn, gmlp_w, gb, avg, l)
        x = _lat_mixer(x, oa, f, g, mods, norm_post, w_out_b, fc, fn_lat, gn, gmlp_w, gb, avg, l)
        xs, _, _ = _ffn((x,), mods, norm_pre, norm_post, ffn_w_in, ffn_w_out, l, 1,
                        split_out=(l == DEPTH - 1))
        if l < DEPTH - 1:
            xs = (xs,)
    xl, xc = xs

    kv_out = (N_CTX_B, DEPTH, CTX_SEQ, NH, HD)
    return (xc.reshape(N_CTX_B, CTX_SEQ, D), xl.reshape(N_LAT_B, LAT_SEQ, D),
            jnp.swapaxes(new_k, 2, 3).reshape(kv_out), jnp.swapaxes(new_v, 2, 3).reshape(kv_out))
```

```python
import functools

import numpy as np
import jax
import jax.numpy as jnp
from jax import lax
from jax.experimental import pallas as pl
from jax.experimental.pallas import tpu as pltpu

D = 1024
DEPTH = 4
N_CTX_B = 16
CTX_SEQ = 256
N_LAT_B = 4
LAT_SEQ = 1024
N_LAT = N_LAT_B * LAT_SEQ
N_CTX = N_CTX_B * CTX_SEQ
N_TOK = N_LAT + N_CTX
PAST = 256
GRID_W = 64
GRID_ROWS = LAT_SEQ // GRID_W
HD = 64
NH = 8
WA = 512
WB = 256
WC = 256
NGRP = 4
GC = 64
CHUNK = 128
PROJ_W = 3 * WA + WB + 2 * WC
D_FF = 2816
FF_CHUNK = 256
N_FF_CHUNKS = D_FF // FF_CHUNK
FFN_TILE = 1024
FFN_SUBTILE = 512
WIN_ROWS = 8
WIN_COLS = 16
N_ROW_OFF = 2 * WIN_ROWS - 1
N_COL_OFF = 2 * WIN_COLS - 1
N_DELTA = 8
EPS = 1e-6
NEG = -1e30
ADA_TILE = 4608
MOD_ROWS = 8
CTX_MOD_ROW = 4
CTX_PER_STEP = 4
HEADS_PER_GROUP = 4
N_HEAD_GROUPS = NH // HEADS_PER_GROUP

BF = jnp.bfloat16
F32 = jnp.float32
QK_SCALE = HD ** -0.5

VMEM_LIMIT = 56 * 1024 * 1024


def _dot(a, b):
    return jnp.dot(a, b, preferred_element_type=F32)


def _dot_nt(a, b):
    return lax.dot_general(a, b, (((1,), (1,)), ((), ())), preferred_element_type=F32)


def _split2_dot(a, b):
    a0 = a.astype(BF)
    a1 = (a - a0.astype(F32)).astype(BF)
    return _dot(a0, b) + _dot(a1, b)


def _rms(x, g):
    return x * lax.rsqrt(jnp.mean(x * x, axis=-1, keepdims=True) + EPS) * g


def _mod_row(tile, tm):
    return jnp.minimum((tile * tm) // LAT_SEQ, CTX_MOD_ROW)


def _ada_kernel(cs_ref, w_ref, b_ref, o_ref):
    cs = cs_ref[...]
    s = cs * jax.nn.sigmoid(cs)
    o_ref[...] = _dot(s.astype(BF), w_ref[...].astype(BF)) + b_ref[...]


def _ada_mods(cs, ada_w, ada_b):
    n_out = 9 * D
    tn = ADA_TILE
    out = pl.pallas_call(
        _ada_kernel,
        grid=(DEPTH, n_out // tn),
        in_specs=[
            pl.BlockSpec((MOD_ROWS, D), lambda l, n: (0, 0)),
            pl.BlockSpec((None, D, tn), lambda l, n: (l, 0, n)),
            pl.BlockSpec((None, 1, tn), lambda l, n: (l, 0, n)),
        ],
        out_specs=pl.BlockSpec((None, MOD_ROWS, tn), lambda l, n: (l, 0, n)),
        out_shape=jax.ShapeDtypeStruct((DEPTH, MOD_ROWS, n_out), F32),
        compiler_params=pltpu.CompilerParams(
            dimension_semantics=("arbitrary", "arbitrary"), vmem_limit_bytes=VMEM_LIMIT),
        name="ada_mods",
    )(cs, ada_w, ada_b.reshape(DEPTH, 1, n_out))
    return out.reshape(DEPTH, MOD_ROWS, 9, D)


RPB_LANES = 128
RPB_HALF = RPB_LANES // 2


def _bias_kernel(rpb_ref, o_ref):
    qq = lax.broadcasted_iota(jnp.int32, (GRID_W, RPB_LANES), 0)
    lane = lax.broadcasted_iota(jnp.int32, (GRID_W, RPB_LANES), 1)
    second = lane >= GRID_W
    kk = jnp.where(second, lane - GRID_W, lane)
    col_start = jnp.clip(qq - WIN_COLS // 2, 0, GRID_W - WIN_COLS)
    visible = (kk >= col_start) & (kk < col_start + WIN_COLS)
    m = lax.broadcasted_iota(jnp.int32, (N_ROW_OFF, RPB_LANES), 1)
    diff = jnp.where(m < RPB_HALF, m, m - RPB_LANES)
    ext_idx = jnp.clip(diff + (WIN_COLS - 1), 0, N_COL_OFF - 1)
    for i in range(HEADS_PER_GROUP):
        ext = jnp.take_along_axis(rpb_ref[i], ext_idx, axis=1)
        for ro in range(N_ROW_OFF - 1):
            e0 = jnp.broadcast_to(ext[ro:ro + 1, :], (GRID_W, RPB_LANES))
            e1 = jnp.broadcast_to(ext[ro + 1:ro + 2, :], (GRID_W, RPB_LANES))
            t = jnp.where(second,
                          pltpu.roll(e1, GRID_W, 1, stride=1, stride_axis=0),
                          pltpu.roll(e0, 0, 1, stride=1, stride_axis=0))
            t = jnp.where(visible, t, NEG).astype(o_ref.dtype)
            for a in range(0, WIN_ROWS, 2):
                delta_idx = ro - a
                if 0 <= delta_idx < N_DELTA:
                    o_ref[delta_idx, i * GRID_W:(i + 1) * GRID_W, a * GRID_W:(a + 2) * GRID_W] = t


def _bias_table(rpb):
    rpb_padded = jnp.pad(rpb, ((0, 0), (0, 0), (0, 0), (0, RPB_LANES - N_COL_OFF)))
    return pl.pallas_call(
        _bias_kernel,
        grid=(DEPTH, N_HEAD_GROUPS),
        in_specs=[pl.BlockSpec((None, HEADS_PER_GROUP, N_ROW_OFF, RPB_LANES),
                               lambda l, g: (l, g, 0, 0))],
        out_specs=pl.BlockSpec((None, None, N_DELTA, HEADS_PER_GROUP * GRID_W, WIN_ROWS * GRID_W),
                               lambda l, g: (l, g, 0, 0, 0)),
        out_shape=jax.ShapeDtypeStruct(
            (DEPTH, N_HEAD_GROUPS, N_DELTA, HEADS_PER_GROUP * GRID_W, WIN_ROWS * GRID_W), BF),
        compiler_params=pltpu.CompilerParams(
            dimension_semantics=("arbitrary", "arbitrary"), vmem_limit_bytes=VMEM_LIMIT),
        name="rpb_table",
    )(rpb_padded)


def _lat_tile(t, n_lat_tiles):
    return jnp.minimum(t, n_lat_tiles - 1)


def _ctx_tile(t, n_lat_tiles):
    return jnp.maximum(t - n_lat_tiles, 0)


def _ffn_kernel(*refs, l, sub, split_in, split_out, n_lat_tiles, n_cast, n_zero):
    refs = list(refs)
    x_refs = [refs.pop(0) for _ in range(2 if split_in else 1)]
    mod_ref, npre_ref, npost_ref, wi_hbm, wo_hbm = [refs.pop(0) for _ in range(5)]
    cast_in_refs = [refs.pop(0) for _ in range(n_cast)]
    o_refs = [refs.pop(0) for _ in range(2 if split_out else 1)]
    cast_out_refs = [refs.pop(0) for _ in range(n_cast)]
    zero_refs = [refs.pop(0) for _ in range(n_zero)]
    wg_ref, wu_ref, wo_ref, sg_ref, su_ref, so_ref, h_ref, acc_ref, act_ref, sem = refs

    mo = 0 if sub == 0 else 6
    ni = 0 if sub == 0 else 2
    t = pl.program_id(0)
    is_lat = t < n_lat_tiles

    for src, dst in zip(cast_in_refs, cast_out_refs):
        dst[...] = src[...].astype(BF)
    for z in zero_refs:
        z[...] = jnp.zeros(z.shape, z.dtype)

    def load_x(rows=slice(None)):
        if split_in:
            return jnp.where(is_lat, x_refs[0][rows, :], x_refs[1][rows, :])
        return x_refs[0][rows, :]

    def store(out, rows=slice(None)):
        if not split_out:
            o_refs[0][rows, :] = out
            return

        @pl.when(is_lat)
        def _():
            o_refs[0][rows, :] = out

        @pl.when(jnp.logical_not(is_lat))
        def _():
            o_refs[1][rows, :] = out

    def modulated(x):
        shift = mod_ref[mo:mo + 1, :]
        scale = mod_ref[mo + 1:mo + 2, :]
        return (_rms(x, npre_ref[ni:ni + 1, :]) * (1.0 + scale) + shift).astype(BF)

    def residual(x, acc):
        gate = mod_ref[mo + 2:mo + 3, :]
        return x + 0.5 * gate * _rms(acc, npost_ref[ni:ni + 1, :])

    def chunk_update(h, wg, wu, wo):
        g = _dot(h, wg)
        u = _dot(h, wu)
        a = (g * jax.nn.sigmoid(g) * u).astype(BF)
        return _dot(a, wo)

    def chunk_copies(j, slot):
        cols = pl.ds(pl.multiple_of(j * FF_CHUNK, FF_CHUNK), FF_CHUNK)
        up = pl.ds(pl.multiple_of(D_FF + j * FF_CHUNK, FF_CHUNK), FF_CHUNK)
        return (
            pltpu.make_async_copy(wi_hbm.at[l, sub, :, cols], sg_ref.at[slot], sem.at[0, slot]),
            pltpu.make_async_copy(wi_hbm.at[l, sub, :, up], su_ref.at[slot], sem.at[1, slot]),
            pltpu.make_async_copy(wo_hbm.at[l, sub, cols, :], so_ref.at[slot], sem.at[2, slot]),
        )

    @pl.when(t == 0)
    def _():
        for cp in chunk_copies(0, 0):
            cp.start()
        h_ref[...] = modulated(load_x())
        acc_ref[...] = jnp.zeros(acc_ref.shape, F32)

        def body(j, carry):
            slot = lax.rem(j, 2)

            @pl.when(j + 1 < N_FF_CHUNKS)
            def _():
                for cp in chunk_copies(j + 1, 1 - slot):
                    cp.start()

            for cp in chunk_copies(j, slot):
                cp.wait()
            wg = sg_ref[slot].astype(BF)
            wu = su_ref[slot].astype(BF)
            wo = so_ref[slot].astype(BF)
            wg_ref[j] = wg
            wu_ref[j] = wu
            wo_ref[pl.ds(pl.multiple_of(j * FF_CHUNK, FF_CHUNK), FF_CHUNK), :] = wo
            acc_ref[...] += chunk_update(h_ref[...], wg, wu, wo)
            return carry

        lax.fori_loop(0, N_FF_CHUNKS, body, 0)
        store(residual(load_x(), acc_ref[...]))

    @pl.when(t > 0)
    def _():
        def sub_tile(s, carry):
            rows = pl.ds(pl.multiple_of(s * FFN_SUBTILE, FFN_SUBTILE), FFN_SUBTILE)
            x = load_x(rows)
            h = modulated(x)
            for j in range(N_FF_CHUNKS):
                g = _dot(h, wg_ref[j])
                u = _dot(h, wu_ref[j])
                act_ref[:, j * FF_CHUNK:(j + 1) * FF_CHUNK] = (g * jax.nn.sigmoid(g) * u).astype(BF)
            acc = _dot(act_ref[...], wo_ref[...])
            store(residual(x, acc), rows)
            return carry

        n_sub = acc_ref.shape[0] // FFN_SUBTILE
        if n_sub == 1:
            sub_tile(0, 0)
        else:
            lax.fori_loop(0, n_sub, sub_tile, 0)


def _ffn(xs, mods, npre, npost, wi, wo, l, sub, split_out=False, cast_weights=(), zero_shapes=()):
    split_in = len(xs) == 2
    tm = FFN_SUBTILE if (split_in or split_out) else FFN_TILE
    n_tiles = N_TOK // tm
    nl = N_LAT // tm
    lat_idx = lambda t: (_lat_tile(t, nl), 0)
    ctx_idx = lambda t: (_ctx_tile(t, nl), 0)
    split_specs = [pl.BlockSpec((tm, D), lat_idx), pl.BlockSpec((tm, D), ctx_idx)]
    joint_specs = [pl.BlockSpec((tm, D), lambda t: (t, 0))]
    split_shapes = [jax.ShapeDtypeStruct((N_LAT, D), F32), jax.ShapeDtypeStruct((N_CTX, D), F32)]
    joint_shapes = [jax.ShapeDtypeStruct((N_TOK, D), F32)]
    cast_in_specs = [pl.BlockSpec((None, w.shape[1] // n_tiles, w.shape[2]), lambda t: (l, t, 0))
                     for w in cast_weights]
    cast_out_specs = [pl.BlockSpec((w.shape[1] // n_tiles, w.shape[2]), lambda t: (t, 0))
                      for w in cast_weights]
    cast_out_shapes = [jax.ShapeDtypeStruct(w.shape[1:], BF) for w in cast_weights]
    zero_specs = [pl.BlockSpec((s[0] // n_tiles,) + tuple(s[1:]),
                               lambda t, nd=len(s): (t,) + (0,) * (nd - 1)) for s in zero_shapes]
    zero_out_shapes = [jax.ShapeDtypeStruct(tuple(s), F32) for s in zero_shapes]
    n_act = 2 if split_out else 1
    out = pl.pallas_call(
        functools.partial(_ffn_kernel, l=l, sub=sub, split_in=split_in, split_out=split_out,
                          n_lat_tiles=nl, n_cast=len(cast_weights), n_zero=len(zero_shapes)),
        grid=(n_tiles,),
        in_specs=(split_specs if split_in else joint_specs) + [
            pl.BlockSpec((None, None, 9, D), lambda t: (l, _mod_row(t, tm), 0, 0)),
            pl.BlockSpec((None, 3, D), lambda t: (l, 0, 0)),
            pl.BlockSpec((None, 3, D), lambda t: (l, 0, 0)),
            pl.BlockSpec(memory_space=pl.ANY),
            pl.BlockSpec(memory_space=pl.ANY),
        ] + cast_in_specs,
        out_specs=(split_specs if split_out else joint_specs) + cast_out_specs + zero_specs,
        out_shape=(split_shapes if split_out else joint_shapes) + cast_out_shapes + zero_out_shapes,
        scratch_shapes=[
            pltpu.VMEM((N_FF_CHUNKS, D, FF_CHUNK), BF),
            pltpu.VMEM((N_FF_CHUNKS, D, FF_CHUNK), BF),
            pltpu.VMEM((D_FF, D), BF),
            pltpu.VMEM((2, D, FF_CHUNK), F32),
            pltpu.VMEM((2, D, FF_CHUNK), F32),
            pltpu.VMEM((2, FF_CHUNK, D), F32),
            pltpu.VMEM((tm, D), BF),
            pltpu.VMEM((tm, D), F32),
            pltpu.VMEM((FFN_SUBTILE, D_FF), BF),
            pltpu.SemaphoreType.DMA((3, 2)),
        ],
        compiler_params=pltpu.CompilerParams(
            dimension_semantics=("arbitrary",), vmem_limit_bytes=VMEM_LIMIT),
        name=f"ffn_l{l}_s{sub}",
    )(*xs, mods, npre, npost, wi, wo, *cast_weights)
    acts = tuple(out[:n_act]) if split_out else out[0]
    n_cast = len(cast_weights)
    return acts, tuple(out[n_act:n_act + n_cast]), tuple(out[n_act + n_cast:])


def _proj_kernel(x_ref, mod_ref, npre_ref, w_ref, *rest, n_carried):
    qkv_ref, f_ref, g_ref, k_ref, v_ref = rest[n_carried:]
    x = x_ref[...]
    shift = mod_ref[3:4, :]
    scale = mod_ref[4:5, :]
    h = (_rms(x, npre_ref[1:2, :]) * (1.0 + scale) + shift).astype(BF)
    z = _dot(h, w_ref[...])
    qkv_ref[...] = z[:, :3 * WA].astype(BF)
    f_ref[...] = z[:, 3 * WA:3 * WA + WB].astype(BF)
    g_ref[...] = z[:, 3 * WA + WB:].astype(BF)
    for i in range(k_ref.shape[0]):
        rows = slice(i * CTX_SEQ, (i + 1) * CTX_SEQ)
        k_ref[i] = z[rows, WA:2 * WA].T
        v_ref[i] = z[rows, 2 * WA:3 * WA].T


def _proj(x, mods, npre, w_in, carried, l, tm=1024):
    n_tiles = N_TOK // tm
    nl = N_LAT // tm
    bpt = tm // CTX_SEQ
    kv_shape = jax.ShapeDtypeStruct((N_CTX_B, DEPTH, WA, CTX_SEQ), F32)
    kv_spec = pl.BlockSpec((bpt, None, WA, CTX_SEQ), lambda t: (_ctx_tile(t, nl), l, 0, 0))
    n_fixed = 4
    return pl.pallas_call(
        functools.partial(_proj_kernel, n_carried=len(carried)),
        grid=(n_tiles,),
        in_specs=[
            pl.BlockSpec((tm, D), lambda t: (t, 0)),
            pl.BlockSpec((None, None, 9, D), lambda t: (l, _mod_row(t, tm), 0, 0)),
            pl.BlockSpec((None, 3, D), lambda t: (l, 0, 0)),
            pl.BlockSpec((D, PROJ_W), lambda t: (0, 0), pipeline_mode=pl.Buffered(1)),
        ] + [pl.BlockSpec(memory_space=pl.ANY)] * len(carried),
        out_specs=[
            pl.BlockSpec((tm, 3 * WA), lambda t: (t, 0)),
            pl.BlockSpec((tm, WB), lambda t: (t, 0)),
            pl.BlockSpec((tm, 2 * WC), lambda t: (t, 0)),
            kv_spec,
            kv_spec,
        ],
        out_shape=[
            jax.ShapeDtypeStruct((N_TOK, 3 * WA), BF),
            jax.ShapeDtypeStruct((N_TOK, WB), BF),
            jax.ShapeDtypeStruct((N_TOK, 2 * WC), BF),
            kv_shape,
            kv_shape,
        ],
        input_output_aliases={n_fixed + i: 3 + i for i in range(len(carried))},
        compiler_params=pltpu.CompilerParams(
            dimension_semantics=("arbitrary",), vmem_limit_bytes=VMEM_LIMIT),
        name=f"proj_l{l}",
    )(x, mods, npre, w_in, *carried)


def _fourier(f, fc_ref, fn_ref):
    n = fn_ref.shape[0]
    xcs = _dot(f, fc_ref[...]).astype(BF)
    outs = []
    for b in range(f.shape[0] // n):
        xb = xcs[b * n:(b + 1) * n]
        stacked = jnp.concatenate([xb[:, :WB], xb[:, WB:]], axis=0)
        outs.append(_dot(fn_ref[...], stacked))
    return outs[0] if len(outs) == 1 else jnp.concatenate(outs, axis=0)


def _gmlp(g, gn_ref, ws_ref, gb_ref, avg_ref):
    gl = jax.nn.gelu(g.astype(F32), approximate=True)
    u = gl[:, :WC]
    v = gl[:, WC:]
    avg = avg_ref[...]
    mu = _split2_dot(v, avg)
    d = v - mu
    var = _dot((d * d).astype(BF), avg)
    vn = (d * lax.rsqrt(var + EPS) * gn_ref[...]).astype(BF)
    lane_group = lax.broadcasted_iota(jnp.int32, (CHUNK, WC), 1) // GC
    outs = []
    for c in range(g.shape[0] // CHUNK):
        vc = vn[c * CHUNK:(c + 1) * CHUNK, :]
        sp = _dot(ws_ref[0].astype(BF), vc)
        for grp in range(1, NGRP):
            sp = jnp.where(lane_group == grp, _dot(ws_ref[grp].astype(BF), vc), sp)
        outs.append(u[c * CHUNK:(c + 1) * CHUNK, :] * (sp + gb_ref[...]))
    return jnp.concatenate(outs, axis=0)


def _mix_out(x, o, gate, npost, wo_ref):
    y = _dot(o, wo_ref[...])
    return x + gate * _rms(y, npost)


def _ctx_mixer_kernel(x_ref, qkv_ref, f_ref, g_ref, mod_ref, npost_ref, wo_ref, fc_ref, fn_ref,
                      gn_ref, ws_ref, gb_ref, avg_ref, o_ref, mix_ref):
    gw = HEADS_PER_GROUP * HD
    pair_w = 2 * HD
    first_of_pair = lax.broadcasted_iota(jnp.int32, (CTX_SEQ, pair_w), 1) < HD
    own = (lax.broadcasted_iota(jnp.int32, (HEADS_PER_GROUP * CTX_SEQ, gw), 0) // CTX_SEQ
           == lax.broadcasted_iota(jnp.int32, (HEADS_PER_GROUP * CTX_SEQ, gw), 1) // HD)
    lane_head = lax.broadcasted_iota(jnp.int32, (CTX_SEQ, gw), 1) // HD
    zero = jnp.zeros((), BF)
    for b in range(x_ref.shape[0] // CTX_SEQ):
        rows = slice(b * CTX_SEQ, (b + 1) * CTX_SEQ)
        scores = []
        for h in range(NH):
            lanes = slice((h // 2) * pair_w, (h // 2 + 1) * pair_w)
            q2 = qkv_ref[rows, lanes] * QK_SCALE
            k2 = qkv_ref[rows, WA + (h // 2) * pair_w:WA + (h // 2 + 1) * pair_w]
            qh = jnp.where(first_of_pair, q2, zero) if h % 2 == 0 else jnp.where(first_of_pair, zero, q2)
            scores.append(_dot_nt(qh, k2))
        probs, dens = [], []
        for s in scores:
            p = jnp.exp(s - jnp.max(s, axis=-1, keepdims=True))
            dens.append(jnp.sum(p, axis=-1, keepdims=True))
            probs.append(p.astype(BF))
        for grp in range(N_HEAD_GROUPS):
            hs = range(grp * HEADS_PER_GROUP, (grp + 1) * HEADS_PER_GROUP)
            v4 = qkv_ref[rows, 2 * WA + grp * gw:2 * WA + (grp + 1) * gw]
            vbd = jnp.where(own, jnp.concatenate([v4] * HEADS_PER_GROUP, axis=0), zero)
            o = _dot(jnp.concatenate([probs[h] for h in hs], axis=1), vbd)
            den = dens[hs[0]]
            for i in range(1, HEADS_PER_GROUP):
                den = jnp.where(lane_head == i, dens[hs[i]], den)
            mix_ref[rows, grp * gw:(grp + 1) * gw] = (o / den).astype(BF)
    mix_ref[:, WA:WA + WB] = _fourier(f_ref[...], fc_ref, fn_ref).astype(BF)
    mix_ref[:, WA + WB:] = _gmlp(g_ref[...], gn_ref, ws_ref, gb_ref, avg_ref).astype(BF)
    o_ref[...] = _mix_out(x_ref[...], mix_ref[...], mod_ref[5:6, :], npost_ref[1:2, :], wo_ref)


def _const_spec(shape):
    nd = len(shape)
    return pl.BlockSpec(shape, lambda *_: (0,) * nd)


def _ctx_mixer(x, qkv, f, g, mods, npost, w_out, fc, fn, gn, ws, gb, avg, l):
    tm = CTX_PER_STEP * CTX_SEQ
    off = N_LAT // tm
    row = lambda b: (b + off, 0)
    return pl.pallas_call(
        _ctx_mixer_kernel,
        grid=(N_CTX_B // CTX_PER_STEP,),
        in_specs=[
            pl.BlockSpec((tm, D), row),
            pl.BlockSpec((tm, 3 * WA), row),
            pl.BlockSpec((tm, WB), row),
            pl.BlockSpec((tm, 2 * WC), row),
            pl.BlockSpec((None, None, 9, D), lambda b: (l, CTX_MOD_ROW, 0, 0)),
            pl.BlockSpec((None, 3, D), lambda b: (l, 0, 0)),
            pl.BlockSpec((D, D), lambda b: (0, 0), pipeline_mode=pl.Buffered(1)),
            _const_spec(fc.shape),
            _const_spec(fn.shape),
            pl.BlockSpec((None, 1, WC), lambda b: (l, 0, 0)),
            pl.BlockSpec((None, NGRP, CHUNK, CHUNK), lambda b: (l, 0, 0, 0)),
            pl.BlockSpec((None, CHUNK, WC), lambda b: (l, 0, 0)),
            _const_spec(avg.shape),
        ],
        out_specs=pl.BlockSpec((tm, D), row),
        out_shape=jax.ShapeDtypeStruct((N_TOK, D), F32),
        input_output_aliases={0: 0},
        scratch_shapes=[pltpu.VMEM((tm, D), BF)],
        compiler_params=pltpu.CompilerParams(
            dimension_semantics=("arbitrary",), vmem_limit_bytes=VMEM_LIMIT),
        name=f"ctx_mixer_l{l}",
    )(x, qkv, f, g, mods, npost, w_out, fc, fn, gn, ws, gb, avg)


def _window_start(r):
    return min(max(r - WIN_ROWS // 2, 0), GRID_ROWS - WIN_ROWS)


def _nbr_attn_kernel(qkv_ref, ck_ref, cv_ref, bias_ref, o_ref):
    gw = HEADS_PER_GROUP * HD
    row_head = lax.broadcasted_iota(jnp.int32, (HEADS_PER_GROUP * GRID_W, gw), 0) // GRID_W
    lane_head = lax.broadcasted_iota(jnp.int32, (HEADS_PER_GROUP * GRID_W, gw), 1) // HD
    own = row_head == lane_head
    for hg in range(N_HEAD_GROUPS):
        ck = ck_ref[hg * gw:(hg + 1) * gw, :].astype(BF)
        cv = cv_ref[hg * gw:(hg + 1) * gw, :].astype(BF)
        for r in range(GRID_ROWS):
            ws = _window_start(r)
            q = qkv_ref[r * GRID_W:(r + 1) * GRID_W, hg * gw:(hg + 1) * gw] * QK_SCALE
            qs = jnp.where(own, jnp.concatenate([q] * HEADS_PER_GROUP, axis=0), jnp.zeros((), BF))
            kw = qkv_ref[ws * GRID_W:(ws + WIN_ROWS) * GRID_W, WA + hg * gw:WA + (hg + 1) * gw]
            vw = qkv_ref[ws * GRID_W:(ws + WIN_ROWS) * GRID_W, 2 * WA + hg * gw:2 * WA + (hg + 1) * gw]
            s_loc = _dot_nt(qs, kw) + bias_ref[hg, ws - r + N_DELTA - 1].astype(F32)
            s_ctx = _dot(qs, ck)
            m = jnp.maximum(jnp.max(s_loc, axis=-1, keepdims=True),
                            jnp.max(s_ctx, axis=-1, keepdims=True))
            p_loc = jnp.exp(s_loc - m)
            p_ctx = jnp.exp(s_ctx - m)
            den = jnp.sum(p_loc, axis=-1, keepdims=True) + jnp.sum(p_ctx, axis=-1, keepdims=True)
            o = (_dot(p_loc.astype(BF), vw) + _dot_nt(p_ctx.astype(BF), cv)) / den
            o = jnp.where(own, o, 0.0)
            o = (o[0:GRID_W] + o[GRID_W:2 * GRID_W]) + (o[2 * GRID_W:3 * GRID_W] + o[3 * GRID_W:])
            o_ref[r * GRID_W:(r + 1) * GRID_W, hg * gw:(hg + 1) * gw] = o.astype(BF)


def _nbr_attn(qkv, cache_k, cache_v, bias, l):
    return pl.pallas_call(
        _nbr_attn_kernel,
        grid=(N_LAT_B,),
        in_specs=[
            pl.BlockSpec((LAT_SEQ, 3 * WA), lambda b: (b, 0)),
            pl.BlockSpec((None, None, WA, PAST), lambda b: (b, l, 0, 0)),
            pl.BlockSpec((None, None, WA, PAST), lambda b: (b, l, 0, 0)),
            pl.BlockSpec((None,) + bias.shape[1:], lambda b: (l, 0, 0, 0, 0)),
        ],
        out_specs=pl.BlockSpec((LAT_SEQ, WA), lambda b: (b, 0)),
        out_shape=jax.ShapeDtypeStruct((N_LAT, WA), BF),
        compiler_params=pltpu.CompilerParams(
            dimension_semantics=("arbitrary",), vmem_limit_bytes=VMEM_LIMIT),
        name=f"nbr_attn_l{l}",
    )(qkv, cache_k, cache_v, bias)


def _lat_mixer_kernel(x_ref, oa_ref, f_ref, g_ref, mod_ref, npost_ref, wo_ref, fc_ref, fn_ref,
                      gn_ref, ws_ref, gb_ref, avg_ref, o_ref, mix_ref):
    mix_ref[:, :WA] = oa_ref[...]
    mix_ref[:, WA:WA + WB] = _fourier(f_ref[...], fc_ref, fn_ref).astype(BF)
    mix_ref[:, WA + WB:] = _gmlp(g_ref[...], gn_ref, ws_ref, gb_ref, avg_ref).astype(BF)
    o_ref[...] = _mix_out(x_ref[...], mix_ref[...], mod_ref[5:6, :], npost_ref[1:2, :], wo_ref)


def _lat_mixer(x, oa, f, g, mods, npost, w_out, fc, fn, gn, ws, gb, avg, l):
    row = lambda b: (b, 0)
    return pl.pallas_call(
        _lat_mixer_kernel,
        grid=(N_LAT_B,),
        in_specs=[
            pl.BlockSpec((LAT_SEQ, D), row),
            pl.BlockSpec((LAT_SEQ, WA), row),
            pl.BlockSpec((LAT_SEQ, WB), row),
            pl.BlockSpec((LAT_SEQ, 2 * WC), row),
            pl.BlockSpec((None, None, 9, D), lambda b: (l, b, 0, 0)),
            pl.BlockSpec((None, 3, D), lambda b: (l, 0, 0)),
            pl.BlockSpec((D, D), lambda b: (0, 0), pipeline_mode=pl.Buffered(1)),
            _const_spec(fc.shape),
            pl.BlockSpec(fn.shape, lambda b: (0, 0), pipeline_mode=pl.Buffered(1)),
            pl.BlockSpec((None, 1, WC), lambda b: (l, 0, 0)),
            pl.BlockSpec((None, NGRP, CHUNK, CHUNK), lambda b: (l, 0, 0, 0)),
            pl.BlockSpec((None, CHUNK, WC), lambda b: (l, 0, 0)),
            _const_spec(avg.shape),
        ],
        out_specs=pl.BlockSpec((LAT_SEQ, D), row),
        out_shape=jax.ShapeDtypeStruct((N_TOK, D), F32),
        input_output_aliases={0: 0},
        scratch_shapes=[pltpu.VMEM((LAT_SEQ, D), BF)],
        compiler_params=pltpu.CompilerParams(
            dimension_semantics=("arbitrary",), vmem_limit_bytes=VMEM_LIMIT),
        name=f"lat_mixer_l{l}",
    )(x, oa, f, g, mods, npost, w_out, fc, fn, gn, ws, gb, avg)


def _dft_tables(n):
    idx = np.arange(n)
    ang = 2.0 * np.pi * ((idx[:, None] * idx[None, :]) % n) / n
    return (np.concatenate([np.cos(ang), -np.sin(ang)], axis=1) / np.sqrt(n)).astype(np.float32)


def _channel_dft_table():
    idx = np.arange(GC)
    ang = 2.0 * np.pi * ((idx[:, None] * idx[None, :]) % GC) / GC
    eye = np.eye(NGRP)
    return (np.concatenate([np.kron(eye, np.cos(ang)), np.kron(eye, np.sin(ang))], axis=1)
            / np.sqrt(GC)).astype(np.float32)


def _group_average_table():
    return np.kron(np.eye(NGRP), np.full((GC, GC), 1.0 / GC)).astype(np.float32)


def kernel(x_prompt, x_sample, cache_k, cache_v, c, c_ctx, ada_w, ada_b, norm_pre, norm_post,
           ffn_w_in, ffn_w_out, w_in, w_out, rpb, gmlp_norm, gmlp_w, gmlp_b):
    xs = (x_sample.reshape(N_LAT, D), x_prompt.reshape(N_CTX, D))
    cs = jnp.concatenate([c, c_ctx[None, :], jnp.zeros((MOD_ROWS - N_LAT_B - 1, D), F32)], axis=0)
    mods = _ada_mods(cs, ada_w, ada_b)
    bias = _bias_table(rpb)

    gn = gmlp_norm.reshape(DEPTH, 1, WC)
    gb = jnp.repeat(jnp.transpose(gmlp_b, (0, 2, 1)), GC, axis=2)
    ck = jnp.swapaxes(cache_k.reshape(N_LAT_B, DEPTH, PAST, WA), 2, 3)
    cv = jnp.swapaxes(cache_v.reshape(N_LAT_B, DEPTH, PAST, WA), 2, 3)

    fc = jnp.asarray(_channel_dft_table()).astype(BF)
    fn_ctx = jnp.asarray(_dft_tables(CTX_SEQ)).astype(BF)
    fn_lat = jnp.asarray(_dft_tables(LAT_SEQ)).astype(BF)
    avg = jnp.asarray(_group_average_table()).astype(BF)

    kv_carrier = (N_CTX_B, DEPTH, WA, CTX_SEQ)
    carried = None
    for l in range(DEPTH):
        x, (w_in_b, w_out_b), zeros = _ffn(
            xs, mods, norm_pre, norm_post, ffn_w_in, ffn_w_out, l, 0,
            cast_weights=(w_in, w_out), zero_shapes=(kv_carrier,) * 2 if l == 0 else ())
        if l == 0:
            carried = zeros
        qkv, f, g, new_k, new_v = _proj(x, mods, norm_pre, w_in_b, carried, l)
        carried = (new_k, new_v)
        oa = _nbr_attn(qkv, ck, cv, bias, l)
        x = _ctx_mixer(x, qkv, f, g, mods, norm_post, w_out_b, fc, fn_ctx, gn, gmlp_w, gb, avg, l)
        x = _lat_mixer(x, oa, f, g, mods, norm_post, w_out_b, fc, fn_lat, gn, gmlp_w, gb, avg, l)
        xs, _, _ = _ffn((x,), mods, norm_pre, norm_post, ffn_w_in, ffn_w_out, l, 1,
                        split_out=(l == DEPTH - 1))
        if l < DEPTH - 1:
            xs = (xs,)
    xl, xc = xs

    kv_out = (N_CTX_B, DEPTH, CTX_SEQ, NH, HD)
    return (xc.reshape(N_CTX_B, CTX_SEQ, D), xl.reshape(N_LAT_B, LAT_SEQ, D),
            jnp.swapaxes(new_k, 2, 3).reshape(kv_out), jnp.swapaxes(new_v, 2, 3).reshape(kv_out))
```

```python
import functools

import numpy as np
import jax
import jax.numpy as jnp
from jax import lax
from jax.experimental import pallas as pl
from jax.experimental.pallas import tpu as pltpu

D = 1024
DEPTH = 4
N_CTX_B = 16
CTX_SEQ = 256
N_LAT_B = 4
LAT_SEQ = 1024
N_LAT = N_LAT_B * LAT_SEQ
N_CTX = N_CTX_B * CTX_SEQ
N_TOK = N_LAT + N_CTX
PAST = 256
GRID_W = 64
GRID_ROWS = LAT_SEQ // GRID_W
HD = 64
NH = 8
WA = 512
WB = 256
WC = 256
NGRP = 4
GC = 64
CHUNK = 128
PROJ_W = 3 * WA + WB + 2 * WC
D_FF = 2816
FF_CHUNK = 256
N_FF_CHUNKS = D_FF // FF_CHUNK
FFN_TILE = 1024
FFN_SUBTILE = 512
WIN_ROWS = 8
WIN_COLS = 16
N_ROW_OFF = 2 * WIN_ROWS - 1
N_COL_OFF = 2 * WIN_COLS - 1
N_DELTA = 8
EPS = 1e-6
NEG = -1e30
ADA_TILE = 2304
MOD_ROWS = 8
CTX_MOD_ROW = 4
CTX_PER_STEP = 4
HEADS_PER_GROUP = 4
N_HEAD_GROUPS = NH // HEADS_PER_GROUP

BF = jnp.bfloat16
F32 = jnp.float32
QK_SCALE = HD ** -0.5

VMEM_LIMIT = 56 * 1024 * 1024


def _dot(a, b):
    return jnp.dot(a, b, preferred_element_type=F32)


def _dot_nt(a, b):
    return lax.dot_general(a, b, (((1,), (1,)), ((), ())), preferred_element_type=F32)


def _split2_dot(a, b):
    a0 = a.astype(BF)
    a1 = (a - a0.astype(F32)).astype(BF)
    return _dot(a0, b) + _dot(a1, b)


def _rms(x, g):
    return x * lax.rsqrt(jnp.mean(x * x, axis=-1, keepdims=True) + EPS) * g


def _mod_row(tile, tm):
    return jnp.minimum((tile * tm) // LAT_SEQ, CTX_MOD_ROW)


def _ada_kernel(cs_ref, w_ref, b_ref, o_ref):
    cs = cs_ref[...]
    s = cs * jax.nn.sigmoid(cs)
    o_ref[...] = _dot(s.astype(BF), w_ref[...].astype(BF)) + b_ref[...]


def _ada_mods(cs, ada_w, ada_b):
    n_out = 9 * D
    tn = ADA_TILE
    out = pl.pallas_call(
        _ada_kernel,
        grid=(DEPTH, n_out // tn),
        in_specs=[
            pl.BlockSpec((MOD_ROWS, D), lambda l, n: (0, 0)),
            pl.BlockSpec((None, D, tn), lambda l, n: (l, 0, n)),
            pl.BlockSpec((None, 1, tn), lambda l, n: (l, 0, n)),
        ],
        out_specs=pl.BlockSpec((None, MOD_ROWS, tn), lambda l, n: (l, 0, n)),
        out_shape=jax.ShapeDtypeStruct((DEPTH, MOD_ROWS, n_out), F32),
        compiler_params=pltpu.CompilerParams(
            dimension_semantics=("arbitrary", "arbitrary"), vmem_limit_bytes=VMEM_LIMIT),
        name="ada_mods",
    )(cs, ada_w, ada_b.reshape(DEPTH, 1, n_out))
    return out.reshape(DEPTH, MOD_ROWS, 9, D)


RPB_LANES = 128
RPB_HALF = RPB_LANES // 2


def _bias_kernel(rpb_ref, o_ref):
    qq = lax.broadcasted_iota(jnp.int32, (GRID_W, RPB_LANES), 0)
    lane = lax.broadcasted_iota(jnp.int32, (GRID_W, RPB_LANES), 1)
    second = lane >= GRID_W
    kk = jnp.where(second, lane - GRID_W, lane)
    col_start = jnp.clip(qq - WIN_COLS // 2, 0, GRID_W - WIN_COLS)
    visible = (kk >= col_start) & (kk < col_start + WIN_COLS)
    m = lax.broadcasted_iota(jnp.int32, (N_ROW_OFF, RPB_LANES), 1)
    diff = jnp.where(m < RPB_HALF, m, m - RPB_LANES)
    ext_idx = jnp.clip(diff + (WIN_COLS - 1), 0, N_COL_OFF - 1)
    for i in range(HEADS_PER_GROUP):
        ext = jnp.take_along_axis(rpb_ref[i], ext_idx, axis=1)
        for ro in range(N_ROW_OFF - 1):
            e0 = jnp.broadcast_to(ext[ro:ro + 1, :], (GRID_W, RPB_LANES))
            e1 = jnp.broadcast_to(ext[ro + 1:ro + 2, :], (GRID_W, RPB_LANES))
            t = jnp.where(second,
                          pltpu.roll(e1, GRID_W, 1, stride=1, stride_axis=0),
                          pltpu.roll(e0, 0, 1, stride=1, stride_axis=0))
            t = jnp.where(visible, t, NEG).astype(o_ref.dtype)
            for a in range(0, WIN_ROWS, 2):
                delta_idx = ro - a
                if 0 <= delta_idx < N_DELTA:
                    o_ref[delta_idx, i * GRID_W:(i + 1) * GRID_W, a * GRID_W:(a + 2) * GRID_W] = t


def _bias_table(rpb):
    rpb_padded = jnp.pad(rpb, ((0, 0), (0, 0), (0, 0), (0, RPB_LANES - N_COL_OFF)))
    return pl.pallas_call(
        _bias_kernel,
        grid=(DEPTH, N_HEAD_GROUPS),
        in_specs=[pl.BlockSpec((None, HEADS_PER_GROUP, N_ROW_OFF, RPB_LANES),
                               lambda l, g: (l, g, 0, 0))],
        out_specs=pl.BlockSpec((None, None, N_DELTA, HEADS_PER_GROUP * GRID_W, WIN_ROWS * GRID_W),
                               lambda l, g: (l, g, 0, 0, 0)),
        out_shape=jax.ShapeDtypeStruct(
            (DEPTH, N_HEAD_GROUPS, N_DELTA, HEADS_PER_GROUP * GRID_W, WIN_ROWS * GRID_W), BF),
        compiler_params=pltpu.CompilerParams(
            dimension_semantics=("arbitrary", "arbitrary"), vmem_limit_bytes=VMEM_LIMIT),
        name="rpb_table",
    )(rpb_padded)


def _lat_tile(t, n_lat_tiles):
    return jnp.minimum(t, n_lat_tiles - 1)


def _ctx_tile(t, n_lat_tiles):
    return jnp.maximum(t - n_lat_tiles, 0)


def _ffn_kernel(*refs, l, sub, split_in, split_out, n_lat_tiles, n_cast, n_zero):
    refs = list(refs)
    x_refs = [refs.pop(0) for _ in range(2 if split_in else 1)]
    mod_ref, npre_ref, npost_ref, wi_hbm, wo_hbm = [refs.pop(0) for _ in range(5)]
    cast_in_refs = [refs.pop(0) for _ in range(n_cast)]
    o_refs = [refs.pop(0) for _ in range(2 if split_out else 1)]
    cast_out_refs = [refs.pop(0) for _ in range(n_cast)]
    zero_refs = [refs.pop(0) for _ in range(n_zero)]
    wg_ref, wu_ref, wo_ref, sg_ref, su_ref, so_ref, h_ref, acc_ref, act_ref, sem = refs

    mo = 0 if sub == 0 else 6
    ni = 0 if sub == 0 else 2
    t = pl.program_id(0)
    is_lat = t < n_lat_tiles

    for src, dst in zip(cast_in_refs, cast_out_refs):
        dst[...] = src[...].astype(BF)
    for z in zero_refs:
        z[...] = jnp.zeros(z.shape, z.dtype)

    def load_x(rows=slice(None)):
        if split_in:
            return jnp.where(is_lat, x_refs[0][rows, :], x_refs[1][rows, :])
        return x_refs[0][rows, :]

    def store(out, rows=slice(None)):
        if not split_out:
            o_refs[0][rows, :] = out
            return

        @pl.when(is_lat)
        def _():
            o_refs[0][rows, :] = out

        @pl.when(jnp.logical_not(is_lat))
        def _():
            o_refs[1][rows, :] = out

    def modulated(x):
        shift = mod_ref[mo:mo + 1, :]
        scale = mod_ref[mo + 1:mo + 2, :]
        return (_rms(x, npre_ref[ni:ni + 1, :]) * (1.0 + scale) + shift).astype(BF)

    def residual(x, acc):
        gate = mod_ref[mo + 2:mo + 3, :]
        return x + 0.5 * gate * _rms(acc, npost_ref[ni:ni + 1, :])

    def chunk_update(h, wg, wu, wo):
        g = _dot(h, wg)
        u = _dot(h, wu)
        a = (g * jax.nn.sigmoid(g) * u).astype(BF)
        return _dot(a, wo)

    def chunk_copies(j, slot):
        cols = pl.ds(pl.multiple_of(j * FF_CHUNK, FF_CHUNK), FF_CHUNK)
        up = pl.ds(pl.multiple_of(D_FF + j * FF_CHUNK, FF_CHUNK), FF_CHUNK)
        return (
            pltpu.make_async_copy(wi_hbm.at[l, sub, :, cols], sg_ref.at[slot], sem.at[0, slot]),
            pltpu.make_async_copy(wi_hbm.at[l, sub, :, up], su_ref.at[slot], sem.at[1, slot]),
            pltpu.make_async_copy(wo_hbm.at[l, sub, cols, :], so_ref.at[slot], sem.at[2, slot]),
        )

    @pl.when(t == 0)
    def _():
        for cp in chunk_copies(0, 0):
            cp.start()
        h_ref[...] = modulated(load_x())
        acc_ref[...] = jnp.zeros(acc_ref.shape, F32)

        def body(j, carry):
            slot = lax.rem(j, 2)

            @pl.when(j + 1 < N_FF_CHUNKS)
            def _():
                for cp in chunk_copies(j + 1, 1 - slot):
                    cp.start()

            for cp in chunk_copies(j, slot):
                cp.wait()
            wg = sg_ref[slot].astype(BF)
            wu = su_ref[slot].astype(BF)
            wo = so_ref[slot].astype(BF)
            wg_ref[j] = wg
            wu_ref[j] = wu
            wo_ref[pl.ds(pl.multiple_of(j * FF_CHUNK, FF_CHUNK), FF_CHUNK), :] = wo
            acc_ref[...] += chunk_update(h_ref[...], wg, wu, wo)
            return carry

        lax.fori_loop(0, N_FF_CHUNKS, body, 0)
        store(residual(load_x(), acc_ref[...]))

    @pl.when(t > 0)
    def _():
        def sub_tile(s, carry):
            rows = pl.ds(pl.multiple_of(s * FFN_SUBTILE, FFN_SUBTILE), FFN_SUBTILE)
            x = load_x(rows)
            h = modulated(x)
            for j in range(N_FF_CHUNKS):
                g = _dot(h, wg_ref[j])
                u = _dot(h, wu_ref[j])
                act_ref[:, j * FF_CHUNK:(j + 1) * FF_CHUNK] = (g * jax.nn.sigmoid(g) * u).astype(BF)
            acc = _dot(act_ref[...], wo_ref[...])
            store(residual(x, acc), rows)
            return carry

        n_sub = acc_ref.shape[0] // FFN_SUBTILE
        if n_sub == 1:
            sub_tile(0, 0)
        else:
            lax.fori_loop(0, n_sub, sub_tile, 0)


def _ffn(xs, mods, npre, npost, wi, wo, l, sub, split_out=False, cast_weights=(), zero_shapes=()):
    split_in = len(xs) == 2
    tm = FFN_SUBTILE if (split_in or split_out) else FFN_TILE
    n_tiles = N_TOK // tm
    nl = N_LAT // tm
    lat_idx = lambda t: (_lat_tile(t, nl), 0)
    ctx_idx = lambda t: (_ctx_tile(t, nl), 0)
    split_specs = [pl.BlockSpec((tm, D), lat_idx), pl.BlockSpec((tm, D), ctx_idx)]
    joint_specs = [pl.BlockSpec((tm, D), lambda t: (t, 0))]
    split_shapes = [jax.ShapeDtypeStruct((N_LAT, D), F32), jax.ShapeDtypeStruct((N_CTX, D), F32)]
    joint_shapes = [jax.ShapeDtypeStruct((N_TOK, D), F32)]
    cast_in_specs = [pl.BlockSpec((None, w.shape[1] // n_tiles, w.shape[2]), lambda t: (l, t, 0))
                     for w in cast_weights]
    cast_out_specs = [pl.BlockSpec((w.shape[1] // n_tiles, w.shape[2]), lambda t: (t, 0))
                      for w in cast_weights]
    cast_out_shapes = [jax.ShapeDtypeStruct(w.shape[1:], BF) for w in cast_weights]
    zero_specs = [pl.BlockSpec((s[0] // n_tiles,) + tuple(s[1:]),
                               lambda t, nd=len(s): (t,) + (0,) * (nd - 1)) for s in zero_shapes]
    zero_out_shapes = [jax.ShapeDtypeStruct(tuple(s), F32) for s in zero_shapes]
    n_act = 2 if split_out else 1
    out = pl.pallas_call(
        functools.partial(_ffn_kernel, l=l, sub=sub, split_in=split_in, split_out=split_out,
                          n_lat_tiles=nl, n_cast=len(cast_weights), n_zero=len(zero_shapes)),
        grid=(n_tiles,),
        in_specs=(split_specs if split_in else joint_specs) + [
            pl.BlockSpec((None, None, 9, D), lambda t: (l, _mod_row(t, tm), 0, 0)),
            pl.BlockSpec((None, 3, D), lambda t: (l, 0, 0)),
            pl.BlockSpec((None, 3, D), lambda t: (l, 0, 0)),
            pl.BlockSpec(memory_space=pl.ANY),
            pl.BlockSpec(memory_space=pl.ANY),
        ] + cast_in_specs,
        out_specs=(split_specs if split_out else joint_specs) + cast_out_specs + zero_specs,
        out_shape=(split_shapes if split_out else joint_shapes) + cast_out_shapes + zero_out_shapes,
        scratch_shapes=[
            pltpu.VMEM((N_FF_CHUNKS, D, FF_CHUNK), BF),
            pltpu.VMEM((N_FF_CHUNKS, D, FF_CHUNK), BF),
            pltpu.VMEM((D_FF, D), BF),
            pltpu.VMEM((2, D, FF_CHUNK), F32),
            pltpu.VMEM((2, D, FF_CHUNK), F32),
            pltpu.VMEM((2, FF_CHUNK, D), F32),
            pltpu.VMEM((tm, D), BF),
            pltpu.VMEM((tm, D), F32),
            pltpu.VMEM((FFN_SUBTILE, D_FF), BF),
            pltpu.SemaphoreType.DMA((3, 2)),
        ],
        compiler_params=pltpu.CompilerParams(
            dimension_semantics=("arbitrary",), vmem_limit_bytes=VMEM_LIMIT),
        name=f"ffn_l{l}_s{sub}",
    )(*xs, mods, npre, npost, wi, wo, *cast_weights)
    acts = tuple(out[:n_act]) if split_out else out[0]
    n_cast = len(cast_weights)
    return acts, tuple(out[n_act:n_act + n_cast]), tuple(out[n_act + n_cast:])


def _proj_kernel(x_ref, mod_ref, npre_ref, w_ref, *rest, n_carried):
    qkv_ref, f_ref, g_ref, k_ref, v_ref = rest[n_carried:]
    x = x_ref[...]
    shift = mod_ref[3:4, :]
    scale = mod_ref[4:5, :]
    h = (_rms(x, npre_ref[1:2, :]) * (1.0 + scale) + shift).astype(BF)
    z = _dot(h, w_ref[...])
    qkv_ref[...] = z[:, :3 * WA].astype(BF)
    f_ref[...] = z[:, 3 * WA:3 * WA + WB].astype(BF)
    g_ref[...] = z[:, 3 * WA + WB:].astype(BF)
    for i in range(k_ref.shape[0]):
        rows = slice(i * CTX_SEQ, (i + 1) * CTX_SEQ)
        k_ref[i] = z[rows, WA:2 * WA].T
        v_ref[i] = z[rows, 2 * WA:3 * WA].T


def _proj(x, mods, npre, w_in, carried, l, tm=1024):
    n_tiles = N_TOK // tm
    nl = N_LAT // tm
    bpt = tm // CTX_SEQ
    kv_shape = jax.ShapeDtypeStruct((N_CTX_B, DEPTH, WA, CTX_SEQ), F32)
    kv_spec = pl.BlockSpec((bpt, None, WA, CTX_SEQ), lambda t: (_ctx_tile(t, nl), l, 0, 0))
    n_fixed = 4
    return pl.pallas_call(
        functools.partial(_proj_kernel, n_carried=len(carried)),
        grid=(n_tiles,),
        in_specs=[
            pl.BlockSpec((tm, D), lambda t: (t, 0)),
            pl.BlockSpec((None, None, 9, D), lambda t: (l, _mod_row(t, tm), 0, 0)),
            pl.BlockSpec((None, 3, D), lambda t: (l, 0, 0)),
            pl.BlockSpec((D, PROJ_W), lambda t: (0, 0), pipeline_mode=pl.Buffered(1)),
        ] + [pl.BlockSpec(memory_space=pl.ANY)] * len(carried),
        out_specs=[
            pl.BlockSpec((tm, 3 * WA), lambda t: (t, 0)),
            pl.BlockSpec((tm, WB), lambda t: (t, 0)),
            pl.BlockSpec((tm, 2 * WC), lambda t: (t, 0)),
            kv_spec,
            kv_spec,
        ],
        out_shape=[
            jax.ShapeDtypeStruct((N_TOK, 3 * WA), BF),
            jax.ShapeDtypeStruct((N_TOK, WB), BF),
            jax.ShapeDtypeStruct((N_TOK, 2 * WC), BF),
            kv_shape,
            kv_shape,
        ],
        input_output_aliases={n_fixed + i: 3 + i for i in range(len(carried))},
        compiler_params=pltpu.CompilerParams(
            dimension_semantics=("arbitrary",), vmem_limit_bytes=VMEM_LIMIT),
        name=f"proj_l{l}",
    )(x, mods, npre, w_in, *carried)


def _fourier(f, fc_ref, fn_ref):
    n = fn_ref.shape[0]
    xcs = _dot(f, fc_ref[...]).astype(BF)
    outs = []
    for b in range(f.shape[0] // n):
        xb = xcs[b * n:(b + 1) * n]
        stacked = jnp.concatenate([xb[:, :WB], xb[:, WB:]], axis=0)
        outs.append(_dot(fn_ref[...], stacked))
    return outs[0] if len(outs) == 1 else jnp.concatenate(outs, axis=0)


def _gmlp(g, gn_ref, ws_ref, gb_ref, avg_ref):
    gl = jax.nn.gelu(g.astype(F32), approximate=True)
    u = gl[:, :WC]
    v = gl[:, WC:]
    avg = avg_ref[...]
    mu = _split2_dot(v, avg)
    d = v - mu
    var = _dot((d * d).astype(BF), avg)
    vn = (d * lax.rsqrt(var + EPS) * gn_ref[...]).astype(BF)
    w_cat = jnp.concatenate([ws_ref[grp].astype(BF) for grp in range(NGRP)], axis=1)
    keep = (lax.broadcasted_iota(jnp.int32, (NGRP * CHUNK, WC), 0) // CHUNK
            == lax.broadcasted_iota(jnp.int32, (NGRP * CHUNK, WC), 1) // GC)
    outs = []
    for c in range(g.shape[0] // CHUNK):
        vc = vn[c * CHUNK:(c + 1) * CHUNK, :]
        rhs = jnp.where(keep, jnp.concatenate([vc] * NGRP, axis=0), jnp.zeros((), BF))
        sp = _dot(w_cat, rhs)
        outs.append(u[c * CHUNK:(c + 1) * CHUNK, :] * (sp + gb_ref[...]))
    return jnp.concatenate(outs, axis=0)


def _mix_out(x, o, gate, npost, wo_ref):
    y = _dot(o, wo_ref[...])
    return x + gate * _rms(y, npost)


def _ctx_mixer_kernel(x_ref, qkv_ref, f_ref, g_ref, mod_ref, npost_ref, wo_ref, fc_ref, fn_ref,
                      gn_ref, ws_ref, gb_ref, avg_ref, o_ref, mix_ref):
    gw = HEADS_PER_GROUP * HD
    pair_w = 2 * HD
    first_of_pair = lax.broadcasted_iota(jnp.int32, (CTX_SEQ, pair_w), 1) < HD
    own = (lax.broadcasted_iota(jnp.int32, (HEADS_PER_GROUP * CTX_SEQ, gw), 0) // CTX_SEQ
           == lax.broadcasted_iota(jnp.int32, (HEADS_PER_GROUP * CTX_SEQ, gw), 1) // HD)
    lane_head = lax.broadcasted_iota(jnp.int32, (CTX_SEQ, gw), 1) // HD
    zero = jnp.zeros((), BF)
    for b in range(x_ref.shape[0] // CTX_SEQ):
        rows = slice(b * CTX_SEQ, (b + 1) * CTX_SEQ)
        scores = []
        for h in range(NH):
            lanes = slice((h // 2) * pair_w, (h // 2 + 1) * pair_w)
            q2 = qkv_ref[rows, lanes] * QK_SCALE
            k2 = qkv_ref[rows, WA + (h // 2) * pair_w:WA + (h // 2 + 1) * pair_w]
            qh = jnp.where(first_of_pair, q2, zero) if h % 2 == 0 else jnp.where(first_of_pair, zero, q2)
            scores.append(_dot_nt(qh, k2))
        probs, dens = [], []
        for s in scores:
            p = jnp.exp(s - jnp.max(s, axis=-1, keepdims=True))
            dens.append(jnp.sum(p, axis=-1, keepdims=True))
            probs.append(p.astype(BF))
        for grp in range(N_HEAD_GROUPS):
            hs = range(grp * HEADS_PER_GROUP, (grp + 1) * HEADS_PER_GROUP)
            v4 = qkv_ref[rows, 2 * WA + grp * gw:2 * WA + (grp + 1) * gw]
            vbd = jnp.where(own, jnp.concatenate([v4] * HEADS_PER_GROUP, axis=0), zero)
            o = _dot(jnp.concatenate([probs[h] for h in hs], axis=1), vbd)
            den = dens[hs[0]]
            for i in range(1, HEADS_PER_GROUP):
                den = jnp.where(lane_head == i, dens[hs[i]], den)
            mix_ref[rows, grp * gw:(grp + 1) * gw] = (o / den).astype(BF)
    mix_ref[:, WA:WA + WB] = _fourier(f_ref[...], fc_ref, fn_ref).astype(BF)
    mix_ref[:, WA + WB:] = _gmlp(g_ref[...], gn_ref, ws_ref, gb_ref, avg_ref).astype(BF)
    o_ref[...] = _mix_out(x_ref[...], mix_ref[...], mod_ref[5:6, :], npost_ref[1:2, :], wo_ref)


def _const_spec(shape):
    nd = len(shape)
    return pl.BlockSpec(shape, lambda *_: (0,) * nd)


def _ctx_mixer(x, qkv, f, g, mods, npost, w_out, fc, fn, gn, ws, gb, avg, l):
    tm = CTX_PER_STEP * CTX_SEQ
    off = N_LAT // tm
    row = lambda b: (b + off, 0)
    return pl.pallas_call(
        _ctx_mixer_kernel,
        grid=(N_CTX_B // CTX_PER_STEP,),
        in_specs=[
            pl.BlockSpec((tm, D), row),
            pl.BlockSpec((tm, 3 * WA), row),
            pl.BlockSpec((tm, WB), row),
            pl.BlockSpec((tm, 2 * WC), row),
            pl.BlockSpec((None, None, 9, D), lambda b: (l, CTX_MOD_ROW, 0, 0)),
            pl.BlockSpec((None, 3, D), lambda b: (l, 0, 0)),
            pl.BlockSpec((D, D), lambda b: (0, 0), pipeline_mode=pl.Buffered(1)),
            _const_spec(fc.shape),
            _const_spec(fn.shape),
            pl.BlockSpec((None, 1, WC), lambda b: (l, 0, 0)),
            pl.BlockSpec((None, NGRP, CHUNK, CHUNK), lambda b: (l, 0, 0, 0)),
            pl.BlockSpec((None, CHUNK, WC), lambda b: (l, 0, 0)),
            _const_spec(avg.shape),
        ],
        out_specs=pl.BlockSpec((tm, D), row),
        out_shape=jax.ShapeDtypeStruct((N_TOK, D), F32),
        input_output_aliases={0: 0},
        scratch_shapes=[pltpu.VMEM((tm, D), BF)],
        compiler_params=pltpu.CompilerParams(
            dimension_semantics=("arbitrary",), vmem_limit_bytes=VMEM_LIMIT),
        name=f"ctx_mixer_l{l}",
    )(x, qkv, f, g, mods, npost, w_out, fc, fn, gn, ws, gb, avg)


def _window_start(r):
    return min(max(r - WIN_ROWS // 2, 0), GRID_ROWS - WIN_ROWS)


def _nbr_attn_kernel(qkv_ref, ck_ref, cv_ref, bias_ref, o_ref):
    gw = HEADS_PER_GROUP * HD
    row_head = lax.broadcasted_iota(jnp.int32, (HEADS_PER_GROUP * GRID_W, gw), 0) // GRID_W
    lane_head = lax.broadcasted_iota(jnp.int32, (HEADS_PER_GROUP * GRID_W, gw), 1) // HD
    own = row_head == lane_head
    for hg in range(N_HEAD_GROUPS):
        ck = ck_ref[hg * gw:(hg + 1) * gw, :].astype(BF)
        cv = cv_ref[hg * gw:(hg + 1) * gw, :].astype(BF)
        for r in range(GRID_ROWS):
            ws = _window_start(r)
            q = qkv_ref[r * GRID_W:(r + 1) * GRID_W, hg * gw:(hg + 1) * gw] * QK_SCALE
            qs = jnp.where(own, jnp.concatenate([q] * HEADS_PER_GROUP, axis=0), jnp.zeros((), BF))
            kw = qkv_ref[ws * GRID_W:(ws + WIN_ROWS) * GRID_W, WA + hg * gw:WA + (hg + 1) * gw]
            vw = qkv_ref[ws * GRID_W:(ws + WIN_ROWS) * GRID_W, 2 * WA + hg * gw:2 * WA + (hg + 1) * gw]
            s_loc = _dot_nt(qs, kw) + bias_ref[hg, ws - r + N_DELTA - 1].astype(F32)
            s_ctx = _dot(qs, ck)
            m = jnp.maximum(jnp.max(s_loc, axis=-1, keepdims=True),
                            jnp.max(s_ctx, axis=-1, keepdims=True))
            p_loc = jnp.exp(s_loc - m)
            p_ctx = jnp.exp(s_ctx - m)
            den = jnp.sum(p_loc, axis=-1, keepdims=True) + jnp.sum(p_ctx, axis=-1, keepdims=True)
            o = (_dot(p_loc.astype(BF), vw) + _dot_nt(p_ctx.astype(BF), cv)) / den
            o = jnp.where(own, o, 0.0)
            o = (o[0:GRID_W] + o[GRID_W:2 * GRID_W]) + (o[2 * GRID_W:3 * GRID_W] + o[3 * GRID_W:])
            o_ref[r * GRID_W:(r + 1) * GRID_W, hg * gw:(hg + 1) * gw] = o.astype(BF)


def _nbr_attn(qkv, cache_k, cache_v, bias, l):
    return pl.pallas_call(
        _nbr_attn_kernel,
        grid=(N_LAT_B,),
        in_specs=[
            pl.BlockSpec((LAT_SEQ, 3 * WA), lambda b: (b, 0)),
            pl.BlockSpec((None, None, WA, PAST), lambda b: (b, l, 0, 0)),
            pl.BlockSpec((None, None, WA, PAST), lambda b: (b, l, 0, 0)),
            pl.BlockSpec((None,) + bias.shape[1:], lambda b: (l, 0, 0, 0, 0)),
        ],
        out_specs=pl.BlockSpec((LAT_SEQ, WA), lambda b: (b, 0)),
        out_shape=jax.ShapeDtypeStruct((N_LAT, WA), BF),
        compiler_params=pltpu.CompilerParams(
            dimension_semantics=("arbitrary",), vmem_limit_bytes=VMEM_LIMIT),
        name=f"nbr_attn_l{l}",
    )(qkv, cache_k, cache_v, bias)


def _lat_mixer_kernel(x_ref, oa_ref, f_ref, g_ref, mod_ref, npost_ref, wo_ref, fc_ref, fn_ref,
                      gn_ref, ws_ref, gb_ref, avg_ref, o_ref, mix_ref):
    mix_ref[:, :WA] = oa_ref[...]
    mix_ref[:, WA:WA + WB] = _fourier(f_ref[...], fc_ref, fn_ref).astype(BF)
    mix_ref[:, WA + WB:] = _gmlp(g_ref[...], gn_ref, ws_ref, gb_ref, avg_ref).astype(BF)
    o_ref[...] = _mix_out(x_ref[...], mix_ref[...], mod_ref[5:6, :], npost_ref[1:2, :], wo_ref)


def _lat_mixer(x, oa, f, g, mods, npost, w_out, fc, fn, gn, ws, gb, avg, l):
    row = lambda b: (b, 0)
    return pl.pallas_call(
        _lat_mixer_kernel,
        grid=(N_LAT_B,),
        in_specs=[
            pl.BlockSpec((LAT_SEQ, D), row),
            pl.BlockSpec((LAT_SEQ, WA), row),
            pl.BlockSpec((LAT_SEQ, WB), row),
            pl.BlockSpec((LAT_SEQ, 2 * WC), row),
            pl.BlockSpec((None, None, 9, D), lambda b: (l, b, 0, 0)),
            pl.BlockSpec((None, 3, D), lambda b: (l, 0, 0)),
            pl.BlockSpec((D, D), lambda b: (0, 0), pipeline_mode=pl.Buffered(1)),
            _const_spec(fc.shape),
            pl.BlockSpec(fn.shape, lambda b: (0, 0), pipeline_mode=pl.Buffered(1)),
            pl.BlockSpec((None, 1, WC), lambda b: (l, 0, 0)),
            pl.BlockSpec((None, NGRP, CHUNK, CHUNK), lambda b: (l, 0, 0, 0)),
            pl.BlockSpec((None, CHUNK, WC), lambda b: (l, 0, 0)),
            _const_spec(avg.shape),
        ],
        out_specs=pl.BlockSpec((LAT_SEQ, D), row),
        out_shape=jax.ShapeDtypeStruct((N_TOK, D), F32),
        input_output_aliases={0: 0},
        scratch_shapes=[pltpu.VMEM((LAT_SEQ, D), BF)],
        compiler_params=pltpu.CompilerParams(
            dimension_semantics=("arbitrary",), vmem_limit_bytes=VMEM_LIMIT),
        name=f"lat_mixer_l{l}",
    )(x, oa, f, g, mods, npost, w_out, fc, fn, gn, ws, gb, avg)


def _dft_tables(n):
    idx = np.arange(n)
    ang = 2.0 * np.pi * ((idx[:, None] * idx[None, :]) % n) / n
    return (np.concatenate([np.cos(ang), -np.sin(ang)], axis=1) / np.sqrt(n)).astype(np.float32)


def _channel_dft_table():
    idx = np.arange(GC)
    ang = 2.0 * np.pi * ((idx[:, None] * idx[None, :]) % GC) / GC
    eye = np.eye(NGRP)
    return (np.concatenate([np.kron(eye, np.cos(ang)), np.kron(eye, np.sin(ang))], axis=1)
            / np.sqrt(GC)).astype(np.float32)


def _group_average_table():
    return np.kron(np.eye(NGRP), np.full((GC, GC), 1.0 / GC)).astype(np.float32)


def kernel(x_prompt, x_sample, cache_k, cache_v, c, c_ctx, ada_w, ada_b, norm_pre, norm_post,
           ffn_w_in, ffn_w_out, w_in, w_out, rpb, gmlp_norm, gmlp_w, gmlp_b):
    xs = (x_sample.reshape(N_LAT, D), x_prompt.reshape(N_CTX, D))
    cs = jnp.concatenate([c, c_ctx[None, :], jnp.zeros((MOD_ROWS - N_LAT_B - 1, D), F32)], axis=0)
    mods = _ada_mods(cs, ada_w, ada_b)
    bias = _bias_table(rpb)

    gn = gmlp_norm.reshape(DEPTH, 1, WC)
    gb = jnp.repeat(jnp.transpose(gmlp_b, (0, 2, 1)), GC, axis=2)
    ck = jnp.swapaxes(cache_k.reshape(N_LAT_B, DEPTH, PAST, WA), 2, 3)
    cv = jnp.swapaxes(cache_v.reshape(N_LAT_B, DEPTH, PAST, WA), 2, 3)

    fc = jnp.asarray(_channel_dft_table()).astype(BF)
    fn_ctx = jnp.asarray(_dft_tables(CTX_SEQ)).astype(BF)
    fn_lat = jnp.asarray(_dft_tables(LAT_SEQ)).astype(BF)
    avg = jnp.asarray(_group_average_table()).astype(BF)

    kv_carrier = (N_CTX_B, DEPTH, WA, CTX_SEQ)
    carried = None
    for l in range(DEPTH):
        x, (w_in_b, w_out_b), zeros = _ffn(
            xs, mods, norm_pre, norm_post, ffn_w_in, ffn_w_out, l, 0,
            cast_weights=(w_in, w_out), zero_shapes=(kv_carrier,) * 2 if l == 0 else ())
        if l == 0:
            carried = zeros
        qkv, f, g, new_k, new_v = _proj(x, mods, norm_pre, w_in_b, carried, l)
        carried = (new_k, new_v)
        oa = _nbr_attn(qkv, ck, cv, bias, l)
        x = _ctx_mixer(x, qkv, f, g, mods, norm_post, w_out_b, fc, fn_ctx, gn, gmlp_w, gb, avg, l)
        x = _lat_mixer(x, oa, f, g, mods, norm_post, w_out_b, fc, fn_lat, gn, gmlp_w, gb, avg, l)
        xs, _, _ = _ffn((x,), mods, norm_pre, norm_post, ffn_w_in, ffn_w_out, l, 1,
                        split_out=(l == DEPTH - 1))
        if l < DEPTH - 1:
            xs = (xs,)
    xl, xc = xs

    kv_out = (N_CTX_B, DEPTH, CTX_SEQ, NH, HD)
    return (xc.reshape(N_CTX_B, CTX_SEQ, D), xl.reshape(N_LAT_B, LAT_SEQ, D),
            jnp.swapaxes(new_k, 2, 3).reshape(kv_out), jnp.swapaxes(new_v, 2, 3).reshape(kv_out))
```

```python
import functools

import numpy as np
import jax
import jax.numpy as jnp
from jax import lax
from jax.experimental import pallas as pl
from jax.experimental.pallas import tpu as pltpu

D = 1024
DEPTH = 4
N_CTX_B = 16
CTX_SEQ = 256
N_LAT_B = 4
LAT_SEQ = 1024
N_LAT = N_LAT_B * LAT_SEQ
N_CTX = N_CTX_B * CTX_SEQ
N_TOK = N_LAT + N_CTX
PAST = 256
GRID_W = 64
GRID_ROWS = LAT_SEQ // GRID_W
HD = 64
NH = 8
WA = 512
WB = 256
WC = 256
NGRP = 4
GC = 64
CHUNK = 128
PROJ_W = 3 * WA + WB + 2 * WC
D_FF = 2816
FF_CHUNK = 256
N_FF_CHUNKS = D_FF // FF_CHUNK
FFN_TILE = 1024
FFN_SUBTILE = 512
WIN_ROWS = 8
WIN_COLS = 16
N_ROW_OFF = 2 * WIN_ROWS - 1
N_COL_OFF = 2 * WIN_COLS - 1
N_DELTA = 8
EPS = 1e-6
NEG = -1e30
ADA_TILE = 2304
MOD_ROWS = 8
CTX_MOD_ROW = 4
CTX_PER_STEP = 4
HEADS_PER_GROUP = 4
NBR_HEADS = 2
NBR_GROUPS = NH // NBR_HEADS
N_HEAD_GROUPS = NH // HEADS_PER_GROUP

BF = jnp.bfloat16
F32 = jnp.float32
QK_SCALE = HD ** -0.5

VMEM_LIMIT = 56 * 1024 * 1024


def _dot(a, b):
    return jnp.dot(a, b, preferred_element_type=F32)


def _dot_nt(a, b):
    return lax.dot_general(a, b, (((1,), (1,)), ((), ())), preferred_element_type=F32)


def _split2_dot(a, b):
    a0 = a.astype(BF)
    a1 = (a - a0.astype(F32)).astype(BF)
    return _dot(a0, b) + _dot(a1, b)


def _rms(x, g):
    return x * lax.rsqrt(jnp.mean(x * x, axis=-1, keepdims=True) + EPS) * g


def _mod_row(tile, tm):
    return jnp.minimum((tile * tm) // LAT_SEQ, CTX_MOD_ROW)


def _ada_kernel(cs_ref, w_ref, b_ref, o_ref):
    cs = cs_ref[...]
    s = cs * jax.nn.sigmoid(cs)
    o_ref[...] = _dot(s.astype(BF), w_ref[...].astype(BF)) + b_ref[...]


def _ada_mods(cs, ada_w, ada_b):
    n_out = 9 * D
    tn = ADA_TILE
    out = pl.pallas_call(
        _ada_kernel,
        grid=(DEPTH, n_out // tn),
        in_specs=[
            pl.BlockSpec((MOD_ROWS, D), lambda l, n: (0, 0)),
            pl.BlockSpec((None, D, tn), lambda l, n: (l, 0, n)),
            pl.BlockSpec((None, 1, tn), lambda l, n: (l, 0, n)),
        ],
        out_specs=pl.BlockSpec((None, MOD_ROWS, tn), lambda l, n: (l, 0, n)),
        out_shape=jax.ShapeDtypeStruct((DEPTH, MOD_ROWS, n_out), F32),
        compiler_params=pltpu.CompilerParams(
            dimension_semantics=("arbitrary", "arbitrary"), vmem_limit_bytes=VMEM_LIMIT),
        name="ada_mods",
    )(cs, ada_w, ada_b.reshape(DEPTH, 1, n_out))
    return out.reshape(DEPTH, MOD_ROWS, 9, D)


RPB_LANES = 128
RPB_HALF = RPB_LANES // 2


def _bias_kernel(rpb_ref, o_ref):
    qq = lax.broadcasted_iota(jnp.int32, (GRID_W, RPB_LANES), 0)
    lane = lax.broadcasted_iota(jnp.int32, (GRID_W, RPB_LANES), 1)
    second = lane >= GRID_W
    kk = jnp.where(second, lane - GRID_W, lane)
    col_start = jnp.clip(qq - WIN_COLS // 2, 0, GRID_W - WIN_COLS)
    visible = (kk >= col_start) & (kk < col_start + WIN_COLS)
    m = lax.broadcasted_iota(jnp.int32, (N_ROW_OFF, RPB_LANES), 1)
    diff = jnp.where(m < RPB_HALF, m, m - RPB_LANES)
    ext_idx = jnp.clip(diff + (WIN_COLS - 1), 0, N_COL_OFF - 1)
    for i in range(NBR_HEADS):
        ext = jnp.take_along_axis(rpb_ref[i], ext_idx, axis=1)
        for ro in range(N_ROW_OFF - 1):
            e0 = jnp.broadcast_to(ext[ro:ro + 1, :], (GRID_W, RPB_LANES))
            e1 = jnp.broadcast_to(ext[ro + 1:ro + 2, :], (GRID_W, RPB_LANES))
            t = jnp.where(second,
                          pltpu.roll(e1, GRID_W, 1, stride=1, stride_axis=0),
                          pltpu.roll(e0, 0, 1, stride=1, stride_axis=0))
            t = jnp.where(visible, t, NEG).astype(o_ref.dtype)
            for a in range(0, WIN_ROWS, 2):
                delta_idx = ro - a
                if 0 <= delta_idx < N_DELTA:
                    o_ref[delta_idx, i * GRID_W:(i + 1) * GRID_W, a * GRID_W:(a + 2) * GRID_W] = t


def _bias_table(rpb):
    rpb_padded = jnp.pad(rpb, ((0, 0), (0, 0), (0, 0), (0, RPB_LANES - N_COL_OFF)))
    return pl.pallas_call(
        _bias_kernel,
        grid=(DEPTH, NBR_GROUPS),
        in_specs=[pl.BlockSpec((None, NBR_HEADS, N_ROW_OFF, RPB_LANES),
                               lambda l, g: (l, g, 0, 0))],
        out_specs=pl.BlockSpec((None, None, N_DELTA, NBR_HEADS * GRID_W, WIN_ROWS * GRID_W),
                               lambda l, g: (l, g, 0, 0, 0)),
        out_shape=jax.ShapeDtypeStruct(
            (DEPTH, NBR_GROUPS, N_DELTA, NBR_HEADS * GRID_W, WIN_ROWS * GRID_W), BF),
        compiler_params=pltpu.CompilerParams(
            dimension_semantics=("arbitrary", "arbitrary"), vmem_limit_bytes=VMEM_LIMIT),
        name="rpb_table",
    )(rpb_padded)


def _lat_tile(t, n_lat_tiles):
    return jnp.minimum(t, n_lat_tiles - 1)


def _ctx_tile(t, n_lat_tiles):
    return jnp.maximum(t - n_lat_tiles, 0)


def _ffn_kernel(*refs, l, sub, split_in, split_out, n_lat_tiles, n_cast, n_zero):
    refs = list(refs)
    x_refs = [refs.pop(0) for _ in range(2 if split_in else 1)]
    mod_ref, npre_ref, npost_ref, wi_hbm, wo_hbm = [refs.pop(0) for _ in range(5)]
    cast_in_refs = [refs.pop(0) for _ in range(n_cast)]
    o_refs = [refs.pop(0) for _ in range(2 if split_out else 1)]
    cast_out_refs = [refs.pop(0) for _ in range(n_cast)]
    zero_refs = [refs.pop(0) for _ in range(n_zero)]
    wg_ref, wu_ref, wo_ref, sg_ref, su_ref, so_ref, h_ref, acc_ref, act_ref, sem = refs

    mo = 0 if sub == 0 else 6
    ni = 0 if sub == 0 else 2
    t = pl.program_id(0)
    is_lat = t < n_lat_tiles

    for src, dst in zip(cast_in_refs, cast_out_refs):
        dst[...] = src[...].astype(BF)
    for z in zero_refs:
        z[...] = jnp.zeros(z.shape, z.dtype)

    def load_x(rows=slice(None)):
        if split_in:
            return jnp.where(is_lat, x_refs[0][rows, :], x_refs[1][rows, :])
        return x_refs[0][rows, :]

    def store(out, rows=slice(None)):
        if not split_out:
            o_refs[0][rows, :] = out
            return

        @pl.when(is_lat)
        def _():
            o_refs[0][rows, :] = out

        @pl.when(jnp.logical_not(is_lat))
        def _():
            o_refs[1][rows, :] = out

    def modulated(x):
        shift = mod_ref[mo:mo + 1, :]
        scale = mod_ref[mo + 1:mo + 2, :]
        return (_rms(x, npre_ref[ni:ni + 1, :]) * (1.0 + scale) + shift).astype(BF)

    def residual(x, acc):
        gate = mod_ref[mo + 2:mo + 3, :]
        return x + 0.5 * gate * _rms(acc, npost_ref[ni:ni + 1, :])

    def chunk_update(h, wg, wu, wo):
        g = _dot(h, wg)
        u = _dot(h, wu)
        a = (g * jax.nn.sigmoid(g) * u).astype(BF)
        return _dot(a, wo)

    def chunk_copies(j, slot):
        cols = pl.ds(pl.multiple_of(j * FF_CHUNK, FF_CHUNK), FF_CHUNK)
        up = pl.ds(pl.multiple_of(D_FF + j * FF_CHUNK, FF_CHUNK), FF_CHUNK)
        return (
            pltpu.make_async_copy(wi_hbm.at[l, sub, :, cols], sg_ref.at[slot], sem.at[0, slot]),
            pltpu.make_async_copy(wi_hbm.at[l, sub, :, up], su_ref.at[slot], sem.at[1, slot]),
            pltpu.make_async_copy(wo_hbm.at[l, sub, cols, :], so_ref.at[slot], sem.at[2, slot]),
        )

    @pl.when(t == 0)
    def _():
        for cp in chunk_copies(0, 0):
            cp.start()
        h_ref[...] = modulated(load_x())
        acc_ref[...] = jnp.zeros(acc_ref.shape, F32)

        def body(j, carry):
            slot = lax.rem(j, 2)

            @pl.when(j + 1 < N_FF_CHUNKS)
            def _():
                for cp in chunk_copies(j + 1, 1 - slot):
                    cp.start()

            for cp in chunk_copies(j, slot):
                cp.wait()
            wg = sg_ref[slot].astype(BF)
            wu = su_ref[slot].astype(BF)
            wo = so_ref[slot].astype(BF)
            wg_ref[j] = wg
            wu_ref[j] = wu
            wo_ref[pl.ds(pl.multiple_of(j * FF_CHUNK, FF_CHUNK), FF_CHUNK), :] = wo
            acc_ref[...] += chunk_update(h_ref[...], wg, wu, wo)
            return carry

        lax.fori_loop(0, N_FF_CHUNKS, body, 0)
        store(residual(load_x(), acc_ref[...]))

    @pl.when(t > 0)
    def _():
        def sub_tile(s, carry):
            rows = pl.ds(pl.multiple_of(s * FFN_SUBTILE, FFN_SUBTILE), FFN_SUBTILE)
            x = load_x(rows)
            h = modulated(x)
            for j in range(N_FF_CHUNKS):
                g = _dot(h, wg_ref[j])
                u = _dot(h, wu_ref[j])
                act_ref[:, j * FF_CHUNK:(j + 1) * FF_CHUNK] = (g * jax.nn.sigmoid(g) * u).astype(BF)
            acc = _dot(act_ref[...], wo_ref[...])
            store(residual(x, acc), rows)
            return carry

        n_sub = acc_ref.shape[0] // FFN_SUBTILE
        if n_sub == 1:
            sub_tile(0, 0)
        else:
            lax.fori_loop(0, n_sub, sub_tile, 0)


def _ffn(xs, mods, npre, npost, wi, wo, l, sub, split_out=False, cast_weights=(), zero_shapes=()):
    split_in = len(xs) == 2
    tm = FFN_SUBTILE if (split_in or split_out) else FFN_TILE
    n_tiles = N_TOK // tm
    nl = N_LAT // tm
    lat_idx = lambda t: (_lat_tile(t, nl), 0)
    ctx_idx = lambda t: (_ctx_tile(t, nl), 0)
    split_specs = [pl.BlockSpec((tm, D), lat_idx), pl.BlockSpec((tm, D), ctx_idx)]
    joint_specs = [pl.BlockSpec((tm, D), lambda t: (t, 0))]
    split_shapes = [jax.ShapeDtypeStruct((N_LAT, D), F32), jax.ShapeDtypeStruct((N_CTX, D), F32)]
    joint_shapes = [jax.ShapeDtypeStruct((N_TOK, D), F32)]
    cast_in_specs = [pl.BlockSpec((None, w.shape[1] // n_tiles, w.shape[2]), lambda t: (l, t, 0))
                     for w in cast_weights]
    cast_out_specs = [pl.BlockSpec((w.shape[1] // n_tiles, w.shape[2]), lambda t: (t, 0))
                      for w in cast_weights]
    cast_out_shapes = [jax.ShapeDtypeStruct(w.shape[1:], BF) for w in cast_weights]
    zero_specs = [pl.BlockSpec((s[0] // n_tiles,) + tuple(s[1:]),
                               lambda t, nd=len(s): (t,) + (0,) * (nd - 1)) for s in zero_shapes]
    zero_out_shapes = [jax.ShapeDtypeStruct(tuple(s), F32) for s in zero_shapes]
    n_act = 2 if split_out else 1
    out = pl.pallas_call(
        functools.partial(_ffn_kernel, l=l, sub=sub, split_in=split_in, split_out=split_out,
                          n_lat_tiles=nl, n_cast=len(cast_weights), n_zero=len(zero_shapes)),
        grid=(n_tiles,),
        in_specs=(split_specs if split_in else joint_specs) + [
            pl.BlockSpec((None, None, 9, D), lambda t: (l, _mod_row(t, tm), 0, 0)),
            pl.BlockSpec((None, 3, D), lambda t: (l, 0, 0)),
            pl.BlockSpec((None, 3, D), lambda t: (l, 0, 0)),
            pl.BlockSpec(memory_space=pl.ANY),
            pl.BlockSpec(memory_space=pl.ANY),
        ] + cast_in_specs,
        out_specs=(split_specs if split_out else joint_specs) + cast_out_specs + zero_specs,
        out_shape=(split_shapes if split_out else joint_shapes) + cast_out_shapes + zero_out_shapes,
        scratch_shapes=[
            pltpu.VMEM((N_FF_CHUNKS, D, FF_CHUNK), BF),
            pltpu.VMEM((N_FF_CHUNKS, D, FF_CHUNK), BF),
            pltpu.VMEM((D_FF, D), BF),
            pltpu.VMEM((2, D, FF_CHUNK), F32),
            pltpu.VMEM((2, D, FF_CHUNK), F32),
            pltpu.VMEM((2, FF_CHUNK, D), F32),
            pltpu.VMEM((tm, D), BF),
            pltpu.VMEM((tm, D), F32),
            pltpu.VMEM((FFN_SUBTILE, D_FF), BF),
            pltpu.SemaphoreType.DMA((3, 2)),
        ],
        compiler_params=pltpu.CompilerParams(
            dimension_semantics=("arbitrary",), vmem_limit_bytes=VMEM_LIMIT),
        name=f"ffn_l{l}_s{sub}",
    )(*xs, mods, npre, npost, wi, wo, *cast_weights)
    acts = tuple(out[:n_act]) if split_out else out[0]
    n_cast = len(cast_weights)
    return acts, tuple(out[n_act:n_act + n_cast]), tuple(out[n_act + n_cast:])


def _proj_kernel(x_ref, mod_ref, npre_ref, w_ref, *rest, n_carried):
    qkv_ref, f_ref, g_ref, k_ref, v_ref = rest[n_carried:]
    x = x_ref[...]
    shift = mod_ref[3:4, :]
    scale = mod_ref[4:5, :]
    h = (_rms(x, npre_ref[1:2, :]) * (1.0 + scale) + shift).astype(BF)
    z = _dot(h, w_ref[...])
    qkv_ref[...] = z[:, :3 * WA].astype(BF)
    f_ref[...] = z[:, 3 * WA:3 * WA + WB].astype(BF)
    g_ref[...] = z[:, 3 * WA + WB:].astype(BF)
    for i in range(k_ref.shape[0]):
        rows = slice(i * CTX_SEQ, (i + 1) * CTX_SEQ)
        k_ref[i] = z[rows, WA:2 * WA].T
        v_ref[i] = z[rows, 2 * WA:3 * WA].T


def _proj(x, mods, npre, w_in, carried, l, tm=1024):
    n_tiles = N_TOK // tm
    nl = N_LAT // tm
    bpt = tm // CTX_SEQ
    kv_shape = jax.ShapeDtypeStruct((N_CTX_B, DEPTH, WA, CTX_SEQ), F32)
    kv_spec = pl.BlockSpec((bpt, None, WA, CTX_SEQ), lambda t: (_ctx_tile(t, nl), l, 0, 0))
    n_fixed = 4
    return pl.pallas_call(
        functools.partial(_proj_kernel, n_carried=len(carried)),
        grid=(n_tiles,),
        in_specs=[
            pl.BlockSpec((tm, D), lambda t: (t, 0)),
            pl.BlockSpec((None, None, 9, D), lambda t: (l, _mod_row(t, tm), 0, 0)),
            pl.BlockSpec((None, 3, D), lambda t: (l, 0, 0)),
            pl.BlockSpec((D, PROJ_W), lambda t: (0, 0), pipeline_mode=pl.Buffered(1)),
        ] + [pl.BlockSpec(memory_space=pl.ANY)] * len(carried),
        out_specs=[
            pl.BlockSpec((tm, 3 * WA), lambda t: (t, 0)),
            pl.BlockSpec((tm, WB), lambda t: (t, 0)),
            pl.BlockSpec((tm, 2 * WC), lambda t: (t, 0)),
            kv_spec,
            kv_spec,
        ],
        out_shape=[
            jax.ShapeDtypeStruct((N_TOK, 3 * WA), BF),
            jax.ShapeDtypeStruct((N_TOK, WB), BF),
            jax.ShapeDtypeStruct((N_TOK, 2 * WC), BF),
            kv_shape,
            kv_shape,
        ],
        input_output_aliases={n_fixed + i: 3 + i for i in range(len(carried))},
        compiler_params=pltpu.CompilerParams(
            dimension_semantics=("arbitrary",), vmem_limit_bytes=VMEM_LIMIT),
        name=f"proj_l{l}",
    )(x, mods, npre, w_in, *carried)


def _fourier(f, fc_ref, fn_ref):
    n = fn_ref.shape[0]
    xcs = _dot(f, fc_ref[...]).astype(BF)
    outs = []
    for b in range(f.shape[0] // n):
        xb = xcs[b * n:(b + 1) * n]
        stacked = jnp.concatenate([xb[:, :WB], xb[:, WB:]], axis=0)
        outs.append(_dot(fn_ref[...], stacked))
    return outs[0] if len(outs) == 1 else jnp.concatenate(outs, axis=0)


def _gmlp(g, gn_ref, ws_ref, gb_ref, avg_ref):
    gl = jax.nn.gelu(g.astype(F32), approximate=True)
    u = gl[:, :WC]
    v = gl[:, WC:]
    avg = avg_ref[...]
    mu = _split2_dot(v, avg)
    d = v - mu
    var = _dot((d * d).astype(BF), avg)
    vn = (d * lax.rsqrt(var + EPS) * gn_ref[...]).astype(BF)
    w_cat = jnp.concatenate([ws_ref[grp].astype(BF) for grp in range(NGRP)], axis=1)
    keep = (lax.broadcasted_iota(jnp.int32, (NGRP * CHUNK, WC), 0) // CHUNK
            == lax.broadcasted_iota(jnp.int32, (NGRP * CHUNK, WC), 1) // GC)
    outs = []
    for c in range(g.shape[0] // CHUNK):
        vc = vn[c * CHUNK:(c + 1) * CHUNK, :]
        rhs = jnp.where(keep, jnp.concatenate([vc] * NGRP, axis=0), jnp.zeros((), BF))
        sp = _dot(w_cat, rhs)
        outs.append(u[c * CHUNK:(c + 1) * CHUNK, :] * (sp + gb_ref[...]))
    return jnp.concatenate(outs, axis=0)


def _mix_out(x, o, gate, npost, wo_ref):
    y = _dot(o, wo_ref[...])
    return x + gate * _rms(y, npost)


def _ctx_mixer_kernel(x_ref, qkv_ref, f_ref, g_ref, mod_ref, npost_ref, wo_ref, fc_ref, fn_ref,
                      gn_ref, ws_ref, gb_ref, avg_ref, o_ref, mix_ref):
    gw = HEADS_PER_GROUP * HD
    pair_w = 2 * HD
    first_of_pair = lax.broadcasted_iota(jnp.int32, (CTX_SEQ, pair_w), 1) < HD
    own = (lax.broadcasted_iota(jnp.int32, (HEADS_PER_GROUP * CTX_SEQ, gw), 0) // CTX_SEQ
           == lax.broadcasted_iota(jnp.int32, (HEADS_PER_GROUP * CTX_SEQ, gw), 1) // HD)
    lane_head = lax.broadcasted_iota(jnp.int32, (CTX_SEQ, gw), 1) // HD
    zero = jnp.zeros((), BF)
    for b in range(x_ref.shape[0] // CTX_SEQ):
        rows = slice(b * CTX_SEQ, (b + 1) * CTX_SEQ)
        scores = []
        for h in range(NH):
            lanes = slice((h // 2) * pair_w, (h // 2 + 1) * pair_w)
            q2 = qkv_ref[rows, lanes] * QK_SCALE
            k2 = qkv_ref[rows, WA + (h // 2) * pair_w:WA + (h // 2 + 1) * pair_w]
            qh = jnp.where(first_of_pair, q2, zero) if h % 2 == 0 else jnp.where(first_of_pair, zero, q2)
            scores.append(_dot_nt(qh, k2))
        probs, dens = [], []
        for s in scores:
            p = jnp.exp(s - jnp.max(s, axis=-1, keepdims=True))
            dens.append(jnp.sum(p, axis=-1, keepdims=True))
            probs.append(p.astype(BF))
        for grp in range(N_HEAD_GROUPS):
            hs = range(grp * HEADS_PER_GROUP, (grp + 1) * HEADS_PER_GROUP)
            v4 = qkv_ref[rows, 2 * WA + grp * gw:2 * WA + (grp + 1) * gw]
            vbd = jnp.where(own, jnp.concatenate([v4] * HEADS_PER_GROUP, axis=0), zero)
            o = _dot(jnp.concatenate([probs[h] for h in hs], axis=1), vbd)
            den = dens[hs[0]]
            for i in range(1, HEADS_PER_GROUP):
                den = jnp.where(lane_head == i, dens[hs[i]], den)
            mix_ref[rows, grp * gw:(grp + 1) * gw] = (o / den).astype(BF)
    mix_ref[:, WA:WA + WB] = _fourier(f_ref[...], fc_ref, fn_ref).astype(BF)
    mix_ref[:, WA + WB:] = _gmlp(g_ref[...], gn_ref, ws_ref, gb_ref, avg_ref).astype(BF)
    o_ref[...] = _mix_out(x_ref[...], mix_ref[...], mod_ref[5:6, :], npost_ref[1:2, :], wo_ref)


def _const_spec(shape):
    nd = len(shape)
    return pl.BlockSpec(shape, lambda *_: (0,) * nd)


def _ctx_mixer(x, qkv, f, g, mods, npost, w_out, fc, fn, gn, ws, gb, avg, l):
    tm = CTX_PER_STEP * CTX_SEQ
    off = N_LAT // tm
    row = lambda b: (b + off, 0)
    return pl.pallas_call(
        _ctx_mixer_kernel,
        grid=(N_CTX_B // CTX_PER_STEP,),
        in_specs=[
            pl.BlockSpec((tm, D), row),
            pl.BlockSpec((tm, 3 * WA), row),
            pl.BlockSpec((tm, WB), row),
            pl.BlockSpec((tm, 2 * WC), row),
            pl.BlockSpec((None, None, 9, D), lambda b: (l, CTX_MOD_ROW, 0, 0)),
            pl.BlockSpec((None, 3, D), lambda b: (l, 0, 0)),
            pl.BlockSpec((D, D), lambda b: (0, 0), pipeline_mode=pl.Buffered(1)),
            _const_spec(fc.shape),
            _const_spec(fn.shape),
            pl.BlockSpec((None, 1, WC), lambda b: (l, 0, 0)),
            pl.BlockSpec((None, NGRP, CHUNK, CHUNK), lambda b: (l, 0, 0, 0)),
            pl.BlockSpec((None, CHUNK, WC), lambda b: (l, 0, 0)),
            _const_spec(avg.shape),
        ],
        out_specs=pl.BlockSpec((tm, D), row),
        out_shape=jax.ShapeDtypeStruct((N_TOK, D), F32),
        input_output_aliases={0: 0},
        scratch_shapes=[pltpu.VMEM((tm, D), BF)],
        compiler_params=pltpu.CompilerParams(
            dimension_semantics=("arbitrary",), vmem_limit_bytes=VMEM_LIMIT),
        name=f"ctx_mixer_l{l}",
    )(x, qkv, f, g, mods, npost, w_out, fc, fn, gn, ws, gb, avg)


def _window_start(r):
    return min(max(r - WIN_ROWS // 2, 0), GRID_ROWS - WIN_ROWS)


def _nbr_attn_kernel(qkv_ref, ck_ref, cv_ref, bias_ref, o_ref):
    gw = NBR_HEADS * HD
    row_head = lax.broadcasted_iota(jnp.int32, (NBR_HEADS * GRID_W, gw), 0) // GRID_W
    lane_head = lax.broadcasted_iota(jnp.int32, (NBR_HEADS * GRID_W, gw), 1) // HD
    own = row_head == lane_head
    for hg in range(NBR_GROUPS):
        ck = ck_ref[hg * gw:(hg + 1) * gw, :].astype(BF)
        cv = cv_ref[hg * gw:(hg + 1) * gw, :].astype(BF)
        for r in range(GRID_ROWS):
            ws = _window_start(r)
            q = qkv_ref[r * GRID_W:(r + 1) * GRID_W, hg * gw:(hg + 1) * gw] * QK_SCALE
            qs = jnp.where(own, jnp.concatenate([q] * NBR_HEADS, axis=0), jnp.zeros((), BF))
            kw = qkv_ref[ws * GRID_W:(ws + WIN_ROWS) * GRID_W, WA + hg * gw:WA + (hg + 1) * gw]
            vw = qkv_ref[ws * GRID_W:(ws + WIN_ROWS) * GRID_W, 2 * WA + hg * gw:2 * WA + (hg + 1) * gw]
            s_loc = _dot_nt(qs, kw) + bias_ref[hg, ws - r + N_DELTA - 1].astype(F32)
            s_ctx = _dot(qs, ck)
            m = jnp.maximum(jnp.max(s_loc, axis=-1, keepdims=True),
                            jnp.max(s_ctx, axis=-1, keepdims=True))
            p_loc = jnp.exp(s_loc - m)
            p_ctx = jnp.exp(s_ctx - m)
            den = jnp.sum(p_loc, axis=-1, keepdims=True) + jnp.sum(p_ctx, axis=-1, keepdims=True)
            o = (_dot(p_loc.astype(BF), vw) + _dot_nt(p_ctx.astype(BF), cv)) / den
            o = jnp.where(own, o, 0.0)
            o = sum([o[i * GRID_W:(i + 1) * GRID_W] for i in range(1, NBR_HEADS)], o[0:GRID_W])
            o_ref[r * GRID_W:(r + 1) * GRID_W, hg * gw:(hg + 1) * gw] = o.astype(BF)


def _nbr_attn(qkv, cache_k, cache_v, bias, l):
    return pl.pallas_call(
        _nbr_attn_kernel,
        grid=(N_LAT_B,),
        in_specs=[
            pl.BlockSpec((LAT_SEQ, 3 * WA), lambda b: (b, 0)),
            pl.BlockSpec((None, None, WA, PAST), lambda b: (b, l, 0, 0)),
            pl.BlockSpec((None, None, WA, PAST), lambda b: (b, l, 0, 0)),
            pl.BlockSpec((None,) + bias.shape[1:], lambda b: (l, 0, 0, 0, 0)),
        ],
        out_specs=pl.BlockSpec((LAT_SEQ, WA), lambda b: (b, 0)),
        out_shape=jax.ShapeDtypeStruct((N_LAT, WA), BF),
        compiler_params=pltpu.CompilerParams(
            dimension_semantics=("arbitrary",), vmem_limit_bytes=VMEM_LIMIT),
        name=f"nbr_attn_l{l}",
    )(qkv, cache_k, cache_v, bias)


def _lat_mixer_kernel(x_ref, oa_ref, f_ref, g_ref, mod_ref, npost_ref, wo_ref, fc_ref, fn_ref,
                      gn_ref, ws_ref, gb_ref, avg_ref, o_ref, mix_ref):
    mix_ref[:, :WA] = oa_ref[...]
    mix_ref[:, WA:WA + WB] = _fourier(f_ref[...], fc_ref, fn_ref).astype(BF)
    mix_ref[:, WA + WB:] = _gmlp(g_ref[...], gn_ref, ws_ref, gb_ref, avg_ref).astype(BF)
    o_ref[...] = _mix_out(x_ref[...], mix_ref[...], mod_ref[5:6, :], npost_ref[1:2, :], wo_ref)


def _lat_mixer(x, oa, f, g, mods, npost, w_out, fc, fn, gn, ws, gb, avg, l):
    row = lambda b: (b, 0)
    return pl.pallas_call(
        _lat_mixer_kernel,
        grid=(N_LAT_B,),
        in_specs=[
            pl.BlockSpec((LAT_SEQ, D), row),
            pl.BlockSpec((LAT_SEQ, WA), row),
            pl.BlockSpec((LAT_SEQ, WB), row),
            pl.BlockSpec((LAT_SEQ, 2 * WC), row),
            pl.BlockSpec((None, None, 9, D), lambda b: (l, b, 0, 0)),
            pl.BlockSpec((None, 3, D), lambda b: (l, 0, 0)),
            pl.BlockSpec((D, D), lambda b: (0, 0), pipeline_mode=pl.Buffered(1)),
            _const_spec(fc.shape),
            pl.BlockSpec(fn.shape, lambda b: (0, 0), pipeline_mode=pl.Buffered(1)),
            pl.BlockSpec((None, 1, WC), lambda b: (l, 0, 0)),
            pl.BlockSpec((None, NGRP, CHUNK, CHUNK), lambda b: (l, 0, 0, 0)),
            pl.BlockSpec((None, CHUNK, WC), lambda b: (l, 0, 0)),
            _const_spec(avg.shape),
        ],
        out_specs=pl.BlockSpec((LAT_SEQ, D), row),
        out_shape=jax.ShapeDtypeStruct((N_TOK, D), F32),
        input_output_aliases={0: 0},
        scratch_shapes=[pltpu.VMEM((LAT_SEQ, D), BF)],
        compiler_params=pltpu.CompilerParams(
            dimension_semantics=("arbitrary",), vmem_limit_bytes=VMEM_LIMIT),
        name=f"lat_mixer_l{l}",
    )(x, oa, f, g, mods, npost, w_out, fc, fn, gn, ws, gb, avg)


def _dft_tables(n):
    idx = np.arange(n)
    ang = 2.0 * np.pi * ((idx[:, None] * idx[None, :]) % n) / n
    return (np.concatenate([np.cos(ang), -np.sin(ang)], axis=1) / np.sqrt(n)).astype(np.float32)


def _channel_dft_table():
    idx = np.arange(GC)
    ang = 2.0 * np.pi * ((idx[:, None] * idx[None, :]) % GC) / GC
    eye = np.eye(NGRP)
    return (np.concatenate([np.kron(eye, np.cos(ang)), np.kron(eye, np.sin(ang))], axis=1)
            / np.sqrt(GC)).astype(np.float32)


def _group_average_table():
    return np.kron(np.eye(NGRP), np.full((GC, GC), 1.0 / GC)).astype(np.float32)


def kernel(x_prompt, x_sample, cache_k, cache_v, c, c_ctx, ada_w, ada_b, norm_pre, norm_post,
           ffn_w_in, ffn_w_out, w_in, w_out, rpb, gmlp_norm, gmlp_w, gmlp_b):
    xs = (x_sample.reshape(N_LAT, D), x_prompt.reshape(N_CTX, D))
    cs = jnp.concatenate([c, c_ctx[None, :], jnp.zeros((MOD_ROWS - N_LAT_B - 1, D), F32)], axis=0)
    mods = _ada_mods(cs, ada_w, ada_b)
    bias = _bias_table(rpb)

    gn = gmlp_norm.reshape(DEPTH, 1, WC)
    gb = jnp.repeat(jnp.transpose(gmlp_b, (0, 2, 1)), GC, axis=2)
    ck = jnp.swapaxes(cache_k.reshape(N_LAT_B, DEPTH, PAST, WA), 2, 3)
    cv = jnp.swapaxes(cache_v.reshape(N_LAT_B, DEPTH, PAST, WA), 2, 3)

    fc = jnp.asarray(_channel_dft_table()).astype(BF)
    fn_ctx = jnp.asarray(_dft_tables(CTX_SEQ)).astype(BF)
    fn_lat = jnp.asarray(_dft_tables(LAT_SEQ)).astype(BF)
    avg = jnp.asarray(_group_average_table()).astype(BF)

    kv_carrier = (N_CTX_B, DEPTH, WA, CTX_SEQ)
    carried = None
    for l in range(DEPTH):
        x, (w_in_b, w_out_b), zeros = _ffn(
            xs, mods, norm_pre, norm_post, ffn_w_in, ffn_w_out, l, 0,
            cast_weights=(w_in, w_out), zero_shapes=(kv_carrier,) * 2 if l == 0 else ())
        if l == 0:
            carried = zeros
        qkv, f, g, new_k, new_v = _proj(x, mods, norm_pre, w_in_b, carried, l)
        carried = (new_k, new_v)
        oa = _nbr_attn(qkv, ck, cv, bias, l)
        x = _ctx_mixer(x, qkv, f, g, mods, norm_post, w_out_b, fc, fn_ctx, gn, gmlp_w, gb, avg, l)
        x = _lat_mixer(x, oa, f, g, mods, norm_post, w_out_b, fc, fn_lat, gn, gmlp_w, gb, avg, l)
        xs, _, _ = _ffn((x,), mods, norm_pre, norm_post, ffn_w_in, ffn_w_out, l, 1,
                        split_out=(l == DEPTH - 1))
        if l < DEPTH - 1:
            xs = (xs,)
    xl, xc = xs

    kv_out = (N_CTX_B, DEPTH, CTX_SEQ, NH, HD)
    return (xc.reshape(N_CTX_B, CTX_SEQ, D), xl.reshape(N_LAT_B, LAT_SEQ, D),
            jnp.swapaxes(new_k, 2, 3).reshape(kv_out), jnp.swapaxes(new_v, 2, 3).reshape(kv_out))
```

```python
import functools

import numpy as np
import jax
import jax.numpy as jnp
from jax import lax
from jax.experimental import pallas as pl
from jax.experimental.pallas import tpu as pltpu

D = 1024
DEPTH = 4
N_CTX_B = 16
CTX_SEQ = 256
N_LAT_B = 4
LAT_SEQ = 1024
N_LAT = N_LAT_B * LAT_SEQ
N_CTX = N_CTX_B * CTX_SEQ
N_TOK = N_LAT + N_CTX
PAST = 256
GRID_W = 64
GRID_ROWS = LAT_SEQ // GRID_W
HD = 64
NH = 8
WA = 512
WB = 256
WC = 256
NGRP = 4
GC = 64
CHUNK = 128
PROJ_W = 3 * WA + WB + 2 * WC
D_FF = 2816
FF_CHUNK = 256
N_FF_CHUNKS = D_FF // FF_CHUNK
FFN_TILE = 1024
FFN_SUBTILE = 512
WIN_ROWS = 8
WIN_COLS = 16
N_ROW_OFF = 2 * WIN_ROWS - 1
N_COL_OFF = 2 * WIN_COLS - 1
N_DELTA = 8
EPS = 1e-6
NEG = -1e30
ADA_TILE = 2304
MOD_ROWS = 8
CTX_MOD_ROW = 4
CTX_PER_STEP = 4
HEADS_PER_GROUP = 4
N_HEAD_GROUPS = NH // HEADS_PER_GROUP

BF = jnp.bfloat16
F32 = jnp.float32
QK_SCALE = HD ** -0.5

VMEM_LIMIT = 56 * 1024 * 1024


def _dot(a, b):
    return jnp.dot(a, b, preferred_element_type=F32)


def _dot_nt(a, b):
    return lax.dot_general(a, b, (((1,), (1,)), ((), ())), preferred_element_type=F32)


def _split2_dot(a, b):
    a0 = a.astype(BF)
    a1 = (a - a0.astype(F32)).astype(BF)
    return _dot(a0, b) + _dot(a1, b)


def _rms(x, g):
    return x * lax.rsqrt(jnp.mean(x * x, axis=-1, keepdims=True) + EPS) * g


def _mod_row(tile, tm):
    return jnp.minimum((tile * tm) // LAT_SEQ, CTX_MOD_ROW)


def _ada_kernel(cs_ref, w_ref, b_ref, o_ref):
    cs = cs_ref[...]
    s = cs * jax.nn.sigmoid(cs)
    o_ref[...] = _dot(s.astype(BF), w_ref[...].astype(BF)) + b_ref[...]


def _ada_mods(cs, ada_w, ada_b):
    n_out = 9 * D
    tn = ADA_TILE
    out = pl.pallas_call(
        _ada_kernel,
        grid=(DEPTH, n_out // tn),
        in_specs=[
            pl.BlockSpec((MOD_ROWS, D), lambda l, n: (0, 0)),
            pl.BlockSpec((None, D, tn), lambda l, n: (l, 0, n)),
            pl.BlockSpec((None, 1, tn), lambda l, n: (l, 0, n)),
        ],
        out_specs=pl.BlockSpec((None, MOD_ROWS, tn), lambda l, n: (l, 0, n)),
        out_shape=jax.ShapeDtypeStruct((DEPTH, MOD_ROWS, n_out), F32),
        compiler_params=pltpu.CompilerParams(
            dimension_semantics=("arbitrary", "arbitrary"), vmem_limit_bytes=VMEM_LIMIT),
        name="ada_mods",
    )(cs, ada_w, ada_b.reshape(DEPTH, 1, n_out))
    return out.reshape(DEPTH, MOD_ROWS, 9, D)


RPB_LANES = 128
RPB_HALF = RPB_LANES // 2


def _bias_kernel(rpb_ref, o_ref):
    qq = lax.broadcasted_iota(jnp.int32, (GRID_W, RPB_LANES), 0)
    lane = lax.broadcasted_iota(jnp.int32, (GRID_W, RPB_LANES), 1)
    second = lane >= GRID_W
    kk = jnp.where(second, lane - GRID_W, lane)
    col_start = jnp.clip(qq - WIN_COLS // 2, 0, GRID_W - WIN_COLS)
    visible = (kk >= col_start) & (kk < col_start + WIN_COLS)
    m = lax.broadcasted_iota(jnp.int32, (N_ROW_OFF, RPB_LANES), 1)
    diff = jnp.where(m < RPB_HALF, m, m - RPB_LANES)
    ext_idx = jnp.clip(diff + (WIN_COLS - 1), 0, N_COL_OFF - 1)
    for i in range(HEADS_PER_GROUP):
        ext = jnp.take_along_axis(rpb_ref[i], ext_idx, axis=1)
        for ro in range(N_ROW_OFF - 1):
            e0 = jnp.broadcast_to(ext[ro:ro + 1, :], (GRID_W, RPB_LANES))
            e1 = jnp.broadcast_to(ext[ro + 1:ro + 2, :], (GRID_W, RPB_LANES))
            t = jnp.where(second,
                          pltpu.roll(e1, GRID_W, 1, stride=1, stride_axis=0),
                          pltpu.roll(e0, 0, 1, stride=1, stride_axis=0))
            t = jnp.where(visible, t, NEG).astype(o_ref.dtype)
            for a in range(0, WIN_ROWS, 2):
                delta_idx = ro - a
                if 0 <= delta_idx < N_DELTA:
                    o_ref[delta_idx, i * GRID_W:(i + 1) * GRID_W, a * GRID_W:(a + 2) * GRID_W] = t


def _bias_table(rpb):
    rpb_padded = jnp.pad(rpb, ((0, 0), (0, 0), (0, 0), (0, RPB_LANES - N_COL_OFF)))
    return pl.pallas_call(
        _bias_kernel,
        grid=(DEPTH, N_HEAD_GROUPS),
        in_specs=[pl.BlockSpec((None, HEADS_PER_GROUP, N_ROW_OFF, RPB_LANES),
                               lambda l, g: (l, g, 0, 0))],
        out_specs=pl.BlockSpec((None, None, N_DELTA, HEADS_PER_GROUP * GRID_W, WIN_ROWS * GRID_W),
                               lambda l, g: (l, g, 0, 0, 0)),
        out_shape=jax.ShapeDtypeStruct(
            (DEPTH, N_HEAD_GROUPS, N_DELTA, HEADS_PER_GROUP * GRID_W, WIN_ROWS * GRID_W), F32),
        compiler_params=pltpu.CompilerParams(
            dimension_semantics=("arbitrary", "arbitrary"), vmem_limit_bytes=VMEM_LIMIT),
        name="rpb_table",
    )(rpb_padded)


def _lat_tile(t, n_lat_tiles):
    return jnp.minimum(t, n_lat_tiles - 1)


def _ctx_tile(t, n_lat_tiles):
    return jnp.maximum(t - n_lat_tiles, 0)


def _ffn_kernel(*refs, l, sub, split_in, split_out, n_lat_tiles, n_cast, n_zero):
    refs = list(refs)
    x_refs = [refs.pop(0) for _ in range(2 if split_in else 1)]
    mod_ref, npre_ref, npost_ref, wi_hbm, wo_hbm = [refs.pop(0) for _ in range(5)]
    cast_in_refs = [refs.pop(0) for _ in range(n_cast)]
    o_refs = [refs.pop(0) for _ in range(2 if split_out else 1)]
    cast_out_refs = [refs.pop(0) for _ in range(n_cast)]
    zero_refs = [refs.pop(0) for _ in range(n_zero)]
    wg_ref, wu_ref, wo_ref, sg_ref, su_ref, so_ref, h_ref, acc_ref, act_ref, sem = refs

    mo = 0 if sub == 0 else 6
    ni = 0 if sub == 0 else 2
    t = pl.program_id(0)
    is_lat = t < n_lat_tiles

    for src, dst in zip(cast_in_refs, cast_out_refs):
        dst[...] = src[...].astype(BF)
    for z in zero_refs:
        z[...] = jnp.zeros(z.shape, z.dtype)

    def load_x(rows=slice(None)):
        if split_in:
            return jnp.where(is_lat, x_refs[0][rows, :], x_refs[1][rows, :])
        return x_refs[0][rows, :]

    def store(out, rows=slice(None)):
        if not split_out:
            o_refs[0][rows, :] = out
            return

        @pl.when(is_lat)
        def _():
            o_refs[0][rows, :] = out

        @pl.when(jnp.logical_not(is_lat))
        def _():
            o_refs[1][rows, :] = out

    def modulated(x):
        shift = mod_ref[mo:mo + 1, :]
        scale = mod_ref[mo + 1:mo + 2, :]
        return (_rms(x, npre_ref[ni:ni + 1, :]) * (1.0 + scale) + shift).astype(BF)

    def residual(x, acc):
        gate = mod_ref[mo + 2:mo + 3, :]
        return x + 0.5 * gate * _rms(acc, npost_ref[ni:ni + 1, :])

    def chunk_update(h, wg, wu, wo):
        g = _dot(h, wg)
        u = _dot(h, wu)
        a = (g * jax.nn.sigmoid(g) * u).astype(BF)
        return _dot(a, wo)

    def chunk_copies(j, slot):
        cols = pl.ds(pl.multiple_of(j * FF_CHUNK, FF_CHUNK), FF_CHUNK)
        up = pl.ds(pl.multiple_of(D_FF + j * FF_CHUNK, FF_CHUNK), FF_CHUNK)
        return (
            pltpu.make_async_copy(wi_hbm.at[l, sub, :, cols], sg_ref.at[slot], sem.at[0, slot]),
            pltpu.make_async_copy(wi_hbm.at[l, sub, :, up], su_ref.at[slot], sem.at[1, slot]),
            pltpu.make_async_copy(wo_hbm.at[l, sub, cols, :], so_ref.at[slot], sem.at[2, slot]),
        )

    @pl.when(t == 0)
    def _():
        for cp in chunk_copies(0, 0):
            cp.start()
        h_ref[...] = modulated(load_x())
        acc_ref[...] = jnp.zeros(acc_ref.shape, F32)

        def body(j, carry):
            slot = lax.rem(j, 2)

            @pl.when(j + 1 < N_FF_CHUNKS)
            def _():
                for cp in chunk_copies(j + 1, 1 - slot):
                    cp.start()

            for cp in chunk_copies(j, slot):
                cp.wait()
            wg = sg_ref[slot].astype(BF)
            wu = su_ref[slot].astype(BF)
            wo = so_ref[slot].astype(BF)
            wg_ref[j] = wg
            wu_ref[j] = wu
            wo_ref[pl.ds(pl.multiple_of(j * FF_CHUNK, FF_CHUNK), FF_CHUNK), :] = wo
            acc_ref[...] += chunk_update(h_ref[...], wg, wu, wo)
            return carry

        lax.fori_loop(0, N_FF_CHUNKS, body, 0)
        store(residual(load_x(), acc_ref[...]))

    @pl.when(t > 0)
    def _():
        def sub_tile(s, carry):
            rows = pl.ds(pl.multiple_of(s * FFN_SUBTILE, FFN_SUBTILE), FFN_SUBTILE)
            x = load_x(rows)
            h = modulated(x)
            for j in range(N_FF_CHUNKS):
                g = _dot(h, wg_ref[j])
                u = _dot(h, wu_ref[j])
                act_ref[:, j * FF_CHUNK:(j + 1) * FF_CHUNK] = (g * jax.nn.sigmoid(g) * u).astype(BF)
            acc = _dot(act_ref[...], wo_ref[...])
            store(residual(x, acc), rows)
            return carry

        n_sub = acc_ref.shape[0] // FFN_SUBTILE
        if n_sub == 1:
            sub_tile(0, 0)
        else:
            lax.fori_loop(0, n_sub, sub_tile, 0)


def _ffn(xs, mods, npre, npost, wi, wo, l, sub, split_out=False, cast_weights=(), zero_shapes=()):
    split_in = len(xs) == 2
    tm = FFN_SUBTILE if (split_in or split_out) else FFN_TILE
    n_tiles = N_TOK // tm
    nl = N_LAT // tm
    lat_idx = lambda t: (_lat_tile(t, nl), 0)
    ctx_idx = lambda t: (_ctx_tile(t, nl), 0)
    split_specs = [pl.BlockSpec((tm, D), lat_idx), pl.BlockSpec((tm, D), ctx_idx)]
    joint_specs = [pl.BlockSpec((tm, D), lambda t: (t, 0))]
    split_shapes = [jax.ShapeDtypeStruct((N_LAT, D), F32), jax.ShapeDtypeStruct((N_CTX, D), F32)]
    joint_shapes = [jax.ShapeDtypeStruct((N_TOK, D), F32)]
    cast_in_specs = [pl.BlockSpec((None, w.shape[1] // n_tiles, w.shape[2]), lambda t: (l, t, 0))
                     for w in cast_weights]
    cast_out_specs = [pl.BlockSpec((w.shape[1] // n_tiles, w.shape[2]), lambda t: (t, 0))
                      for w in cast_weights]
    cast_out_shapes = [jax.ShapeDtypeStruct(w.shape[1:], BF) for w in cast_weights]
    zero_specs = [pl.BlockSpec((s[0] // n_tiles,) + tuple(s[1:]),
                               lambda t, nd=len(s): (t,) + (0,) * (nd - 1)) for s in zero_shapes]
    zero_out_shapes = [jax.ShapeDtypeStruct(tuple(s), F32) for s in zero_shapes]
    n_act = 2 if split_out else 1
    out = pl.pallas_call(
        functools.partial(_ffn_kernel, l=l, sub=sub, split_in=split_in, split_out=split_out,
                          n_lat_tiles=nl, n_cast=len(cast_weights), n_zero=len(zero_shapes)),
        grid=(n_tiles,),
        in_specs=(split_specs if split_in else joint_specs) + [
            pl.BlockSpec((None, None, 9, D), lambda t: (l, _mod_row(t, tm), 0, 0)),
            pl.BlockSpec((None, 3, D), lambda t: (l, 0, 0)),
            pl.BlockSpec((None, 3, D), lambda t: (l, 0, 0)),
            pl.BlockSpec(memory_space=pl.ANY),
            pl.BlockSpec(memory_space=pl.ANY),
        ] + cast_in_specs,
        out_specs=(split_specs if split_out else joint_specs) + cast_out_specs + zero_specs,
        out_shape=(split_shapes if split_out else joint_shapes) + cast_out_shapes + zero_out_shapes,
        scratch_shapes=[
            pltpu.VMEM((N_FF_CHUNKS, D, FF_CHUNK), BF),
            pltpu.VMEM((N_FF_CHUNKS, D, FF_CHUNK), BF),
            pltpu.VMEM((D_FF, D), BF),
            pltpu.VMEM((2, D, FF_CHUNK), F32),
            pltpu.VMEM((2, D, FF_CHUNK), F32),
            pltpu.VMEM((2, FF_CHUNK, D), F32),
            pltpu.VMEM((tm, D), BF),
            pltpu.VMEM((tm, D), F32),
            pltpu.VMEM((FFN_SUBTILE, D_FF), BF),
            pltpu.SemaphoreType.DMA((3, 2)),
        ],
        compiler_params=pltpu.CompilerParams(
            dimension_semantics=("arbitrary",), vmem_limit_bytes=VMEM_LIMIT),
        name=f"ffn_l{l}_s{sub}",
    )(*xs, mods, npre, npost, wi, wo, *cast_weights)
    acts = tuple(out[:n_act]) if split_out else out[0]
    n_cast = len(cast_weights)
    return acts, tuple(out[n_act:n_act + n_cast]), tuple(out[n_act + n_cast:])


def _proj_kernel(x_ref, mod_ref, npre_ref, w_ref, *rest, n_carried):
    qkv_ref, f_ref, g_ref, k_ref, v_ref = rest[n_carried:]
    x = x_ref[...]
    shift = mod_ref[3:4, :]
    scale = mod_ref[4:5, :]
    h = (_rms(x, npre_ref[1:2, :]) * (1.0 + scale) + shift).astype(BF)
    z = _dot(h, w_ref[...])
    qkv_ref[...] = z[:, :3 * WA].astype(BF)
    f_ref[...] = z[:, 3 * WA:3 * WA + WB].astype(BF)
    g_ref[...] = z[:, 3 * WA + WB:].astype(BF)
    for i in range(k_ref.shape[0]):
        rows = slice(i * CTX_SEQ, (i + 1) * CTX_SEQ)
        k_ref[i] = z[rows, WA:2 * WA].T
        v_ref[i] = z[rows, 2 * WA:3 * WA].T


def _proj(x, mods, npre, w_in, carried, l, tm=1024):
    n_tiles = N_TOK // tm
    nl = N_LAT // tm
    bpt = tm // CTX_SEQ
    kv_shape = jax.ShapeDtypeStruct((N_CTX_B, DEPTH, WA, CTX_SEQ), F32)
    kv_spec = pl.BlockSpec((bpt, None, WA, CTX_SEQ), lambda t: (_ctx_tile(t, nl), l, 0, 0))
    n_fixed = 4
    return pl.pallas_call(
        functools.partial(_proj_kernel, n_carried=len(carried)),
        grid=(n_tiles,),
        in_specs=[
            pl.BlockSpec((tm, D), lambda t: (t, 0)),
            pl.BlockSpec((None, None, 9, D), lambda t: (l, _mod_row(t, tm), 0, 0)),
            pl.BlockSpec((None, 3, D), lambda t: (l, 0, 0)),
            pl.BlockSpec((D, PROJ_W), lambda t: (0, 0), pipeline_mode=pl.Buffered(1)),
        ] + [pl.BlockSpec(memory_space=pl.ANY)] * len(carried),
        out_specs=[
            pl.BlockSpec((tm, 3 * WA), lambda t: (t, 0)),
            pl.BlockSpec((tm, WB), lambda t: (t, 0)),
            pl.BlockSpec((tm, 2 * WC), lambda t: (t, 0)),
            kv_spec,
            kv_spec,
        ],
        out_shape=[
            jax.ShapeDtypeStruct((N_TOK, 3 * WA), BF),
            jax.ShapeDtypeStruct((N_TOK, WB), BF),
            jax.ShapeDtypeStruct((N_TOK, 2 * WC), BF),
            kv_shape,
            kv_shape,
        ],
        input_output_aliases={n_fixed + i: 3 + i for i in range(len(carried))},
        compiler_params=pltpu.CompilerParams(
            dimension_semantics=("arbitrary",), vmem_limit_bytes=VMEM_LIMIT),
        name=f"proj_l{l}",
    )(x, mods, npre, w_in, *carried)


def _fourier(f, fc_ref, fn_ref):
    n = fn_ref.shape[0]
    xcs = _dot(f, fc_ref[...]).astype(BF)
    outs = []
    for b in range(f.shape[0] // n):
        xb = xcs[b * n:(b + 1) * n]
        stacked = jnp.concatenate([xb[:, :WB], xb[:, WB:]], axis=0)
        outs.append(_dot(fn_ref[...], stacked))
    return outs[0] if len(outs) == 1 else jnp.concatenate(outs, axis=0)


def _gmlp(g, gn_ref, ws_ref, gb_ref, avg_ref):
    gl = jax.nn.gelu(g.astype(F32), approximate=True)
    u = gl[:, :WC]
    v = gl[:, WC:]
    avg = avg_ref[...]
    mu = _split2_dot(v, avg)
    d = v - mu
    var = _dot((d * d).astype(BF), avg)
    vn = (d * lax.rsqrt(var + EPS) * gn_ref[...]).astype(BF)
    w_cat = jnp.concatenate([ws_ref[grp].astype(BF) for grp in range(NGRP)], axis=1)
    keep = (lax.broadcasted_iota(jnp.int32, (NGRP * CHUNK, WC), 0) // CHUNK
            == lax.broadcasted_iota(jnp.int32, (NGRP * CHUNK, WC), 1) // GC)
    outs = []
    for c in range(g.shape[0] // CHUNK):
        vc = vn[c * CHUNK:(c + 1) * CHUNK, :]
        rhs = jnp.where(keep, jnp.concatenate([vc] * NGRP, axis=0), jnp.zeros((), BF))
        sp = _dot(w_cat, rhs)
        outs.append(u[c * CHUNK:(c + 1) * CHUNK, :] * (sp + gb_ref[...]))
    return jnp.concatenate(outs, axis=0)


def _mix_out(x, o, gate, npost, wo_ref):
    y = _dot(o, wo_ref[...])
    return x + gate * _rms(y, npost)


def _ctx_mixer_kernel(x_ref, qkv_ref, f_ref, g_ref, mod_ref, npost_ref, wo_ref, fc_ref, fn_ref,
                      gn_ref, ws_ref, gb_ref, avg_ref, o_ref, mix_ref):
    gw = HEADS_PER_GROUP * HD
    pair_w = 2 * HD
    first_of_pair = lax.broadcasted_iota(jnp.int32, (CTX_SEQ, pair_w), 1) < HD
    own = (lax.broadcasted_iota(jnp.int32, (HEADS_PER_GROUP * CTX_SEQ, gw), 0) // CTX_SEQ
           == lax.broadcasted_iota(jnp.int32, (HEADS_PER_GROUP * CTX_SEQ, gw), 1) // HD)
    lane_head = lax.broadcasted_iota(jnp.int32, (CTX_SEQ, gw), 1) // HD
    zero = jnp.zeros((), BF)
    for b in range(x_ref.shape[0] // CTX_SEQ):
        rows = slice(b * CTX_SEQ, (b + 1) * CTX_SEQ)
        scores = []
        for h in range(NH):
            lanes = slice((h // 2) * pair_w, (h // 2 + 1) * pair_w)
            q2 = qkv_ref[rows, lanes] * QK_SCALE
            k2 = qkv_ref[rows, WA + (h // 2) * pair_w:WA + (h // 2 + 1) * pair_w]
            qh = jnp.where(first_of_pair, q2, zero) if h % 2 == 0 else jnp.where(first_of_pair, zero, q2)
            scores.append(_dot_nt(qh, k2))
        probs, dens = [], []
        for s in scores:
            p = jnp.exp(s - jnp.max(s, axis=-1, keepdims=True))
            dens.append(jnp.sum(p, axis=-1, keepdims=True))
            probs.append(p.astype(BF))
        for grp in range(N_HEAD_GROUPS):
            hs = range(grp * HEADS_PER_GROUP, (grp + 1) * HEADS_PER_GROUP)
            v4 = qkv_ref[rows, 2 * WA + grp * gw:2 * WA + (grp + 1) * gw]
            vbd = jnp.where(own, jnp.concatenate([v4] * HEADS_PER_GROUP, axis=0), zero)
            o = _dot(jnp.concatenate([probs[h] for h in hs], axis=1), vbd)
            den = dens[hs[0]]
            for i in range(1, HEADS_PER_GROUP):
                den = jnp.where(lane_head == i, dens[hs[i]], den)
            mix_ref[rows, grp * gw:(grp + 1) * gw] = (o / den).astype(BF)
    mix_ref[:, WA:WA + WB] = _fourier(f_ref[...], fc_ref, fn_ref).astype(BF)
    mix_ref[:, WA + WB:] = _gmlp(g_ref[...], gn_ref, ws_ref, gb_ref, avg_ref).astype(BF)
    o_ref[...] = _mix_out(x_ref[...], mix_ref[...], mod_ref[5:6, :], npost_ref[1:2, :], wo_ref)


def _const_spec(shape):
    nd = len(shape)
    return pl.BlockSpec(shape, lambda *_: (0,) * nd)


def _ctx_mixer(x, qkv, f, g, mods, npost, w_out, fc, fn, gn, ws, gb, avg, l):
    tm = CTX_PER_STEP * CTX_SEQ
    off = N_LAT // tm
    row = lambda b: (b + off, 0)
    return pl.pallas_call(
        _ctx_mixer_kernel,
        grid=(N_CTX_B // CTX_PER_STEP,),
        in_specs=[
            pl.BlockSpec((tm, D), row),
            pl.BlockSpec((tm, 3 * WA), row),
            pl.BlockSpec((tm, WB), row),
            pl.BlockSpec((tm, 2 * WC), row),
            pl.BlockSpec((None, None, 9, D), lambda b: (l, CTX_MOD_ROW, 0, 0)),
            pl.BlockSpec((None, 3, D), lambda b: (l, 0, 0)),
            pl.BlockSpec((D, D), lambda b: (0, 0), pipeline_mode=pl.Buffered(1)),
            _const_spec(fc.shape),
            _const_spec(fn.shape),
            pl.BlockSpec((None, 1, WC), lambda b: (l, 0, 0)),
            pl.BlockSpec((None, NGRP, CHUNK, CHUNK), lambda b: (l, 0, 0, 0)),
            pl.BlockSpec((None, CHUNK, WC), lambda b: (l, 0, 0)),
            _const_spec(avg.shape),
        ],
        out_specs=pl.BlockSpec((tm, D), row),
        out_shape=jax.ShapeDtypeStruct((N_TOK, D), F32),
        input_output_aliases={0: 0},
        scratch_shapes=[pltpu.VMEM((tm, D), BF)],
        compiler_params=pltpu.CompilerParams(
            dimension_semantics=("arbitrary",), vmem_limit_bytes=VMEM_LIMIT),
        name=f"ctx_mixer_l{l}",
    )(x, qkv, f, g, mods, npost, w_out, fc, fn, gn, ws, gb, avg)


def _window_start(r):
    return min(max(r - WIN_ROWS // 2, 0), GRID_ROWS - WIN_ROWS)


def _nbr_attn_kernel(qkv_ref, ck_ref, cv_ref, bias_ref, o_ref):
    gw = HEADS_PER_GROUP * HD
    row_head = lax.broadcasted_iota(jnp.int32, (HEADS_PER_GROUP * GRID_W, gw), 0) // GRID_W
    lane_head = lax.broadcasted_iota(jnp.int32, (HEADS_PER_GROUP * GRID_W, gw), 1) // HD
    own = row_head == lane_head
    for hg in range(N_HEAD_GROUPS):
        ck = ck_ref[hg * gw:(hg + 1) * gw, :].astype(BF)
        cv = cv_ref[hg * gw:(hg + 1) * gw, :].astype(BF)
        for r in range(GRID_ROWS):
            ws = _window_start(r)
            q = qkv_ref[r * GRID_W:(r + 1) * GRID_W, hg * gw:(hg + 1) * gw] * QK_SCALE
            qs = jnp.where(own, jnp.concatenate([q] * HEADS_PER_GROUP, axis=0), jnp.zeros((), BF))
            kw = qkv_ref[ws * GRID_W:(ws + WIN_ROWS) * GRID_W, WA + hg * gw:WA + (hg + 1) * gw]
            vw = qkv_ref[ws * GRID_W:(ws + WIN_ROWS) * GRID_W, 2 * WA + hg * gw:2 * WA + (hg + 1) * gw]
            s_loc = _dot_nt(qs, kw) + bias_ref[hg, ws - r + N_DELTA - 1]
            s_ctx = _dot(qs, ck)
            m = jnp.maximum(jnp.max(s_loc, axis=-1, keepdims=True),
                            jnp.max(s_ctx, axis=-1, keepdims=True))
            p_loc = jnp.exp(s_loc - m)
            p_ctx = jnp.exp(s_ctx - m)
            den = jnp.sum(p_loc, axis=-1, keepdims=True) + jnp.sum(p_ctx, axis=-1, keepdims=True)
            o = (_dot(p_loc.astype(BF), vw) + _dot_nt(p_ctx.astype(BF), cv)) / den
            o = jnp.where(own, o, 0.0)
            o = (o[0:GRID_W] + o[GRID_W:2 * GRID_W]) + (o[2 * GRID_W:3 * GRID_W] + o[3 * GRID_W:])
            o_ref[r * GRID_W:(r + 1) * GRID_W, hg * gw:(hg + 1) * gw] = o.astype(BF)


def _nbr_attn(qkv, cache_k, cache_v, bias, l):
    return pl.pallas_call(
        _nbr_attn_kernel,
        grid=(N_LAT_B,),
        in_specs=[
            pl.BlockSpec((LAT_SEQ, 3 * WA), lambda b: (b, 0)),
            pl.BlockSpec((None, None, WA, PAST), lambda b: (b, l, 0, 0)),
            pl.BlockSpec((None, None, WA, PAST), lambda b: (b, l, 0, 0)),
            pl.BlockSpec((None,) + bias.shape[1:], lambda b: (l, 0, 0, 0, 0)),
        ],
        out_specs=pl.BlockSpec((LAT_SEQ, WA), lambda b: (b, 0)),
        out_shape=jax.ShapeDtypeStruct((N_LAT, WA), BF),
        compiler_params=pltpu.CompilerParams(
            dimension_semantics=("arbitrary",), vmem_limit_bytes=VMEM_LIMIT),
        name=f"nbr_attn_l{l}",
    )(qkv, cache_k, cache_v, bias)


def _lat_mixer_kernel(x_ref, oa_ref, f_ref, g_ref, mod_ref, npost_ref, wo_ref, fc_ref, fn_ref,
                      gn_ref, ws_ref, gb_ref, avg_ref, o_ref, mix_ref):
    mix_ref[:, :WA] = oa_ref[...]
    mix_ref[:, WA:WA + WB] = _fourier(f_ref[...], fc_ref, fn_ref).astype(BF)
    mix_ref[:, WA + WB:] = _gmlp(g_ref[...], gn_ref, ws_ref, gb_ref, avg_ref).astype(BF)
    o_ref[...] = _mix_out(x_ref[...], mix_ref[...], mod_ref[5:6, :], npost_ref[1:2, :], wo_ref)


def _lat_mixer(x, oa, f, g, mods, npost, w_out, fc, fn, gn, ws, gb, avg, l):
    row = lambda b: (b, 0)
    return pl.pallas_call(
        _lat_mixer_kernel,
        grid=(N_LAT_B,),
        in_specs=[
            pl.BlockSpec((LAT_SEQ, D), row),
            pl.BlockSpec((LAT_SEQ, WA), row),
            pl.BlockSpec((LAT_SEQ, WB), row),
            pl.BlockSpec((LAT_SEQ, 2 * WC), row),
            pl.BlockSpec((None, None, 9, D), lambda b: (l, b, 0, 0)),
            pl.BlockSpec((None, 3, D), lambda b: (l, 0, 0)),
            pl.BlockSpec((D, D), lambda b: (0, 0), pipeline_mode=pl.Buffered(1)),
            _const_spec(fc.shape),
            pl.BlockSpec(fn.shape, lambda b: (0, 0), pipeline_mode=pl.Buffered(1)),
            pl.BlockSpec((None, 1, WC), lambda b: (l, 0, 0)),
            pl.BlockSpec((None, NGRP, CHUNK, CHUNK), lambda b: (l, 0, 0, 0)),
            pl.BlockSpec((None, CHUNK, WC), lambda b: (l, 0, 0)),
            _const_spec(avg.shape),
        ],
        out_specs=pl.BlockSpec((LAT_SEQ, D), row),
        out_shape=jax.ShapeDtypeStruct((N_TOK, D), F32),
        input_output_aliases={0: 0},
        scratch_shapes=[pltpu.VMEM((LAT_SEQ, D), BF)],
        compiler_params=pltpu.CompilerParams(
            dimension_semantics=("arbitrary",), vmem_limit_bytes=VMEM_LIMIT),
        name=f"lat_mixer_l{l}",
    )(x, oa, f, g, mods, npost, w_out, fc, fn, gn, ws, gb, avg)


def _dft_tables(n):
    idx = np.arange(n)
    ang = 2.0 * np.pi * ((idx[:, None] * idx[None, :]) % n) / n
    return (np.concatenate([np.cos(ang), -np.sin(ang)], axis=1) / np.sqrt(n)).astype(np.float32)


def _channel_dft_table():
    idx = np.arange(GC)
    ang = 2.0 * np.pi * ((idx[:, None] * idx[None, :]) % GC) / GC
    eye = np.eye(NGRP)
    return (np.concatenate([np.kron(eye, np.cos(ang)), np.kron(eye, np.sin(ang))], axis=1)
            / np.sqrt(GC)).astype(np.float32)


def _group_average_table():
    return np.kron(np.eye(NGRP), np.full((GC, GC), 1.0 / GC)).astype(np.float32)


def kernel(x_prompt, x_sample, cache_k, cache_v, c, c_ctx, ada_w, ada_b, norm_pre, norm_post,
           ffn_w_in, ffn_w_out, w_in, w_out, rpb, gmlp_norm, gmlp_w, gmlp_b):
    xs = (x_sample.reshape(N_LAT, D), x_prompt.reshape(N_CTX, D))
    cs = jnp.concatenate([c, c_ctx[None, :], jnp.zeros((MOD_ROWS - N_LAT_B - 1, D), F32)], axis=0)
    mods = _ada_mods(cs, ada_w, ada_b)
    bias = _bias_table(rpb)

    gn = gmlp_norm.reshape(DEPTH, 1, WC)
    gb = jnp.repeat(jnp.transpose(gmlp_b, (0, 2, 1)), GC, axis=2)
    ck = jnp.swapaxes(cache_k.reshape(N_LAT_B, DEPTH, PAST, WA), 2, 3)
    cv = jnp.swapaxes(cache_v.reshape(N_LAT_B, DEPTH, PAST, WA), 2, 3)

    fc = jnp.asarray(_channel_dft_table()).astype(BF)
    fn_ctx = jnp.asarray(_dft_tables(CTX_SEQ)).astype(BF)
    fn_lat = jnp.asarray(_dft_tables(LAT_SEQ)).astype(BF)
    avg = jnp.asarray(_group_average_table()).astype(BF)

    kv_carrier = (N_CTX_B, DEPTH, WA, CTX_SEQ)
    carried = None
    for l in range(DEPTH):
        x, (w_in_b, w_out_b), zeros = _ffn(
            xs, mods, norm_pre, norm_post, ffn_w_in, ffn_w_out, l, 0,
            cast_weights=(w_in, w_out), zero_shapes=(kv_carrier,) * 2 if l == 0 else ())
        if l == 0:
            carried = zeros
        qkv, f, g, new_k, new_v = _proj(x, mods, norm_pre, w_in_b, carried, l)
        carried = (new_k, new_v)
        oa = _nbr_attn(qkv, ck, cv, bias, l)
        x = _ctx_mixer(x, qkv, f, g, mods, norm_post, w_out_b, fc, fn_ctx, gn, gmlp_w, gb, avg, l)
        x = _lat_mixer(x, oa, f, g, mods, norm_post, w_out_b, fc, fn_lat, gn, gmlp_w, gb, avg, l)
        xs, _, _ = _ffn((x,), mods, norm_pre, norm_post, ffn_w_in, ffn_w_out, l, 1,
                        split_out=(l == DEPTH - 1))
        if l < DEPTH - 1:
            xs = (xs,)
    xl, xc = xs

    kv_out = (N_CTX_B, DEPTH, CTX_SEQ, NH, HD)
    return (xc.reshape(N_CTX_B, CTX_SEQ, D), xl.reshape(N_LAT_B, LAT_SEQ, D),
            jnp.swapaxes(new_k, 2, 3).reshape(kv_out), jnp.swapaxes(new_v, 2, 3).reshape(kv_out))
```
